```python
import math
import jax, jax.numpy as jnp
from jax import lax
import numpy as np

D_MODEL = 1024
BATCH = 4
SEQ = 8192
DEPTH = 2

GRID_W = 64
CTX_LEN = 256
HEAD_DIM = 64
ROPE_BASE = 10000.0
Q_BLOCK = 128
EPS = 1e-6
BRANCH_WIDTH = D_MODEL // 2
N_BRANCH = 3
A_HEADS = BRANCH_WIDTH // HEAD_DIM
A_KV_HEADS = A_HEADS // 4
A_GROUP = A_HEADS // A_KV_HEADS
S5_CH_PER_GROUP = 16
S5_STATE = 64
S5_GROUPS = BRANCH_WIDTH // S5_CH_PER_GROUP
C_HEADS = BRANCH_WIDTH // (2 * HEAD_DIM)
C_VDIM = 2 * HEAD_DIM
D_FF = 2816
IN_SIZES = (A_HEADS * HEAD_DIM, A_KV_HEADS * HEAD_DIM, A_KV_HEADS * HEAD_DIM, BRANCH_WIDTH,
            2 * C_HEADS * HEAD_DIM, 2 * C_HEADS * HEAD_DIM, C_HEADS * C_VDIM, N_BRANCH * D_MODEL)
IN_SPLITS = tuple(int(s) for s in np.cumsum(IN_SIZES)[:-1])
N_IN = int(sum(IN_SIZES))

kernel_name = "hybrid_gqa_s5_diffattn_convffn_dit"


def rms_norm(x, gain):
    xf = x.astype(jnp.float32)
    y = xf * lax.rsqrt(jnp.mean(xf * xf, axis=-1, keepdims=True) + EPS)
    return (y * gain.astype(jnp.float32)).astype(x.dtype)


def modulate(h, shift, scale):
    return h * (1.0 + scale) + shift


def axial_rope_tables(n_tokens, dtype):
    rows = n_tokens // GRID_W
    row = jnp.repeat(jnp.arange(rows, dtype=jnp.float32), GRID_W)
    col = jnp.tile(jnp.arange(GRID_W, dtype=jnp.float32), rows)
    n_freq = HEAD_DIM // 4
    inv_freq = ROPE_BASE ** (-jnp.arange(n_freq, dtype=jnp.float32) / n_freq)
    ang = jnp.concatenate([row[:, None] * inv_freq, col[:, None] * inv_freq], axis=-1)
    ang = jnp.concatenate([ang, ang], axis=-1)
    return jnp.cos(ang).astype(dtype)[:, None, :], jnp.sin(ang).astype(dtype)[:, None, :]


def apply_rope(x, cos, sin):
    half = HEAD_DIM // 2
    rot = jnp.concatenate([-x[..., half:], x[..., :half]], axis=-1)
    return x * cos + rot * sin


def sweep_query_blocks(attend, *qs):
    b, n = qs[0].shape[:2]
    nb = n // Q_BLOCK
    blocks = tuple(jnp.swapaxes(q.reshape(b, nb, Q_BLOCK, *q.shape[2:]), 0, 1) for q in qs)
    out = lax.map(lambda blk: attend(*blk), blocks)
    return jnp.swapaxes(out, 0, 1).reshape(b, n, *out.shape[3:])


def gqa_attend(q, k, v):
    s = jnp.einsum('btkgd,bskd->bkgts', q, k).astype(jnp.float32) * (HEAD_DIM ** -0.5)
    p = jax.nn.softmax(s, axis=-1).astype(v.dtype)
    return jnp.einsum('bkgts,bskd->btkgd', p, v)


def diff_attend(q1, q2, k1, k2, v, lam):
    scale = HEAD_DIM ** -0.5
    p1 = jax.nn.softmax(jnp.einsum('bthd,bshd->bhts', q1, k1).astype(jnp.float32) * scale, axis=-1)
    p2 = jax.nn.softmax(jnp.einsum('bthd,bshd->bhts', q2, k2).astype(jnp.float32) * scale, axis=-1)
    p = (p1 - lam * p2).astype(v.dtype)
    return jnp.einsum('bhts,bshe->bthe', p, v)


def heads(t, n_heads, dim):
    return t.reshape(t.shape[0], t.shape[1], n_heads, dim)


def gqa_mixer(q, k, v, qc, kc, vc, g_q, g_k, cos, sin, ctx_out):
    b, n = q.shape[:2]
    q = apply_rope(rms_norm(heads(q, A_HEADS, HEAD_DIM), g_q), cos, sin)
    q = q.reshape(b, n, A_KV_HEADS, A_GROUP, HEAD_DIM)
    k = apply_rope(rms_norm(heads(k, A_KV_HEADS, HEAD_DIM), g_k), cos, sin)
    v = heads(v, A_KV_HEADS, HEAD_DIM)
    kc = rms_norm(heads(kc, A_KV_HEADS, HEAD_DIM), g_k)
    vc = heads(vc, A_KV_HEADS, HEAD_DIM)
    k_all = jnp.concatenate([kc, k], axis=1)
    v_all = jnp.concatenate([vc, v], axis=1)
    y = sweep_query_blocks(lambda qb: gqa_attend(qb, k_all, v_all), q).reshape(b, n, BRANCH_WIDTH)
    yc = None
    if ctx_out:
        nc = qc.shape[1]
        qc = rms_norm(heads(qc, A_HEADS, HEAD_DIM), g_q).reshape(b, nc, A_KV_HEADS, A_GROUP, HEAD_DIM)
        yc = gqa_attend(qc, kc, vc).reshape(b, nc, BRANCH_WIDTH)
    return y, yc


def diff_mixer(q, k, v, qc, kc, vc, g_q, g_k, lam_vecs, g_out, lam_init, cos, sin, ctx_out):
    def split_qk(t, gain, rope):
        b, n = t.shape[:2]
        t = rms_norm(heads(t, 2 * C_HEADS, HEAD_DIM), gain)
        if rope:
            t = apply_rope(t, cos, sin)
        t = t.reshape(b, n, C_HEADS, 2, HEAD_DIM)
        return t[..., 0, :], t[..., 1, :]

    lv = lam_vecs.astype(jnp.float32)
    lam = jnp.exp(jnp.sum(lv[0] * lv[1])) - jnp.exp(jnp.sum(lv[2] * lv[3])) + lam_init

    def finish(o):
        o = rms_norm(o, g_out) * (1.0 - lam_init)
        return o.reshape(o.shape[0], o.shape[1], BRANCH_WIDTH)

    q1, q2 = split_qk(q, g_q, True)
    k1, k2 = split_qk(k, g_k, True)
    kc1, kc2 = split_qk(kc, g_k, False)
    v = heads(v, C_HEADS, C_VDIM)
    vc = heads(vc, C_HEADS, C_VDIM)
    k1_all = jnp.concatenate([kc1, k1], axis=1)
    k2_all = jnp.concatenate([kc2, k2], axis=1)
    v_all = jnp.concatenate([vc, v], axis=1)
    y = finish(sweep_query_blocks(lambda a, b_: diff_attend(a, b_, k1_all, k2_all, v_all, lam), q1, q2))
    yc = None
    if ctx_out:
        qc1, qc2 = split_qk(qc, g_q, False)
        yc = finish(diff_attend(qc1, qc2, kc1, kc2, vc, lam))
    return y, yc


def zoh(lam_re, lam_im, log_dt, b_re, b_im):
    dt = jnp.exp(log_dt)[:, None]
    mag = jnp.exp(lam_re * dt)
    a_re = mag * jnp.cos(lam_im * dt)
    a_im = mag * jnp.sin(lam_im * dt)
    den = lam_re * lam_re + lam_im * lam_im
    f_re = ((a_re - 1.0) * lam_re + a_im * lam_im) / den
    f_im = (a_im * lam_re - (a_re - 1.0) * lam_im) / den
    bb_re = f_re[..., None] * b_re - f_im[..., None] * b_im
    bb_im = f_re[..., None] * b_im + f_im[..., None] * b_re
    return a_re, a_im, bb_re, bb_im


def complex_linear_scan(a_re, a_im, b_re, b_im):
    n = b_re.shape[1]
    a_re_t = jnp.broadcast_to(a_re, (1, n) + a_re.shape)
    a_im_t = jnp.broadcast_to(a_im, (1, n) + a_im.shape)

    def combine(earlier, later):
        ar, ai, br, bi = earlier
        cr, ci, dr, di = later
        return (cr * ar - ci * ai, cr * ai + ci * ar,
                cr * br - ci * bi + dr, cr * bi + ci * br + di)

    _, _, x_re, x_im = lax.associative_scan(combine, (a_re_t, a_im_t, b_re, b_im), axis=1)
    return x_re, x_im


def s5_direction(u, uc, a_re, a_im, bb_re, bb_im, c_re, c_im, reverse, ctx_out):
    def flip(t):
        return jnp.flip(t, axis=1) if reverse else t

    def drive(t):
        return (jnp.einsum('blgc,gpc->blgp', t, bb_re), jnp.einsum('blgc,gpc->blgp', t, bb_im))

    def readout(xr, xi):
        return jnp.einsum('blgp,gcp->blgc', xr, c_re) - jnp.einsum('blgp,gcp->blgc', xi, c_im)

    bc_re, bc_im = drive(flip(uc))
    xc_re, xc_im = complex_linear_scan(a_re, a_im, bc_re, bc_im)
    s_re, s_im = xc_re[:, -1], xc_im[:, -1]
    b_re, b_im = drive(flip(u))
    b_re = b_re.at[:, 0].add(a_re * s_re - a_im * s_im)
    b_im = b_im.at[:, 0].add(a_re * s_im + a_im * s_re)
    x_re, x_im = complex_linear_scan(a_re, a_im, b_re, b_im)
    y = flip(readout(x_re, x_im))
    yc = flip(readout(xc_re, xc_im)) if ctx_out else None
    return y, yc


def s5_mixer(u, uc, lam_re, lam_im, log_dt, b_re, b_im, c_re, c_im, d_skip, w_glu, b_glu, ctx_out):
    dtype = u.dtype
    f = lambda t: t.astype(jnp.float32)
    to_groups = lambda t: f(t).reshape(t.shape[0], t.shape[1], S5_GROUPS, S5_CH_PER_GROUP)
    ug, ucg = to_groups(u), to_groups(uc)
    outs = [s5_direction(ug, ucg, *zoh(f(lam_re[d]), f(lam_im[d]), f(log_dt[d]), f(b_re[d]), f(b_im[d])),
                         f(c_re[d]), f(c_im[d]), reverse=(d == 1), ctx_out=ctx_out)
            for d in range(2)]
    d_g = f(d_skip).reshape(S5_GROUPS, S5_CH_PER_GROUP)

    def glu(y, ugr):
        y = y + d_g * ugr
        g = jax.nn.gelu(y.reshape(y.shape[0], y.shape[1], BRANCH_WIDTH))
        return (g * jax.nn.sigmoid(g @ f(w_glu) + f(b_glu))).astype(dtype)

    y = glu(outs[0][0] + outs[1][0], ug)
    yc = glu(outs[0][1] + outs[1][1], ucg) if ctx_out else None
    return y, yc


def merge_branches(gate_logits, ya, yb, yc, w_branch, w_out):
    g = jax.nn.sigmoid(gate_logits.reshape(*gate_logits.shape[:-1], N_BRANCH, D_MODEL))
    m = (g[..., 0, :] * (ya @ w_branch[0]) + g[..., 1, :] * (yb @ w_branch[1])
         + g[..., 2, :] * (yc @ w_branch[2]))
    return m @ w_out


def hybrid_mixer(h, hc, w_in, qk_gain, lam_re, lam_im, log_dt, b_re, b_im, c_re, c_im, d_skip,
                 w_glu, b_glu, diff_lam, diff_norm_g, w_branch, w_out, lam_init, cos, sin, ctx_out):
    z = jnp.split(h @ w_in, IN_SPLITS, axis=-1)
    zc = jnp.split(hc @ w_in, IN_SPLITS, axis=-1)
    ya, yca = gqa_mixer(z[0], z[1], z[2], zc[0], zc[1], zc[2], qk_gain[0], qk_gain[1], cos, sin, ctx_out)
    yb, ycb = s5_mixer(z[3], zc[3], lam_re, lam_im, log_dt, b_re, b_im, c_re, c_im, d_skip,
                       w_glu, b_glu, ctx_out)
    yd, ycd = diff_mixer(z[4], z[5], z[6], zc[4], zc[5], zc[6], qk_gain[2], qk_gain[3], diff_lam,
                         diff_norm_g, lam_init, cos, sin, ctx_out)
    y = merge_branches(z[7], ya, yb, yd, w_branch, w_out)
    yc = merge_branches(zc[7], yca, ycb, ycd, w_branch, w_out) if ctx_out else None
    return y, yc


def dwconv3(u, w, b):
    up = jnp.pad(u, ((0, 0), (1, 1), (0, 0)))
    return up[:, :-2] * w[0] + up[:, 1:-1] * w[1] + up[:, 2:] * w[2] + b


def conv_ffn(h, w_up, conv_w, conv_b, w_down):
    u = dwconv3(h @ w_up, conv_w, conv_b)
    a, g = jnp.split(u, 2, axis=-1)
    return (jax.nn.silu(g) * a) @ w_down


def setup_inputs(seed: int = 0) -> dict:
    key = jax.random.key(seed)
    ks = jax.random.split(key, 32)
    f32 = jnp.float32
    nrm = lambda k, shape, scale: jax.random.normal(k, shape, f32) * scale
    G, P, C = S5_GROUPS, S5_STATE, S5_CH_PER_GROUP
    lam_im_base = math.pi * jnp.arange(P, dtype=f32)
    return {
        "x": nrm(ks[0], (BATCH, SEQ, D_MODEL), 1.0),
        "c": nrm(ks[1], (BATCH, D_MODEL), 1.0),
        "ctx": nrm(ks[2], (BATCH, CTX_LEN, D_MODEL), 1.0),
        "c_ctx": nrm(ks[3], (D_MODEL,), 1.0),
        "w_ada": nrm(ks[4], (DEPTH, D_MODEL, 6 * D_MODEL), 0.5 * D_MODEL ** -0.5),
        "b_ada": nrm(ks[5], (DEPTH, 6 * D_MODEL), 0.02),
        "norm_g": 1.0 + nrm(ks[6], (DEPTH, 2, D_MODEL), 0.05),
        "w_in": nrm(ks[7], (DEPTH, D_MODEL, N_IN), D_MODEL ** -0.5),
        "qk_gain": 1.0 + nrm(ks[8], (DEPTH, 4, HEAD_DIM), 0.05),
        "ssm_lam_re": -0.5 + nrm(ks[9], (DEPTH, 2, G, P), 0.01),
        "ssm_lam_im": lam_im_base + nrm(ks[10], (DEPTH, 2, G, P), 0.01),
        "ssm_log_dt": jax.random.uniform(ks[11], (DEPTH, 2, G), f32, math.log(1e-3), math.log(1e-1)),
        "ssm_b_re": nrm(ks[12], (DEPTH, 2, G, P, C), (2 * C) ** -0.5),
        "ssm_b_im": nrm(ks[13], (DEPTH, 2, G, P, C), (2 * C) ** -0.5),
        "ssm_c_re": nrm(ks[14], (DEPTH, 2, G, C, P), (2 * P) ** -0.5),
        "ssm_c_im": nrm(ks[15], (DEPTH, 2, G, C, P), (2 * P) ** -0.5),
        "ssm_d": nrm(ks[16], (DEPTH, BRANCH_WIDTH), 0.5),
        "w_glu": nrm(ks[17], (DEPTH, BRANCH_WIDTH, BRANCH_WIDTH), BRANCH_WIDTH ** -0.5),
        "b_glu": nrm(ks[18], (DEPTH, BRANCH_WIDTH), 0.02),
        "diff_lam": nrm(ks[19], (DEPTH, 4, HEAD_DIM), 0.1),
        "diff_norm_g": 1.0 + nrm(ks[20], (DEPTH, C_VDIM), 0.05),
        "w_branch": nrm(ks[21], (DEPTH, N_BRANCH, BRANCH_WIDTH, D_MODEL), BRANCH_WIDTH ** -0.5),
        "w_out": nrm(ks[22], (DEPTH, D_MODEL, D_MODEL), D_MODEL ** -0.5),
        "w_up": nrm(ks[23], (DEPTH, D_MODEL, 2 * D_FF), D_MODEL ** -0.5),
        "conv_w": nrm(ks[24], (DEPTH, 3, 2 * D_FF), 3 ** -0.5),
        "conv_b": nrm(ks[25], (DEPTH, 2 * D_FF), 0.02),
        "w_down": nrm(ks[26], (DEPTH, D_FF, D_MODEL), D_FF ** -0.5),
    }


def reference(x, c, ctx, c_ctx, w_ada, b_ada, norm_g, w_in, qk_gain, ssm_lam_re, ssm_lam_im,
              ssm_log_dt, ssm_b_re, ssm_b_im, ssm_c_re, ssm_c_im, ssm_d, w_glu, b_glu, diff_lam,
              diff_norm_g, w_branch, w_out, w_up, conv_w, conv_b, w_down):
    cos, sin = axial_rope_tables(x.shape[1], x.dtype)
    xc = ctx
    for i in range(DEPTH):
        last = i == DEPTH - 1
        lam_init = 0.8 - 0.6 * math.exp(-0.3 * i)
        mod = jnp.split((jax.nn.silu(c) @ w_ada[i] + b_ada[i])[:, None, :], 6, axis=-1)
        modc = jnp.split(jax.nn.silu(c_ctx) @ w_ada[i] + b_ada[i], 6, axis=-1)
        h = modulate(rms_norm(x, norm_g[i, 0]), mod[0], mod[1])
        hc = modulate(rms_norm(xc, norm_g[i, 0]), modc[0], modc[1])
        y, yc = hybrid_mixer(h, hc, w_in[i], qk_gain[i], ssm_lam_re[i], ssm_lam_im[i], ssm_log_dt[i],
                             ssm_b_re[i], ssm_b_im[i], ssm_c_re[i], ssm_c_im[i], ssm_d[i], w_glu[i],
                             b_glu[i], diff_lam[i], diff_norm_g[i], w_branch[i], w_out[i], lam_init,
                             cos, sin, ctx_out=not last)
        x = x + mod[2] * y
        h = modulate(rms_norm(x, norm_g[i, 1]), mod[3], mod[4])
        x = x + mod[5] * conv_ffn(h, w_up[i], conv_w[i], conv_b[i], w_down[i])
        if not last:
            xc = xc + modc[2] * yc
            hc = modulate(rms_norm(xc, norm_g[i, 1]), modc[3], modc[4])
            xc = xc + modc[5] * conv_ffn(hc, w_up[i], conv_w[i], conv_b[i], w_down[i])
    return x
```

```python
import functools
import math

import jax
import jax.numpy as jnp
import numpy as np
from jax import lax
from jax.experimental import pallas as pl
from jax.experimental.pallas import tpu as pltpu

F32 = jnp.float32
BF16 = jnp.bfloat16

D_MODEL = 1024
HEAD_DIM = 64
GRID_W = 64
ROPE_BASE = 10000.0
EPS = 1e-6
BRANCH_WIDTH = D_MODEL // 2
A_HEADS = BRANCH_WIDTH // HEAD_DIM
A_KV_HEADS = A_HEADS // 4
C_HEADS = BRANCH_WIDTH // (2 * HEAD_DIM)
S5_CH = 16
S5_STATE = 64
S5_GROUPS = BRANCH_WIDTH // S5_CH
D_FF = 2816
IN_SIZES = (512, 128, 128, 512, 512, 512, 512, 3 * D_MODEL)
IN_OFFS = tuple(int(v) for v in np.cumsum((0,) + IN_SIZES))
N_QKVU = IN_OFFS[7]

LANES = 128
SUBLANES = 8
TM = 256
TK = 256
TC = 64
FF_CHUNK = 256
HALO = 16
NEG_BIG = -1e30
VMEM_LIMIT = 56 * 1024 * 1024
SCORE_SCALE = HEAD_DIM ** -0.5 * math.log2(math.e)


def _cparams(n_axes):
    return pltpu.CompilerParams(dimension_semantics=("arbitrary",) * n_axes,
                                vmem_limit_bytes=VMEM_LIMIT)


def _rms_mod(x, gain, shift, scale):
    y = x * lax.rsqrt(jnp.mean(x * x, axis=-1, keepdims=True) + EPS)
    return (y * gain) * (1.0 + scale) + shift


def _head_avg_matrix(width):
    r = lax.broadcasted_iota(jnp.int32, (width, width), 0) >> 6
    c = lax.broadcasted_iota(jnp.int32, (width, width), 1) >> 6
    return jnp.where(r == c, 1.0 / HEAD_DIM, 0.0).astype(BF16)


def _head_rms(z, gain, avg):
    sq = z * z
    hi = sq.astype(BF16)
    lo = (sq - hi.astype(F32)).astype(BF16)
    ms = (jnp.dot(hi, avg, preferred_element_type=F32)
          + jnp.dot(lo, avg, preferred_element_type=F32))
    return z * lax.rsqrt(ms + EPS) * gain


def _tile_lanes(t, width):
    reps = width // t.shape[1]
    return t if reps == 1 else jnp.concatenate([t] * reps, axis=1)


def _rope(z, cos, sin_signed):
    width = z.shape[1]
    lane = lax.broadcasted_iota(jnp.int32, z.shape, 1)
    first_half = (lane & (HEAD_DIM - 1)) < HEAD_DIM // 2
    rot = jnp.where(first_half,
                    pltpu.roll(z, width - HEAD_DIM // 2, 1),
                    pltpu.roll(z, HEAD_DIM // 2, 1))
    return z * _tile_lanes(cos, width) + rot * _tile_lanes(sin_signed, width)


def _dup_halves(z):
    lane = lax.broadcasted_iota(jnp.int32, z.shape, 1)
    low = lane < HEAD_DIM
    sw = pltpu.roll(z, HEAD_DIM, 1)
    return jnp.where(low, z, sw), jnp.where(low, sw, z)


def _ada_kernel(c_ref, w_ref, b_ref, o_ref):
    a = jax.nn.silu(c_ref[...])
    o_ref[0] = jnp.dot(a, w_ref[0], preferred_element_type=F32,
                       precision=lax.Precision.HIGHEST) + b_ref[0]


def _ada_mods(cc, w_ada, b_ada):
    depth, _, n = w_ada.shape
    tn = 1536
    return pl.pallas_call(
        _ada_kernel,
        grid=(depth, n // tn),
        in_specs=[pl.BlockSpec((SUBLANES, D_MODEL), lambda l, j: (0, 0)),
                  pl.BlockSpec((1, D_MODEL, tn), lambda l, j: (l, 0, j)),
                  pl.BlockSpec((1, 1, tn), lambda l, j: (l, 0, j))],
        out_specs=pl.BlockSpec((1, SUBLANES, tn), lambda l, j: (l, 0, j)),
        out_shape=jax.ShapeDtypeStruct((depth, SUBLANES, n), F32),
        compiler_params=_cparams(2),
        name="ada_mods",
    )(cc, w_ada, b_ada.reshape(depth, 1, n))


def _inproj_kernel(x_ref, mod_ref, g_ref, w_ref, qk_ref, cos_ref, sin_ref,
                   qa_ref, kta_ref, va_ref, u_ref, qc_ref, ktc_ref, vc_ref):
    mods = mod_ref[0]
    h = _rms_mod(x_ref[...], g_ref[...], mods[0:1], mods[1:2]).astype(BF16)
    cos = cos_ref[...]
    sin = sin_ref[...]
    avg512 = _head_avg_matrix(512)

    def seg(k):
        return jnp.dot(h, w_ref[:, IN_OFFS[k]:IN_OFFS[k + 1]], preferred_element_type=F32)

    def gain(k, width):
        return qk_ref[k:k + 1, :width]

    q = _rope(_head_rms(seg(0), gain(0, 512), avg512), cos, sin)
    qa_ref[...] = (q * SCORE_SCALE).astype(BF16)
    k = _rope(_head_rms(seg(1), gain(1, 128), avg512[:128, :128]), cos, sin)
    for hh, kd in enumerate(_dup_halves(k)):
        kta_ref[0, hh] = kd.T.astype(BF16)
    for hh, vd in enumerate(_dup_halves(seg(2))):
        va_ref[0, hh] = vd.astype(BF16)
    u_ref[...] = seg(3)
    q = _rope(_head_rms(seg(4), gain(2, 512), avg512), cos, sin)
    qc_ref[...] = (q * SCORE_SCALE).astype(BF16)
    k = _rope(_head_rms(seg(5), gain(3, 512), avg512), cos, sin)
    v = seg(6)
    for hh in range(C_HEADS):
        ktc_ref[0, hh] = k[:, hh * LANES:(hh + 1) * LANES].T.astype(BF16)
        vc_ref[0, hh] = v[:, hh * LANES:(hh + 1) * LANES].astype(BF16)


def _inproj(xa, mods, norm_g, w_qkvu, qk_gain, cos_t, sin_t, batch, nblk):
    t_rows = xa.shape[0]
    s_len = nblk * TM
    row = lambda i: (i, 0)
    mod_map = lambda i: (jnp.where(i % nblk == 0, batch, i // nblk), 0, 0)
    pos = lambda i: (i % nblk, 0)
    kt_map = lambda i: (i // nblk, 0, 0, i % nblk)
    v_map = lambda i: (i // nblk, 0, i % nblk, 0)
    return pl.pallas_call(
        _inproj_kernel,
        grid=(t_rows // TM,),
        in_specs=[pl.BlockSpec((TM, D_MODEL), row),
                  pl.BlockSpec((1, 6, D_MODEL), mod_map),
                  pl.BlockSpec((1, D_MODEL), lambda i: (0, 0)),
                  pl.BlockSpec((D_MODEL, N_QKVU), lambda i: (0, 0)),
                  pl.BlockSpec((4, BRANCH_WIDTH), lambda i: (0, 0)),
                  pl.BlockSpec((TM, LANES), pos),
                  pl.BlockSpec((TM, LANES), pos)],
        out_specs=[pl.BlockSpec((TM, 512), row),
                   pl.BlockSpec((1, A_KV_HEADS, LANES, TM), kt_map),
                   pl.BlockSpec((1, A_KV_HEADS, TM, LANES), v_map),
                   pl.BlockSpec((TM, 512), row),
                   pl.BlockSpec((TM, 512), row),
                   pl.BlockSpec((1, C_HEADS, LANES, TM), kt_map),
                   pl.BlockSpec((1, C_HEADS, TM, LANES), v_map)],
        out_shape=[jax.ShapeDtypeStruct((t_rows, 512), BF16),
                   jax.ShapeDtypeStruct((batch, A_KV_HEADS, LANES, s_len), BF16),
                   jax.ShapeDtypeStruct((batch, A_KV_HEADS, s_len, LANES), BF16),
                   jax.ShapeDtypeStruct((t_rows, 512), F32),
                   jax.ShapeDtypeStruct((t_rows, 512), BF16),
                   jax.ShapeDtypeStruct((batch, C_HEADS, LANES, s_len), BF16),
                   jax.ShapeDtypeStruct((batch, C_HEADS, s_len, LANES), BF16)],
        compiler_params=_cparams(1),
        name="in_proj",
    )(xa, mods, norm_g, w_qkvu, qk_gain, cos_t, sin_t)


def _attn_kernel(*refs, n_pairs, diff, lam_init, ctx_chunks, all_chunks):
    if diff:
        q_ref, kt_ref, v_ref, lam_ref, gout_ref, o_ref, lhs_sc, m_sc, l_sc, acc_sc = refs
    else:
        q_ref, kt_ref, v_ref, o_ref, lhs_sc, m_sc, l_sc, acc_sc = refs
    j = pl.program_id(2)
    rows = 2 * n_pairs * TM
    lane = lax.broadcasted_iota(jnp.int32, (TM, LANES), 1)
    low = lane < HEAD_DIM
    for p in range(n_pairs):
        qt = q_ref[:, p * LANES:(p + 1) * LANES].astype(F32)
        lhs_sc[(2 * p) * TM:(2 * p + 1) * TM, :] = jnp.where(low, qt, 0.0).astype(BF16)
        lhs_sc[(2 * p + 1) * TM:(2 * p + 2) * TM, :] = jnp.where(low, 0.0, qt).astype(BF16)
    m_sc[...] = jnp.full((rows, LANES), NEG_BIG, F32)
    l_sc[...] = jnp.zeros((rows, LANES), F32)
    acc_sc[...] = jnp.zeros((rows, LANES), F32)

    def chunk(c, carry):
        off = pl.multiple_of(c * TK, TK)
        s = jnp.dot(lhs_sc[...], kt_ref[0, 0, :, pl.ds(off, TK)], preferred_element_type=F32)
        m_prev = m_sc[...]
        m_next = jnp.maximum(m_prev, jnp.max(s, axis=1, keepdims=True))
        alpha = jnp.exp2(m_prev - m_next)
        p = jnp.exp2(s - _tile_lanes(m_next, TK))
        psum = p[:, 0:LANES]
        for t in range(1, TK // LANES):
            psum = psum + p[:, t * LANES:(t + 1) * LANES]
        l_sc[...] = alpha * l_sc[...] + psum
        acc_sc[...] = alpha * acc_sc[...] + jnp.dot(
            p.astype(BF16), v_ref[0, 0, pl.ds(off, TK), :], preferred_element_type=F32)
        m_sc[...] = m_next
        return carry

    lax.fori_loop(0, jnp.where(j == 0, ctx_chunks, all_chunks), chunk, 0)

    o = acc_sc[...] / jnp.sum(l_sc[...], axis=1, keepdims=True)
    if diff:
        lv = lam_ref[...]
        lam = (jnp.exp(jnp.sum(lv[0:1] * lv[1:2], axis=1, keepdims=True))
               - jnp.exp(jnp.sum(lv[2:3] * lv[3:4], axis=1, keepdims=True)) + lam_init)
        od = o[0:TM] - lam * o[TM:2 * TM]
        od = od * lax.rsqrt(jnp.mean(od * od, axis=-1, keepdims=True) + EPS)
        o_ref[...] = ((od * gout_ref[...]) * (1.0 - lam_init)).astype(o_ref.dtype)
    else:
        for p in range(n_pairs):
            pair = jnp.where(low, o[(2 * p) * TM:(2 * p + 1) * TM],
                             o[(2 * p + 1) * TM:(2 * p + 2) * TM])
            o_ref[:, p * LANES:(p + 1) * LANES] = pair.astype(o_ref.dtype)


def _attention(q, kt, v, batch, nblk, *, diff, lam_vecs=None, g_out=None, lam_init=0.0):
    heads = kt.shape[1]
    s_len = kt.shape[3]
    n_pairs = 1 if diff else 2
    qw = n_pairs * LANES
    rows = 2 * n_pairs * TM
    q_map = lambda b, h, j: (b * nblk + j, h)
    kv_map = lambda b, h, j: (b, h, 0, 0)
    const = lambda b, h, j: (0, 0)
    in_specs = [pl.BlockSpec((TM, qw), q_map),
                pl.BlockSpec((1, 1, LANES, s_len), kv_map),
                pl.BlockSpec((1, 1, s_len, LANES), kv_map)]
    args = [q, kt, v]
    if diff:
        in_specs += [pl.BlockSpec((4, HEAD_DIM), const), pl.BlockSpec((1, LANES), const)]
        args += [lam_vecs, g_out]
    kern = functools.partial(_attn_kernel, n_pairs=n_pairs, diff=diff, lam_init=lam_init,
                             ctx_chunks=TM // TK, all_chunks=s_len // TK)
    return pl.pallas_call(
        kern,
        grid=(batch, heads, nblk),
        in_specs=in_specs,
        out_specs=pl.BlockSpec((TM, qw), q_map),
        out_shape=jax.ShapeDtypeStruct((batch * s_len, BRANCH_WIDTH), BF16),
        scratch_shapes=[pltpu.VMEM((rows, LANES), BF16),
                        pltpu.VMEM((rows, LANES), F32),
                        pltpu.VMEM((rows, LANES), F32),
                        pltpu.VMEM((rows, LANES), F32)],
        compiler_params=_cparams(3),
        name="diff_attn" if diff else "gqa_attn",
    )(*args)


def _s5_kernel(u_ref, wd_ref, a_ref, cm_ref, y_ref, bu_sc, st_sc):
    n_tiles = BRANCH_WIDTH // LANES
    half = SUBLANES * S5_STATE
    rows = TC * SUBLANES

    @pl.when(pl.program_id(0) == 0)
    def _():
        st_sc[...] = jnp.zeros_like(st_sc)

    seq = lax.broadcasted_iota(jnp.int32, (rows, LANES), 0) & (SUBLANES - 1)
    fwd = seq < SUBLANES // 2
    for j in range(n_tiles):
        uj = u_ref[:, j * LANES:(j + 1) * LANES]
        lhs = jnp.concatenate([jnp.where(fwd, uj, 0.0), jnp.where(fwd, 0.0, uj)], axis=1)
        bu_sc[:, 2 * half * j:2 * half * (j + 1)] = jnp.dot(
            lhs.astype(BF16), wd_ref[j], preferred_element_type=F32)

    for j in range(n_tiles):
        re = slice(2 * half * j, 2 * half * j + half)
        im = slice(2 * half * j + half, 2 * half * (j + 1))
        ar = a_ref[0, :, half * j:half * (j + 1)]
        ai = a_ref[1, :, half * j:half * (j + 1)]

        def step(t, carry, re=re, im=im, ar=ar, ai=ai):
            xr, xi = carry
            r = pl.multiple_of(t * SUBLANES, SUBLANES)
            nxr = ar * xr - ai * xi + bu_sc[pl.ds(r, SUBLANES), re]
            nxi = ar * xi + ai * xr + bu_sc[pl.ds(r, SUBLANES), im]
            bu_sc[pl.ds(r, SUBLANES), re] = nxr
            bu_sc[pl.ds(r, SUBLANES), im] = nxi
            return nxr, nxi

        xr, xi = lax.fori_loop(0, TC, step, (st_sc[:, re], st_sc[:, im]), unroll=4)
        st_sc[:, re] = xr
        st_sc[:, im] = xi

    for j in range(n_tiles):
        x = bu_sc[:, 2 * half * j:2 * half * (j + 1)].astype(BF16)
        yy = jnp.dot(x, cm_ref[j], preferred_element_type=F32)
        y_ref[:, j * LANES:(j + 1) * LANES] = jnp.where(fwd, yy[:, :LANES], yy[:, LANES:])


def _s5_scan(u8, wd, a8, cm):
    rows_total = u8.shape[0]
    rows = TC * SUBLANES
    n_tiles = BRANCH_WIDTH // LANES
    n_state = 2 * SUBLANES * S5_STATE * n_tiles
    return pl.pallas_call(
        _s5_kernel,
        grid=(rows_total // rows,),
        in_specs=[pl.BlockSpec((rows, BRANCH_WIDTH), lambda t: (t, 0)),
                  pl.BlockSpec(wd.shape, lambda t: (0, 0, 0)),
                  pl.BlockSpec(a8.shape, lambda t: (0, 0, 0)),
                  pl.BlockSpec(cm.shape, lambda t: (0, 0, 0))],
        out_specs=pl.BlockSpec((rows, BRANCH_WIDTH), lambda t: (t, 0)),
        out_shape=jax.ShapeDtypeStruct((rows_total, BRANCH_WIDTH), F32),
        scratch_shapes=[pltpu.VMEM((rows, n_state), F32),
                        pltpu.VMEM((SUBLANES, n_state), F32)],
        compiler_params=_cparams(1),
        name="s5_scan",
    )(u8, wd, a8, cm)


def _zoh(lam_re, lam_im, log_dt, b_re, b_im):
    dt = jnp.exp(log_dt)[..., None]
    mag = jnp.exp(lam_re * dt)
    a_re = mag * jnp.cos(lam_im * dt)
    a_im = mag * jnp.sin(lam_im * dt)
    den = lam_re * lam_re + lam_im * lam_im
    f_re = ((a_re - 1.0) * lam_re + a_im * lam_im) / den
    f_im = (a_im * lam_re - (a_re - 1.0) * lam_im) / den
    bb_re = f_re[..., None] * b_re - f_im[..., None] * b_im
    bb_im = f_re[..., None] * b_im + f_im[..., None] * b_re
    return a_re, a_im, bb_re, bb_im


def _s5_weights(lam_re, lam_im, log_dt, b_re, b_im, c_re, c_im, batch):
    n_tiles = BRANCH_WIDTH // LANES
    gpt = S5_GROUPS // n_tiles
    a_re, a_im, bb_re, bb_im = _zoh(lam_re, lam_im, log_dt, b_re, b_im)
    eye = jnp.eye(gpt, dtype=F32)

    def drive(bb):
        t = bb.reshape(2, n_tiles, gpt, S5_STATE, S5_CH)
        w = jnp.einsum('djgpc,gh->jdgchp', t, eye)
        return w.reshape(n_tiles, 2 * gpt * S5_CH, gpt * S5_STATE)

    def read(cc):
        t = cc.reshape(2, n_tiles, gpt, S5_CH, S5_STATE)
        w = jnp.einsum('djgcp,gh->jgpdhc', t, eye)
        return w.reshape(n_tiles, gpt * S5_STATE, 2 * gpt * S5_CH)

    wd = jnp.concatenate([drive(bb_re), drive(bb_im)], axis=2).astype(BF16)
    cm = jnp.concatenate([read(c_re), read(-c_im)], axis=1).astype(BF16)

    def per_seq(a):
        return jnp.repeat(a.reshape(2, 1, S5_GROUPS * S5_STATE), batch, axis=1).reshape(
            2 * batch, S5_GROUPS * S5_STATE)

    a8 = jnp.stack([per_seq(a_re), per_seq(a_im)])
    return wd, a8, cm


def _merge_kernel(x_ref, mod_ref, g1_ref, g2_ref, ya_ref, yd_ref, yf_ref, yr_ref, u_ref,
                  dsk_ref, wglu_ref, bglu_ref, wgate_ref, wbr_ref, wout_ref, x1_ref, h2_ref):
    mods = mod_ref[0]
    x = x_ref[...]
    h = _rms_mod(x, g1_ref[...], mods[0:1], mods[1:2]).astype(BF16)
    ys = yf_ref[...] + yr_ref[...] + dsk_ref[...] * u_ref[...]
    g = jax.nn.gelu(ys)
    yb = g * jax.nn.sigmoid(
        jnp.dot(g.astype(BF16), wglu_ref[...], preferred_element_type=F32) + bglu_ref[...])
    branches = (ya_ref[...], yb.astype(BF16), yd_ref[...])
    m = None
    for k, y in enumerate(branches):
        gate = jax.nn.sigmoid(jnp.dot(h, wgate_ref[:, k * D_MODEL:(k + 1) * D_MODEL],
                                      preferred_element_type=F32))
        term = gate * jnp.dot(y, wbr_ref[k], preferred_element_type=F32)
        m = term if m is None else m + term
    y = jnp.dot(m.astype(BF16), wout_ref[...], preferred_element_type=F32)
    x1 = x + mods[2:3] * y
    x1_ref[...] = x1
    h2_ref[...] = _rms_mod(x1, g2_ref[...], mods[3:4], mods[4:5]).astype(BF16)


def _merge(xa, mods, g1, g2, ya, yd, yf, yr, u, d_skip, w_glu, b_glu, w_gate, w_branch, w_out,
           batch, nblk):
    t_rows = xa.shape[0]
    row = lambda i: (i, 0)
    mod_map = lambda i: (jnp.where(i % nblk == 0, batch, i // nblk), 0, 0)
    c2 = lambda i: (0, 0)
    c3 = lambda i: (0, 0, 0)
    return pl.pallas_call(
        _merge_kernel,
        grid=(t_rows // TM,),
        in_specs=[pl.BlockSpec((TM, D_MODEL), row),
                  pl.BlockSpec((1, 6, D_MODEL), mod_map),
                  pl.BlockSpec((1, D_MODEL), c2),
                  pl.BlockSpec((1, D_MODEL), c2),
                  pl.BlockSpec((TM, BRANCH_WIDTH), row),
                  pl.BlockSpec((TM, BRANCH_WIDTH), row),
                  pl.BlockSpec((TM, BRANCH_WIDTH), row),
                  pl.BlockSpec((TM, BRANCH_WIDTH), row),
                  pl.BlockSpec((TM, BRANCH_WIDTH), row),
                  pl.BlockSpec((1, BRANCH_WIDTH), c2),
                  pl.BlockSpec((BRANCH_WIDTH, BRANCH_WIDTH), c2),
                  pl.BlockSpec((1, BRANCH_WIDTH), c2),
                  pl.BlockSpec((D_MODEL, 3 * D_MODEL), c2),
                  pl.BlockSpec((3, BRANCH_WIDTH, D_MODEL), c3),
                  pl.BlockSpec((D_MODEL, D_MODEL), c2)],
        out_specs=[pl.BlockSpec((TM, D_MODEL), row), pl.BlockSpec((TM, D_MODEL), row)],
        out_shape=[jax.ShapeDtypeStruct((t_rows, D_MODEL), F32),
                   jax.ShapeDtypeStruct((t_rows, D_MODEL), BF16)],
        compiler_params=_cparams(1),
        name="merge",
    )(xa, mods, g1, g2, ya, yd, yf, yr, u, d_skip, w_glu, b_glu, w_gate, w_branch, w_out)


def _ffn_kernel(h_ref, hp_ref, hn_ref, x1_ref, mod_ref, wup_ref, cw_ref, wdn_ref, o_ref,
                lhs_sc, ua_sc, ug_sc, acc_sc, *, nblk, n_chunks):
    pj = pl.program_id(0) % nblk
    left_ok = (pj >= 2).astype(F32)
    right_ok = jnp.logical_and(pj != 0, pj != nblk - 1).astype(F32)
    lhs_sc[0:HALO, :] = (hp_ref[...].astype(F32) * left_ok).astype(BF16)
    lhs_sc[HALO:HALO + TM, :] = h_ref[...]
    lhs_sc[HALO + TM:2 * HALO + TM, :] = (hn_ref[...].astype(F32) * right_ok).astype(BF16)
    acc_sc[...] = jnp.zeros_like(acc_sc)

    def conv(sc, cw, base):
        return (sc[HALO - 1:HALO - 1 + TM, :] * cw[base:base + 1]
                + sc[HALO:HALO + TM, :] * cw[base + 1:base + 2]
                + sc[HALO + 1:HALO + 1 + TM, :] * cw[base + 2:base + 3]
                + cw[base + 3:base + 4])

    def chunk(k, carry):
        lhs = lhs_sc[...]
        ua_sc[...] = jnp.dot(lhs, wup_ref[0, k], preferred_element_type=F32)
        ug_sc[...] = jnp.dot(lhs, wup_ref[1, k], preferred_element_type=F32)
        cw = cw_ref[k]
        act = jax.nn.silu(conv(ug_sc, cw, 4)) * conv(ua_sc, cw, 0)
        acc_sc[...] += jnp.dot(act.astype(BF16), wdn_ref[k], preferred_element_type=F32)
        return carry

    lax.fori_loop(0, n_chunks, chunk, 0)
    o_ref[...] = x1_ref[...] + mod_ref[0][5:6] * acc_sc[...]


def _ffn(h2, x1, mods, wup, cw, wdn, batch, nblk):
    t_rows = x1.shape[0]
    n_chunks = wdn.shape[0]
    per = TM // HALO
    last = t_rows // HALO - 1
    row = lambda i: (i, 0)
    mod_map = lambda i: (jnp.where(i % nblk == 0, batch, i // nblk), 0, 0)
    kern = functools.partial(_ffn_kernel, nblk=nblk, n_chunks=n_chunks)
    return pl.pallas_call(
        kern,
        grid=(t_rows // TM,),
        in_specs=[pl.BlockSpec((TM, D_MODEL), row),
                  pl.BlockSpec((HALO, D_MODEL), lambda i: (jnp.maximum(i * per - 1, 0), 0)),
                  pl.BlockSpec((HALO, D_MODEL), lambda i: (jnp.minimum((i + 1) * per, last), 0)),
                  pl.BlockSpec((TM, D_MODEL), row),
                  pl.BlockSpec((1, 6, D_MODEL), mod_map),
                  pl.BlockSpec(wup.shape, lambda i: (0, 0, 0, 0)),
                  pl.BlockSpec(cw.shape, lambda i: (0, 0, 0)),
                  pl.BlockSpec(wdn.shape, lambda i: (0, 0, 0))],
        out_specs=pl.BlockSpec((TM, D_MODEL), row),
        out_shape=jax.ShapeDtypeStruct((t_rows, D_MODEL), F32),
        scratch_shapes=[pltpu.VMEM((TM + 2 * HALO, D_MODEL), BF16),
                        pltpu.VMEM((TM + 2 * HALO, FF_CHUNK), F32),
                        pltpu.VMEM((TM + 2 * HALO, FF_CHUNK), F32),
                        pltpu.VMEM((TM, D_MODEL), F32)],
        compiler_params=_cparams(1),
        name="conv_ffn",
    )(h2, h2, h2, x1, mods, wup, cw, wdn)


def _rope_tables(seq, ctx):
    rows = seq // GRID_W
    row = jnp.repeat(jnp.arange(rows, dtype=F32), GRID_W)
    col = jnp.tile(jnp.arange(GRID_W, dtype=F32), rows)
    n_freq = HEAD_DIM // 4
    inv_freq = ROPE_BASE ** (-jnp.arange(n_freq, dtype=F32) / n_freq)
    ang = jnp.concatenate([row[:, None] * inv_freq, col[:, None] * inv_freq], axis=-1)
    ang = jnp.concatenate([ang, ang], axis=-1)
    cos = jnp.concatenate([jnp.ones((ctx, HEAD_DIM), F32), jnp.cos(ang)], axis=0)
    sin = jnp.concatenate([jnp.zeros((ctx, HEAD_DIM), F32), jnp.sin(ang)], axis=0)
    sign = jnp.where(jnp.arange(HEAD_DIM) < HEAD_DIM // 2, -1.0, 1.0).astype(F32)
    two = lambda t: jnp.concatenate([t, t], axis=1)
    return two(cos), two(sin * sign)


def kernel(x, c, ctx, c_ctx, w_ada, b_ada, norm_g, w_in, qk_gain, ssm_lam_re, ssm_lam_im, ssm_log_dt, ssm_b_re, ssm_b_im, ssm_c_re, ssm_c_im, ssm_d, w_glu, b_glu, diff_lam, diff_norm_g, w_branch, w_out, w_up, conv_w, conv_b, w_down):
    batch, seq, _ = x.shape
    n_ctx = ctx.shape[1]
    depth = w_in.shape[0]
    assert n_ctx == TM and seq % TM == 0 and 2 * batch == SUBLANES
    s_len = n_ctx + seq
    nblk = s_len // TM
    assert s_len % TC == 0 and D_FF % FF_CHUNK == 0
    n_chunks = D_FF // FF_CHUNK

    cos_t, sin_t = _rope_tables(seq, n_ctx)
    cc = jnp.concatenate([c, c_ctx[None, :], jnp.zeros((SUBLANES - batch - 1, D_MODEL), F32)], axis=0)
    mods_all = _ada_mods(cc, w_ada, b_ada).reshape(depth, SUBLANES, 6, D_MODEL)

    xa = jnp.concatenate([ctx, x], axis=1).reshape(batch * s_len, D_MODEL)
    for i in range(depth):
        lam_init = 0.8 - 0.6 * math.exp(-0.3 * i)
        mods = mods_all[i]
        g1 = norm_g[i, 0:1]
        g2 = norm_g[i, 1:2]
        w_i = w_in[i].astype(BF16)
        qa, kta, va, u, qc, ktc, vc = _inproj(xa, mods, g1, w_i[:, :N_QKVU],
                                              jnp.tile(qk_gain[i], (1, BRANCH_WIDTH // HEAD_DIM)),
                                              cos_t, sin_t, batch, nblk)
        ya = _attention(qa, kta, va, batch, nblk, diff=False)
        yd = _attention(qc, ktc, vc, batch, nblk, diff=True, lam_vecs=diff_lam[i],
                        g_out=diff_norm_g[i][None, :], lam_init=lam_init)

        wd, a8, cm = _s5_weights(ssm_lam_re[i], ssm_lam_im[i], ssm_log_dt[i], ssm_b_re[i],
                                 ssm_b_im[i], ssm_c_re[i], ssm_c_im[i], batch)
        u3 = u.reshape(batch, s_len, BRANCH_WIDTH)
        seg_flip = lambda t: jnp.concatenate(
            [jnp.flip(t[:, :n_ctx], axis=1), jnp.flip(t[:, n_ctx:], axis=1)], axis=1)
        u8 = jnp.concatenate([u3, seg_flip(u3)], axis=0)
        u8 = jnp.transpose(u8, (1, 0, 2)).reshape(s_len * SUBLANES, BRANCH_WIDTH)
        y8 = _s5_scan(u8, wd, a8, cm).reshape(s_len, SUBLANES, BRANCH_WIDTH)
        y8 = jnp.transpose(y8, (1, 0, 2))
        yf = y8[:batch].reshape(batch * s_len, BRANCH_WIDTH)
        yr = seg_flip(y8[batch:]).reshape(batch * s_len, BRANCH_WIDTH)

        x1, h2 = _merge(xa, mods, g1, g2, ya, yd, yf, yr, u, ssm_d[i][None, :],
                        w_glu[i].astype(BF16), b_glu[i][None, :], w_i[:, N_QKVU:],
                        w_branch[i].astype(BF16), w_out[i].astype(BF16), batch, nblk)

        wup = w_up[i].astype(BF16).reshape(D_MODEL, 2, n_chunks, FF_CHUNK).transpose(1, 2, 0, 3)
        cwb = jnp.concatenate([conv_w[i], conv_b[i][None, :]], axis=0)
        cw = cwb.reshape(4, 2, n_chunks, FF_CHUNK).transpose(2, 1, 0, 3).reshape(
            n_chunks, SUBLANES, FF_CHUNK)
        wdn = w_down[i].astype(BF16).reshape(n_chunks, FF_CHUNK, D_MODEL)
        xa = _ffn(h2, x1, mods, wup, cw, wdn, batch, nblk)

    return xa.reshape(batch, s_len, D_MODEL)[:, n_ctx:]
```

```python
import functools
import math

import jax
import jax.numpy as jnp
import numpy as np
from jax import lax
from jax.experimental import pallas as pl
from jax.experimental.pallas import tpu as pltpu

F32 = jnp.float32
BF16 = jnp.bfloat16

D_MODEL = 1024
HEAD_DIM = 64
GRID_W = 64
ROPE_BASE = 10000.0
EPS = 1e-6
BRANCH_WIDTH = D_MODEL // 2
A_HEADS = BRANCH_WIDTH // HEAD_DIM
A_KV_HEADS = A_HEADS // 4
C_HEADS = BRANCH_WIDTH // (2 * HEAD_DIM)
S5_CH = 16
S5_STATE = 64
S5_GROUPS = BRANCH_WIDTH // S5_CH
D_FF = 2816
IN_SIZES = (512, 128, 128, 512, 512, 512, 512, 3 * D_MODEL)
IN_OFFS = tuple(int(v) for v in np.cumsum((0,) + IN_SIZES))
N_QKVU = IN_OFFS[7]

LANES = 128
SUBLANES = 8
TM = 256
SCORE_ELEMS = 512 * 1024
TC = 64
FF_CHUNK = 256
HALO = 16
NEG_BIG = -1e30
VMEM_LIMIT = 56 * 1024 * 1024
SCORE_SCALE = HEAD_DIM ** -0.5 * math.log2(math.e)


def _cparams(n_axes):
    return pltpu.CompilerParams(dimension_semantics=("arbitrary",) * n_axes,
                                vmem_limit_bytes=VMEM_LIMIT)


def _rms_mod(x, gain, shift, scale):
    y = x * lax.rsqrt(jnp.mean(x * x, axis=-1, keepdims=True) + EPS)
    return (y * gain) * (1.0 + scale) + shift


def _head_avg_matrix(width):
    r = lax.broadcasted_iota(jnp.int32, (width, width), 0) >> 6
    c = lax.broadcasted_iota(jnp.int32, (width, width), 1) >> 6
    return jnp.where(r == c, 1.0 / HEAD_DIM, 0.0).astype(BF16)


def _head_rms(z, gain, avg):
    sq = z * z
    hi = sq.astype(BF16)
    lo = (sq - hi.astype(F32)).astype(BF16)
    ms = (jnp.dot(hi, avg, preferred_element_type=F32)
          + jnp.dot(lo, avg, preferred_element_type=F32))
    return z * lax.rsqrt(ms + EPS) * gain


def _tile_lanes(t, width):
    reps = width // t.shape[1]
    return t if reps == 1 else jnp.concatenate([t] * reps, axis=1)


def _rope(z, cos, sin_signed):
    width = z.shape[1]
    lane = lax.broadcasted_iota(jnp.int32, z.shape, 1)
    first_half = (lane & (HEAD_DIM - 1)) < HEAD_DIM // 2
    rot = jnp.where(first_half,
                    pltpu.roll(z, width - HEAD_DIM // 2, 1),
                    pltpu.roll(z, HEAD_DIM // 2, 1))
    return z * _tile_lanes(cos, width) + rot * _tile_lanes(sin_signed, width)


def _dup_halves(z):
    lane = lax.broadcasted_iota(jnp.int32, z.shape, 1)
    low = lane < HEAD_DIM
    sw = pltpu.roll(z, HEAD_DIM, 1)
    return jnp.where(low, z, sw), jnp.where(low, sw, z)


def _ada_kernel(c_ref, w_ref, b_ref, o_ref):
    a = jax.nn.silu(c_ref[...])
    o_ref[0] = jnp.dot(a, w_ref[0], preferred_element_type=F32,
                       precision=lax.Precision.HIGHEST) + b_ref[0]


def _ada_mods(cc, w_ada, b_ada):
    depth, _, n = w_ada.shape
    tn = 1536
    return pl.pallas_call(
        _ada_kernel,
        grid=(depth, n // tn),
        in_specs=[pl.BlockSpec((SUBLANES, D_MODEL), lambda l, j: (0, 0)),
                  pl.BlockSpec((1, D_MODEL, tn), lambda l, j: (l, 0, j)),
                  pl.BlockSpec((1, 1, tn), lambda l, j: (l, 0, j))],
        out_specs=pl.BlockSpec((1, SUBLANES, tn), lambda l, j: (l, 0, j)),
        out_shape=jax.ShapeDtypeStruct((depth, SUBLANES, n), F32),
        compiler_params=_cparams(2),
        name="ada_mods",
    )(cc, w_ada, b_ada.reshape(depth, 1, n))


def _inproj_kernel(x_ref, mod_ref, g_ref, w_ref, qk_ref, cos_ref, sin_ref,
                   qa_ref, kta_ref, va_ref, u_ref, qc_ref, ktc_ref, vc_ref):
    mods = mod_ref[0]
    h = _rms_mod(x_ref[...], g_ref[...], mods[0:1], mods[1:2]).astype(BF16)
    cos = cos_ref[...]
    sin = sin_ref[...]
    avg512 = _head_avg_matrix(512)

    def seg(k):
        return jnp.dot(h, w_ref[:, IN_OFFS[k]:IN_OFFS[k + 1]], preferred_element_type=F32)

    def gain(k, width):
        return qk_ref[k:k + 1, :width]

    q = _rope(_head_rms(seg(0), gain(0, 512), avg512), cos, sin)
    qa_ref[...] = (q * SCORE_SCALE).astype(BF16)
    k = _rope(_head_rms(seg(1), gain(1, 128), avg512[:128, :128]), cos, sin)
    for hh, kd in enumerate(_dup_halves(k)):
        kta_ref[0, hh] = kd.T.astype(BF16)
    for hh, vd in enumerate(_dup_halves(seg(2))):
        va_ref[0, hh] = vd.astype(BF16)
    u_ref[...] = seg(3)
    q = _rope(_head_rms(seg(4), gain(2, 512), avg512), cos, sin)
    qc_ref[...] = (q * SCORE_SCALE).astype(BF16)
    k = _rope(_head_rms(seg(5), gain(3, 512), avg512), cos, sin)
    v = seg(6)
    for hh in range(C_HEADS):
        ktc_ref[0, hh] = k[:, hh * LANES:(hh + 1) * LANES].T.astype(BF16)
        vc_ref[0, hh] = v[:, hh * LANES:(hh + 1) * LANES].astype(BF16)


def _inproj(xa, mods, norm_g, w_qkvu, qk_gain, cos_t, sin_t, batch, nblk):
    t_rows = xa.shape[0]
    s_len = nblk * TM
    row = lambda i: (i, 0)
    mod_map = lambda i: (jnp.where(i % nblk == 0, batch, i // nblk), 0, 0)
    pos = lambda i: (i % nblk, 0)
    kt_map = lambda i: (i // nblk, 0, 0, i % nblk)
    v_map = lambda i: (i // nblk, 0, i % nblk, 0)
    return pl.pallas_call(
        _inproj_kernel,
        grid=(t_rows // TM,),
        in_specs=[pl.BlockSpec((TM, D_MODEL), row),
                  pl.BlockSpec((1, 6, D_MODEL), mod_map),
                  pl.BlockSpec((1, D_MODEL), lambda i: (0, 0)),
                  pl.BlockSpec((D_MODEL, N_QKVU), lambda i: (0, 0)),
                  pl.BlockSpec((4, BRANCH_WIDTH), lambda i: (0, 0)),
                  pl.BlockSpec((TM, LANES), pos),
                  pl.BlockSpec((TM, LANES), pos)],
        out_specs=[pl.BlockSpec((TM, 512), row),
                   pl.BlockSpec((1, A_KV_HEADS, LANES, TM), kt_map),
                   pl.BlockSpec((1, A_KV_HEADS, TM, LANES), v_map),
                   pl.BlockSpec((TM, 512), row),
                   pl.BlockSpec((TM, 512), row),
                   pl.BlockSpec((1, C_HEADS, LANES, TM), kt_map),
                   pl.BlockSpec((1, C_HEADS, TM, LANES), v_map)],
        out_shape=[jax.ShapeDtypeStruct((t_rows, 512), BF16),
                   jax.ShapeDtypeStruct((batch, A_KV_HEADS, LANES, s_len), BF16),
                   jax.ShapeDtypeStruct((batch, A_KV_HEADS, s_len, LANES), BF16),
                   jax.ShapeDtypeStruct((t_rows, 512), F32),
                   jax.ShapeDtypeStruct((t_rows, 512), BF16),
                   jax.ShapeDtypeStruct((batch, C_HEADS, LANES, s_len), BF16),
                   jax.ShapeDtypeStruct((batch, C_HEADS, s_len, LANES), BF16)],
        compiler_params=_cparams(1),
        name="in_proj",
    )(xa, mods, norm_g, w_qkvu, qk_gain, cos_t, sin_t)


def _attn_kernel(*refs, n_pairs, diff, lam_init, n_ctx, tkl, n_lat):
    if diff:
        (q_ref, kt_ref, v_ref, lam_ref, gout_ref, o_ref,
         lhs_sc, m_sc, l_sc, acc_sc, s0_sc, s1_sc) = refs
    else:
        q_ref, kt_ref, v_ref, o_ref, lhs_sc, m_sc, l_sc, acc_sc, s0_sc, s1_sc = refs
    j = pl.program_id(2)
    rows = 2 * n_pairs * TM
    lane = lax.broadcasted_iota(jnp.int32, (TM, LANES), 1)
    low = lane < HEAD_DIM
    for p in range(n_pairs):
        qt = q_ref[:, p * LANES:(p + 1) * LANES].astype(F32)
        lhs_sc[(2 * p) * TM:(2 * p + 1) * TM, :] = jnp.where(low, qt, 0.0).astype(BF16)
        lhs_sc[(2 * p + 1) * TM:(2 * p + 2) * TM, :] = jnp.where(low, 0.0, qt).astype(BF16)
    m_sc[...] = jnp.full((rows, LANES), NEG_BIG, F32)
    l_sc[...] = jnp.zeros((rows, LANES), F32)
    acc_sc[...] = jnp.zeros((rows, LANES), F32)

    def lat_off(c):
        return pl.multiple_of(n_ctx + c * tkl, TM)

    def scores(c):
        return jnp.dot(lhs_sc[...], kt_ref[0, 0, :, pl.ds(lat_off(c), tkl)],
                       preferred_element_type=F32)

    def softmax_pv(s, v):
        width = s.shape[1]
        mx = s[:, 0:LANES]
        for t in range(1, width // LANES):
            mx = jnp.maximum(mx, s[:, t * LANES:(t + 1) * LANES])
        m_prev = m_sc[...]
        m_next = jnp.maximum(m_prev, jnp.max(mx, axis=1, keepdims=True))
        alpha = jnp.exp2(m_prev - m_next)
        p = jnp.exp2(s - _tile_lanes(m_next, width))
        psum = p[:, 0:LANES]
        for t in range(1, width // LANES):
            psum = psum + p[:, t * LANES:(t + 1) * LANES]
        l_sc[...] = alpha * l_sc[...] + psum
        acc_sc[...] = alpha * acc_sc[...] + jnp.dot(p.astype(BF16), v,
                                                    preferred_element_type=F32)
        m_sc[...] = m_next

    def ctx_chunk():
        s = jnp.dot(lhs_sc[...], kt_ref[0, 0, :, 0:n_ctx], preferred_element_type=F32)
        softmax_pv(s, v_ref[0, 0, 0:n_ctx, :])

    def lat_chunk(s_sc, c):
        softmax_pv(s_sc[...], v_ref[0, 0, pl.ds(lat_off(c), tkl), :])

    @pl.when(j == 0)
    def _():
        ctx_chunk()

    @pl.when(j > 0)
    def _():
        s0_sc[...] = scores(0)
        ctx_chunk()
        n_loop = (n_lat - 1) // 2

        def pair(i, carry):
            c0 = 2 * i
            s1_sc[...] = scores(c0 + 1)
            lat_chunk(s0_sc, c0)
            s0_sc[...] = scores(c0 + 2)
            lat_chunk(s1_sc, c0 + 1)
            return carry

        lax.fori_loop(0, n_loop, pair, 0)
        if n_lat - 2 * n_loop == 2:
            s1_sc[...] = scores(n_lat - 1)
            lat_chunk(s0_sc, n_lat - 2)
            lat_chunk(s1_sc, n_lat - 1)
        else:
            lat_chunk(s0_sc, n_lat - 1)

    o = acc_sc[...] / jnp.sum(l_sc[...], axis=1, keepdims=True)
    if diff:
        lv = lam_ref[...]
        lam = (jnp.exp(jnp.sum(lv[0:1] * lv[1:2], axis=1, keepdims=True))
               - jnp.exp(jnp.sum(lv[2:3] * lv[3:4], axis=1, keepdims=True)) + lam_init)
        od = o[0:TM] - lam * o[TM:2 * TM]
        od = od * lax.rsqrt(jnp.mean(od * od, axis=-1, keepdims=True) + EPS)
        o_ref[...] = ((od * gout_ref[...]) * (1.0 - lam_init)).astype(o_ref.dtype)
    else:
        for p in range(n_pairs):
            pair = jnp.where(low, o[(2 * p) * TM:(2 * p + 1) * TM],
                             o[(2 * p + 1) * TM:(2 * p + 2) * TM])
            o_ref[:, p * LANES:(p + 1) * LANES] = pair.astype(o_ref.dtype)


def _attention(q, kt, v, batch, nblk, *, diff, lam_vecs=None, g_out=None, lam_init=0.0):
    heads = kt.shape[1]
    s_len = kt.shape[3]
    n_pairs = 1 if diff else 2
    qw = n_pairs * LANES
    rows = 2 * n_pairs * TM
    q_map = lambda b, h, j: (b * nblk + j, h)
    kv_map = lambda b, h, j: (b, h, 0, 0)
    const = lambda b, h, j: (0, 0)
    in_specs = [pl.BlockSpec((TM, qw), q_map),
                pl.BlockSpec((1, 1, LANES, s_len), kv_map),
                pl.BlockSpec((1, 1, s_len, LANES), kv_map)]
    args = [q, kt, v]
    if diff:
        in_specs += [pl.BlockSpec((4, HEAD_DIM), const), pl.BlockSpec((1, LANES), const)]
        args += [lam_vecs, g_out]
    n_ctx = TM
    tkl = min(SCORE_ELEMS // rows, s_len - n_ctx)
    assert (s_len - n_ctx) % tkl == 0
    kern = functools.partial(_attn_kernel, n_pairs=n_pairs, diff=diff, lam_init=lam_init,
                             n_ctx=n_ctx, tkl=tkl, n_lat=(s_len - n_ctx) // tkl)
    return pl.pallas_call(
        kern,
        grid=(batch, heads, nblk),
        in_specs=in_specs,
        out_specs=pl.BlockSpec((TM, qw), q_map),
        out_shape=jax.ShapeDtypeStruct((batch * s_len, BRANCH_WIDTH), BF16),
        scratch_shapes=[pltpu.VMEM((rows, LANES), BF16),
                        pltpu.VMEM((rows, LANES), F32),
                        pltpu.VMEM((rows, LANES), F32),
                        pltpu.VMEM((rows, LANES), F32),
                        pltpu.VMEM((rows, tkl), F32),
                        pltpu.VMEM((rows, tkl), F32)],
        compiler_params=_cparams(3),
        name="diff_attn" if diff else "gqa_attn",
    )(*args)


def _s5_kernel(u_ref, wd_ref, a_ref, cm_ref, y_ref, bu_sc, st_sc):
    n_tiles = BRANCH_WIDTH // LANES
    half = SUBLANES * S5_STATE
    rows = TC * SUBLANES

    @pl.when(pl.program_id(0) == 0)
    def _():
        st_sc[...] = jnp.zeros_like(st_sc)

    seq = lax.broadcasted_iota(jnp.int32, (rows, LANES), 0) & (SUBLANES - 1)
    fwd = seq < SUBLANES // 2
    for j in range(n_tiles):
        uj = u_ref[:, j * LANES:(j + 1) * LANES]
        lhs = jnp.concatenate([jnp.where(fwd, uj, 0.0), jnp.where(fwd, 0.0, uj)], axis=1)
        bu_sc[:, 2 * half * j:2 * half * (j + 1)] = jnp.dot(
            lhs.astype(BF16), wd_ref[j], preferred_element_type=F32)

    for j in range(n_tiles):
        re = slice(2 * half * j, 2 * half * j + half)
        im = slice(2 * half * j + half, 2 * half * (j + 1))
        ar = a_ref[0, :, half * j:half * (j + 1)]
        ai = a_ref[1, :, half * j:half * (j + 1)]

        def step(t, carry, re=re, im=im, ar=ar, ai=ai):
            xr, xi = carry
            r = pl.multiple_of(t * SUBLANES, SUBLANES)
            nxr = ar * xr - ai * xi + bu_sc[pl.ds(r, SUBLANES), re]
            nxi = ar * xi + ai * xr + bu_sc[pl.ds(r, SUBLANES), im]
            bu_sc[pl.ds(r, SUBLANES), re] = nxr
            bu_sc[pl.ds(r, SUBLANES), im] = nxi
            return nxr, nxi

        xr, xi = lax.fori_loop(0, TC, step, (st_sc[:, re], st_sc[:, im]), unroll=4)
        st_sc[:, re] = xr
        st_sc[:, im] = xi

    for j in range(n_tiles):
        x = bu_sc[:, 2 * half * j:2 * half * (j + 1)].astype(BF16)
        yy = jnp.dot(x, cm_ref[j], preferred_element_type=F32)
        y_ref[:, j * LANES:(j + 1) * LANES] = jnp.where(fwd, yy[:, :LANES], yy[:, LANES:])


def _s5_scan(u8, wd, a8, cm):
    rows_total = u8.shape[0]
    rows = TC * SUBLANES
    n_tiles = BRANCH_WIDTH // LANES
    n_state = 2 * SUBLANES * S5_STATE * n_tiles
    return pl.pallas_call(
        _s5_kernel,
        grid=(rows_total // rows,),
        in_specs=[pl.BlockSpec((rows, BRANCH_WIDTH), lambda t: (t, 0)),
                  pl.BlockSpec(wd.shape, lambda t: (0, 0, 0)),
                  pl.BlockSpec(a8.shape, lambda t: (0, 0, 0)),
                  pl.BlockSpec(cm.shape, lambda t: (0, 0, 0))],
        out_specs=pl.BlockSpec((rows, BRANCH_WIDTH), lambda t: (t, 0)),
        out_shape=jax.ShapeDtypeStruct((rows_total, BRANCH_WIDTH), F32),
        scratch_shapes=[pltpu.VMEM((rows, n_state), F32),
                        pltpu.VMEM((SUBLANES, n_state), F32)],
        compiler_params=_cparams(1),
        name="s5_scan",
    )(u8, wd, a8, cm)


def _zoh(lam_re, lam_im, log_dt, b_re, b_im):
    dt = jnp.exp(log_dt)[..., None]
    mag = jnp.exp(lam_re * dt)
    a_re = mag * jnp.cos(lam_im * dt)
    a_im = mag * jnp.sin(lam_im * dt)
    den = lam_re * lam_re + lam_im * lam_im
    f_re = ((a_re - 1.0) * lam_re + a_im * lam_im) / den
    f_im = (a_im * lam_re - (a_re - 1.0) * lam_im) / den
    bb_re = f_re[..., None] * b_re - f_im[..., None] * b_im
    bb_im = f_re[..., None] * b_im + f_im[..., None] * b_re
    return a_re, a_im, bb_re, bb_im


def _s5_weights(lam_re, lam_im, log_dt, b_re, b_im, c_re, c_im, batch):
    n_tiles = BRANCH_WIDTH // LANES
    gpt = S5_GROUPS // n_tiles
    a_re, a_im, bb_re, bb_im = _zoh(lam_re, lam_im, log_dt, b_re, b_im)
    eye = jnp.eye(gpt, dtype=F32)

    def drive(bb):
        t = bb.reshape(2, n_tiles, gpt, S5_STATE, S5_CH)
        w = jnp.einsum('djgpc,gh->jdgchp', t, eye)
        return w.reshape(n_tiles, 2 * gpt * S5_CH, gpt * S5_STATE)

    def read(cc):
        t = cc.reshape(2, n_tiles, gpt, S5_CH, S5_STATE)
        w = jnp.einsum('djgcp,gh->jgpdhc', t, eye)
        return w.reshape(n_tiles, gpt * S5_STATE, 2 * gpt * S5_CH)

    wd = jnp.concatenate([drive(bb_re), drive(bb_im)], axis=2).astype(BF16)
    cm = jnp.concatenate([read(c_re), read(-c_im)], axis=1).astype(BF16)

    def per_seq(a):
        return jnp.repeat(a.reshape(2, 1, S5_GROUPS * S5_STATE), batch, axis=1).reshape(
            2 * batch, S5_GROUPS * S5_STATE)

    a8 = jnp.stack([per_seq(a_re), per_seq(a_im)])
    return wd, a8, cm


def _merge_kernel(x_ref, mod_ref, g1_ref, g2_ref, ya_ref, yd_ref, yf_ref, yr_ref, u_ref,
                  dsk_ref, wglu_ref, bglu_ref, wgate_ref, wbr_ref, wout_ref, x1_ref, h2_ref):
    mods = mod_ref[0]
    x = x_ref[...]
    h = _rms_mod(x, g1_ref[...], mods[0:1], mods[1:2]).astype(BF16)
    ys = yf_ref[...] + yr_ref[...] + dsk_ref[...] * u_ref[...]
    g = jax.nn.gelu(ys)
    yb = g * jax.nn.sigmoid(
        jnp.dot(g.astype(BF16), wglu_ref[...], preferred_element_type=F32) + bglu_ref[...])
    branches = (ya_ref[...], yb.astype(BF16), yd_ref[...])
    m = None
    for k, y in enumerate(branches):
        gate = jax.nn.sigmoid(jnp.dot(h, wgate_ref[:, k * D_MODEL:(k + 1) * D_MODEL],
                                      preferred_element_type=F32))
        term = gate * jnp.dot(y, wbr_ref[k], preferred_element_type=F32)
        m = term if m is None else m + term
    y = jnp.dot(m.astype(BF16), wout_ref[...], preferred_element_type=F32)
    x1 = x + mods[2:3] * y
    x1_ref[...] = x1
    h2_ref[...] = _rms_mod(x1, g2_ref[...], mods[3:4], mods[4:5]).astype(BF16)


def _merge(xa, mods, g1, g2, ya, yd, yf, yr, u, d_skip, w_glu, b_glu, w_gate, w_branch, w_out,
           batch, nblk):
    t_rows = xa.shape[0]
    row = lambda i: (i, 0)
    mod_map = lambda i: (jnp.where(i % nblk == 0, batch, i // nblk), 0, 0)
    c2 = lambda i: (0, 0)
    c3 = lambda i: (0, 0, 0)
    return pl.pallas_call(
        _merge_kernel,
        grid=(t_rows // TM,),
        in_specs=[pl.BlockSpec((TM, D_MODEL), row),
                  pl.BlockSpec((1, 6, D_MODEL), mod_map),
                  pl.BlockSpec((1, D_MODEL), c2),
                  pl.BlockSpec((1, D_MODEL), c2),
                  pl.BlockSpec((TM, BRANCH_WIDTH), row),
                  pl.BlockSpec((TM, BRANCH_WIDTH), row),
                  pl.BlockSpec((TM, BRANCH_WIDTH), row),
                  pl.BlockSpec((TM, BRANCH_WIDTH), row),
                  pl.BlockSpec((TM, BRANCH_WIDTH), row),
                  pl.BlockSpec((1, BRANCH_WIDTH), c2),
                  pl.BlockSpec((BRANCH_WIDTH, BRANCH_WIDTH), c2),
                  pl.BlockSpec((1, BRANCH_WIDTH), c2),
                  pl.BlockSpec((D_MODEL, 3 * D_MODEL), c2),
                  pl.BlockSpec((3, BRANCH_WIDTH, D_MODEL), c3),
                  pl.BlockSpec((D_MODEL, D_MODEL), c2)],
        out_specs=[pl.BlockSpec((TM, D_MODEL), row), pl.BlockSpec((TM, D_MODEL), row)],
        out_shape=[jax.ShapeDtypeStruct((t_rows, D_MODEL), F32),
                   jax.ShapeDtypeStruct((t_rows, D_MODEL), BF16)],
        compiler_params=_cparams(1),
        name="merge",
    )(xa, mods, g1, g2, ya, yd, yf, yr, u, d_skip, w_glu, b_glu, w_gate, w_branch, w_out)


def _ffn_kernel(h_ref, hp_ref, hn_ref, x1_ref, mod_ref, wup_ref, cw_ref, wdn_ref, o_ref,
                lhs_sc, ua_sc, ug_sc, acc_sc, *, nblk, n_chunks):
    pj = pl.program_id(0) % nblk
    left_ok = (pj >= 2).astype(F32)
    right_ok = jnp.logical_and(pj != 0, pj != nblk - 1).astype(F32)
    lhs_sc[0:HALO, :] = (hp_ref[...].astype(F32) * left_ok).astype(BF16)
    lhs_sc[HALO:HALO + TM, :] = h_ref[...]
    lhs_sc[HALO + TM:2 * HALO + TM, :] = (hn_ref[...].astype(F32) * right_ok).astype(BF16)
    acc_sc[...] = jnp.zeros_like(acc_sc)

    def conv(sc, cw, base):
        return (sc[HALO - 1:HALO - 1 + TM, :] * cw[base:base + 1]
                + sc[HALO:HALO + TM, :] * cw[base + 1:base + 2]
                + sc[HALO + 1:HALO + 1 + TM, :] * cw[base + 2:base + 3]
                + cw[base + 3:base + 4])

    def chunk(k, carry):
        lhs = lhs_sc[...]
        ua_sc[...] = jnp.dot(lhs, wup_ref[0, k], preferred_element_type=F32)
        ug_sc[...] = jnp.dot(lhs, wup_ref[1, k], preferred_element_type=F32)
        cw = cw_ref[k]
        act = jax.nn.silu(conv(ug_sc, cw, 4)) * conv(ua_sc, cw, 0)
        acc_sc[...] += jnp.dot(act.astype(BF16), wdn_ref[k], preferred_element_type=F32)
        return carry

    lax.fori_loop(0, n_chunks, chunk, 0)
    o_ref[...] = x1_ref[...] + mod_ref[0][5:6] * acc_sc[...]


def _ffn(h2, x1, mods, wup, cw, wdn, batch, nblk):
    t_rows = x1.shape[0]
    n_chunks = wdn.shape[0]
    per = TM // HALO
    last = t_rows // HALO - 1
    row = lambda i: (i, 0)
    mod_map = lambda i: (jnp.where(i % nblk == 0, batch, i // nblk), 0, 0)
    kern = functools.partial(_ffn_kernel, nblk=nblk, n_chunks=n_chunks)
    return pl.pallas_call(
        kern,
        grid=(t_rows // TM,),
        in_specs=[pl.BlockSpec((TM, D_MODEL), row),
                  pl.BlockSpec((HALO, D_MODEL), lambda i: (jnp.maximum(i * per - 1, 0), 0)),
                  pl.BlockSpec((HALO, D_MODEL), lambda i: (jnp.minimum((i + 1) * per, last), 0)),
                  pl.BlockSpec((TM, D_MODEL), row),
                  pl.BlockSpec((1, 6, D_MODEL), mod_map),
                  pl.BlockSpec(wup.shape, lambda i: (0, 0, 0, 0)),
                  pl.BlockSpec(cw.shape, lambda i: (0, 0, 0)),
                  pl.BlockSpec(wdn.shape, lambda i: (0, 0, 0))],
        out_specs=pl.BlockSpec((TM, D_MODEL), row),
        out_shape=jax.ShapeDtypeStruct((t_rows, D_MODEL), F32),
        scratch_shapes=[pltpu.VMEM((TM + 2 * HALO, D_MODEL), BF16),
                        pltpu.VMEM((TM + 2 * HALO, FF_CHUNK), F32),
                        pltpu.VMEM((TM + 2 * HALO, FF_CHUNK), F32),
                        pltpu.VMEM((TM, D_MODEL), F32)],
        compiler_params=_cparams(1),
        name="conv_ffn",
    )(h2, h2, h2, x1, mods, wup, cw, wdn)


def _rope_tables(seq, ctx):
    rows = seq // GRID_W
    row = jnp.repeat(jnp.arange(rows, dtype=F32), GRID_W)
    col = jnp.tile(jnp.arange(GRID_W, dtype=F32), rows)
    n_freq = HEAD_DIM // 4
    inv_freq = ROPE_BASE ** (-jnp.arange(n_freq, dtype=F32) / n_freq)
    ang = jnp.concatenate([row[:, None] * inv_freq, col[:, None] * inv_freq], axis=-1)
    ang = jnp.concatenate([ang, ang], axis=-1)
    cos = jnp.concatenate([jnp.ones((ctx, HEAD_DIM), F32), jnp.cos(ang)], axis=0)
    sin = jnp.concatenate([jnp.zeros((ctx, HEAD_DIM), F32), jnp.sin(ang)], axis=0)
    sign = jnp.where(jnp.arange(HEAD_DIM) < HEAD_DIM // 2, -1.0, 1.0).astype(F32)
    two = lambda t: jnp.concatenate([t, t], axis=1)
    return two(cos), two(sin * sign)


def kernel(x, c, ctx, c_ctx, w_ada, b_ada, norm_g, w_in, qk_gain, ssm_lam_re, ssm_lam_im, ssm_log_dt, ssm_b_re, ssm_b_im, ssm_c_re, ssm_c_im, ssm_d, w_glu, b_glu, diff_lam, diff_norm_g, w_branch, w_out, w_up, conv_w, conv_b, w_down):
    batch, seq, _ = x.shape
    n_ctx = ctx.shape[1]
    depth = w_in.shape[0]
    assert n_ctx == TM and seq % TM == 0 and 2 * batch == SUBLANES
    s_len = n_ctx + seq
    nblk = s_len // TM
    assert s_len % TC == 0 and D_FF % FF_CHUNK == 0
    n_chunks = D_FF // FF_CHUNK

    cos_t, sin_t = _rope_tables(seq, n_ctx)
    cc = jnp.concatenate([c, c_ctx[None, :], jnp.zeros((SUBLANES - batch - 1, D_MODEL), F32)], axis=0)
    mods_all = _ada_mods(cc, w_ada, b_ada).reshape(depth, SUBLANES, 6, D_MODEL)

    xa = jnp.concatenate([ctx, x], axis=1).reshape(batch * s_len, D_MODEL)
    for i in range(depth):
        lam_init = 0.8 - 0.6 * math.exp(-0.3 * i)
        mods = mods_all[i]
        g1 = norm_g[i, 0:1]
        g2 = norm_g[i, 1:2]
        w_i = w_in[i].astype(BF16)
        qa, kta, va, u, qc, ktc, vc = _inproj(xa, mods, g1, w_i[:, :N_QKVU],
                                              jnp.tile(qk_gain[i], (1, BRANCH_WIDTH // HEAD_DIM)),
                                              cos_t, sin_t, batch, nblk)
        ya = _attention(qa, kta, va, batch, nblk, diff=False)
        yd = _attention(qc, ktc, vc, batch, nblk, diff=True, lam_vecs=diff_lam[i],
                        g_out=diff_norm_g[i][None, :], lam_init=lam_init)

        wd, a8, cm = _s5_weights(ssm_lam_re[i], ssm_lam_im[i], ssm_log_dt[i], ssm_b_re[i],
                                 ssm_b_im[i], ssm_c_re[i], ssm_c_im[i], batch)
        u3 = u.reshape(batch, s_len, BRANCH_WIDTH)
        seg_flip = lambda t: jnp.concatenate(
            [jnp.flip(t[:, :n_ctx], axis=1), jnp.flip(t[:, n_ctx:], axis=1)], axis=1)
        u8 = jnp.concatenate([u3, seg_flip(u3)], axis=0)
        u8 = jnp.transpose(u8, (1, 0, 2)).reshape(s_len * SUBLANES, BRANCH_WIDTH)
        y8 = _s5_scan(u8, wd, a8, cm).reshape(s_len, SUBLANES, BRANCH_WIDTH)
        y8 = jnp.transpose(y8, (1, 0, 2))
        yf = y8[:batch].reshape(batch * s_len, BRANCH_WIDTH)
        yr = seg_flip(y8[batch:]).reshape(batch * s_len, BRANCH_WIDTH)

        x1, h2 = _merge(xa, mods, g1, g2, ya, yd, yf, yr, u, ssm_d[i][None, :],
                        w_glu[i].astype(BF16), b_glu[i][None, :], w_i[:, N_QKVU:],
                        w_branch[i].astype(BF16), w_out[i].astype(BF16), batch, nblk)

        wup = w_up[i].astype(BF16).reshape(D_MODEL, 2, n_chunks, FF_CHUNK).transpose(1, 2, 0, 3)
        cwb = jnp.concatenate([conv_w[i], conv_b[i][None, :]], axis=0)
        cw = cwb.reshape(4, 2, n_chunks, FF_CHUNK).transpose(2, 1, 0, 3).reshape(
            n_chunks, SUBLANES, FF_CHUNK)
        wdn = w_down[i].astype(BF16).reshape(n_chunks, FF_CHUNK, D_MODEL)
        xa = _ffn(h2, x1, mods, wup, cw, wdn, batch, nblk)

    return xa.reshape(batch, s_len, D_MODEL)[:, n_ctx:]
```

```python
import functools
import math

import jax
import jax.numpy as jnp
import numpy as np
from jax import lax
from jax.experimental import pallas as pl
from jax.experimental.pallas import tpu as pltpu

F32 = jnp.float32
BF16 = jnp.bfloat16

D_MODEL = 1024
HEAD_DIM = 64
GRID_W = 64
ROPE_BASE = 10000.0
EPS = 1e-6
BRANCH_WIDTH = D_MODEL // 2
A_HEADS = BRANCH_WIDTH // HEAD_DIM
A_KV_HEADS = A_HEADS // 4
C_HEADS = BRANCH_WIDTH // (2 * HEAD_DIM)
S5_CH = 16
S5_STATE = 64
S5_GROUPS = BRANCH_WIDTH // S5_CH
D_FF = 2816
IN_SIZES = (512, 128, 128, 512, 512, 512, 512, 3 * D_MODEL)
IN_OFFS = tuple(int(v) for v in np.cumsum((0,) + IN_SIZES))
N_QKVU = IN_OFFS[7]

LANES = 128
SUBLANES = 8
TM = 256
SCORE_ELEMS = 512 * 1024
TC = 64
FF_CHUNK = 256
HALO = 16
NEG_BIG = -1e30
VMEM_LIMIT = 56 * 1024 * 1024
SCORE_SCALE = HEAD_DIM ** -0.5 * math.log2(math.e)


def _cparams(n_axes):
    return pltpu.CompilerParams(dimension_semantics=("arbitrary",) * n_axes,
                                vmem_limit_bytes=VMEM_LIMIT)


def _rms_mod(x, gain, shift, scale):
    y = x * lax.rsqrt(jnp.mean(x * x, axis=-1, keepdims=True) + EPS)
    return (y * gain) * (1.0 + scale) + shift


def _head_avg_matrix(width):
    r = lax.broadcasted_iota(jnp.int32, (width, width), 0) >> 6
    c = lax.broadcasted_iota(jnp.int32, (width, width), 1) >> 6
    return jnp.where(r == c, 1.0 / HEAD_DIM, 0.0).astype(BF16)


def _head_rms(z, gain, avg):
    sq = z * z
    hi = sq.astype(BF16)
    lo = (sq - hi.astype(F32)).astype(BF16)
    ms = (jnp.dot(hi, avg, preferred_element_type=F32)
          + jnp.dot(lo, avg, preferred_element_type=F32))
    return z * lax.rsqrt(ms + EPS) * gain


def _tile_lanes(t, width):
    reps = width // t.shape[1]
    return t if reps == 1 else jnp.concatenate([t] * reps, axis=1)


def _rope(z, cos, sin_signed):
    width = z.shape[1]
    lane = lax.broadcasted_iota(jnp.int32, z.shape, 1)
    first_half = (lane & (HEAD_DIM - 1)) < HEAD_DIM // 2
    rot = jnp.where(first_half,
                    pltpu.roll(z, width - HEAD_DIM // 2, 1),
                    pltpu.roll(z, HEAD_DIM // 2, 1))
    return z * _tile_lanes(cos, width) + rot * _tile_lanes(sin_signed, width)


def _dup_halves(z):
    lane = lax.broadcasted_iota(jnp.int32, z.shape, 1)
    low = lane < HEAD_DIM
    sw = pltpu.roll(z, HEAD_DIM, 1)
    return jnp.where(low, z, sw), jnp.where(low, sw, z)


def _ada_kernel(c_ref, w_ref, b_ref, o_ref):
    a = jax.nn.silu(c_ref[...])
    o_ref[0] = jnp.dot(a, w_ref[0], preferred_element_type=F32,
                       precision=lax.Precision.HIGHEST) + b_ref[0]


def _ada_mods(cc, w_ada, b_ada):
    depth, _, n = w_ada.shape
    tn = 1536
    return pl.pallas_call(
        _ada_kernel,
        grid=(depth, n // tn),
        in_specs=[pl.BlockSpec((SUBLANES, D_MODEL), lambda l, j: (0, 0)),
                  pl.BlockSpec((1, D_MODEL, tn), lambda l, j: (l, 0, j)),
                  pl.BlockSpec((1, 1, tn), lambda l, j: (l, 0, j))],
        out_specs=pl.BlockSpec((1, SUBLANES, tn), lambda l, j: (l, 0, j)),
        out_shape=jax.ShapeDtypeStruct((depth, SUBLANES, n), F32),
        compiler_params=_cparams(2),
        name="ada_mods",
    )(cc, w_ada, b_ada.reshape(depth, 1, n))


def _inproj_kernel(x_ref, mod_ref, g_ref, w_ref, qk_ref, cos_ref, sin_ref,
                   qa_ref, kta_ref, va_ref, u_ref, qc_ref, ktc_ref, vc_ref):
    mods = mod_ref[0]
    h = _rms_mod(x_ref[...], g_ref[...], mods[0:1], mods[1:2]).astype(BF16)
    cos = cos_ref[...]
    sin = sin_ref[...]
    avg512 = _head_avg_matrix(512)

    def seg(k):
        return jnp.dot(h, w_ref[:, IN_OFFS[k]:IN_OFFS[k + 1]], preferred_element_type=F32)

    def gain(k, width):
        return qk_ref[k:k + 1, :width]

    q = _rope(_head_rms(seg(0), gain(0, 512), avg512), cos, sin)
    qa_ref[...] = (q * SCORE_SCALE).astype(BF16)
    k = _rope(_head_rms(seg(1), gain(1, 128), avg512[:128, :128]), cos, sin)
    for hh, kd in enumerate(_dup_halves(k)):
        kta_ref[0, hh] = kd.T.astype(BF16)
    for hh, vd in enumerate(_dup_halves(seg(2))):
        va_ref[0, hh] = vd.astype(BF16)
    u_ref[...] = seg(3)
    q = _rope(_head_rms(seg(4), gain(2, 512), avg512), cos, sin)
    qc_ref[...] = (q * SCORE_SCALE).astype(BF16)
    k = _rope(_head_rms(seg(5), gain(3, 512), avg512), cos, sin)
    v = seg(6)
    for hh in range(C_HEADS):
        ktc_ref[0, hh] = k[:, hh * LANES:(hh + 1) * LANES].T.astype(BF16)
        vc_ref[0, hh] = v[:, hh * LANES:(hh + 1) * LANES].astype(BF16)


def _inproj(xa, mods, norm_g, w_qkvu, qk_gain, cos_t, sin_t, batch, nblk):
    t_rows = xa.shape[0]
    s_len = nblk * TM
    row = lambda i: (i, 0)
    mod_map = lambda i: (jnp.where(i % nblk == 0, batch, i // nblk), 0, 0)
    pos = lambda i: (i % nblk, 0)
    kt_map = lambda i: (i // nblk, 0, 0, i % nblk)
    v_map = lambda i: (i // nblk, 0, i % nblk, 0)
    return pl.pallas_call(
        _inproj_kernel,
        grid=(t_rows // TM,),
        in_specs=[pl.BlockSpec((TM, D_MODEL), row),
                  pl.BlockSpec((1, 6, D_MODEL), mod_map),
                  pl.BlockSpec((1, D_MODEL), lambda i: (0, 0)),
                  pl.BlockSpec((D_MODEL, N_QKVU), lambda i: (0, 0)),
                  pl.BlockSpec((4, BRANCH_WIDTH), lambda i: (0, 0)),
                  pl.BlockSpec((TM, LANES), pos),
                  pl.BlockSpec((TM, LANES), pos)],
        out_specs=[pl.BlockSpec((TM, 512), row),
                   pl.BlockSpec((1, A_KV_HEADS, LANES, TM), kt_map),
                   pl.BlockSpec((1, A_KV_HEADS, TM, LANES), v_map),
                   pl.BlockSpec((TM, 512), row),
                   pl.BlockSpec((TM, 512), row),
                   pl.BlockSpec((1, C_HEADS, LANES, TM), kt_map),
                   pl.BlockSpec((1, C_HEADS, TM, LANES), v_map)],
        out_shape=[jax.ShapeDtypeStruct((t_rows, 512), BF16),
                   jax.ShapeDtypeStruct((batch, A_KV_HEADS, LANES, s_len), BF16),
                   jax.ShapeDtypeStruct((batch, A_KV_HEADS, s_len, LANES), BF16),
                   jax.ShapeDtypeStruct((t_rows, 512), F32),
                   jax.ShapeDtypeStruct((t_rows, 512), BF16),
                   jax.ShapeDtypeStruct((batch, C_HEADS, LANES, s_len), BF16),
                   jax.ShapeDtypeStruct((batch, C_HEADS, s_len, LANES), BF16)],
        compiler_params=_cparams(1),
        name="in_proj",
    )(xa, mods, norm_g, w_qkvu, qk_gain, cos_t, sin_t)


def _attn_kernel(*refs, diff, lam_init, n_ctx, tkl, n_lat):
    if diff:
        (q_ref, kt_ref, v_ref, lam_ref, gout_ref, o_ref,
         lhs_sc, m_sc, l_sc, acc_sc, s0_sc, s1_sc) = refs
    else:
        q_ref, kt_ref, v_ref, o_ref, lhs_sc, m_sc, l_sc, acc_sc, s0_sc, s1_sc = refs
    j = pl.program_id(2)
    rows = 2 * TM
    lane = lax.broadcasted_iota(jnp.int32, (TM, LANES), 1)
    low = lane < HEAD_DIM
    qt = q_ref[...].astype(F32)
    lhs_sc[0:TM, :] = jnp.where(low, qt, 0.0).astype(BF16)
    lhs_sc[TM:rows, :] = jnp.where(low, 0.0, qt).astype(BF16)
    m_sc[...] = jnp.full((rows, LANES), NEG_BIG, F32)
    l_sc[...] = jnp.zeros((rows, LANES), F32)
    acc_sc[...] = jnp.zeros((rows, LANES), F32)

    def lat_off(c):
        return pl.multiple_of(n_ctx + c * tkl, TM)

    def scores(c):
        return jnp.dot(lhs_sc[...], kt_ref[0, 0, :, pl.ds(lat_off(c), tkl)],
                       preferred_element_type=F32)

    def softmax_pv(s, v):
        width = s.shape[1]
        mx = s[:, 0:LANES]
        for t in range(1, width // LANES):
            mx = jnp.maximum(mx, s[:, t * LANES:(t + 1) * LANES])
        m_prev = m_sc[...]
        m_next = jnp.maximum(m_prev, jnp.max(mx, axis=1, keepdims=True))
        alpha = jnp.exp2(m_prev - m_next)
        p = jnp.exp2(s - _tile_lanes(m_next, width))
        psum = p[:, 0:LANES]
        for t in range(1, width // LANES):
            psum = psum + p[:, t * LANES:(t + 1) * LANES]
        l_sc[...] = alpha * l_sc[...] + psum
        acc_sc[...] = alpha * acc_sc[...] + jnp.dot(p.astype(BF16), v,
                                                    preferred_element_type=F32)
        m_sc[...] = m_next

    def ctx_chunk():
        s = jnp.dot(lhs_sc[...], kt_ref[0, 0, :, 0:n_ctx], preferred_element_type=F32)
        softmax_pv(s, v_ref[0, 0, 0:n_ctx, :])

    def lat_chunk(s_sc, c):
        softmax_pv(s_sc[...], v_ref[0, 0, pl.ds(lat_off(c), tkl), :])

    @pl.when(j == 0)
    def _():
        ctx_chunk()

    @pl.when(j > 0)
    def _():
        s0_sc[...] = scores(0)
        ctx_chunk()
        n_loop = (n_lat - 1) // 2

        def pair(i, carry):
            c0 = 2 * i
            s1_sc[...] = scores(c0 + 1)
            lat_chunk(s0_sc, c0)
            s0_sc[...] = scores(c0 + 2)
            lat_chunk(s1_sc, c0 + 1)
            return carry

        lax.fori_loop(0, n_loop, pair, 0)
        if n_lat - 2 * n_loop == 2:
            s1_sc[...] = scores(n_lat - 1)
            lat_chunk(s0_sc, n_lat - 2)
            lat_chunk(s1_sc, n_lat - 1)
        else:
            lat_chunk(s0_sc, n_lat - 1)

    o = acc_sc[...] / jnp.sum(l_sc[...], axis=1, keepdims=True)
    if diff:
        lv = lam_ref[...]
        lam = (jnp.exp(jnp.sum(lv[0:1] * lv[1:2], axis=1, keepdims=True))
               - jnp.exp(jnp.sum(lv[2:3] * lv[3:4], axis=1, keepdims=True)) + lam_init)
        od = o[0:TM] - lam * o[TM:rows]
        od = od * lax.rsqrt(jnp.mean(od * od, axis=-1, keepdims=True) + EPS)
        o_ref[...] = ((od * gout_ref[...]) * (1.0 - lam_init)).astype(o_ref.dtype)
    else:
        o_ref[...] = jnp.where(low, o[0:TM], o[TM:rows]).astype(o_ref.dtype)


def _attention(q, kt, v, batch, nblk, *, diff, lam_vecs=None, g_out=None, lam_init=0.0):
    kv_heads = kt.shape[1]
    s_len = kt.shape[3]
    tiles = BRANCH_WIDTH // LANES
    rows = 2 * TM
    q_map = lambda b, h, j: (b * nblk + j, h)
    kv_map = lambda b, h, j: (b, h * kv_heads // tiles, 0, 0)
    const = lambda b, h, j: (0, 0)
    in_specs = [pl.BlockSpec((TM, LANES), q_map),
                pl.BlockSpec((1, 1, LANES, s_len), kv_map),
                pl.BlockSpec((1, 1, s_len, LANES), kv_map)]
    args = [q, kt, v]
    if diff:
        in_specs += [pl.BlockSpec((4, HEAD_DIM), const), pl.BlockSpec((1, LANES), const)]
        args += [lam_vecs, g_out]
    n_ctx = TM
    tkl = min(SCORE_ELEMS // rows, s_len - n_ctx)
    assert (s_len - n_ctx) % tkl == 0 and tkl >= n_ctx
    kern = functools.partial(_attn_kernel, diff=diff, lam_init=lam_init,
                             n_ctx=n_ctx, tkl=tkl, n_lat=(s_len - n_ctx) // tkl)
    return pl.pallas_call(
        kern,
        grid=(batch, tiles, nblk),
        in_specs=in_specs,
        out_specs=pl.BlockSpec((TM, LANES), q_map),
        out_shape=jax.ShapeDtypeStruct((batch * s_len, BRANCH_WIDTH), BF16),
        scratch_shapes=[pltpu.VMEM((rows, LANES), BF16),
                        pltpu.VMEM((rows, LANES), F32),
                        pltpu.VMEM((rows, LANES), F32),
                        pltpu.VMEM((rows, LANES), F32),
                        pltpu.VMEM((rows, tkl), F32),
                        pltpu.VMEM((rows, tkl), F32)],
        compiler_params=_cparams(3),
        name="diff_attn" if diff else "gqa_attn",
    )(*args)


def _s5_kernel(uf_ref, ub_ref, pin_ref, pout_ref, wd_ref, a_ref, cm_ref, yf_ref, yb_ref,
               bu_sc, st_sc):
    n_tiles = BRANCH_WIDTH // LANES
    half = SUBLANES * S5_STATE
    rows = TC * SUBLANES
    tok = (SUBLANES // 2) * TC

    @pl.when(pl.program_id(0) == 0)
    def _():
        st_sc[...] = jnp.zeros_like(st_sc)

    seq = lax.broadcasted_iota(jnp.int32, (rows, LANES), 0) & (SUBLANES - 1)
    fwd = seq < SUBLANES // 2
    lhs_dir = []
    for d, ref in enumerate((uf_ref, ub_ref)):
        ud = ref[...].reshape(tok, BRANCH_WIDTH).astype(BF16)
        lhs_dir.append(jnp.dot(pin_ref[d], ud, preferred_element_type=F32).astype(BF16))
    for j in range(n_tiles):
        cs = slice(j * LANES, (j + 1) * LANES)
        lhs = jnp.concatenate([lhs_dir[0][:, cs], lhs_dir[1][:, cs]], axis=1)
        bu_sc[:, 2 * half * j:2 * half * (j + 1)] = jnp.dot(
            lhs, wd_ref[j], preferred_element_type=F32)

    for j in range(n_tiles):
        re = slice(2 * half * j, 2 * half * j + half)
        im = slice(2 * half * j + half, 2 * half * (j + 1))
        ar = a_ref[0, :, half * j:half * (j + 1)]
        ai = a_ref[1, :, half * j:half * (j + 1)]

        def step(t, carry, re=re, im=im, ar=ar, ai=ai):
            xr, xi = carry
            r = pl.multiple_of(t * SUBLANES, SUBLANES)
            nxr = ar * xr - ai * xi + bu_sc[pl.ds(r, SUBLANES), re]
            nxi = ar * xi + ai * xr + bu_sc[pl.ds(r, SUBLANES), im]
            bu_sc[pl.ds(r, SUBLANES), re] = nxr
            bu_sc[pl.ds(r, SUBLANES), im] = nxi
            return nxr, nxi

        xr, xi = lax.fori_loop(0, TC, step, (st_sc[:, re], st_sc[:, im]), unroll=4)
        st_sc[:, re] = xr
        st_sc[:, im] = xi

    ys = []
    for j in range(n_tiles):
        x = bu_sc[:, 2 * half * j:2 * half * (j + 1)].astype(BF16)
        yy = jnp.dot(x, cm_ref[j], preferred_element_type=F32)
        ys.append(jnp.where(fwd, yy[:, :LANES], yy[:, LANES:]))
    y = jnp.concatenate(ys, axis=1)
    parts = []
    rest = y
    for _ in range(3):
        part = rest.astype(BF16)
        parts.append(part)
        rest = rest - part.astype(F32)
    for d, ref in enumerate((yf_ref, yb_ref)):
        out = None
        for part in parts:
            term = jnp.dot(pout_ref[d], part, preferred_element_type=F32)
            out = term if out is None else out + term
        ref[...] = out.reshape(ref.shape)


def _scan_row_placement(n_seq):
    p = np.zeros((2, TC * 2 * n_seq, n_seq * TC), np.float32)
    for b in range(n_seq):
        for k in range(TC):
            p[0, 2 * n_seq * k + b, b * TC + k] = 1.0
            p[1, 2 * n_seq * (TC - 1 - k) + n_seq + b, b * TC + k] = 1.0
    return jnp.asarray(p, BF16), jnp.asarray(p.transpose(0, 2, 1), BF16)


def _s5_scan(u3, wd, a8, cm, n_ctx):
    batch, s_len, _ = u3.shape
    rows = TC * SUBLANES
    n_tiles = BRANCH_WIDTH // LANES
    n_state = 2 * SUBLANES * S5_STATE * n_tiles
    n_steps = s_len // TC
    ctx_steps = n_ctx // TC
    p_in, p_out = _scan_row_placement(batch)
    fwd_map = lambda g: (0, g, 0)
    bwd_map = lambda g: (0, jnp.where(g < ctx_steps, ctx_steps - 1 - g,
                                      n_steps + ctx_steps - 1 - g), 0)
    c3 = lambda g: (0, 0, 0)
    blk = (batch, TC, BRANCH_WIDTH)
    out = jax.ShapeDtypeStruct(u3.shape, F32)
    return pl.pallas_call(
        _s5_kernel,
        grid=(n_steps,),
        in_specs=[pl.BlockSpec(blk, fwd_map),
                  pl.BlockSpec(blk, bwd_map),
                  pl.BlockSpec(p_in.shape, c3),
                  pl.BlockSpec(p_out.shape, c3),
                  pl.BlockSpec(wd.shape, c3),
                  pl.BlockSpec(a8.shape, c3),
                  pl.BlockSpec(cm.shape, c3)],
        out_specs=[pl.BlockSpec(blk, fwd_map), pl.BlockSpec(blk, bwd_map)],
        out_shape=[out, out],
        scratch_shapes=[pltpu.VMEM((rows, n_state), F32),
                        pltpu.VMEM((SUBLANES, n_state), F32)],
        compiler_params=_cparams(1),
        name="s5_scan",
    )(u3, u3, p_in, p_out, wd, a8, cm)


def _zoh(lam_re, lam_im, log_dt, b_re, b_im):
    dt = jnp.exp(log_dt)[..., None]
    mag = jnp.exp(lam_re * dt)
    a_re = mag * jnp.cos(lam_im * dt)
    a_im = mag * jnp.sin(lam_im * dt)
    den = lam_re * lam_re + lam_im * lam_im
    f_re = ((a_re - 1.0) * lam_re + a_im * lam_im) / den
    f_im = (a_im * lam_re - (a_re - 1.0) * lam_im) / den
    bb_re = f_re[..., None] * b_re - f_im[..., None] * b_im
    bb_im = f_re[..., None] * b_im + f_im[..., None] * b_re
    return a_re, a_im, bb_re, bb_im


def _s5_weights(lam_re, lam_im, log_dt, b_re, b_im, c_re, c_im, batch):
    n_tiles = BRANCH_WIDTH // LANES
    gpt = S5_GROUPS // n_tiles
    a_re, a_im, bb_re, bb_im = _zoh(lam_re, lam_im, log_dt, b_re, b_im)
    eye = jnp.eye(gpt, dtype=F32)

    def drive(bb):
        t = bb.reshape(2, n_tiles, gpt, S5_STATE, S5_CH)
        w = jnp.einsum('djgpc,gh->jdgchp', t, eye)
        return w.reshape(n_tiles, 2 * gpt * S5_CH, gpt * S5_STATE)

    def read(cc):
        t = cc.reshape(2, n_tiles, gpt, S5_CH, S5_STATE)
        w = jnp.einsum('djgcp,gh->jgpdhc', t, eye)
        return w.reshape(n_tiles, gpt * S5_STATE, 2 * gpt * S5_CH)

    wd = jnp.concatenate([drive(bb_re), drive(bb_im)], axis=2).astype(BF16)
    cm = jnp.concatenate([read(c_re), read(-c_im)], axis=1).astype(BF16)

    def per_seq(a):
        return jnp.repeat(a.reshape(2, 1, S5_GROUPS * S5_STATE), batch, axis=1).reshape(
            2 * batch, S5_GROUPS * S5_STATE)

    a8 = jnp.stack([per_seq(a_re), per_seq(a_im)])
    return wd, a8, cm


def _merge_kernel(x_ref, mod_ref, g1_ref, g2_ref, ya_ref, yd_ref, yf_ref, yr_ref, u_ref,
                  dsk_ref, wglu_ref, bglu_ref, wgate_ref, wbr_ref, wout_ref, x1_ref, h2_ref):
    mods = mod_ref[0]
    x = x_ref[...]
    h = _rms_mod(x, g1_ref[...], mods[0:1], mods[1:2]).astype(BF16)
    ys = yf_ref[...] + yr_ref[...] + dsk_ref[...] * u_ref[...]
    g = jax.nn.gelu(ys)
    yb = g * jax.nn.sigmoid(
        jnp.dot(g.astype(BF16), wglu_ref[...], preferred_element_type=F32) + bglu_ref[...])
    branches = (ya_ref[...], yb.astype(BF16), yd_ref[...])
    m = None
    for k, y in enumerate(branches):
        gate = jax.nn.sigmoid(jnp.dot(h, wgate_ref[:, k * D_MODEL:(k + 1) * D_MODEL],
                                      preferred_element_type=F32))
        term = gate * jnp.dot(y, wbr_ref[k], preferred_element_type=F32)
        m = term if m is None else m + term
    y = jnp.dot(m.astype(BF16), wout_ref[...], preferred_element_type=F32)
    x1 = x + mods[2:3] * y
    x1_ref[...] = x1
    h2_ref[...] = _rms_mod(x1, g2_ref[...], mods[3:4], mods[4:5]).astype(BF16)


def _merge(xa, mods, g1, g2, ya, yd, yf, yr, u, d_skip, w_glu, b_glu, w_gate, w_branch, w_out,
           batch, nblk):
    t_rows = xa.shape[0]
    row = lambda i: (i, 0)
    mod_map = lambda i: (jnp.where(i % nblk == 0, batch, i // nblk), 0, 0)
    c2 = lambda i: (0, 0)
    c3 = lambda i: (0, 0, 0)
    return pl.pallas_call(
        _merge_kernel,
        grid=(t_rows // TM,),
        in_specs=[pl.BlockSpec((TM, D_MODEL), row),
                  pl.BlockSpec((1, 6, D_MODEL), mod_map),
                  pl.BlockSpec((1, D_MODEL), c2),
                  pl.BlockSpec((1, D_MODEL), c2),
                  pl.BlockSpec((TM, BRANCH_WIDTH), row),
                  pl.BlockSpec((TM, BRANCH_WIDTH), row),
                  pl.BlockSpec((TM, BRANCH_WIDTH), row),
                  pl.BlockSpec((TM, BRANCH_WIDTH), row),
                  pl.BlockSpec((TM, BRANCH_WIDTH), row),
                  pl.BlockSpec((1, BRANCH_WIDTH), c2),
                  pl.BlockSpec((BRANCH_WIDTH, BRANCH_WIDTH), c2),
                  pl.BlockSpec((1, BRANCH_WIDTH), c2),
                  pl.BlockSpec((D_MODEL, 3 * D_MODEL), c2),
                  pl.BlockSpec((3, BRANCH_WIDTH, D_MODEL), c3),
                  pl.BlockSpec((D_MODEL, D_MODEL), c2)],
        out_specs=[pl.BlockSpec((TM, D_MODEL), row), pl.BlockSpec((TM, D_MODEL), row)],
        out_shape=[jax.ShapeDtypeStruct((t_rows, D_MODEL), F32),
                   jax.ShapeDtypeStruct((t_rows, D_MODEL), BF16)],
        compiler_params=_cparams(1),
        name="merge",
    )(xa, mods, g1, g2, ya, yd, yf, yr, u, d_skip, w_glu, b_glu, w_gate, w_branch, w_out)


def _ffn_kernel(h_ref, hp_ref, hn_ref, x1_ref, mod_ref, wup_ref, cw_ref, wdn_ref, o_ref,
                lhs_sc, ua_sc, ug_sc, acc_sc, *, nblk, n_chunks):
    pj = pl.program_id(0) % nblk
    left_ok = (pj >= 2).astype(F32)
    right_ok = jnp.logical_and(pj != 0, pj != nblk - 1).astype(F32)
    lhs_sc[0:HALO, :] = (hp_ref[...].astype(F32) * left_ok).astype(BF16)
    lhs_sc[HALO:HALO + TM, :] = h_ref[...]
    lhs_sc[HALO + TM:2 * HALO + TM, :] = (hn_ref[...].astype(F32) * right_ok).astype(BF16)
    acc_sc[...] = jnp.zeros_like(acc_sc)

    def conv(sc, cw, base):
        return (sc[HALO - 1:HALO - 1 + TM, :] * cw[base:base + 1]
                + sc[HALO:HALO + TM, :] * cw[base + 1:base + 2]
                + sc[HALO + 1:HALO + 1 + TM, :] * cw[base + 2:base + 3]
                + cw[base + 3:base + 4])

    def chunk(k, carry):
        lhs = lhs_sc[...]
        ua_sc[...] = jnp.dot(lhs, wup_ref[0, k], preferred_element_type=F32)
        ug_sc[...] = jnp.dot(lhs, wup_ref[1, k], preferred_element_type=F32)
        cw = cw_ref[k]
        act = jax.nn.silu(conv(ug_sc, cw, 4)) * conv(ua_sc, cw, 0)
        acc_sc[...] += jnp.dot(act.astype(BF16), wdn_ref[k], preferred_element_type=F32)
        return carry

    lax.fori_loop(0, n_chunks, chunk, 0)
    o_ref[...] = x1_ref[...] + mod_ref[0][5:6] * acc_sc[...]


def _ffn(h2, x1, mods, wup, cw, wdn, batch, nblk):
    t_rows = x1.shape[0]
    n_chunks = wdn.shape[0]
    per = TM // HALO
    last = t_rows // HALO - 1
    row = lambda i: (i, 0)
    mod_map = lambda i: (jnp.where(i % nblk == 0, batch, i // nblk), 0, 0)
    kern = functools.partial(_ffn_kernel, nblk=nblk, n_chunks=n_chunks)
    return pl.pallas_call(
        kern,
        grid=(t_rows // TM,),
        in_specs=[pl.BlockSpec((TM, D_MODEL), row),
                  pl.BlockSpec((HALO, D_MODEL), lambda i: (jnp.maximum(i * per - 1, 0), 0)),
                  pl.BlockSpec((HALO, D_MODEL), lambda i: (jnp.minimum((i + 1) * per, last), 0)),
                  pl.BlockSpec((TM, D_MODEL), row),
                  pl.BlockSpec((1, 6, D_MODEL), mod_map),
                  pl.BlockSpec(wup.shape, lambda i: (0, 0, 0, 0)),
                  pl.BlockSpec(cw.shape, lambda i: (0, 0, 0)),
                  pl.BlockSpec(wdn.shape, lambda i: (0, 0, 0))],
        out_specs=pl.BlockSpec((TM, D_MODEL), row),
        out_shape=jax.ShapeDtypeStruct((t_rows, D_MODEL), F32),
        scratch_shapes=[pltpu.VMEM((TM + 2 * HALO, D_MODEL), BF16),
                        pltpu.VMEM((TM + 2 * HALO, FF_CHUNK), F32),
                        pltpu.VMEM((TM + 2 * HALO, FF_CHUNK), F32),
                        pltpu.VMEM((TM, D_MODEL), F32)],
        compiler_params=_cparams(1),
        name="conv_ffn",
    )(h2, h2, h2, x1, mods, wup, cw, wdn)


def _rope_tables(seq, ctx):
    rows = seq // GRID_W
    row = jnp.repeat(jnp.arange(rows, dtype=F32), GRID_W)
    col = jnp.tile(jnp.arange(GRID_W, dtype=F32), rows)
    n_freq = HEAD_DIM // 4
    inv_freq = ROPE_BASE ** (-jnp.arange(n_freq, dtype=F32) / n_freq)
    ang = jnp.concatenate([row[:, None] * inv_freq, col[:, None] * inv_freq], axis=-1)
    ang = jnp.concatenate([ang, ang], axis=-1)
    cos = jnp.concatenate([jnp.ones((ctx, HEAD_DIM), F32), jnp.cos(ang)], axis=0)
    sin = jnp.concatenate([jnp.zeros((ctx, HEAD_DIM), F32), jnp.sin(ang)], axis=0)
    sign = jnp.where(jnp.arange(HEAD_DIM) < HEAD_DIM // 2, -1.0, 1.0).astype(F32)
    two = lambda t: jnp.concatenate([t, t], axis=1)
    return two(cos), two(sin * sign)


def kernel(x, c, ctx, c_ctx, w_ada, b_ada, norm_g, w_in, qk_gain, ssm_lam_re, ssm_lam_im, ssm_log_dt, ssm_b_re, ssm_b_im, ssm_c_re, ssm_c_im, ssm_d, w_glu, b_glu, diff_lam, diff_norm_g, w_branch, w_out, w_up, conv_w, conv_b, w_down):
    batch, seq, _ = x.shape
    n_ctx = ctx.shape[1]
    depth = w_in.shape[0]
    assert n_ctx == TM and seq % TM == 0 and 2 * batch == SUBLANES
    s_len = n_ctx + seq
    nblk = s_len // TM
    assert s_len % TC == 0 and D_FF % FF_CHUNK == 0
    n_chunks = D_FF // FF_CHUNK

    cos_t, sin_t = _rope_tables(seq, n_ctx)
    cc = jnp.concatenate([c, c_ctx[None, :], jnp.zeros((SUBLANES - batch - 1, D_MODEL), F32)], axis=0)
    mods_all = _ada_mods(cc, w_ada, b_ada).reshape(depth, SUBLANES, 6, D_MODEL)

    xa = jnp.concatenate([ctx, x], axis=1).reshape(batch * s_len, D_MODEL)
    for i in range(depth):
        lam_init = 0.8 - 0.6 * math.exp(-0.3 * i)
        mods = mods_all[i]
        g1 = norm_g[i, 0:1]
        g2 = norm_g[i, 1:2]
        w_i = w_in[i].astype(BF16)
        qa, kta, va, u, qc, ktc, vc = _inproj(xa, mods, g1, w_i[:, :N_QKVU],
                                              jnp.tile(qk_gain[i], (1, BRANCH_WIDTH // HEAD_DIM)),
                                              cos_t, sin_t, batch, nblk)
        ya = _attention(qa, kta, va, batch, nblk, diff=False)
        yd = _attention(qc, ktc, vc, batch, nblk, diff=True, lam_vecs=diff_lam[i],
                        g_out=diff_norm_g[i][None, :], lam_init=lam_init)

        wd, a8, cm = _s5_weights(ssm_lam_re[i], ssm_lam_im[i], ssm_log_dt[i], ssm_b_re[i],
                                 ssm_b_im[i], ssm_c_re[i], ssm_c_im[i], batch)
        yf, yr = _s5_scan(u.reshape(batch, s_len, BRANCH_WIDTH), wd, a8, cm, n_ctx)
        yf = yf.reshape(batch * s_len, BRANCH_WIDTH)
        yr = yr.reshape(batch * s_len, BRANCH_WIDTH)

        x1, h2 = _merge(xa, mods, g1, g2, ya, yd, yf, yr, u, ssm_d[i][None, :],
                        w_glu[i].astype(BF16), b_glu[i][None, :], w_i[:, N_QKVU:],
                        w_branch[i].astype(BF16), w_out[i].astype(BF16), batch, nblk)

        wup = w_up[i].astype(BF16).reshape(D_MODEL, 2, n_chunks, FF_CHUNK).transpose(1, 2, 0, 3)
        cwb = jnp.concatenate([conv_w[i], conv_b[i][None, :]], axis=0)
        cw = cwb.reshape(4, 2, n_chunks, FF_CHUNK).transpose(2, 1, 0, 3).reshape(
            n_chunks, SUBLANES, FF_CHUNK)
        wdn = w_down[i].astype(BF16).reshape(n_chunks, FF_CHUNK, D_MODEL)
        xa = _ffn(h2, x1, mods, wup, cw, wdn, batch, nblk)

    return xa.reshape(batch, s_len, D_MODEL)[:, n_ctx:]
```

```python
import functools
import math

import jax
import jax.numpy as jnp
import numpy as np
from jax import lax
from jax.experimental import pallas as pl
from jax.experimental.pallas import tpu as pltpu

F32 = jnp.float32
BF16 = jnp.bfloat16

D_MODEL = 1024
HEAD_DIM = 64
GRID_W = 64
ROPE_BASE = 10000.0
EPS = 1e-6
BRANCH_WIDTH = D_MODEL // 2
A_HEADS = BRANCH_WIDTH // HEAD_DIM
A_KV_HEADS = A_HEADS // 4
C_HEADS = BRANCH_WIDTH // (2 * HEAD_DIM)
S5_CH = 16
S5_STATE = 64
S5_GROUPS = BRANCH_WIDTH // S5_CH
D_FF = 2816
IN_SIZES = (512, 128, 128, 512, 512, 512, 512, 3 * D_MODEL)
IN_OFFS = tuple(int(v) for v in np.cumsum((0,) + IN_SIZES))
N_QKVU = IN_OFFS[7]

LANES = 128
SUBLANES = 8
TM = 256
SCORE_ELEMS = 512 * 1024
TC = 64
FF_CHUNK = 256
HALO = 16
NEG_BIG = -1e30
VMEM_LIMIT = 56 * 1024 * 1024
SCORE_SCALE = HEAD_DIM ** -0.5 * math.log2(math.e)


def _cparams(n_axes):
    return pltpu.CompilerParams(dimension_semantics=("arbitrary",) * n_axes,
                                vmem_limit_bytes=VMEM_LIMIT)


def _rms_mod(x, gain, shift, scale):
    y = x * lax.rsqrt(jnp.mean(x * x, axis=-1, keepdims=True) + EPS)
    return (y * gain) * (1.0 + scale) + shift


def _head_avg_matrix(width):
    r = lax.broadcasted_iota(jnp.int32, (width, width), 0) >> 6
    c = lax.broadcasted_iota(jnp.int32, (width, width), 1) >> 6
    return jnp.where(r == c, 1.0 / HEAD_DIM, 0.0).astype(BF16)


def _head_rms(z, gain, avg):
    sq = z * z
    hi = sq.astype(BF16)
    lo = (sq - hi.astype(F32)).astype(BF16)
    ms = (jnp.dot(hi, avg, preferred_element_type=F32)
          + jnp.dot(lo, avg, preferred_element_type=F32))
    return z * lax.rsqrt(ms + EPS) * gain


def _tile_lanes(t, width):
    reps = width // t.shape[1]
    return t if reps == 1 else jnp.concatenate([t] * reps, axis=1)


def _rope(z, cos, sin_signed):
    width = z.shape[1]
    lane = lax.broadcasted_iota(jnp.int32, z.shape, 1)
    first_half = (lane & (HEAD_DIM - 1)) < HEAD_DIM // 2
    rot = jnp.where(first_half,
                    pltpu.roll(z, width - HEAD_DIM // 2, 1),
                    pltpu.roll(z, HEAD_DIM // 2, 1))
    return z * _tile_lanes(cos, width) + rot * _tile_lanes(sin_signed, width)


def _dup_halves(z):
    lane = lax.broadcasted_iota(jnp.int32, z.shape, 1)
    low = lane < HEAD_DIM
    sw = pltpu.roll(z, HEAD_DIM, 1)
    return jnp.where(low, z, sw), jnp.where(low, sw, z)


def _ada_kernel(c_ref, w_ref, b_ref, o_ref):
    a = jax.nn.silu(c_ref[...])
    o_ref[0] = jnp.dot(a, w_ref[0], preferred_element_type=F32,
                       precision=lax.Precision.HIGHEST) + b_ref[0]


def _ada_mods(cc, w_ada, b_ada):
    depth, _, n = w_ada.shape
    tn = 1536
    return pl.pallas_call(
        _ada_kernel,
        grid=(depth, n // tn),
        in_specs=[pl.BlockSpec((SUBLANES, D_MODEL), lambda l, j: (0, 0)),
                  pl.BlockSpec((1, D_MODEL, tn), lambda l, j: (l, 0, j)),
                  pl.BlockSpec((1, 1, tn), lambda l, j: (l, 0, j))],
        out_specs=pl.BlockSpec((1, SUBLANES, tn), lambda l, j: (l, 0, j)),
        out_shape=jax.ShapeDtypeStruct((depth, SUBLANES, n), F32),
        compiler_params=_cparams(2),
        name="ada_mods",
    )(cc, w_ada, b_ada.reshape(depth, 1, n))


def _inproj_kernel(x_ref, mod_ref, g_ref, w_ref, qk_ref, cos_ref, sin_ref,
                   qa_ref, kta_ref, va_ref, u_ref, qc_ref, ktc_ref, vc_ref):
    mods = mod_ref[0]
    h = _rms_mod(x_ref[...], g_ref[...], mods[0:1], mods[1:2]).astype(BF16)
    cos = cos_ref[...]
    sin = sin_ref[...]
    avg512 = _head_avg_matrix(512)

    def seg(k):
        return jnp.dot(h, w_ref[:, IN_OFFS[k]:IN_OFFS[k + 1]], preferred_element_type=F32)

    def gain(k, width):
        return qk_ref[k:k + 1, :width]

    q = _rope(_head_rms(seg(0), gain(0, 512), avg512), cos, sin)
    qa_ref[...] = (q * SCORE_SCALE).astype(BF16)
    k = _rope(_head_rms(seg(1), gain(1, 128), avg512[:128, :128]), cos, sin)
    for hh, kd in enumerate(_dup_halves(k)):
        kta_ref[0, hh] = kd.T.astype(BF16)
    for hh, vd in enumerate(_dup_halves(seg(2))):
        va_ref[0, hh] = vd.astype(BF16)
    u_ref[...] = seg(3)
    q = _rope(_head_rms(seg(4), gain(2, 512), avg512), cos, sin)
    qc_ref[...] = (q * SCORE_SCALE).astype(BF16)
    k = _rope(_head_rms(seg(5), gain(3, 512), avg512), cos, sin)
    v = seg(6)
    for hh in range(C_HEADS):
        ktc_ref[0, hh] = k[:, hh * LANES:(hh + 1) * LANES].T.astype(BF16)
        vc_ref[0, hh] = v[:, hh * LANES:(hh + 1) * LANES].astype(BF16)


def _inproj(xa, mods, norm_g, w_qkvu, qk_gain, cos_t, sin_t, batch, nblk):
    t_rows = xa.shape[0]
    s_len = nblk * TM
    row = lambda i: (i, 0)
    mod_map = lambda i: (jnp.where(i % nblk == 0, batch, i // nblk), 0, 0)
    pos = lambda i: (i % nblk, 0)
    kt_map = lambda i: (i // nblk, 0, 0, i % nblk)
    v_map = lambda i: (i // nblk, 0, i % nblk, 0)
    return pl.pallas_call(
        _inproj_kernel,
        grid=(t_rows // TM,),
        in_specs=[pl.BlockSpec((TM, D_MODEL), row),
                  pl.BlockSpec((1, 6, D_MODEL), mod_map),
                  pl.BlockSpec((1, D_MODEL), lambda i: (0, 0)),
                  pl.BlockSpec((D_MODEL, N_QKVU), lambda i: (0, 0)),
                  pl.BlockSpec((4, BRANCH_WIDTH), lambda i: (0, 0)),
                  pl.BlockSpec((TM, LANES), pos),
                  pl.BlockSpec((TM, LANES), pos)],
        out_specs=[pl.BlockSpec((TM, 512), row),
                   pl.BlockSpec((1, A_KV_HEADS, LANES, TM), kt_map),
                   pl.BlockSpec((1, A_KV_HEADS, TM, LANES), v_map),
                   pl.BlockSpec((TM, 512), row),
                   pl.BlockSpec((TM, 512), row),
                   pl.BlockSpec((1, C_HEADS, LANES, TM), kt_map),
                   pl.BlockSpec((1, C_HEADS, TM, LANES), v_map)],
        out_shape=[jax.ShapeDtypeStruct((t_rows, 512), BF16),
                   jax.ShapeDtypeStruct((batch, A_KV_HEADS, LANES, s_len), BF16),
                   jax.ShapeDtypeStruct((batch, A_KV_HEADS, s_len, LANES), BF16),
                   jax.ShapeDtypeStruct((t_rows, 512), F32),
                   jax.ShapeDtypeStruct((t_rows, 512), BF16),
                   jax.ShapeDtypeStruct((batch, C_HEADS, LANES, s_len), BF16),
                   jax.ShapeDtypeStruct((batch, C_HEADS, s_len, LANES), BF16)],
        compiler_params=_cparams(1),
        name="in_proj",
    )(xa, mods, norm_g, w_qkvu, qk_gain, cos_t, sin_t)


def _attn_kernel(*refs, diff, lam_init, n_ctx, tkl, n_lat):
    if diff:
        (q_ref, kt_ref, v_ref, lam_ref, gout_ref, o_ref,
         lhs_sc, m_sc, l_sc, acc_sc, s0_sc, s1_sc) = refs
    else:
        q_ref, kt_ref, v_ref, o_ref, lhs_sc, m_sc, l_sc, acc_sc, s0_sc, s1_sc = refs
    j = pl.program_id(2)
    rows = 2 * TM
    lane = lax.broadcasted_iota(jnp.int32, (TM, LANES), 1)
    low = lane < HEAD_DIM
    qt = q_ref[...].astype(F32)
    lhs_sc[0:TM, :] = jnp.where(low, qt, 0.0).astype(BF16)
    lhs_sc[TM:rows, :] = jnp.where(low, 0.0, qt).astype(BF16)
    m_sc[...] = jnp.full((rows, LANES), NEG_BIG, F32)
    l_sc[...] = jnp.zeros((rows, LANES), F32)
    acc_sc[...] = jnp.zeros((rows, LANES), F32)

    def lat_keys(c):
        return slice(n_ctx + c * tkl, n_ctx + (c + 1) * tkl)

    def scores(c):
        return jnp.dot(lhs_sc[...], kt_ref[0, 0, :, lat_keys(c)], preferred_element_type=F32)

    def softmax_pv(s, v):
        width = s.shape[1]
        mx = s[:, 0:LANES]
        for t in range(1, width // LANES):
            mx = jnp.maximum(mx, s[:, t * LANES:(t + 1) * LANES])
        m_prev = m_sc[...]
        m_next = jnp.maximum(m_prev, jnp.max(mx, axis=1, keepdims=True))
        alpha = jnp.exp2(m_prev - m_next)
        p = jnp.exp2(s - _tile_lanes(m_next, width))
        psum = p[:, 0:LANES]
        for t in range(1, width // LANES):
            psum = psum + p[:, t * LANES:(t + 1) * LANES]
        l_sc[...] = alpha * l_sc[...] + psum
        acc_sc[...] = alpha * acc_sc[...] + jnp.dot(p.astype(BF16), v,
                                                    preferred_element_type=F32)
        m_sc[...] = m_next

    def ctx_chunk():
        s = jnp.dot(lhs_sc[...], kt_ref[0, 0, :, 0:n_ctx], preferred_element_type=F32)
        softmax_pv(s, v_ref[0, 0, 0:n_ctx, :])

    def lat_chunk(s_sc, c):
        softmax_pv(s_sc[...], v_ref[0, 0, lat_keys(c), :])

    @pl.when(j == 0)
    def _():
        ctx_chunk()

    @pl.when(j > 0)
    def _():
        bufs = (s0_sc, s1_sc)
        s0_sc[...] = scores(0)
        ctx_chunk()
        for c in range(n_lat):
            if c + 1 < n_lat:
                bufs[(c + 1) % 2][...] = scores(c + 1)
            lat_chunk(bufs[c % 2], c)

    o = acc_sc[...] / jnp.sum(l_sc[...], axis=1, keepdims=True)
    if diff:
        lv = lam_ref[...]
        lam = (jnp.exp(jnp.sum(lv[0:1] * lv[1:2], axis=1, keepdims=True))
               - jnp.exp(jnp.sum(lv[2:3] * lv[3:4], axis=1, keepdims=True)) + lam_init)
        od = o[0:TM] - lam * o[TM:rows]
        od = od * lax.rsqrt(jnp.mean(od * od, axis=-1, keepdims=True) + EPS)
        o_ref[...] = ((od * gout_ref[...]) * (1.0 - lam_init)).astype(o_ref.dtype)
    else:
        o_ref[...] = jnp.where(low, o[0:TM], o[TM:rows]).astype(o_ref.dtype)


def _attention(q, kt, v, batch, nblk, *, diff, lam_vecs=None, g_out=None, lam_init=0.0):
    kv_heads = kt.shape[1]
    s_len = kt.shape[3]
    tiles = BRANCH_WIDTH // LANES
    rows = 2 * TM
    q_map = lambda b, h, j: (b * nblk + j, h)
    kv_map = lambda b, h, j: (b, h * kv_heads // tiles, 0, 0)
    const = lambda b, h, j: (0, 0)
    in_specs = [pl.BlockSpec((TM, LANES), q_map),
                pl.BlockSpec((1, 1, LANES, s_len), kv_map),
                pl.BlockSpec((1, 1, s_len, LANES), kv_map)]
    args = [q, kt, v]
    if diff:
        in_specs += [pl.BlockSpec((4, HEAD_DIM), const), pl.BlockSpec((1, LANES), const)]
        args += [lam_vecs, g_out]
    n_ctx = TM
    tkl = min(SCORE_ELEMS // rows, s_len - n_ctx)
    assert (s_len - n_ctx) % tkl == 0 and tkl >= n_ctx
    kern = functools.partial(_attn_kernel, diff=diff, lam_init=lam_init,
                             n_ctx=n_ctx, tkl=tkl, n_lat=(s_len - n_ctx) // tkl)
    return pl.pallas_call(
        kern,
        grid=(batch, tiles, nblk),
        in_specs=in_specs,
        out_specs=pl.BlockSpec((TM, LANES), q_map),
        out_shape=jax.ShapeDtypeStruct((batch * s_len, BRANCH_WIDTH), BF16),
        scratch_shapes=[pltpu.VMEM((rows, LANES), BF16),
                        pltpu.VMEM((rows, LANES), F32),
                        pltpu.VMEM((rows, LANES), F32),
                        pltpu.VMEM((rows, LANES), F32),
                        pltpu.VMEM((rows, tkl), F32),
                        pltpu.VMEM((rows, tkl), F32)],
        compiler_params=_cparams(3),
        name="diff_attn" if diff else "gqa_attn",
    )(*args)


def _s5_kernel(uf_ref, ub_ref, pin_ref, pout_ref, wd_ref, a_ref, cm_ref, yf_ref, yb_ref,
               bu_sc, st_sc):
    n_tiles = BRANCH_WIDTH // LANES
    half = SUBLANES * S5_STATE
    rows = TC * SUBLANES
    tok = (SUBLANES // 2) * TC

    @pl.when(pl.program_id(0) == 0)
    def _():
        st_sc[...] = jnp.zeros_like(st_sc)

    seq = lax.broadcasted_iota(jnp.int32, (rows, LANES), 0) & (SUBLANES - 1)
    fwd = seq < SUBLANES // 2
    lhs_dir = []
    for d, ref in enumerate((uf_ref, ub_ref)):
        ud = ref[...].reshape(tok, BRANCH_WIDTH).astype(BF16)
        lhs_dir.append(jnp.dot(pin_ref[d], ud, preferred_element_type=F32).astype(BF16))
    for j in range(n_tiles):
        cs = slice(j * LANES, (j + 1) * LANES)
        lhs = jnp.concatenate([lhs_dir[0][:, cs], lhs_dir[1][:, cs]], axis=1)
        bu_sc[:, 2 * half * j:2 * half * (j + 1)] = jnp.dot(
            lhs, wd_ref[j], preferred_element_type=F32)

    for j in range(n_tiles):
        re = slice(2 * half * j, 2 * half * j + half)
        im = slice(2 * half * j + half, 2 * half * (j + 1))
        ar = a_ref[0, :, half * j:half * (j + 1)]
        ai = a_ref[1, :, half * j:half * (j + 1)]

        def step(t, carry, re=re, im=im, ar=ar, ai=ai):
            xr, xi = carry
            r = pl.multiple_of(t * SUBLANES, SUBLANES)
            nxr = ar * xr - ai * xi + bu_sc[pl.ds(r, SUBLANES), re]
            nxi = ar * xi + ai * xr + bu_sc[pl.ds(r, SUBLANES), im]
            bu_sc[pl.ds(r, SUBLANES), re] = nxr
            bu_sc[pl.ds(r, SUBLANES), im] = nxi
            return nxr, nxi

        xr, xi = lax.fori_loop(0, TC, step, (st_sc[:, re], st_sc[:, im]), unroll=4)
        st_sc[:, re] = xr
        st_sc[:, im] = xi

    ys = []
    for j in range(n_tiles):
        x = bu_sc[:, 2 * half * j:2 * half * (j + 1)].astype(BF16)
        yy = jnp.dot(x, cm_ref[j], preferred_element_type=F32)
        ys.append(jnp.where(fwd, yy[:, :LANES], yy[:, LANES:]))
    y = jnp.concatenate(ys, axis=1)
    parts = []
    rest = y
    for _ in range(3):
        part = rest.astype(BF16)
        parts.append(part)
        rest = rest - part.astype(F32)
    for d, ref in enumerate((yf_ref, yb_ref)):
        out = None
        for part in parts:
            term = jnp.dot(pout_ref[d], part, preferred_element_type=F32)
            out = term if out is None else out + term
        ref[...] = out.reshape(ref.shape)


def _scan_row_placement(n_seq):
    p = np.zeros((2, TC * 2 * n_seq, n_seq * TC), np.float32)
    for b in range(n_seq):
        for k in range(TC):
            p[0, 2 * n_seq * k + b, b * TC + k] = 1.0
            p[1, 2 * n_seq * (TC - 1 - k) + n_seq + b, b * TC + k] = 1.0
    return jnp.asarray(p, BF16), jnp.asarray(p.transpose(0, 2, 1), BF16)


def _s5_scan(u3, wd, a8, cm, n_ctx):
    batch, s_len, _ = u3.shape
    rows = TC * SUBLANES
    n_tiles = BRANCH_WIDTH // LANES
    n_state = 2 * SUBLANES * S5_STATE * n_tiles
    n_steps = s_len // TC
    ctx_steps = n_ctx // TC
    p_in, p_out = _scan_row_placement(batch)
    fwd_map = lambda g: (0, g, 0)
    bwd_map = lambda g: (0, jnp.where(g < ctx_steps, ctx_steps - 1 - g,
                                      n_steps + ctx_steps - 1 - g), 0)
    c3 = lambda g: (0, 0, 0)
    blk = (batch, TC, BRANCH_WIDTH)
    out = jax.ShapeDtypeStruct(u3.shape, F32)
    return pl.pallas_call(
        _s5_kernel,
        grid=(n_steps,),
        in_specs=[pl.BlockSpec(blk, fwd_map),
                  pl.BlockSpec(blk, bwd_map),
                  pl.BlockSpec(p_in.shape, c3),
                  pl.BlockSpec(p_out.shape, c3),
                  pl.BlockSpec(wd.shape, c3),
                  pl.BlockSpec(a8.shape, c3),
                  pl.BlockSpec(cm.shape, c3)],
        out_specs=[pl.BlockSpec(blk, fwd_map), pl.BlockSpec(blk, bwd_map)],
        out_shape=[out, out],
        scratch_shapes=[pltpu.VMEM((rows, n_state), F32),
                        pltpu.VMEM((SUBLANES, n_state), F32)],
        compiler_params=_cparams(1),
        name="s5_scan",
    )(u3, u3, p_in, p_out, wd, a8, cm)


def _zoh(lam_re, lam_im, log_dt, b_re, b_im):
    dt = jnp.exp(log_dt)[..., None]
    mag = jnp.exp(lam_re * dt)
    a_re = mag * jnp.cos(lam_im * dt)
    a_im = mag * jnp.sin(lam_im * dt)
    den = lam_re * lam_re + lam_im * lam_im
    f_re = ((a_re - 1.0) * lam_re + a_im * lam_im) / den
    f_im = (a_im * lam_re - (a_re - 1.0) * lam_im) / den
    bb_re = f_re[..., None] * b_re - f_im[..., None] * b_im
    bb_im = f_re[..., None] * b_im + f_im[..., None] * b_re
    return a_re, a_im, bb_re, bb_im


def _s5_weights(lam_re, lam_im, log_dt, b_re, b_im, c_re, c_im, batch):
    n_tiles = BRANCH_WIDTH // LANES
    gpt = S5_GROUPS // n_tiles
    a_re, a_im, bb_re, bb_im = _zoh(lam_re, lam_im, log_dt, b_re, b_im)
    eye = jnp.eye(gpt, dtype=F32)

    def drive(bb):
        t = bb.reshape(2, n_tiles, gpt, S5_STATE, S5_CH)
        w = jnp.einsum('djgpc,gh->jdgchp', t, eye)
        return w.reshape(n_tiles, 2 * gpt * S5_CH, gpt * S5_STATE)

    def read(cc):
        t = cc.reshape(2, n_tiles, gpt, S5_CH, S5_STATE)
        w = jnp.einsum('djgcp,gh->jgpdhc', t, eye)
        return w.reshape(n_tiles, gpt * S5_STATE, 2 * gpt * S5_CH)

    wd = jnp.concatenate([drive(bb_re), drive(bb_im)], axis=2).astype(BF16)
    cm = jnp.concatenate([read(c_re), read(-c_im)], axis=1).astype(BF16)

    def per_seq(a):
        return jnp.repeat(a.reshape(2, 1, S5_GROUPS * S5_STATE), batch, axis=1).reshape(
            2 * batch, S5_GROUPS * S5_STATE)

    a8 = jnp.stack([per_seq(a_re), per_seq(a_im)])
    return wd, a8, cm


def _merge_kernel(x_ref, mod_ref, g1_ref, g2_ref, ya_ref, yd_ref, yf_ref, yr_ref, u_ref,
                  dsk_ref, wglu_ref, bglu_ref, wgate_ref, wbr_ref, wout_ref, x1_ref, h2_ref):
    mods = mod_ref[0]
    x = x_ref[...]
    h = _rms_mod(x, g1_ref[...], mods[0:1], mods[1:2]).astype(BF16)
    ys = yf_ref[...] + yr_ref[...] + dsk_ref[...] * u_ref[...]
    g = jax.nn.gelu(ys)
    yb = g * jax.nn.sigmoid(
        jnp.dot(g.astype(BF16), wglu_ref[...], preferred_element_type=F32) + bglu_ref[...])
    branches = (ya_ref[...], yb.astype(BF16), yd_ref[...])
    m = None
    for k, y in enumerate(branches):
        gate = jax.nn.sigmoid(jnp.dot(h, wgate_ref[:, k * D_MODEL:(k + 1) * D_MODEL],
                                      preferred_element_type=F32))
        term = gate * jnp.dot(y, wbr_ref[k], preferred_element_type=F32)
        m = term if m is None else m + term
    y = jnp.dot(m.astype(BF16), wout_ref[...], preferred_element_type=F32)
    x1 = x + mods[2:3] * y
    x1_ref[...] = x1
    h2_ref[...] = _rms_mod(x1, g2_ref[...], mods[3:4], mods[4:5]).astype(BF16)


def _merge(xa, mods, g1, g2, ya, yd, yf, yr, u, d_skip, w_glu, b_glu, w_gate, w_branch, w_out,
           batch, nblk):
    t_rows = xa.shape[0]
    row = lambda i: (i, 0)
    mod_map = lambda i: (jnp.where(i % nblk == 0, batch, i // nblk), 0, 0)
    c2 = lambda i: (0, 0)
    c3 = lambda i: (0, 0, 0)
    return pl.pallas_call(
        _merge_kernel,
        grid=(t_rows // TM,),
        in_specs=[pl.BlockSpec((TM, D_MODEL), row),
                  pl.BlockSpec((1, 6, D_MODEL), mod_map),
                  pl.BlockSpec((1, D_MODEL), c2),
                  pl.BlockSpec((1, D_MODEL), c2),
                  pl.BlockSpec((TM, BRANCH_WIDTH), row),
                  pl.BlockSpec((TM, BRANCH_WIDTH), row),
                  pl.BlockSpec((TM, BRANCH_WIDTH), row),
                  pl.BlockSpec((TM, BRANCH_WIDTH), row),
                  pl.BlockSpec((TM, BRANCH_WIDTH), row),
                  pl.BlockSpec((1, BRANCH_WIDTH), c2),
                  pl.BlockSpec((BRANCH_WIDTH, BRANCH_WIDTH), c2),
                  pl.BlockSpec((1, BRANCH_WIDTH), c2),
                  pl.BlockSpec((D_MODEL, 3 * D_MODEL), c2),
                  pl.BlockSpec((3, BRANCH_WIDTH, D_MODEL), c3),
                  pl.BlockSpec((D_MODEL, D_MODEL), c2)],
        out_specs=[pl.BlockSpec((TM, D_MODEL), row), pl.BlockSpec((TM, D_MODEL), row)],
        out_shape=[jax.ShapeDtypeStruct((t_rows, D_MODEL), F32),
                   jax.ShapeDtypeStruct((t_rows, D_MODEL), BF16)],
        compiler_params=_cparams(1),
        name="merge",
    )(xa, mods, g1, g2, ya, yd, yf, yr, u, d_skip, w_glu, b_glu, w_gate, w_branch, w_out)


def _ffn_kernel(h_ref, hp_ref, hn_ref, x1_ref, mod_ref, wup_ref, cw_ref, wdn_ref, o_ref,
                lhs_sc, ua0_sc, ug0_sc, ua1_sc, ug1_sc, acc_sc, *, nblk, n_chunks):
    pj = pl.program_id(0) % nblk
    left_ok = (pj >= 2).astype(F32)
    right_ok = jnp.logical_and(pj != 0, pj != nblk - 1).astype(F32)
    lhs_sc[0:HALO, :] = (hp_ref[...].astype(F32) * left_ok).astype(BF16)
    lhs_sc[HALO:HALO + TM, :] = h_ref[...]
    lhs_sc[HALO + TM:2 * HALO + TM, :] = (hn_ref[...].astype(F32) * right_ok).astype(BF16)
    acc_sc[...] = jnp.zeros_like(acc_sc)

    def conv(sc, cw, base):
        return (sc[HALO - 1:HALO - 1 + TM, :] * cw[base:base + 1]
                + sc[HALO:HALO + TM, :] * cw[base + 1:base + 2]
                + sc[HALO + 1:HALO + 1 + TM, :] * cw[base + 2:base + 3]
                + cw[base + 3:base + 4])

    def up(k, bufs):
        lhs = lhs_sc[...]
        bufs[0][...] = jnp.dot(lhs, wup_ref[0, k], preferred_element_type=F32)
        bufs[1][...] = jnp.dot(lhs, wup_ref[1, k], preferred_element_type=F32)

    def down(k, bufs):
        cw = cw_ref[k]
        act = jax.nn.silu(conv(bufs[1], cw, 4)) * conv(bufs[0], cw, 0)
        acc_sc[...] += jnp.dot(act.astype(BF16), wdn_ref[k], preferred_element_type=F32)

    bufs = ((ua0_sc, ug0_sc), (ua1_sc, ug1_sc))
    up(0, bufs[0])
    for k in range(n_chunks):
        if k + 1 < n_chunks:
            up(k + 1, bufs[(k + 1) % 2])
        down(k, bufs[k % 2])
    o_ref[...] = x1_ref[...] + mod_ref[0][5:6] * acc_sc[...]


def _ffn(h2, x1, mods, wup, cw, wdn, batch, nblk):
    t_rows = x1.shape[0]
    n_chunks = wdn.shape[0]
    per = TM // HALO
    last = t_rows // HALO - 1
    row = lambda i: (i, 0)
    mod_map = lambda i: (jnp.where(i % nblk == 0, batch, i // nblk), 0, 0)
    kern = functools.partial(_ffn_kernel, nblk=nblk, n_chunks=n_chunks)
    return pl.pallas_call(
        kern,
        grid=(t_rows // TM,),
        in_specs=[pl.BlockSpec((TM, D_MODEL), row),
                  pl.BlockSpec((HALO, D_MODEL), lambda i: (jnp.maximum(i * per - 1, 0), 0)),
                  pl.BlockSpec((HALO, D_MODEL), lambda i: (jnp.minimum((i + 1) * per, last), 0)),
                  pl.BlockSpec((TM, D_MODEL), row),
                  pl.BlockSpec((1, 6, D_MODEL), mod_map),
                  pl.BlockSpec(wup.shape, lambda i: (0, 0, 0, 0)),
                  pl.BlockSpec(cw.shape, lambda i: (0, 0, 0)),
                  pl.BlockSpec(wdn.shape, lambda i: (0, 0, 0))],
        out_specs=pl.BlockSpec((TM, D_MODEL), row),
        out_shape=jax.ShapeDtypeStruct((t_rows, D_MODEL), F32),
        scratch_shapes=([pltpu.VMEM((TM + 2 * HALO, D_MODEL), BF16)]
                        + [pltpu.VMEM((TM + 2 * HALO, FF_CHUNK), F32)] * 4
                        + [pltpu.VMEM((TM, D_MODEL), F32)]),
        compiler_params=_cparams(1),
        name="conv_ffn",
    )(h2, h2, h2, x1, mods, wup, cw, wdn)


def _rope_tables(seq, ctx):
    rows = seq // GRID_W
    row = jnp.repeat(jnp.arange(rows, dtype=F32), GRID_W)
    col = jnp.tile(jnp.arange(GRID_W, dtype=F32), rows)
    n_freq = HEAD_DIM // 4
    inv_freq = ROPE_BASE ** (-jnp.arange(n_freq, dtype=F32) / n_freq)
    ang = jnp.concatenate([row[:, None] * inv_freq, col[:, None] * inv_freq], axis=-1)
    ang = jnp.concatenate([ang, ang], axis=-1)
    cos = jnp.concatenate([jnp.ones((ctx, HEAD_DIM), F32), jnp.cos(ang)], axis=0)
    sin = jnp.concatenate([jnp.zeros((ctx, HEAD_DIM), F32), jnp.sin(ang)], axis=0)
    sign = jnp.where(jnp.arange(HEAD_DIM) < HEAD_DIM // 2, -1.0, 1.0).astype(F32)
    two = lambda t: jnp.concatenate([t, t], axis=1)
    return two(cos), two(sin * sign)


def kernel(x, c, ctx, c_ctx, w_ada, b_ada, norm_g, w_in, qk_gain, ssm_lam_re, ssm_lam_im, ssm_log_dt, ssm_b_re, ssm_b_im, ssm_c_re, ssm_c_im, ssm_d, w_glu, b_glu, diff_lam, diff_norm_g, w_branch, w_out, w_up, conv_w, conv_b, w_down):
    batch, seq, _ = x.shape
    n_ctx = ctx.shape[1]
    depth = w_in.shape[0]
    assert n_ctx == TM and seq % TM == 0 and 2 * batch == SUBLANES
    s_len = n_ctx + seq
    nblk = s_len // TM
    assert s_len % TC == 0 and D_FF % FF_CHUNK == 0
    n_chunks = D_FF // FF_CHUNK

    cos_t, sin_t = _rope_tables(seq, n_ctx)
    cc = jnp.concatenate([c, c_ctx[None, :], jnp.zeros((SUBLANES - batch - 1, D_MODEL), F32)], axis=0)
    mods_all = _ada_mods(cc, w_ada, b_ada).reshape(depth, SUBLANES, 6, D_MODEL)

    xa = jnp.concatenate([ctx, x], axis=1).reshape(batch * s_len, D_MODEL)
    for i in range(depth):
        lam_init = 0.8 - 0.6 * math.exp(-0.3 * i)
        mods = mods_all[i]
        g1 = norm_g[i, 0:1]
        g2 = norm_g[i, 1:2]
        w_i = w_in[i].astype(BF16)
        qa, kta, va, u, qc, ktc, vc = _inproj(xa, mods, g1, w_i[:, :N_QKVU],
                                              jnp.tile(qk_gain[i], (1, BRANCH_WIDTH // HEAD_DIM)),
                                              cos_t, sin_t, batch, nblk)
        ya = _attention(qa, kta, va, batch, nblk, diff=False)
        yd = _attention(qc, ktc, vc, batch, nblk, diff=True, lam_vecs=diff_lam[i],
                        g_out=diff_norm_g[i][None, :], lam_init=lam_init)

        wd, a8, cm = _s5_weights(ssm_lam_re[i], ssm_lam_im[i], ssm_log_dt[i], ssm_b_re[i],
                                 ssm_b_im[i], ssm_c_re[i], ssm_c_im[i], batch)
        yf, yr = _s5_scan(u.reshape(batch, s_len, BRANCH_WIDTH), wd, a8, cm, n_ctx)
        yf = yf.reshape(batch * s_len, BRANCH_WIDTH)
        yr = yr.reshape(batch * s_len, BRANCH_WIDTH)

        x1, h2 = _merge(xa, mods, g1, g2, ya, yd, yf, yr, u, ssm_d[i][None, :],
                        w_glu[i].astype(BF16), b_glu[i][None, :], w_i[:, N_QKVU:],
                        w_branch[i].astype(BF16), w_out[i].astype(BF16), batch, nblk)

        wup = w_up[i].astype(BF16).reshape(D_MODEL, 2, n_chunks, FF_CHUNK).transpose(1, 2, 0, 3)
        cwb = jnp.concatenate([conv_w[i], conv_b[i][None, :]], axis=0)
        cw = cwb.reshape(4, 2, n_chunks, FF_CHUNK).transpose(2, 1, 0, 3).reshape(
            n_chunks, SUBLANES, FF_CHUNK)
        wdn = w_down[i].astype(BF16).reshape(n_chunks, FF_CHUNK, D_MODEL)
        xa = _ffn(h2, x1, mods, wup, cw, wdn, batch, nblk)

    return xa.reshape(batch, s_len, D_MODEL)[:, n_ctx:]
```

```python
import functools
import math

import jax
import jax.numpy as jnp
import numpy as np
from jax import lax
from jax.experimental import pallas as pl
from jax.experimental.pallas import tpu as pltpu

F32 = jnp.float32
BF16 = jnp.bfloat16

D_MODEL = 1024
HEAD_DIM = 64
GRID_W = 64
ROPE_BASE = 10000.0
EPS = 1e-6
BRANCH_WIDTH = D_MODEL // 2
A_HEADS = BRANCH_WIDTH // HEAD_DIM
A_KV_HEADS = A_HEADS // 4
C_HEADS = BRANCH_WIDTH // (2 * HEAD_DIM)
S5_CH = 16
S5_STATE = 64
S5_GROUPS = BRANCH_WIDTH // S5_CH
D_FF = 2816
IN_SIZES = (512, 128, 128, 512, 512, 512, 512, 3 * D_MODEL)
IN_OFFS = tuple(int(v) for v in np.cumsum((0,) + IN_SIZES))
N_QKVU = IN_OFFS[7]

LANES = 128
SUBLANES = 8
TM = 256
SCORE_ELEMS = 512 * 1024
TC = 64
FF_CHUNK = 256
HALO = 16
NEG_BIG = -1e30
VMEM_LIMIT = 56 * 1024 * 1024
SCORE_SCALE = HEAD_DIM ** -0.5 * math.log2(math.e)


def _cparams(n_axes, flags=None):
    return pltpu.CompilerParams(dimension_semantics=("arbitrary",) * n_axes,
                                vmem_limit_bytes=VMEM_LIMIT, flags=flags)


def _rms_mod(x, gain, shift, scale):
    y = x * lax.rsqrt(jnp.mean(x * x, axis=-1, keepdims=True) + EPS)
    return (y * gain) * (1.0 + scale) + shift


def _head_avg_matrix(width):
    r = lax.broadcasted_iota(jnp.int32, (width, width), 0) >> 6
    c = lax.broadcasted_iota(jnp.int32, (width, width), 1) >> 6
    return jnp.where(r == c, 1.0 / HEAD_DIM, 0.0).astype(BF16)


def _head_rms(z, gain, avg):
    sq = z * z
    hi = sq.astype(BF16)
    lo = (sq - hi.astype(F32)).astype(BF16)
    ms = (jnp.dot(hi, avg, preferred_element_type=F32)
          + jnp.dot(lo, avg, preferred_element_type=F32))
    return z * lax.rsqrt(ms + EPS) * gain


def _tile_lanes(t, width):
    reps = width // t.shape[1]
    return t if reps == 1 else jnp.concatenate([t] * reps, axis=1)


def _rope(z, cos, sin_signed):
    width = z.shape[1]
    lane = lax.broadcasted_iota(jnp.int32, z.shape, 1)
    first_half = (lane & (HEAD_DIM - 1)) < HEAD_DIM // 2
    rot = jnp.where(first_half,
                    pltpu.roll(z, width - HEAD_DIM // 2, 1),
                    pltpu.roll(z, HEAD_DIM // 2, 1))
    return z * _tile_lanes(cos, width) + rot * _tile_lanes(sin_signed, width)


def _dup_halves(z):
    lane = lax.broadcasted_iota(jnp.int32, z.shape, 1)
    low = lane < HEAD_DIM
    sw = pltpu.roll(z, HEAD_DIM, 1)
    return jnp.where(low, z, sw), jnp.where(low, sw, z)


def _ada_kernel(c_ref, w_ref, b_ref, o_ref):
    a = jax.nn.silu(c_ref[...])
    o_ref[0] = jnp.dot(a, w_ref[0], preferred_element_type=F32,
                       precision=lax.Precision.HIGHEST) + b_ref[0]


def _ada_mods(cc, w_ada, b_ada):
    depth, _, n = w_ada.shape
    tn = 1536
    return pl.pallas_call(
        _ada_kernel,
        grid=(depth, n // tn),
        in_specs=[pl.BlockSpec((SUBLANES, D_MODEL), lambda l, j: (0, 0)),
                  pl.BlockSpec((1, D_MODEL, tn), lambda l, j: (l, 0, j)),
                  pl.BlockSpec((1, 1, tn), lambda l, j: (l, 0, j))],
        out_specs=pl.BlockSpec((1, SUBLANES, tn), lambda l, j: (l, 0, j)),
        out_shape=jax.ShapeDtypeStruct((depth, SUBLANES, n), F32),
        compiler_params=_cparams(2),
        name="ada_mods",
    )(cc, w_ada, b_ada.reshape(depth, 1, n))


def _inproj_kernel(x_ref, mod_ref, g_ref, w_ref, qk_ref, cos_ref, sin_ref,
                   qa_ref, kta_ref, va_ref, u_ref, qc_ref, ktc_ref, vc_ref):
    mods = mod_ref[0]
    h = _rms_mod(x_ref[...], g_ref[...], mods[0:1], mods[1:2]).astype(BF16)
    cos = cos_ref[...]
    sin = sin_ref[...]
    avg512 = _head_avg_matrix(512)

    def seg(k):
        return jnp.dot(h, w_ref[:, IN_OFFS[k]:IN_OFFS[k + 1]], preferred_element_type=F32)

    def gain(k, width):
        return qk_ref[k:k + 1, :width]

    q = _rope(_head_rms(seg(0), gain(0, 512), avg512), cos, sin)
    qa_ref[...] = (q * SCORE_SCALE).astype(BF16)
    k = _rope(_head_rms(seg(1), gain(1, 128), avg512[:128, :128]), cos, sin)
    for hh, kd in enumerate(_dup_halves(k)):
        kta_ref[0, hh] = kd.T.astype(BF16)
    for hh, vd in enumerate(_dup_halves(seg(2))):
        va_ref[0, hh] = vd.astype(BF16)
    u_ref[...] = seg(3)
    q = _rope(_head_rms(seg(4), gain(2, 512), avg512), cos, sin)
    qc_ref[...] = (q * SCORE_SCALE).astype(BF16)
    k = _rope(_head_rms(seg(5), gain(3, 512), avg512), cos, sin)
    v = seg(6)
    for hh in range(C_HEADS):
        ktc_ref[0, hh] = k[:, hh * LANES:(hh + 1) * LANES].T.astype(BF16)
        vc_ref[0, hh] = v[:, hh * LANES:(hh + 1) * LANES].astype(BF16)


def _inproj(xa, mods, norm_g, w_qkvu, qk_gain, cos_t, sin_t, batch, nblk):
    t_rows = xa.shape[0]
    s_len = nblk * TM
    row = lambda i: (i, 0)
    mod_map = lambda i: (jnp.where(i % nblk == 0, batch, i // nblk), 0, 0)
    pos = lambda i: (i % nblk, 0)
    kt_map = lambda i: (i // nblk, 0, 0, i % nblk)
    v_map = lambda i: (i // nblk, 0, i % nblk, 0)
    return pl.pallas_call(
        _inproj_kernel,
        grid=(t_rows // TM,),
        in_specs=[pl.BlockSpec((TM, D_MODEL), row),
                  pl.BlockSpec((1, 6, D_MODEL), mod_map),
                  pl.BlockSpec((1, D_MODEL), lambda i: (0, 0)),
                  pl.BlockSpec((D_MODEL, N_QKVU), lambda i: (0, 0)),
                  pl.BlockSpec((4, BRANCH_WIDTH), lambda i: (0, 0)),
                  pl.BlockSpec((TM, LANES), pos),
                  pl.BlockSpec((TM, LANES), pos)],
        out_specs=[pl.BlockSpec((TM, 512), row),
                   pl.BlockSpec((1, A_KV_HEADS, LANES, TM), kt_map),
                   pl.BlockSpec((1, A_KV_HEADS, TM, LANES), v_map),
                   pl.BlockSpec((TM, 512), row),
                   pl.BlockSpec((TM, 512), row),
                   pl.BlockSpec((1, C_HEADS, LANES, TM), kt_map),
                   pl.BlockSpec((1, C_HEADS, TM, LANES), v_map)],
        out_shape=[jax.ShapeDtypeStruct((t_rows, 512), BF16),
                   jax.ShapeDtypeStruct((batch, A_KV_HEADS, LANES, s_len), BF16),
                   jax.ShapeDtypeStruct((batch, A_KV_HEADS, s_len, LANES), BF16),
                   jax.ShapeDtypeStruct((t_rows, 512), F32),
                   jax.ShapeDtypeStruct((t_rows, 512), BF16),
                   jax.ShapeDtypeStruct((batch, C_HEADS, LANES, s_len), BF16),
                   jax.ShapeDtypeStruct((batch, C_HEADS, s_len, LANES), BF16)],
        compiler_params=_cparams(1),
        name="in_proj",
    )(xa, mods, norm_g, w_qkvu, qk_gain, cos_t, sin_t)


def _attn_kernel(*refs, diff, lam_init, n_ctx, tkl, n_lat):
    if diff:
        q_ref, kt_ref, v_ref, lam_ref, gout_ref, o_ref = refs[:6]
    else:
        q_ref, kt_ref, v_ref, o_ref = refs[:4]
    lhs_sc, m_sc, l_sc, acc_sc, s0_sc, s1_sc = refs[-6:]
    j = pl.program_id(2)
    rows = 2 * TM
    lane = lax.broadcasted_iota(jnp.int32, (TM, LANES), 1)
    low = lane < HEAD_DIM
    qt = q_ref[...].astype(F32)
    lhs_sc[0:TM, :] = jnp.where(low, qt, 0.0).astype(BF16)
    lhs_sc[TM:rows, :] = jnp.where(low, 0.0, qt).astype(BF16)
    m_sc[...] = jnp.full((rows, LANES), NEG_BIG, F32)
    l_sc[...] = jnp.zeros((rows, LANES), F32)
    acc_sc[...] = jnp.zeros((rows, LANES), F32)

    def chunk_keys(c):
        return slice(0 if c == 0 else n_ctx + c * tkl, n_ctx + (c + 1) * tkl)

    def scores(keys):
        return jnp.dot(lhs_sc[...], kt_ref[0, 0, :, keys], preferred_element_type=F32)

    def softmax_pv(s, keys):
        width = s.shape[1]
        mx = s[:, 0:LANES]
        for t in range(1, width // LANES):
            mx = jnp.maximum(mx, s[:, t * LANES:(t + 1) * LANES])
        m_prev = m_sc[...]
        m_next = jnp.maximum(m_prev, jnp.max(mx, axis=1, keepdims=True))
        alpha = jnp.exp2(m_prev - m_next)
        p = jnp.exp2(s - _tile_lanes(m_next, width))
        psum = p[:, 0:LANES]
        for t in range(1, width // LANES):
            psum = psum + p[:, t * LANES:(t + 1) * LANES]
        l_sc[...] = alpha * l_sc[...] + psum
        acc_sc[...] = alpha * acc_sc[...] + jnp.dot(p.astype(BF16), v_ref[0, 0, keys, :],
                                                    preferred_element_type=F32)
        m_sc[...] = m_next

    @pl.when(j == 0)
    def _():
        ctx_keys = slice(0, n_ctx)
        softmax_pv(scores(ctx_keys), ctx_keys)

    @pl.when(j > 0)
    def _():
        bufs = (s0_sc, s1_sc)

        def put(c):
            keys = chunk_keys(c)
            bufs[c % 2][:, 0:keys.stop - keys.start] = scores(keys)

        def take(c):
            keys = chunk_keys(c)
            softmax_pv(bufs[c % 2][:, 0:keys.stop - keys.start], keys)

        put(0)
        for c in range(n_lat):
            if c + 1 < n_lat:
                put(c + 1)
            take(c)

    o = acc_sc[...] / jnp.sum(l_sc[...], axis=1, keepdims=True)
    if diff:
        lv = lam_ref[...]
        lam = (jnp.exp(jnp.sum(lv[0:1] * lv[1:2], axis=1, keepdims=True))
               - jnp.exp(jnp.sum(lv[2:3] * lv[3:4], axis=1, keepdims=True)) + lam_init)
        od = o[0:TM] - lam * o[TM:rows]
        od = od * lax.rsqrt(jnp.mean(od * od, axis=-1, keepdims=True) + EPS)
        o_ref[...] = ((od * gout_ref[...]) * (1.0 - lam_init)).astype(o_ref.dtype)
    else:
        o_ref[...] = jnp.where(low, o[0:TM], o[TM:rows]).astype(o_ref.dtype)


def _attention(q, kt, v, batch, nblk, *, diff, lam_vecs=None, g_out=None, lam_init=0.0):
    kv_heads = kt.shape[1]
    s_len = kt.shape[3]
    tiles = BRANCH_WIDTH // LANES
    rows = 2 * TM
    q_map = lambda b, h, j: (b * nblk + j, h)
    kv_map = lambda b, h, j: (b, h * kv_heads // tiles, 0, 0)
    const = lambda b, h, j: (0, 0)
    in_specs = [pl.BlockSpec((TM, LANES), q_map),
                pl.BlockSpec((1, 1, LANES, s_len), kv_map),
                pl.BlockSpec((1, 1, s_len, LANES), kv_map)]
    args = [q, kt, v]
    if diff:
        in_specs += [pl.BlockSpec((4, HEAD_DIM), const), pl.BlockSpec((1, LANES), const)]
        args += [lam_vecs, g_out]
    n_ctx = TM
    tkl = min(SCORE_ELEMS // rows, s_len - n_ctx)
    assert (s_len - n_ctx) % tkl == 0 and tkl >= n_ctx
    kern = functools.partial(_attn_kernel, diff=diff, lam_init=lam_init,
                             n_ctx=n_ctx, tkl=tkl, n_lat=(s_len - n_ctx) // tkl)
    return pl.pallas_call(
        kern,
        grid=(batch, tiles, nblk),
        in_specs=in_specs,
        out_specs=pl.BlockSpec((TM, LANES), q_map),
        out_shape=jax.ShapeDtypeStruct((batch * s_len, BRANCH_WIDTH), BF16),
        scratch_shapes=([pltpu.VMEM((rows, LANES), BF16)]
                        + [pltpu.VMEM((rows, LANES), F32)] * 3
                        + [pltpu.VMEM((rows, n_ctx + tkl), F32)] * 2),
        compiler_params=_cparams(3),
        name="diff_attn" if diff else "gqa_attn",
    )(*args)


def _s5_kernel(uf_ref, ub_ref, pin_ref, pout_ref, wd_ref, a_ref, cm_ref, yf_ref, yb_ref,
               bu_sc, st_sc):
    n_tiles = BRANCH_WIDTH // LANES
    half = SUBLANES * S5_STATE
    rows = TC * SUBLANES
    tok = (SUBLANES // 2) * TC

    @pl.when(pl.program_id(0) == 0)
    def _():
        st_sc[...] = jnp.zeros_like(st_sc)

    seq = lax.broadcasted_iota(jnp.int32, (rows, LANES), 0) & (SUBLANES - 1)
    fwd = seq < SUBLANES // 2
    lhs_dir = []
    for d, ref in enumerate((uf_ref, ub_ref)):
        ud = ref[...].reshape(tok, BRANCH_WIDTH).astype(BF16)
        lhs_dir.append(jnp.dot(pin_ref[d], ud, preferred_element_type=F32).astype(BF16))
    for j in range(n_tiles):
        cs = slice(j * LANES, (j + 1) * LANES)
        lhs = jnp.concatenate([lhs_dir[0][:, cs], lhs_dir[1][:, cs]], axis=1)
        bu_sc[:, 2 * half * j:2 * half * (j + 1)] = jnp.dot(
            lhs, wd_ref[j], preferred_element_type=F32)

    for j in range(n_tiles):
        re = slice(2 * half * j, 2 * half * j + half)
        im = slice(2 * half * j + half, 2 * half * (j + 1))
        ar = a_ref[0, :, half * j:half * (j + 1)]
        ai = a_ref[1, :, half * j:half * (j + 1)]

        def step(t, carry, re=re, im=im, ar=ar, ai=ai):
            xr, xi = carry
            r = pl.multiple_of(t * SUBLANES, SUBLANES)
            nxr = ar * xr - ai * xi + bu_sc[pl.ds(r, SUBLANES), re]
            nxi = ar * xi + ai * xr + bu_sc[pl.ds(r, SUBLANES), im]
            bu_sc[pl.ds(r, SUBLANES), re] = nxr
            bu_sc[pl.ds(r, SUBLANES), im] = nxi
            return nxr, nxi

        xr, xi = lax.fori_loop(0, TC, step, (st_sc[:, re], st_sc[:, im]), unroll=True)
        st_sc[:, re] = xr
        st_sc[:, im] = xi

    ys = []
    for j in range(n_tiles):
        x = bu_sc[:, 2 * half * j:2 * half * (j + 1)].astype(BF16)
        yy = jnp.dot(x, cm_ref[j], preferred_element_type=F32)
        ys.append(jnp.where(fwd, yy[:, :LANES], yy[:, LANES:]))
    y = jnp.concatenate(ys, axis=1)
    parts = []
    rest = y
    for _ in range(3):
        part = rest.astype(BF16)
        parts.append(part)
        rest = rest - part.astype(F32)
    for d, ref in enumerate((yf_ref, yb_ref)):
        out = None
        for part in parts:
            term = jnp.dot(pout_ref[d], part, preferred_element_type=F32)
            out = term if out is None else out + term
        ref[...] = out.reshape(ref.shape)


def _scan_row_placement(n_seq):
    p = np.zeros((2, TC * 2 * n_seq, n_seq * TC), np.float32)
    for b in range(n_seq):
        for k in range(TC):
            p[0, 2 * n_seq * k + b, b * TC + k] = 1.0
            p[1, 2 * n_seq * (TC - 1 - k) + n_seq + b, b * TC + k] = 1.0
    return jnp.asarray(p, BF16), jnp.asarray(p.transpose(0, 2, 1), BF16)


def _s5_scan(u3, wd, a8, cm, n_ctx):
    batch, s_len, _ = u3.shape
    rows = TC * SUBLANES
    n_tiles = BRANCH_WIDTH // LANES
    n_state = 2 * SUBLANES * S5_STATE * n_tiles
    n_steps = s_len // TC
    ctx_steps = n_ctx // TC
    p_in, p_out = _scan_row_placement(batch)
    fwd_map = lambda g: (0, g, 0)
    bwd_map = lambda g: (0, jnp.where(g < ctx_steps, ctx_steps - 1 - g,
                                      n_steps + ctx_steps - 1 - g), 0)
    c3 = lambda g: (0, 0, 0)
    blk = (batch, TC, BRANCH_WIDTH)
    out = jax.ShapeDtypeStruct(u3.shape, F32)
    return pl.pallas_call(
        _s5_kernel,
        grid=(n_steps,),
        in_specs=[pl.BlockSpec(blk, fwd_map),
                  pl.BlockSpec(blk, bwd_map),
                  pl.BlockSpec(p_in.shape, c3),
                  pl.BlockSpec(p_out.shape, c3),
                  pl.BlockSpec(wd.shape, c3),
                  pl.BlockSpec(a8.shape, c3),
                  pl.BlockSpec(cm.shape, c3)],
        out_specs=[pl.BlockSpec(blk, fwd_map), pl.BlockSpec(blk, bwd_map)],
        out_shape=[out, out],
        scratch_shapes=[pltpu.VMEM((rows, n_state), F32),
                        pltpu.VMEM((SUBLANES, n_state), F32)],
        compiler_params=_cparams(1),
        name="s5_scan",
    )(u3, u3, p_in, p_out, wd, a8, cm)


def _zoh(lam_re, lam_im, log_dt, b_re, b_im):
    dt = jnp.exp(log_dt)[..., None]
    mag = jnp.exp(lam_re * dt)
    a_re = mag * jnp.cos(lam_im * dt)
    a_im = mag * jnp.sin(lam_im * dt)
    den = lam_re * lam_re + lam_im * lam_im
    f_re = ((a_re - 1.0) * lam_re + a_im * lam_im) / den
    f_im = (a_im * lam_re - (a_re - 1.0) * lam_im) / den
    bb_re = f_re[..., None] * b_re - f_im[..., None] * b_im
    bb_im = f_re[..., None] * b_im + f_im[..., None] * b_re
    return a_re, a_im, bb_re, bb_im


def _s5_weights(lam_re, lam_im, log_dt, b_re, b_im, c_re, c_im, batch):
    n_tiles = BRANCH_WIDTH // LANES
    gpt = S5_GROUPS // n_tiles
    a_re, a_im, bb_re, bb_im = _zoh(lam_re, lam_im, log_dt, b_re, b_im)
    eye = jnp.eye(gpt, dtype=F32)

    def drive(bb):
        t = bb.reshape(2, n_tiles, gpt, S5_STATE, S5_CH)
        w = jnp.einsum('djgpc,gh->jdgchp', t, eye)
        return w.reshape(n_tiles, 2 * gpt * S5_CH, gpt * S5_STATE)

    def read(cc):
        t = cc.reshape(2, n_tiles, gpt, S5_CH, S5_STATE)
        w = jnp.einsum('djgcp,gh->jgpdhc', t, eye)
        return w.reshape(n_tiles, gpt * S5_STATE, 2 * gpt * S5_CH)

    wd = jnp.concatenate([drive(bb_re), drive(bb_im)], axis=2).astype(BF16)
    cm = jnp.concatenate([read(c_re), read(-c_im)], axis=1).astype(BF16)

    def per_seq(a):
        return jnp.repeat(a.reshape(2, 1, S5_GROUPS * S5_STATE), batch, axis=1).reshape(
            2 * batch, S5_GROUPS * S5_STATE)

    a8 = jnp.stack([per_seq(a_re), per_seq(a_im)])
    return wd, a8, cm


def _merge_kernel(x_ref, mod_ref, g1_ref, g2_ref, ya_ref, yd_ref, yf_ref, yr_ref, u_ref,
                  dsk_ref, wglu_ref, bglu_ref, wgate_ref, wbr_ref, wout_ref, x1_ref, h2_ref):
    mods = mod_ref[0]
    x = x_ref[...]
    h = _rms_mod(x, g1_ref[...], mods[0:1], mods[1:2]).astype(BF16)
    ys = yf_ref[...] + yr_ref[...] + dsk_ref[...] * u_ref[...]
    g = jax.nn.gelu(ys)
    yb = g * jax.nn.sigmoid(
        jnp.dot(g.astype(BF16), wglu_ref[...], preferred_element_type=F32) + bglu_ref[...])
    branches = (ya_ref[...], yb.astype(BF16), yd_ref[...])
    m = None
    for k, y in enumerate(branches):
        gate = jax.nn.sigmoid(jnp.dot(h, wgate_ref[:, k * D_MODEL:(k + 1) * D_MODEL],
                                      preferred_element_type=F32))
        term = gate * jnp.dot(y, wbr_ref[k], preferred_element_type=F32)
        m = term if m is None else m + term
    y = jnp.dot(m.astype(BF16), wout_ref[...], preferred_element_type=F32)
    x1 = x + mods[2:3] * y
    x1_ref[...] = x1
    h2_ref[...] = _rms_mod(x1, g2_ref[...], mods[3:4], mods[4:5]).astype(BF16)


def _merge(xa, mods, g1, g2, ya, yd, yf, yr, u, d_skip, w_glu, b_glu, w_gate, w_branch, w_out,
           batch, nblk):
    t_rows = xa.shape[0]
    row = lambda i: (i, 0)
    mod_map = lambda i: (jnp.where(i % nblk == 0, batch, i // nblk), 0, 0)
    c2 = lambda i: (0, 0)
    c3 = lambda i: (0, 0, 0)
    return pl.pallas_call(
        _merge_kernel,
        grid=(t_rows // TM,),
        in_specs=[pl.BlockSpec((TM, D_MODEL), row),
                  pl.BlockSpec((1, 6, D_MODEL), mod_map),
                  pl.BlockSpec((1, D_MODEL), c2),
                  pl.BlockSpec((1, D_MODEL), c2),
                  pl.BlockSpec((TM, BRANCH_WIDTH), row),
                  pl.BlockSpec((TM, BRANCH_WIDTH), row),
                  pl.BlockSpec((TM, BRANCH_WIDTH), row),
                  pl.BlockSpec((TM, BRANCH_WIDTH), row),
                  pl.BlockSpec((TM, BRANCH_WIDTH), row),
                  pl.BlockSpec((1, BRANCH_WIDTH), c2),
                  pl.BlockSpec((BRANCH_WIDTH, BRANCH_WIDTH), c2),
                  pl.BlockSpec((1, BRANCH_WIDTH), c2),
                  pl.BlockSpec((D_MODEL, 3 * D_MODEL), c2),
                  pl.BlockSpec((3, BRANCH_WIDTH, D_MODEL), c3),
                  pl.BlockSpec((D_MODEL, D_MODEL), c2)],
        out_specs=[pl.BlockSpec((TM, D_MODEL), row), pl.BlockSpec((TM, D_MODEL), row)],
        out_shape=[jax.ShapeDtypeStruct((t_rows, D_MODEL), F32),
                   jax.ShapeDtypeStruct((t_rows, D_MODEL), BF16)],
        compiler_params=_cparams(1),
        name="merge",
    )(xa, mods, g1, g2, ya, yd, yf, yr, u, d_skip, w_glu, b_glu, w_gate, w_branch, w_out)


def _ffn_kernel(h_ref, hp_ref, hn_ref, x1_ref, mod_ref, wup_ref, cw_ref, wdn_ref, o_ref,
                lhs_sc, ua0_sc, ug0_sc, ua1_sc, ug1_sc, acc_sc, *, nblk, n_chunks):
    pj = pl.program_id(0) % nblk
    left_ok = (pj >= 2).astype(F32)
    right_ok = jnp.logical_and(pj != 0, pj != nblk - 1).astype(F32)
    lhs_sc[0:HALO, :] = (hp_ref[...].astype(F32) * left_ok).astype(BF16)
    lhs_sc[HALO:HALO + TM, :] = h_ref[...]
    lhs_sc[HALO + TM:2 * HALO + TM, :] = (hn_ref[...].astype(F32) * right_ok).astype(BF16)
    acc_sc[...] = jnp.zeros_like(acc_sc)

    def conv(sc, cw, base):
        return (sc[HALO - 1:HALO - 1 + TM, :] * cw[base:base + 1]
                + sc[HALO:HALO + TM, :] * cw[base + 1:base + 2]
                + sc[HALO + 1:HALO + 1 + TM, :] * cw[base + 2:base + 3]
                + cw[base + 3:base + 4])

    def up(k, bufs):
        lhs = lhs_sc[...]
        bufs[0][...] = jnp.dot(lhs, wup_ref[0, k], preferred_element_type=F32)
        bufs[1][...] = jnp.dot(lhs, wup_ref[1, k], preferred_element_type=F32)

    def down(k, bufs):
        cw = cw_ref[k]
        act = jax.nn.silu(conv(bufs[1], cw, 4)) * conv(bufs[0], cw, 0)
        acc_sc[...] += jnp.dot(act.astype(BF16), wdn_ref[k], preferred_element_type=F32)

    bufs = ((ua0_sc, ug0_sc), (ua1_sc, ug1_sc))
    up(0, bufs[0])
    for k in range(n_chunks):
        if k + 1 < n_chunks:
            up(k + 1, bufs[(k + 1) % 2])
        down(k, bufs[k % 2])
    o_ref[...] = x1_ref[...] + mod_ref[0][5:6] * acc_sc[...]


def _ffn(h2, x1, mods, wup, cw, wdn, batch, nblk):
    t_rows = x1.shape[0]
    n_chunks = wdn.shape[0]
    per = TM // HALO
    last = t_rows // HALO - 1
    row = lambda i: (i, 0)
    mod_map = lambda i: (jnp.where(i % nblk == 0, batch, i // nblk), 0, 0)
    kern = functools.partial(_ffn_kernel, nblk=nblk, n_chunks=n_chunks)
    return pl.pallas_call(
        kern,
        grid=(t_rows // TM,),
        in_specs=[pl.BlockSpec((TM, D_MODEL), row),
                  pl.BlockSpec((HALO, D_MODEL), lambda i: (jnp.maximum(i * per - 1, 0), 0)),
                  pl.BlockSpec((HALO, D_MODEL), lambda i: (jnp.minimum((i + 1) * per, last), 0)),
                  pl.BlockSpec((TM, D_MODEL), row),
                  pl.BlockSpec((1, 6, D_MODEL), mod_map),
                  pl.BlockSpec(wup.shape, lambda i: (0, 0, 0, 0)),
                  pl.BlockSpec(cw.shape, lambda i: (0, 0, 0)),
                  pl.BlockSpec(wdn.shape, lambda i: (0, 0, 0))],
        out_specs=pl.BlockSpec((TM, D_MODEL), row),
        out_shape=jax.ShapeDtypeStruct((t_rows, D_MODEL), F32),
        scratch_shapes=([pltpu.VMEM((TM + 2 * HALO, D_MODEL), BF16)]
                        + [pltpu.VMEM((TM + 2 * HALO, FF_CHUNK), F32)] * 4
                        + [pltpu.VMEM((TM, D_MODEL), F32)]),
        compiler_params=_cparams(1),
        name="conv_ffn",
    )(h2, h2, h2, x1, mods, wup, cw, wdn)


def _rope_tables(seq, ctx):
    rows = seq // GRID_W
    row = jnp.repeat(jnp.arange(rows, dtype=F32), GRID_W)
    col = jnp.tile(jnp.arange(GRID_W, dtype=F32), rows)
    n_freq = HEAD_DIM // 4
    inv_freq = ROPE_BASE ** (-jnp.arange(n_freq, dtype=F32) / n_freq)
    ang = jnp.concatenate([row[:, None] * inv_freq, col[:, None] * inv_freq], axis=-1)
    ang = jnp.concatenate([ang, ang], axis=-1)
    cos = jnp.concatenate([jnp.ones((ctx, HEAD_DIM), F32), jnp.cos(ang)], axis=0)
    sin = jnp.concatenate([jnp.zeros((ctx, HEAD_DIM), F32), jnp.sin(ang)], axis=0)
    sign = jnp.where(jnp.arange(HEAD_DIM) < HEAD_DIM // 2, -1.0, 1.0).astype(F32)
    two = lambda t: jnp.concatenate([t, t], axis=1)
    return two(cos), two(sin * sign)


def kernel(x, c, ctx, c_ctx, w_ada, b_ada, norm_g, w_in, qk_gain, ssm_lam_re, ssm_lam_im, ssm_log_dt, ssm_b_re, ssm_b_im, ssm_c_re, ssm_c_im, ssm_d, w_glu, b_glu, diff_lam, diff_norm_g, w_branch, w_out, w_up, conv_w, conv_b, w_down):
    batch, seq, _ = x.shape
    n_ctx = ctx.shape[1]
    depth = w_in.shape[0]
    assert n_ctx == TM and seq % TM == 0 and 2 * batch == SUBLANES
    s_len = n_ctx + seq
    nblk = s_len // TM
    assert s_len % TC == 0 and D_FF % FF_CHUNK == 0
    n_chunks = D_FF // FF_CHUNK

    cos_t, sin_t = _rope_tables(seq, n_ctx)
    cc = jnp.concatenate([c, c_ctx[None, :], jnp.zeros((SUBLANES - batch - 1, D_MODEL), F32)], axis=0)
    mods_all = _ada_mods(cc, w_ada, b_ada).reshape(depth, SUBLANES, 6, D_MODEL)

    xa = jnp.concatenate([ctx, x], axis=1).reshape(batch * s_len, D_MODEL)
    for i in range(depth):
        lam_init = 0.8 - 0.6 * math.exp(-0.3 * i)
        mods = mods_all[i]
        g1 = norm_g[i, 0:1]
        g2 = norm_g[i, 1:2]
        w_i = w_in[i].astype(BF16)
        qa, kta, va, u, qc, ktc, vc = _inproj(xa, mods, g1, w_i[:, :N_QKVU],
                                              jnp.tile(qk_gain[i], (1, BRANCH_WIDTH // HEAD_DIM)),
                                              cos_t, sin_t, batch, nblk)
        ya = _attention(qa, kta, va, batch, nblk, diff=False)
        yd = _attention(qc, ktc, vc, batch, nblk, diff=True, lam_vecs=diff_lam[i],
                        g_out=diff_norm_g[i][None, :], lam_init=lam_init)

        wd, a8, cm = _s5_weights(ssm_lam_re[i], ssm_lam_im[i], ssm_log_dt[i], ssm_b_re[i],
                                 ssm_b_im[i], ssm_c_re[i], ssm_c_im[i], batch)
        yf, yr = _s5_scan(u.reshape(batch, s_len, BRANCH_WIDTH), wd, a8, cm, n_ctx)
        yf = yf.reshape(batch * s_len, BRANCH_WIDTH)
        yr = yr.reshape(batch * s_len, BRANCH_WIDTH)

        x1, h2 = _merge(xa, mods, g1, g2, ya, yd, yf, yr, u, ssm_d[i][None, :],
                        w_glu[i].astype(BF16), b_glu[i][None, :], w_i[:, N_QKVU:],
                        w_branch[i].astype(BF16), w_out[i].astype(BF16), batch, nblk)

        wup = w_up[i].astype(BF16).reshape(D_MODEL, 2, n_chunks, FF_CHUNK).transpose(1, 2, 0, 3)
        cwb = jnp.concatenate([conv_w[i], conv_b[i][None, :]], axis=0)
        cw = cwb.reshape(4, 2, n_chunks, FF_CHUNK).transpose(2, 1, 0, 3).reshape(
            n_chunks, SUBLANES, FF_CHUNK)
        wdn = w_down[i].astype(BF16).reshape(n_chunks, FF_CHUNK, D_MODEL)
        xa = _ffn(h2, x1, mods, wup, cw, wdn, batch, nblk)

    return xa.reshape(batch, s_len, D_MODEL)[:, n_ctx:]
```

```python
import functools
import math

import jax
import jax.numpy as jnp
import numpy as np
from jax import lax
from jax.experimental import pallas as pl
from jax.experimental.pallas import tpu as pltpu

F32 = jnp.float32
BF16 = jnp.bfloat16

D_MODEL = 1024
HEAD_DIM = 64
GRID_W = 64
ROPE_BASE = 10000.0
EPS = 1e-6
BRANCH_WIDTH = D_MODEL // 2
A_HEADS = BRANCH_WIDTH // HEAD_DIM
A_KV_HEADS = A_HEADS // 4
C_HEADS = BRANCH_WIDTH // (2 * HEAD_DIM)
S5_CH = 16
S5_STATE = 64
S5_GROUPS = BRANCH_WIDTH // S5_CH
D_FF = 2816
IN_SIZES = (512, 128, 128, 512, 512, 512, 512, 3 * D_MODEL)
IN_OFFS = tuple(int(v) for v in np.cumsum((0,) + IN_SIZES))
N_QKVU = IN_OFFS[7]

LANES = 128
SUBLANES = 8
TM = 256
SCORE_ELEMS = 512 * 2048
TC = 64
FF_CHUNK = 256
HALO = 16
VMEM_LIMIT = 56 * 1024 * 1024
SCORE_SCALE = HEAD_DIM ** -0.5 * math.log2(math.e)


def _cparams(n_axes, flags=None):
    return pltpu.CompilerParams(dimension_semantics=("arbitrary",) * n_axes,
                                vmem_limit_bytes=VMEM_LIMIT, flags=flags)


def _rms_mod(x, gain, shift, scale):
    y = x * lax.rsqrt(jnp.mean(x * x, axis=-1, keepdims=True) + EPS)
    return (y * gain) * (1.0 + scale) + shift


def _head_avg_matrix(width):
    r = lax.broadcasted_iota(jnp.int32, (width, width), 0) >> 6
    c = lax.broadcasted_iota(jnp.int32, (width, width), 1) >> 6
    return jnp.where(r == c, 1.0 / HEAD_DIM, 0.0).astype(BF16)


def _head_rms(z, gain, avg):
    sq = z * z
    hi = sq.astype(BF16)
    lo = (sq - hi.astype(F32)).astype(BF16)
    ms = (jnp.dot(hi, avg, preferred_element_type=F32)
          + jnp.dot(lo, avg, preferred_element_type=F32))
    return z * lax.rsqrt(ms + EPS) * gain


def _tile_lanes(t, width):
    reps = width // t.shape[1]
    return t if reps == 1 else jnp.concatenate([t] * reps, axis=1)


def _rope(z, cos, sin_signed):
    width = z.shape[1]
    lane = lax.broadcasted_iota(jnp.int32, z.shape, 1)
    first_half = (lane & (HEAD_DIM - 1)) < HEAD_DIM // 2
    rot = jnp.where(first_half,
                    pltpu.roll(z, width - HEAD_DIM // 2, 1),
                    pltpu.roll(z, HEAD_DIM // 2, 1))
    return z * _tile_lanes(cos, width) + rot * _tile_lanes(sin_signed, width)


def _dup_halves(z):
    lane = lax.broadcasted_iota(jnp.int32, z.shape, 1)
    low = lane < HEAD_DIM
    sw = pltpu.roll(z, HEAD_DIM, 1)
    return jnp.where(low, z, sw), jnp.where(low, sw, z)


def _ada_kernel(c_ref, w_ref, b_ref, o_ref):
    a = jax.nn.silu(c_ref[...])
    o_ref[0] = jnp.dot(a, w_ref[0], preferred_element_type=F32,
                       precision=lax.Precision.HIGHEST) + b_ref[0]


def _ada_mods(cc, w_ada, b_ada):
    depth, _, n = w_ada.shape
    tn = 1536
    return pl.pallas_call(
        _ada_kernel,
        grid=(depth, n // tn),
        in_specs=[pl.BlockSpec((SUBLANES, D_MODEL), lambda l, j: (0, 0)),
                  pl.BlockSpec((1, D_MODEL, tn), lambda l, j: (l, 0, j)),
                  pl.BlockSpec((1, 1, tn), lambda l, j: (l, 0, j))],
        out_specs=pl.BlockSpec((1, SUBLANES, tn), lambda l, j: (l, 0, j)),
        out_shape=jax.ShapeDtypeStruct((depth, SUBLANES, n), F32),
        compiler_params=_cparams(2),
        name="ada_mods",
    )(cc, w_ada, b_ada.reshape(depth, 1, n))


def _inproj_kernel(x_ref, mod_ref, g_ref, w_ref, qk_ref, cos_ref, sin_ref,
                   qa_ref, kta_ref, va_ref, u_ref, qc_ref, ktc_ref, vc_ref):
    mods = mod_ref[0]
    h = _rms_mod(x_ref[...], g_ref[...], mods[0:1], mods[1:2]).astype(BF16)
    cos = cos_ref[...]
    sin = sin_ref[...]
    avg512 = _head_avg_matrix(512)

    def seg(k):
        return jnp.dot(h, w_ref[:, IN_OFFS[k]:IN_OFFS[k + 1]], preferred_element_type=F32)

    def gain(k, width):
        return qk_ref[k:k + 1, :width]

    q = _rope(_head_rms(seg(0), gain(0, 512), avg512), cos, sin)
    qa_ref[...] = (q * SCORE_SCALE).astype(BF16)
    k = _rope(_head_rms(seg(1), gain(1, 128), avg512[:128, :128]), cos, sin)
    for hh, kd in enumerate(_dup_halves(k)):
        kta_ref[0, hh] = kd.T.astype(BF16)
    for hh, vd in enumerate(_dup_halves(seg(2))):
        va_ref[0, hh] = vd.astype(BF16)
    u_ref[...] = seg(3)
    q = _rope(_head_rms(seg(4), gain(2, 512), avg512), cos, sin)
    qc_ref[...] = (q * SCORE_SCALE).astype(BF16)
    k = _rope(_head_rms(seg(5), gain(3, 512), avg512), cos, sin)
    v = seg(6)
    for hh in range(C_HEADS):
        ktc_ref[0, hh] = k[:, hh * LANES:(hh + 1) * LANES].T.astype(BF16)
        vc_ref[0, hh] = v[:, hh * LANES:(hh + 1) * LANES].astype(BF16)


def _inproj(xa, mods, norm_g, w_qkvu, qk_gain, cos_t, sin_t, batch, nblk):
    t_rows = xa.shape[0]
    s_len = nblk * TM
    row = lambda i: (i, 0)
    mod_map = lambda i: (jnp.where(i % nblk == 0, batch, i // nblk), 0, 0)
    pos = lambda i: (i % nblk, 0)
    kt_map = lambda i: (i // nblk, 0, 0, i % nblk)
    v_map = lambda i: (i // nblk, 0, i % nblk, 0)
    return pl.pallas_call(
        _inproj_kernel,
        grid=(t_rows // TM,),
        in_specs=[pl.BlockSpec((TM, D_MODEL), row),
                  pl.BlockSpec((1, 6, D_MODEL), mod_map),
                  pl.BlockSpec((1, D_MODEL), lambda i: (0, 0)),
                  pl.BlockSpec((D_MODEL, N_QKVU), lambda i: (0, 0)),
                  pl.BlockSpec((4, BRANCH_WIDTH), lambda i: (0, 0)),
                  pl.BlockSpec((TM, LANES), pos),
                  pl.BlockSpec((TM, LANES), pos)],
        out_specs=[pl.BlockSpec((TM, 512), row),
                   pl.BlockSpec((1, A_KV_HEADS, LANES, TM), kt_map),
                   pl.BlockSpec((1, A_KV_HEADS, TM, LANES), v_map),
                   pl.BlockSpec((TM, 512), row),
                   pl.BlockSpec((TM, 512), row),
                   pl.BlockSpec((1, C_HEADS, LANES, TM), kt_map),
                   pl.BlockSpec((1, C_HEADS, TM, LANES), v_map)],
        out_shape=[jax.ShapeDtypeStruct((t_rows, 512), BF16),
                   jax.ShapeDtypeStruct((batch, A_KV_HEADS, LANES, s_len), BF16),
                   jax.ShapeDtypeStruct((batch, A_KV_HEADS, s_len, LANES), BF16),
                   jax.ShapeDtypeStruct((t_rows, 512), F32),
                   jax.ShapeDtypeStruct((t_rows, 512), BF16),
                   jax.ShapeDtypeStruct((batch, C_HEADS, LANES, s_len), BF16),
                   jax.ShapeDtypeStruct((batch, C_HEADS, s_len, LANES), BF16)],
        compiler_params=_cparams(1),
        name="in_proj",
    )(xa, mods, norm_g, w_qkvu, qk_gain, cos_t, sin_t)


def _attn_kernel(*refs, diff, lam_init, n_ctx, bounds):
    if diff:
        q_ref, kt_ref, v_ref, lam_ref, gout_ref, o_ref = refs[:6]
    else:
        q_ref, kt_ref, v_ref, o_ref = refs[:4]
    lhs_sc, m_sc, l_sc, acc_sc, s0_sc, s1_sc = refs[-6:]
    bufs = (s0_sc, s1_sc)
    n_chunks = len(bounds) - 1
    j = pl.program_id(2)
    rows = 2 * TM
    lane = lax.broadcasted_iota(jnp.int32, (TM, LANES), 1)
    low = lane < HEAD_DIM
    qt = q_ref[...].astype(F32)
    lhs_sc[0:TM, :] = jnp.where(low, qt, 0.0).astype(BF16)
    lhs_sc[TM:rows, :] = jnp.where(low, 0.0, qt).astype(BF16)

    def chunk_keys(c):
        return slice(bounds[c], bounds[c + 1])

    def scores(keys):
        return jnp.dot(lhs_sc[...], kt_ref[0, 0, :, keys], preferred_element_type=F32)

    def softmax_pv(s, keys, first):
        width = s.shape[1]
        mx = s[:, 0:LANES]
        for t in range(1, width // LANES):
            mx = jnp.maximum(mx, s[:, t * LANES:(t + 1) * LANES])
        m_cur = jnp.max(mx, axis=1, keepdims=True)
        if first:
            m_next = jnp.broadcast_to(m_cur, (rows, LANES))
        else:
            m_prev = m_sc[...]
            m_next = jnp.maximum(m_prev, m_cur)
            alpha = jnp.exp2(m_prev - m_next)
        p = jnp.exp2(s - _tile_lanes(m_next, width))
        psum = p[:, 0:LANES]
        for t in range(1, width // LANES):
            psum = psum + p[:, t * LANES:(t + 1) * LANES]
        pv = jnp.dot(p.astype(BF16), v_ref[0, 0, keys, :], preferred_element_type=F32)
        if first:
            l_sc[...] = psum
            acc_sc[...] = pv
        else:
            l_sc[...] = alpha * l_sc[...] + psum
            acc_sc[...] = alpha * acc_sc[...] + pv
        m_sc[...] = m_next

    @pl.when(j == 0)
    def _():
        ctx_keys = slice(0, n_ctx)
        softmax_pv(scores(ctx_keys), ctx_keys, True)

    @pl.when(j > 0)
    def _():
        def put(c):
            keys = chunk_keys(c)
            bufs[c % 2][:, 0:keys.stop - keys.start] = scores(keys)

        def take(c):
            keys = chunk_keys(c)
            softmax_pv(bufs[c % 2][:, 0:keys.stop - keys.start], keys, c == 0)

        put(0)
        for c in range(n_chunks):
            if c + 1 < n_chunks:
                put(c + 1)
            take(c)

    o = acc_sc[...] / jnp.sum(l_sc[...], axis=1, keepdims=True)
    if diff:
        lv = lam_ref[...]
        lam = (jnp.exp(jnp.sum(lv[0:1] * lv[1:2], axis=1, keepdims=True))
               - jnp.exp(jnp.sum(lv[2:3] * lv[3:4], axis=1, keepdims=True)) + lam_init)
        od = o[0:TM] - lam * o[TM:rows]
        od = od * lax.rsqrt(jnp.mean(od * od, axis=-1, keepdims=True) + EPS)
        o_ref[...] = ((od * gout_ref[...]) * (1.0 - lam_init)).astype(o_ref.dtype)
    else:
        o_ref[...] = jnp.where(low, o[0:TM], o[TM:rows]).astype(o_ref.dtype)


def _attention(q, kt, v, batch, nblk, *, diff, lam_vecs=None, g_out=None, lam_init=0.0):
    kv_heads = kt.shape[1]
    s_len = kt.shape[3]
    tiles = BRANCH_WIDTH // LANES
    rows = 2 * TM
    q_map = lambda b, h, j: (b * nblk + j, h)
    kv_map = lambda b, h, j: (b, h * kv_heads // tiles, 0, 0)
    const = lambda b, h, j: (0, 0)
    in_specs = [pl.BlockSpec((TM, LANES), q_map),
                pl.BlockSpec((1, 1, LANES, s_len), kv_map),
                pl.BlockSpec((1, 1, s_len, LANES), kv_map)]
    args = [q, kt, v]
    if diff:
        in_specs += [pl.BlockSpec((4, HEAD_DIM), const), pl.BlockSpec((1, LANES), const)]
        args += [lam_vecs, g_out]
    n_ctx = TM
    tkl = min(SCORE_ELEMS // rows, s_len - n_ctx)
    assert (s_len - n_ctx) % tkl == 0
    bounds = [0] + list(range(n_ctx + tkl, s_len + 1, tkl))
    kern = functools.partial(_attn_kernel, diff=diff, lam_init=lam_init,
                             n_ctx=n_ctx, bounds=tuple(bounds))
    return pl.pallas_call(
        kern,
        grid=(batch, tiles, nblk),
        in_specs=in_specs,
        out_specs=pl.BlockSpec((TM, LANES), q_map),
        out_shape=jax.ShapeDtypeStruct((batch * s_len, BRANCH_WIDTH), BF16),
        scratch_shapes=([pltpu.VMEM((rows, LANES), BF16)]
                        + [pltpu.VMEM((rows, LANES), F32)] * 3
                        + [pltpu.VMEM((rows, n_ctx + tkl), F32)] * 2),
        compiler_params=_cparams(3),
        name="diff_attn" if diff else "gqa_attn",
    )(*args)


def _s5_kernel(uf_ref, ub_ref, pin_ref, pout_ref, wd_ref, a_ref, cm_ref, yf_ref, yb_ref,
               bu_sc, st_sc):
    n_tiles = BRANCH_WIDTH // LANES
    half = SUBLANES * S5_STATE
    rows = TC * SUBLANES
    tok = (SUBLANES // 2) * TC

    @pl.when(pl.program_id(0) == 0)
    def _():
        st_sc[...] = jnp.zeros_like(st_sc)

    seq = lax.broadcasted_iota(jnp.int32, (rows, LANES), 0) & (SUBLANES - 1)
    fwd = seq < SUBLANES // 2
    lhs_dir = []
    for d, ref in enumerate((uf_ref, ub_ref)):
        ud = ref[...].reshape(tok, BRANCH_WIDTH).astype(BF16)
        lhs_dir.append(jnp.dot(pin_ref[d], ud, preferred_element_type=F32).astype(BF16))
    for j in range(n_tiles):
        cs = slice(j * LANES, (j + 1) * LANES)
        lhs = jnp.concatenate([lhs_dir[0][:, cs], lhs_dir[1][:, cs]], axis=1)
        bu_sc[:, 2 * half * j:2 * half * (j + 1)] = jnp.dot(
            lhs, wd_ref[j], preferred_element_type=F32)

    for j in range(n_tiles):
        re = slice(2 * half * j, 2 * half * j + half)
        im = slice(2 * half * j + half, 2 * half * (j + 1))
        ar = a_ref[0, :, half * j:half * (j + 1)]
        ai = a_ref[1, :, half * j:half * (j + 1)]

        def step(t, carry, re=re, im=im, ar=ar, ai=ai):
            xr, xi = carry
            r = pl.multiple_of(t * SUBLANES, SUBLANES)
            nxr = ar * xr - ai * xi + bu_sc[pl.ds(r, SUBLANES), re]
            nxi = ar * xi + ai * xr + bu_sc[pl.ds(r, SUBLANES), im]
            bu_sc[pl.ds(r, SUBLANES), re] = nxr
            bu_sc[pl.ds(r, SUBLANES), im] = nxi
            return nxr, nxi

        xr, xi = lax.fori_loop(0, TC, step, (st_sc[:, re], st_sc[:, im]), unroll=True)
        st_sc[:, re] = xr
        st_sc[:, im] = xi

    ys = []
    for j in range(n_tiles):
        x = bu_sc[:, 2 * half * j:2 * half * (j + 1)].astype(BF16)
        yy = jnp.dot(x, cm_ref[j], preferred_element_type=F32)
        ys.append(jnp.where(fwd, yy[:, :LANES], yy[:, LANES:]))
    y = jnp.concatenate(ys, axis=1)
    parts = []
    rest = y
    for _ in range(2):
        part = rest.astype(BF16)
        parts.append(part)
        rest = rest - part.astype(F32)
    for d, ref in enumerate((yf_ref, yb_ref)):
        out = None
        for part in parts:
            term = jnp.dot(pout_ref[d], part, preferred_element_type=F32)
            out = term if out is None else out + term
        ref[...] = out.reshape(ref.shape)


def _scan_row_placement(n_seq):
    p = np.zeros((2, TC * 2 * n_seq, n_seq * TC), np.float32)
    for b in range(n_seq):
        for k in range(TC):
            p[0, 2 * n_seq * k + b, b * TC + k] = 1.0
            p[1, 2 * n_seq * (TC - 1 - k) + n_seq + b, b * TC + k] = 1.0
    return jnp.asarray(p, BF16), jnp.asarray(p.transpose(0, 2, 1), BF16)


def _s5_scan(u3, wd, a8, cm, n_ctx):
    batch, s_len, _ = u3.shape
    rows = TC * SUBLANES
    n_tiles = BRANCH_WIDTH // LANES
    n_state = 2 * SUBLANES * S5_STATE * n_tiles
    n_steps = s_len // TC
    ctx_steps = n_ctx // TC
    p_in, p_out = _scan_row_placement(batch)
    fwd_map = lambda g: (0, g, 0)
    bwd_map = lambda g: (0, jnp.where(g < ctx_steps, ctx_steps - 1 - g,
                                      n_steps + ctx_steps - 1 - g), 0)
    c3 = lambda g: (0, 0, 0)
    blk = (batch, TC, BRANCH_WIDTH)
    out = jax.ShapeDtypeStruct(u3.shape, F32)
    return pl.pallas_call(
        _s5_kernel,
        grid=(n_steps,),
        in_specs=[pl.BlockSpec(blk, fwd_map),
                  pl.BlockSpec(blk, bwd_map),
                  pl.BlockSpec(p_in.shape, c3),
                  pl.BlockSpec(p_out.shape, c3),
                  pl.BlockSpec(wd.shape, c3),
                  pl.BlockSpec(a8.shape, c3),
                  pl.BlockSpec(cm.shape, c3)],
        out_specs=[pl.BlockSpec(blk, fwd_map), pl.BlockSpec(blk, bwd_map)],
        out_shape=[out, out],
        scratch_shapes=[pltpu.VMEM((rows, n_state), F32),
                        pltpu.VMEM((SUBLANES, n_state), F32)],
        compiler_params=_cparams(1),
        name="s5_scan",
    )(u3, u3, p_in, p_out, wd, a8, cm)


def _zoh(lam_re, lam_im, log_dt, b_re, b_im):
    dt = jnp.exp(log_dt)[..., None]
    mag = jnp.exp(lam_re * dt)
    a_re = mag * jnp.cos(lam_im * dt)
    a_im = mag * jnp.sin(lam_im * dt)
    den = lam_re * lam_re + lam_im * lam_im
    f_re = ((a_re - 1.0) * lam_re + a_im * lam_im) / den
    f_im = (a_im * lam_re - (a_re - 1.0) * lam_im) / den
    bb_re = f_re[..., None] * b_re - f_im[..., None] * b_im
    bb_im = f_re[..., None] * b_im + f_im[..., None] * b_re
    return a_re, a_im, bb_re, bb_im


def _s5_weights(lam_re, lam_im, log_dt, b_re, b_im, c_re, c_im, batch):
    n_tiles = BRANCH_WIDTH // LANES
    gpt = S5_GROUPS // n_tiles
    a_re, a_im, bb_re, bb_im = _zoh(lam_re, lam_im, log_dt, b_re, b_im)
    eye = jnp.eye(gpt, dtype=F32)

    def drive(bb):
        t = bb.reshape(2, n_tiles, gpt, S5_STATE, S5_CH)
        w = jnp.einsum('djgpc,gh->jdgchp', t, eye)
        return w.reshape(n_tiles, 2 * gpt * S5_CH, gpt * S5_STATE)

    def read(cc):
        t = cc.reshape(2, n_tiles, gpt, S5_CH, S5_STATE)
        w = jnp.einsum('djgcp,gh->jgpdhc', t, eye)
        return w.reshape(n_tiles, gpt * S5_STATE, 2 * gpt * S5_CH)

    wd = jnp.concatenate([drive(bb_re), drive(bb_im)], axis=2).astype(BF16)
    cm = jnp.concatenate([read(c_re), read(-c_im)], axis=1).astype(BF16)

    def per_seq(a):
        return jnp.repeat(a.reshape(2, 1, S5_GROUPS * S5_STATE), batch, axis=1).reshape(
            2 * batch, S5_GROUPS * S5_STATE)

    a8 = jnp.stack([per_seq(a_re), per_seq(a_im)])
    return wd, a8, cm


def _merge_kernel(x_ref, mod_ref, g1_ref, g2_ref, ya_ref, yd_ref, yf_ref, yr_ref, u_ref,
                  dsk_ref, wglu_ref, bglu_ref, wgate_ref, wbr_ref, wout_ref, x1_ref, h2_ref):
    mods = mod_ref[0]
    x = x_ref[...]
    h = _rms_mod(x, g1_ref[...], mods[0:1], mods[1:2]).astype(BF16)
    ys = yf_ref[...] + yr_ref[...] + dsk_ref[...] * u_ref[...]
    g = jax.nn.gelu(ys)
    yb = g * jax.nn.sigmoid(
        jnp.dot(g.astype(BF16), wglu_ref[...], preferred_element_type=F32) + bglu_ref[...])
    branches = (ya_ref[...], yb.astype(BF16), yd_ref[...])
    m = None
    for k, y in enumerate(branches):
        gate = jax.nn.sigmoid(jnp.dot(h, wgate_ref[:, k * D_MODEL:(k + 1) * D_MODEL],
                                      preferred_element_type=F32))
        term = gate * jnp.dot(y, wbr_ref[k], preferred_element_type=F32)
        m = term if m is None else m + term
    y = jnp.dot(m.astype(BF16), wout_ref[...], preferred_element_type=F32)
    x1 = x + mods[2:3] * y
    x1_ref[...] = x1
    h2_ref[...] = _rms_mod(x1, g2_ref[...], mods[3:4], mods[4:5]).astype(BF16)


def _merge(xa, mods, g1, g2, ya, yd, yf, yr, u, d_skip, w_glu, b_glu, w_gate, w_branch, w_out,
           batch, nblk):
    t_rows = xa.shape[0]
    row = lambda i: (i, 0)
    mod_map = lambda i: (jnp.where(i % nblk == 0, batch, i // nblk), 0, 0)
    c2 = lambda i: (0, 0)
    c3 = lambda i: (0, 0, 0)
    return pl.pallas_call(
        _merge_kernel,
        grid=(t_rows // TM,),
        in_specs=[pl.BlockSpec((TM, D_MODEL), row),
                  pl.BlockSpec((1, 6, D_MODEL), mod_map),
                  pl.BlockSpec((1, D_MODEL), c2),
                  pl.BlockSpec((1, D_MODEL), c2),
                  pl.BlockSpec((TM, BRANCH_WIDTH), row),
                  pl.BlockSpec((TM, BRANCH_WIDTH), row),
                  pl.BlockSpec((TM, BRANCH_WIDTH), row),
                  pl.BlockSpec((TM, BRANCH_WIDTH), row),
                  pl.BlockSpec((TM, BRANCH_WIDTH), row),
                  pl.BlockSpec((1, BRANCH_WIDTH), c2),
                  pl.BlockSpec((BRANCH_WIDTH, BRANCH_WIDTH), c2),
                  pl.BlockSpec((1, BRANCH_WIDTH), c2),
                  pl.BlockSpec((D_MODEL, 3 * D_MODEL), c2),
                  pl.BlockSpec((3, BRANCH_WIDTH, D_MODEL), c3),
                  pl.BlockSpec((D_MODEL, D_MODEL), c2)],
        out_specs=[pl.BlockSpec((TM, D_MODEL), row), pl.BlockSpec((TM, D_MODEL), row)],
        out_shape=[jax.ShapeDtypeStruct((t_rows, D_MODEL), F32),
                   jax.ShapeDtypeStruct((t_rows, D_MODEL), BF16)],
        compiler_params=_cparams(1),
        name="merge",
    )(xa, mods, g1, g2, ya, yd, yf, yr, u, d_skip, w_glu, b_glu, w_gate, w_branch, w_out)


def _ffn_kernel(h_ref, hp_ref, hn_ref, x1_ref, mod_ref, wup_ref, cw_ref, wdn_ref, o_ref,
                lhs_sc, ua0_sc, ug0_sc, ua1_sc, ug1_sc, acc_sc, *, nblk, n_chunks):
    pj = pl.program_id(0) % nblk
    left_ok = (pj >= 2).astype(F32)
    right_ok = jnp.logical_and(pj != 0, pj != nblk - 1).astype(F32)
    lhs_sc[0:HALO, :] = (hp_ref[...].astype(F32) * left_ok).astype(BF16)
    lhs_sc[HALO:HALO + TM, :] = h_ref[...]
    lhs_sc[HALO + TM:2 * HALO + TM, :] = (hn_ref[...].astype(F32) * right_ok).astype(BF16)
    acc_sc[...] = jnp.zeros_like(acc_sc)

    def conv(sc, cw, base):
        return (sc[HALO - 1:HALO - 1 + TM, :] * cw[base:base + 1]
                + sc[HALO:HALO + TM, :] * cw[base + 1:base + 2]
                + sc[HALO + 1:HALO + 1 + TM, :] * cw[base + 2:base + 3]
                + cw[base + 3:base + 4])

    def up(k, bufs):
        lhs = lhs_sc[...]
        bufs[0][...] = jnp.dot(lhs, wup_ref[0, k], preferred_element_type=F32)
        bufs[1][...] = jnp.dot(lhs, wup_ref[1, k], preferred_element_type=F32)

    def down(k, bufs):
        cw = cw_ref[k]
        act = jax.nn.silu(conv(bufs[1], cw, 4)) * conv(bufs[0], cw, 0)
        acc_sc[...] += jnp.dot(act.astype(BF16), wdn_ref[k], preferred_element_type=F32)

    bufs = ((ua0_sc, ug0_sc), (ua1_sc, ug1_sc))
    up(0, bufs[0])
    for k in range(n_chunks):
        if k + 1 < n_chunks:
            up(k + 1, bufs[(k + 1) % 2])
        down(k, bufs[k % 2])
    o_ref[...] = x1_ref[...] + mod_ref[0][5:6] * acc_sc[...]


def _ffn(h2, x1, mods, wup, cw, wdn, batch, nblk, latent_only):
    t_rows = x1.shape[0]
    n_chunks = wdn.shape[0]
    per = TM // HALO
    last = t_rows // HALO - 1
    row = lambda i: (i, 0)
    mod_map = lambda i: (jnp.where(i % nblk == 0, batch, i // nblk), 0, 0)
    kern = functools.partial(_ffn_kernel, nblk=nblk, n_chunks=n_chunks)
    if latent_only:
        out_map = lambda i: ((i // nblk) * (nblk - 1) + jnp.maximum(i % nblk - 1, 0), 0)
        out_rows = t_rows - batch * TM
    else:
        out_map, out_rows = row, t_rows
    return pl.pallas_call(
        kern,
        grid=(t_rows // TM,),
        in_specs=[pl.BlockSpec((TM, D_MODEL), row),
                  pl.BlockSpec((HALO, D_MODEL), lambda i: (jnp.maximum(i * per - 1, 0), 0)),
                  pl.BlockSpec((HALO, D_MODEL), lambda i: (jnp.minimum((i + 1) * per, last), 0)),
                  pl.BlockSpec((TM, D_MODEL), row),
                  pl.BlockSpec((1, 6, D_MODEL), mod_map),
                  pl.BlockSpec(wup.shape, lambda i: (0, 0, 0, 0)),
                  pl.BlockSpec(cw.shape, lambda i: (0, 0, 0)),
                  pl.BlockSpec(wdn.shape, lambda i: (0, 0, 0))],
        out_specs=pl.BlockSpec((TM, D_MODEL), out_map),
        out_shape=jax.ShapeDtypeStruct((out_rows, D_MODEL), F32),
        scratch_shapes=([pltpu.VMEM((TM + 2 * HALO, D_MODEL), BF16)]
                        + [pltpu.VMEM((TM + 2 * HALO, FF_CHUNK), F32)] * 4
                        + [pltpu.VMEM((TM, D_MODEL), F32)]),
        compiler_params=_cparams(1),
        name="conv_ffn",
    )(h2, h2, h2, x1, mods, wup, cw, wdn)


def _rope_tables(seq, ctx):
    rows = seq // GRID_W
    row = jnp.repeat(jnp.arange(rows, dtype=F32), GRID_W)
    col = jnp.tile(jnp.arange(GRID_W, dtype=F32), rows)
    n_freq = HEAD_DIM // 4
    inv_freq = ROPE_BASE ** (-jnp.arange(n_freq, dtype=F32) / n_freq)
    ang = jnp.concatenate([row[:, None] * inv_freq, col[:, None] * inv_freq], axis=-1)
    ang = jnp.concatenate([ang, ang], axis=-1)
    cos = jnp.concatenate([jnp.ones((ctx, HEAD_DIM), F32), jnp.cos(ang)], axis=0)
    sin = jnp.concatenate([jnp.zeros((ctx, HEAD_DIM), F32), jnp.sin(ang)], axis=0)
    sign = jnp.where(jnp.arange(HEAD_DIM) < HEAD_DIM // 2, -1.0, 1.0).astype(F32)
    two = lambda t: jnp.concatenate([t, t], axis=1)
    return two(cos), two(sin * sign)


def kernel(x, c, ctx, c_ctx, w_ada, b_ada, norm_g, w_in, qk_gain, ssm_lam_re, ssm_lam_im, ssm_log_dt, ssm_b_re, ssm_b_im, ssm_c_re, ssm_c_im, ssm_d, w_glu, b_glu, diff_lam, diff_norm_g, w_branch, w_out, w_up, conv_w, conv_b, w_down):
    batch, seq, _ = x.shape
    n_ctx = ctx.shape[1]
    depth = w_in.shape[0]
    assert n_ctx == TM and seq % TM == 0 and 2 * batch == SUBLANES
    s_len = n_ctx + seq
    nblk = s_len // TM
    assert s_len % TC == 0 and D_FF % FF_CHUNK == 0
    n_chunks = D_FF // FF_CHUNK

    cos_t, sin_t = _rope_tables(seq, n_ctx)
    cc = jnp.concatenate([c, c_ctx[None, :], jnp.zeros((SUBLANES - batch - 1, D_MODEL), F32)], axis=0)
    mods_all = _ada_mods(cc, w_ada, b_ada).reshape(depth, SUBLANES, 6, D_MODEL)

    xa = jnp.concatenate([ctx, x], axis=1).reshape(batch * s_len, D_MODEL)
    for i in range(depth):
        lam_init = 0.8 - 0.6 * math.exp(-0.3 * i)
        mods = mods_all[i]
        g1 = norm_g[i, 0:1]
        g2 = norm_g[i, 1:2]
        w_i = w_in[i].astype(BF16)
        qa, kta, va, u, qc, ktc, vc = _inproj(xa, mods, g1, w_i[:, :N_QKVU],
                                              jnp.tile(qk_gain[i], (1, BRANCH_WIDTH // HEAD_DIM)),
                                              cos_t, sin_t, batch, nblk)
        ya = _attention(qa, kta, va, batch, nblk, diff=False)
        yd = _attention(qc, ktc, vc, batch, nblk, diff=True, lam_vecs=diff_lam[i],
                        g_out=diff_norm_g[i][None, :], lam_init=lam_init)

        wd, a8, cm = _s5_weights(ssm_lam_re[i], ssm_lam_im[i], ssm_log_dt[i], ssm_b_re[i],
                                 ssm_b_im[i], ssm_c_re[i], ssm_c_im[i], batch)
        yf, yr = _s5_scan(u.reshape(batch, s_len, BRANCH_WIDTH), wd, a8, cm, n_ctx)
        yf = yf.reshape(batch * s_len, BRANCH_WIDTH)
        yr = yr.reshape(batch * s_len, BRANCH_WIDTH)

        x1, h2 = _merge(xa, mods, g1, g2, ya, yd, yf, yr, u, ssm_d[i][None, :],
                        w_glu[i].astype(BF16), b_glu[i][None, :], w_i[:, N_QKVU:],
                        w_branch[i].astype(BF16), w_out[i].astype(BF16), batch, nblk)

        wup = w_up[i].astype(BF16).reshape(D_MODEL, 2, n_chunks, FF_CHUNK).transpose(1, 2, 0, 3)
        cwb = jnp.concatenate([conv_w[i], conv_b[i][None, :]], axis=0)
        cw = cwb.reshape(4, 2, n_chunks, FF_CHUNK).transpose(2, 1, 0, 3).reshape(
            n_chunks, SUBLANES, FF_CHUNK)
        wdn = w_down[i].astype(BF16).reshape(n_chunks, FF_CHUNK, D_MODEL)
        xa = _ffn(h2, x1, mods, wup, cw, wdn, batch, nblk, latent_only=(i == depth - 1))

    return xa.reshape(batch, seq, D_MODEL)
```

```python
import functools
import math

import jax
import jax.numpy as jnp
import numpy as np
from jax import lax
from jax.experimental import pallas as pl
from jax.experimental.pallas import tpu as pltpu

F32 = jnp.float32
BF16 = jnp.bfloat16

D_MODEL = 1024
HEAD_DIM = 64
GRID_W = 64
ROPE_BASE = 10000.0
EPS = 1e-6
BRANCH_WIDTH = D_MODEL // 2
A_HEADS = BRANCH_WIDTH // HEAD_DIM
A_KV_HEADS = A_HEADS // 4
C_HEADS = BRANCH_WIDTH // (2 * HEAD_DIM)
S5_CH = 16
S5_STATE = 64
S5_GROUPS = BRANCH_WIDTH // S5_CH
D_FF = 2816
IN_SIZES = (512, 128, 128, 512, 512, 512, 512, 3 * D_MODEL)
IN_OFFS = tuple(int(v) for v in np.cumsum((0,) + IN_SIZES))
N_QKVU = IN_OFFS[7]

LANES = 128
SUBLANES = 8
TM = 256
SCORE_ELEMS = 512 * 2048
TC = 64
FF_CHUNK = 128
HALO = 16
VMEM_LIMIT = 56 * 1024 * 1024
SCORE_SCALE = HEAD_DIM ** -0.5 * math.log2(math.e)


def _cparams(n_axes, flags=None):
    return pltpu.CompilerParams(dimension_semantics=("arbitrary",) * n_axes,
                                vmem_limit_bytes=VMEM_LIMIT, flags=flags)


def _rms_mod(x, gain, shift, scale):
    y = x * lax.rsqrt(jnp.mean(x * x, axis=-1, keepdims=True) + EPS)
    return (y * gain) * (1.0 + scale) + shift


def _head_avg_matrix(width):
    r = lax.broadcasted_iota(jnp.int32, (width, width), 0) >> 6
    c = lax.broadcasted_iota(jnp.int32, (width, width), 1) >> 6
    return jnp.where(r == c, 1.0 / HEAD_DIM, 0.0).astype(BF16)


def _head_rms(z, gain, avg):
    sq = z * z
    hi = sq.astype(BF16)
    lo = (sq - hi.astype(F32)).astype(BF16)
    ms = (jnp.dot(hi, avg, preferred_element_type=F32)
          + jnp.dot(lo, avg, preferred_element_type=F32))
    return z * lax.rsqrt(ms + EPS) * gain


def _tile_lanes(t, width):
    reps = width // t.shape[1]
    return t if reps == 1 else jnp.concatenate([t] * reps, axis=1)


def _rope(z, cos, sin_signed):
    width = z.shape[1]
    lane = lax.broadcasted_iota(jnp.int32, z.shape, 1)
    first_half = (lane & (HEAD_DIM - 1)) < HEAD_DIM // 2
    rot = jnp.where(first_half,
                    pltpu.roll(z, width - HEAD_DIM // 2, 1),
                    pltpu.roll(z, HEAD_DIM // 2, 1))
    return z * _tile_lanes(cos, width) + rot * _tile_lanes(sin_signed, width)


def _dup_halves(z):
    lane = lax.broadcasted_iota(jnp.int32, z.shape, 1)
    low = lane < HEAD_DIM
    sw = pltpu.roll(z, HEAD_DIM, 1)
    return jnp.where(low, z, sw), jnp.where(low, sw, z)


def _ada_kernel(c_ref, w_ref, b_ref, o_ref):
    a = jax.nn.silu(c_ref[...])
    o_ref[0] = jnp.dot(a, w_ref[0], preferred_element_type=F32,
                       precision=lax.Precision.HIGHEST) + b_ref[0]


def _ada_mods(cc, w_ada, b_ada):
    depth, _, n = w_ada.shape
    tn = 1536
    return pl.pallas_call(
        _ada_kernel,
        grid=(depth, n // tn),
        in_specs=[pl.BlockSpec((SUBLANES, D_MODEL), lambda l, j: (0, 0)),
                  pl.BlockSpec((1, D_MODEL, tn), lambda l, j: (l, 0, j)),
                  pl.BlockSpec((1, 1, tn), lambda l, j: (l, 0, j))],
        out_specs=pl.BlockSpec((1, SUBLANES, tn), lambda l, j: (l, 0, j)),
        out_shape=jax.ShapeDtypeStruct((depth, SUBLANES, n), F32),
        compiler_params=_cparams(2),
        name="ada_mods",
    )(cc, w_ada, b_ada.reshape(depth, 1, n))


def _inproj_kernel(x_ref, mod_ref, g_ref, w_ref, qk_ref, cos_ref, sin_ref,
                   qa_ref, kta_ref, va_ref, u_ref, qc_ref, ktc_ref, vc_ref):
    mods = mod_ref[0]
    h = _rms_mod(x_ref[...], g_ref[...], mods[0:1], mods[1:2]).astype(BF16)
    cos = cos_ref[...]
    sin = sin_ref[...]
    avg512 = _head_avg_matrix(512)

    def seg(k):
        return jnp.dot(h, w_ref[:, IN_OFFS[k]:IN_OFFS[k + 1]], preferred_element_type=F32)

    def gain(k, width):
        return qk_ref[k:k + 1, :width]

    q = _rope(_head_rms(seg(0), gain(0, 512), avg512), cos, sin)
    qa_ref[...] = (q * SCORE_SCALE).astype(BF16)
    k = _rope(_head_rms(seg(1), gain(1, 128), avg512[:128, :128]), cos, sin)
    for hh, kd in enumerate(_dup_halves(k)):
        kta_ref[0, hh] = kd.T.astype(BF16)
    for hh, vd in enumerate(_dup_halves(seg(2))):
        va_ref[0, hh] = vd.astype(BF16)
    u_ref[...] = seg(3)
    q = _rope(_head_rms(seg(4), gain(2, 512), avg512), cos, sin)
    qc_ref[...] = (q * SCORE_SCALE).astype(BF16)
    k = _rope(_head_rms(seg(5), gain(3, 512), avg512), cos, sin)
    v = seg(6)
    for hh in range(C_HEADS):
        ktc_ref[0, hh] = k[:, hh * LANES:(hh + 1) * LANES].T.astype(BF16)
        vc_ref[0, hh] = v[:, hh * LANES:(hh + 1) * LANES].astype(BF16)


def _inproj(xa, mods, norm_g, w_qkvu, qk_gain, cos_t, sin_t, batch, nblk):
    t_rows = xa.shape[0]
    s_len = nblk * TM
    row = lambda i: (i, 0)
    mod_map = lambda i: (jnp.where(i % nblk == 0, batch, i // nblk), 0, 0)
    pos = lambda i: (i % nblk, 0)
    kt_map = lambda i: (i // nblk, 0, 0, i % nblk)
    v_map = lambda i: (i // nblk, 0, i % nblk, 0)
    return pl.pallas_call(
        _inproj_kernel,
        grid=(t_rows // TM,),
        in_specs=[pl.BlockSpec((TM, D_MODEL), row),
                  pl.BlockSpec((1, 6, D_MODEL), mod_map),
                  pl.BlockSpec((1, D_MODEL), lambda i: (0, 0)),
                  pl.BlockSpec((D_MODEL, N_QKVU), lambda i: (0, 0)),
                  pl.BlockSpec((4, BRANCH_WIDTH), lambda i: (0, 0)),
                  pl.BlockSpec((TM, LANES), pos),
                  pl.BlockSpec((TM, LANES), pos)],
        out_specs=[pl.BlockSpec((TM, 512), row),
                   pl.BlockSpec((1, A_KV_HEADS, LANES, TM), kt_map),
                   pl.BlockSpec((1, A_KV_HEADS, TM, LANES), v_map),
                   pl.BlockSpec((TM, 512), row),
                   pl.BlockSpec((TM, 512), row),
                   pl.BlockSpec((1, C_HEADS, LANES, TM), kt_map),
                   pl.BlockSpec((1, C_HEADS, TM, LANES), v_map)],
        out_shape=[jax.ShapeDtypeStruct((t_rows, 512), BF16),
                   jax.ShapeDtypeStruct((batch, A_KV_HEADS, LANES, s_len), BF16),
                   jax.ShapeDtypeStruct((batch, A_KV_HEADS, s_len, LANES), BF16),
                   jax.ShapeDtypeStruct((t_rows, 512), F32),
                   jax.ShapeDtypeStruct((t_rows, 512), BF16),
                   jax.ShapeDtypeStruct((batch, C_HEADS, LANES, s_len), BF16),
                   jax.ShapeDtypeStruct((batch, C_HEADS, s_len, LANES), BF16)],
        compiler_params=_cparams(1),
        name="in_proj",
    )(xa, mods, norm_g, w_qkvu, qk_gain, cos_t, sin_t)


def _attn_kernel(*refs, diff, lam_init, n_ctx, bounds):
    if diff:
        q_ref, kt_ref, v_ref, lam_ref, gout_ref, o_ref = refs[:6]
    else:
        q_ref, kt_ref, v_ref, o_ref = refs[:4]
    lhs_sc, m_sc, l_sc, acc_sc, s0_sc, s1_sc = refs[-6:]
    bufs = (s0_sc, s1_sc)
    n_chunks = len(bounds) - 1
    j = pl.program_id(2)
    rows = 2 * TM
    lane = lax.broadcasted_iota(jnp.int32, (TM, LANES), 1)
    low = lane < HEAD_DIM
    qt = q_ref[...].astype(F32)
    lhs_sc[0:TM, :] = jnp.where(low, qt, 0.0).astype(BF16)
    lhs_sc[TM:rows, :] = jnp.where(low, 0.0, qt).astype(BF16)

    def chunk_keys(c):
        return slice(bounds[c], bounds[c + 1])

    def scores(keys):
        return jnp.dot(lhs_sc[...], kt_ref[0, 0, :, keys], preferred_element_type=F32)

    def softmax_pv(s, keys, first):
        width = s.shape[1]
        mx = s[:, 0:LANES]
        for t in range(1, width // LANES):
            mx = jnp.maximum(mx, s[:, t * LANES:(t + 1) * LANES])
        m_cur = jnp.max(mx, axis=1, keepdims=True)
        if first:
            m_next = jnp.broadcast_to(m_cur, (rows, LANES))
        else:
            m_prev = m_sc[...]
            m_next = jnp.maximum(m_prev, m_cur)
            alpha = jnp.exp2(m_prev - m_next)
        p = jnp.exp2(s - _tile_lanes(m_next, width))
        psum = p[:, 0:LANES]
        for t in range(1, width // LANES):
            psum = psum + p[:, t * LANES:(t + 1) * LANES]
        pv = jnp.dot(p.astype(BF16), v_ref[0, 0, keys, :], preferred_element_type=F32)
        if first:
            l_sc[...] = psum
            acc_sc[...] = pv
        else:
            l_sc[...] = alpha * l_sc[...] + psum
            acc_sc[...] = alpha * acc_sc[...] + pv
        m_sc[...] = m_next

    @pl.when(j == 0)
    def _():
        ctx_keys = slice(0, n_ctx)
        softmax_pv(scores(ctx_keys), ctx_keys, True)

    @pl.when(j > 0)
    def _():
        def put(c):
            keys = chunk_keys(c)
            bufs[c % 2][:, 0:keys.stop - keys.start] = scores(keys)

        def take(c):
            keys = chunk_keys(c)
            softmax_pv(bufs[c % 2][:, 0:keys.stop - keys.start], keys, c == 0)

        put(0)
        for c in range(n_chunks):
            if c + 1 < n_chunks:
                put(c + 1)
            take(c)

    o = acc_sc[...] / jnp.sum(l_sc[...], axis=1, keepdims=True)
    if diff:
        lv = lam_ref[...]
        lam = (jnp.exp(jnp.sum(lv[0:1] * lv[1:2], axis=1, keepdims=True))
               - jnp.exp(jnp.sum(lv[2:3] * lv[3:4], axis=1, keepdims=True)) + lam_init)
        od = o[0:TM] - lam * o[TM:rows]
        od = od * lax.rsqrt(jnp.mean(od * od, axis=-1, keepdims=True) + EPS)
        o_ref[...] = ((od * gout_ref[...]) * (1.0 - lam_init)).astype(o_ref.dtype)
    else:
        o_ref[...] = jnp.where(low, o[0:TM], o[TM:rows]).astype(o_ref.dtype)


def _attention(q, kt, v, batch, nblk, *, diff, lam_vecs=None, g_out=None, lam_init=0.0):
    kv_heads = kt.shape[1]
    s_len = kt.shape[3]
    tiles = BRANCH_WIDTH // LANES
    rows = 2 * TM
    q_map = lambda b, h, j: (b * nblk + j, h)
    kv_map = lambda b, h, j: (b, h * kv_heads // tiles, 0, 0)
    const = lambda b, h, j: (0, 0)
    in_specs = [pl.BlockSpec((TM, LANES), q_map),
                pl.BlockSpec((1, 1, LANES, s_len), kv_map),
                pl.BlockSpec((1, 1, s_len, LANES), kv_map)]
    args = [q, kt, v]
    if diff:
        in_specs += [pl.BlockSpec((4, HEAD_DIM), const), pl.BlockSpec((1, LANES), const)]
        args += [lam_vecs, g_out]
    n_ctx = TM
    tkl = min(SCORE_ELEMS // rows, s_len - n_ctx)
    assert (s_len - n_ctx) % tkl == 0
    bounds = [0] + list(range(n_ctx + tkl, s_len + 1, tkl))
    kern = functools.partial(_attn_kernel, diff=diff, lam_init=lam_init,
                             n_ctx=n_ctx, bounds=tuple(bounds))
    return pl.pallas_call(
        kern,
        grid=(batch, tiles, nblk),
        in_specs=in_specs,
        out_specs=pl.BlockSpec((TM, LANES), q_map),
        out_shape=jax.ShapeDtypeStruct((batch * s_len, BRANCH_WIDTH), BF16),
        scratch_shapes=([pltpu.VMEM((rows, LANES), BF16)]
                        + [pltpu.VMEM((rows, LANES), F32)] * 3
                        + [pltpu.VMEM((rows, n_ctx + tkl), F32)] * 2),
        compiler_params=_cparams(3),
        name="diff_attn" if diff else "gqa_attn",
    )(*args)


def _s5_kernel(uf_ref, ub_ref, pin_ref, pout_ref, wd_ref, a_ref, cm_ref, yf_ref, yb_ref,
               bu_sc, st_sc):
    n_tiles = BRANCH_WIDTH // LANES
    half = SUBLANES * S5_STATE
    rows = TC * SUBLANES
    tok = (SUBLANES // 2) * TC

    @pl.when(pl.program_id(0) == 0)
    def _():
        st_sc[...] = jnp.zeros_like(st_sc)

    seq = lax.broadcasted_iota(jnp.int32, (rows, LANES), 0) & (SUBLANES - 1)
    fwd = seq < SUBLANES // 2
    lhs_dir = []
    for d, ref in enumerate((uf_ref, ub_ref)):
        ud = ref[...].reshape(tok, BRANCH_WIDTH).astype(BF16)
        lhs_dir.append(jnp.dot(pin_ref[d], ud, preferred_element_type=F32).astype(BF16))
    for j in range(n_tiles):
        cs = slice(j * LANES, (j + 1) * LANES)
        lhs = jnp.concatenate([lhs_dir[0][:, cs], lhs_dir[1][:, cs]], axis=1)
        bu_sc[:, 2 * half * j:2 * half * (j + 1)] = jnp.dot(
            lhs, wd_ref[j], preferred_element_type=F32)

    for j in range(n_tiles):
        re = slice(2 * half * j, 2 * half * j + half)
        im = slice(2 * half * j + half, 2 * half * (j + 1))
        ar = a_ref[0, :, half * j:half * (j + 1)]
        ai = a_ref[1, :, half * j:half * (j + 1)]

        def step(t, carry, re=re, im=im, ar=ar, ai=ai):
            xr, xi = carry
            r = pl.multiple_of(t * SUBLANES, SUBLANES)
            nxr = ar * xr - ai * xi + bu_sc[pl.ds(r, SUBLANES), re]
            nxi = ar * xi + ai * xr + bu_sc[pl.ds(r, SUBLANES), im]
            bu_sc[pl.ds(r, SUBLANES), re] = nxr
            bu_sc[pl.ds(r, SUBLANES), im] = nxi
            return nxr, nxi

        xr, xi = lax.fori_loop(0, TC, step, (st_sc[:, re], st_sc[:, im]), unroll=True)
        st_sc[:, re] = xr
        st_sc[:, im] = xi

    ys = []
    for j in range(n_tiles):
        x = bu_sc[:, 2 * half * j:2 * half * (j + 1)].astype(BF16)
        yy = jnp.dot(x, cm_ref[j], preferred_element_type=F32)
        ys.append(jnp.where(fwd, yy[:, :LANES], yy[:, LANES:]))
    y = jnp.concatenate(ys, axis=1)
    parts = []
    rest = y
    for _ in range(2):
        part = rest.astype(BF16)
        parts.append(part)
        rest = rest - part.astype(F32)
    for d, ref in enumerate((yf_ref, yb_ref)):
        out = None
        for part in parts:
            term = jnp.dot(pout_ref[d], part, preferred_element_type=F32)
            out = term if out is None else out + term
        ref[...] = out.reshape(ref.shape)


def _scan_row_placement(n_seq):
    p = np.zeros((2, TC * 2 * n_seq, n_seq * TC), np.float32)
    for b in range(n_seq):
        for k in range(TC):
            p[0, 2 * n_seq * k + b, b * TC + k] = 1.0
            p[1, 2 * n_seq * (TC - 1 - k) + n_seq + b, b * TC + k] = 1.0
    return jnp.asarray(p, BF16), jnp.asarray(p.transpose(0, 2, 1), BF16)


def _s5_scan(u3, wd, a8, cm, n_ctx):
    batch, s_len, _ = u3.shape
    rows = TC * SUBLANES
    n_tiles = BRANCH_WIDTH // LANES
    n_state = 2 * SUBLANES * S5_STATE * n_tiles
    n_steps = s_len // TC
    ctx_steps = n_ctx // TC
    p_in, p_out = _scan_row_placement(batch)
    fwd_map = lambda g: (0, g, 0)
    bwd_map = lambda g: (0, jnp.where(g < ctx_steps, ctx_steps - 1 - g,
                                      n_steps + ctx_steps - 1 - g), 0)
    c3 = lambda g: (0, 0, 0)
    blk = (batch, TC, BRANCH_WIDTH)
    out = jax.ShapeDtypeStruct(u3.shape, F32)
    return pl.pallas_call(
        _s5_kernel,
        grid=(n_steps,),
        in_specs=[pl.BlockSpec(blk, fwd_map),
                  pl.BlockSpec(blk, bwd_map),
                  pl.BlockSpec(p_in.shape, c3),
                  pl.BlockSpec(p_out.shape, c3),
                  pl.BlockSpec(wd.shape, c3),
                  pl.BlockSpec(a8.shape, c3),
                  pl.BlockSpec(cm.shape, c3)],
        out_specs=[pl.BlockSpec(blk, fwd_map), pl.BlockSpec(blk, bwd_map)],
        out_shape=[out, out],
        scratch_shapes=[pltpu.VMEM((rows, n_state), F32),
                        pltpu.VMEM((SUBLANES, n_state), F32)],
        compiler_params=_cparams(1),
        name="s5_scan",
    )(u3, u3, p_in, p_out, wd, a8, cm)


def _zoh(lam_re, lam_im, log_dt, b_re, b_im):
    dt = jnp.exp(log_dt)[..., None]
    mag = jnp.exp(lam_re * dt)
    a_re = mag * jnp.cos(lam_im * dt)
    a_im = mag * jnp.sin(lam_im * dt)
    den = lam_re * lam_re + lam_im * lam_im
    f_re = ((a_re - 1.0) * lam_re + a_im * lam_im) / den
    f_im = (a_im * lam_re - (a_re - 1.0) * lam_im) / den
    bb_re = f_re[..., None] * b_re - f_im[..., None] * b_im
    bb_im = f_re[..., None] * b_im + f_im[..., None] * b_re
    return a_re, a_im, bb_re, bb_im


def _s5_weights(lam_re, lam_im, log_dt, b_re, b_im, c_re, c_im, batch):
    n_tiles = BRANCH_WIDTH // LANES
    gpt = S5_GROUPS // n_tiles
    a_re, a_im, bb_re, bb_im = _zoh(lam_re, lam_im, log_dt, b_re, b_im)
    eye = jnp.eye(gpt, dtype=F32)

    def drive(bb):
        t = bb.reshape(2, n_tiles, gpt, S5_STATE, S5_CH)
        w = jnp.einsum('djgpc,gh->jdgchp', t, eye)
        return w.reshape(n_tiles, 2 * gpt * S5_CH, gpt * S5_STATE)

    def read(cc):
        t = cc.reshape(2, n_tiles, gpt, S5_CH, S5_STATE)
        w = jnp.einsum('djgcp,gh->jgpdhc', t, eye)
        return w.reshape(n_tiles, gpt * S5_STATE, 2 * gpt * S5_CH)

    wd = jnp.concatenate([drive(bb_re), drive(bb_im)], axis=2).astype(BF16)
    cm = jnp.concatenate([read(c_re), read(-c_im)], axis=1).astype(BF16)

    def per_seq(a):
        return jnp.repeat(a.reshape(2, 1, S5_GROUPS * S5_STATE), batch, axis=1).reshape(
            2 * batch, S5_GROUPS * S5_STATE)

    a8 = jnp.stack([per_seq(a_re), per_seq(a_im)])
    return wd, a8, cm


def _merge_kernel(x_ref, mod_ref, g1_ref, g2_ref, ya_ref, yd_ref, yf_ref, yr_ref, u_ref,
                  dsk_ref, wglu_ref, bglu_ref, wgate_ref, wbr_ref, wout_ref, x1_ref, h2_ref):
    mods = mod_ref[0]
    x = x_ref[...]
    h = _rms_mod(x, g1_ref[...], mods[0:1], mods[1:2]).astype(BF16)
    ys = yf_ref[...] + yr_ref[...] + dsk_ref[...] * u_ref[...]
    g = jax.nn.gelu(ys)
    yb = g * jax.nn.sigmoid(
        jnp.dot(g.astype(BF16), wglu_ref[...], preferred_element_type=F32) + bglu_ref[...])
    branches = (ya_ref[...], yb.astype(BF16), yd_ref[...])
    m = None
    for k, y in enumerate(branches):
        gate = jax.nn.sigmoid(jnp.dot(h, wgate_ref[:, k * D_MODEL:(k + 1) * D_MODEL],
                                      preferred_element_type=F32))
        term = gate * jnp.dot(y, wbr_ref[k], preferred_element_type=F32)
        m = term if m is None else m + term
    y = jnp.dot(m.astype(BF16), wout_ref[...], preferred_element_type=F32)
    x1 = x + mods[2:3] * y
    x1_ref[...] = x1
    h2_ref[...] = _rms_mod(x1, g2_ref[...], mods[3:4], mods[4:5]).astype(BF16)


def _merge(xa, mods, g1, g2, ya, yd, yf, yr, u, d_skip, w_glu, b_glu, w_gate, w_branch, w_out,
           batch, nblk):
    t_rows = xa.shape[0]
    row = lambda i: (i, 0)
    mod_map = lambda i: (jnp.where(i % nblk == 0, batch, i // nblk), 0, 0)
    c2 = lambda i: (0, 0)
    c3 = lambda i: (0, 0, 0)
    return pl.pallas_call(
        _merge_kernel,
        grid=(t_rows // TM,),
        in_specs=[pl.BlockSpec((TM, D_MODEL), row),
                  pl.BlockSpec((1, 6, D_MODEL), mod_map),
                  pl.BlockSpec((1, D_MODEL), c2),
                  pl.BlockSpec((1, D_MODEL), c2),
                  pl.BlockSpec((TM, BRANCH_WIDTH), row),
                  pl.BlockSpec((TM, BRANCH_WIDTH), row),
                  pl.BlockSpec((TM, BRANCH_WIDTH), row),
                  pl.BlockSpec((TM, BRANCH_WIDTH), row),
                  pl.BlockSpec((TM, BRANCH_WIDTH), row),
                  pl.BlockSpec((1, BRANCH_WIDTH), c2),
                  pl.BlockSpec((BRANCH_WIDTH, BRANCH_WIDTH), c2),
                  pl.BlockSpec((1, BRANCH_WIDTH), c2),
                  pl.BlockSpec((D_MODEL, 3 * D_MODEL), c2),
                  pl.BlockSpec((3, BRANCH_WIDTH, D_MODEL), c3),
                  pl.BlockSpec((D_MODEL, D_MODEL), c2)],
        out_specs=[pl.BlockSpec((TM, D_MODEL), row), pl.BlockSpec((TM, D_MODEL), row)],
        out_shape=[jax.ShapeDtypeStruct((t_rows, D_MODEL), F32),
                   jax.ShapeDtypeStruct((t_rows, D_MODEL), BF16)],
        compiler_params=_cparams(1),
        name="merge",
    )(xa, mods, g1, g2, ya, yd, yf, yr, u, d_skip, w_glu, b_glu, w_gate, w_branch, w_out)


def _ffn_kernel(h_ref, hp_ref, hn_ref, x1_ref, mod_ref, wup_ref, cw_ref, wdn_ref, o_ref,
                lhs_sc, u0_sc, u1_sc, act_sc, *, nblk, n_chunks):
    pj = pl.program_id(0) % nblk
    left_ok = (pj >= 2).astype(F32)
    right_ok = jnp.logical_and(pj != 0, pj != nblk - 1).astype(F32)
    lhs_sc[0:HALO, :] = (hp_ref[...].astype(F32) * left_ok).astype(BF16)
    lhs_sc[HALO:HALO + TM, :] = h_ref[...]
    lhs_sc[HALO + TM:2 * HALO + TM, :] = (hn_ref[...].astype(F32) * right_ok).astype(BF16)

    def up(k, buf):
        buf[...] = jnp.dot(lhs_sc[...], wup_ref[k], preferred_element_type=F32)

    def activate(k, buf):
        cw = cw_ref[k]
        c = (buf[HALO - 1:HALO - 1 + TM, :] * cw[0:1]
             + buf[HALO:HALO + TM, :] * cw[1:2]
             + buf[HALO + 1:HALO + 1 + TM, :] * cw[2:3]
             + cw[3:4])
        act = jax.nn.silu(c[:, FF_CHUNK:]) * c[:, :FF_CHUNK]
        act_sc[:, k * FF_CHUNK:(k + 1) * FF_CHUNK] = act.astype(BF16)

    bufs = (u0_sc, u1_sc)
    up(0, bufs[0])
    for k in range(n_chunks):
        if k + 1 < n_chunks:
            up(k + 1, bufs[(k + 1) % 2])
        activate(k, bufs[k % 2])
    y = jnp.dot(act_sc[...], wdn_ref[...], preferred_element_type=F32)
    o_ref[...] = x1_ref[...] + mod_ref[0][5:6] * y


def _ffn(h2, x1, mods, wup, cw, wdn, batch, nblk, latent_only):
    t_rows = x1.shape[0]
    n_chunks = wup.shape[0]
    per = TM // HALO
    last = t_rows // HALO - 1
    row = lambda i: (i, 0)
    mod_map = lambda i: (jnp.where(i % nblk == 0, batch, i // nblk), 0, 0)
    kern = functools.partial(_ffn_kernel, nblk=nblk, n_chunks=n_chunks)
    if latent_only:
        out_map = lambda i: ((i // nblk) * (nblk - 1) + jnp.maximum(i % nblk - 1, 0), 0)
        out_rows = t_rows - batch * TM
    else:
        out_map, out_rows = row, t_rows
    return pl.pallas_call(
        kern,
        grid=(t_rows // TM,),
        in_specs=[pl.BlockSpec((TM, D_MODEL), row),
                  pl.BlockSpec((HALO, D_MODEL), lambda i: (jnp.maximum(i * per - 1, 0), 0)),
                  pl.BlockSpec((HALO, D_MODEL), lambda i: (jnp.minimum((i + 1) * per, last), 0)),
                  pl.BlockSpec((TM, D_MODEL), row),
                  pl.BlockSpec((1, 6, D_MODEL), mod_map),
                  pl.BlockSpec(wup.shape, lambda i: (0, 0, 0)),
                  pl.BlockSpec(cw.shape, lambda i: (0, 0, 0)),
                  pl.BlockSpec(wdn.shape, lambda i: (0, 0))],
        out_specs=pl.BlockSpec((TM, D_MODEL), out_map),
        out_shape=jax.ShapeDtypeStruct((out_rows, D_MODEL), F32),
        scratch_shapes=([pltpu.VMEM((TM + 2 * HALO, D_MODEL), BF16)]
                        + [pltpu.VMEM((TM + 2 * HALO, 2 * FF_CHUNK), F32)] * 2
                        + [pltpu.VMEM((TM, D_FF), BF16)]),
        compiler_params=_cparams(1),
        name="conv_ffn",
    )(h2, h2, h2, x1, mods, wup, cw, wdn)


def _rope_tables(seq, ctx):
    rows = seq // GRID_W
    row = jnp.repeat(jnp.arange(rows, dtype=F32), GRID_W)
    col = jnp.tile(jnp.arange(GRID_W, dtype=F32), rows)
    n_freq = HEAD_DIM // 4
    inv_freq = ROPE_BASE ** (-jnp.arange(n_freq, dtype=F32) / n_freq)
    ang = jnp.concatenate([row[:, None] * inv_freq, col[:, None] * inv_freq], axis=-1)
    ang = jnp.concatenate([ang, ang], axis=-1)
    cos = jnp.concatenate([jnp.ones((ctx, HEAD_DIM), F32), jnp.cos(ang)], axis=0)
    sin = jnp.concatenate([jnp.zeros((ctx, HEAD_DIM), F32), jnp.sin(ang)], axis=0)
    sign = jnp.where(jnp.arange(HEAD_DIM) < HEAD_DIM // 2, -1.0, 1.0).astype(F32)
    two = lambda t: jnp.concatenate([t, t], axis=1)
    return two(cos), two(sin * sign)


def kernel(x, c, ctx, c_ctx, w_ada, b_ada, norm_g, w_in, qk_gain, ssm_lam_re, ssm_lam_im, ssm_log_dt, ssm_b_re, ssm_b_im, ssm_c_re, ssm_c_im, ssm_d, w_glu, b_glu, diff_lam, diff_norm_g, w_branch, w_out, w_up, conv_w, conv_b, w_down):
    batch, seq, _ = x.shape
    n_ctx = ctx.shape[1]
    depth = w_in.shape[0]
    assert n_ctx == TM and seq % TM == 0 and 2 * batch == SUBLANES
    s_len = n_ctx + seq
    nblk = s_len // TM
    assert s_len % TC == 0 and D_FF % FF_CHUNK == 0
    n_chunks = D_FF // FF_CHUNK

    cos_t, sin_t = _rope_tables(seq, n_ctx)
    cc = jnp.concatenate([c, c_ctx[None, :], jnp.zeros((SUBLANES - batch - 1, D_MODEL), F32)], axis=0)
    mods_all = _ada_mods(cc, w_ada, b_ada).reshape(depth, SUBLANES, 6, D_MODEL)

    xa = jnp.concatenate([ctx, x], axis=1).reshape(batch * s_len, D_MODEL)
    for i in range(depth):
        lam_init = 0.8 - 0.6 * math.exp(-0.3 * i)
        mods = mods_all[i]
        g1 = norm_g[i, 0:1]
        g2 = norm_g[i, 1:2]
        w_i = w_in[i].astype(BF16)
        qa, kta, va, u, qc, ktc, vc = _inproj(xa, mods, g1, w_i[:, :N_QKVU],
                                              jnp.tile(qk_gain[i], (1, BRANCH_WIDTH // HEAD_DIM)),
                                              cos_t, sin_t, batch, nblk)
        ya = _attention(qa, kta, va, batch, nblk, diff=False)
        yd = _attention(qc, ktc, vc, batch, nblk, diff=True, lam_vecs=diff_lam[i],
                        g_out=diff_norm_g[i][None, :], lam_init=lam_init)

        wd, a8, cm = _s5_weights(ssm_lam_re[i], ssm_lam_im[i], ssm_log_dt[i], ssm_b_re[i],
                                 ssm_b_im[i], ssm_c_re[i], ssm_c_im[i], batch)
        yf, yr = _s5_scan(u.reshape(batch, s_len, BRANCH_WIDTH), wd, a8, cm, n_ctx)
        yf = yf.reshape(batch * s_len, BRANCH_WIDTH)
        yr = yr.reshape(batch * s_len, BRANCH_WIDTH)

        x1, h2 = _merge(xa, mods, g1, g2, ya, yd, yf, yr, u, ssm_d[i][None, :],
                        w_glu[i].astype(BF16), b_glu[i][None, :], w_i[:, N_QKVU:],
                        w_branch[i].astype(BF16), w_out[i].astype(BF16), batch, nblk)

        wup = w_up[i].astype(BF16).reshape(D_MODEL, 2, n_chunks, FF_CHUNK).transpose(
            2, 0, 1, 3).reshape(n_chunks, D_MODEL, 2 * FF_CHUNK)
        cwb = jnp.concatenate([conv_w[i], conv_b[i][None, :]], axis=0)
        cw = cwb.reshape(4, 2, n_chunks, FF_CHUNK).transpose(2, 0, 1, 3).reshape(
            n_chunks, 4, 2 * FF_CHUNK)
        cw = jnp.pad(cw, ((0, 0), (0, SUBLANES - 4), (0, 0)))
        wdn = w_down[i].astype(BF16)
        xa = _ffn(h2, x1, mods, wup, cw, wdn, batch, nblk, latent_only=(i == depth - 1))

    return xa.reshape(batch, seq, D_MODEL)
```

```python
import functools
import math

import jax
import jax.numpy as jnp
import numpy as np
from jax import lax
from jax.experimental import pallas as pl
from jax.experimental.pallas import tpu as pltpu

F32 = jnp.float32
BF16 = jnp.bfloat16

D_MODEL = 1024
HEAD_DIM = 64
GRID_W = 64
ROPE_BASE = 10000.0
EPS = 1e-6
BRANCH_WIDTH = D_MODEL // 2
A_HEADS = BRANCH_WIDTH // HEAD_DIM
A_KV_HEADS = A_HEADS // 4
C_HEADS = BRANCH_WIDTH // (2 * HEAD_DIM)
S5_CH = 16
S5_STATE = 64
S5_GROUPS = BRANCH_WIDTH // S5_CH
D_FF = 2816
IN_SIZES = (512, 128, 128, 512, 512, 512, 512, 3 * D_MODEL)
IN_OFFS = tuple(int(v) for v in np.cumsum((0,) + IN_SIZES))
N_QKVU = IN_OFFS[7]

LANES = 128
SUBLANES = 8
TM = 256
SCORE_KEYS = 2048
TC = 64
FF_CHUNK = 128
HALO = 16
ADA_COLS = 1536
V7X_VMEM_BYTES = 64 * 1024 * 1024
VMEM_LIMIT = V7X_VMEM_BYTES * 7 // 8
SCORE_SCALE = HEAD_DIM ** -0.5 * math.log2(math.e)


def _cparams(n_axes):
    return pltpu.CompilerParams(dimension_semantics=("arbitrary",) * n_axes,
                                vmem_limit_bytes=VMEM_LIMIT)


def _rms_mod(x, gain, shift, scale):
    y = x * lax.rsqrt(jnp.mean(x * x, axis=-1, keepdims=True) + EPS)
    return (y * gain) * (1.0 + scale) + shift


def _head_avg_matrix(width):
    shift = HEAD_DIM.bit_length() - 1
    r = lax.broadcasted_iota(jnp.int32, (width, width), 0) >> shift
    c = lax.broadcasted_iota(jnp.int32, (width, width), 1) >> shift
    return jnp.where(r == c, 1.0 / HEAD_DIM, 0.0).astype(BF16)


def _head_rms(z, gain, avg):
    sq = z * z
    hi = sq.astype(BF16)
    lo = (sq - hi.astype(F32)).astype(BF16)
    ms = (jnp.dot(hi, avg, preferred_element_type=F32)
          + jnp.dot(lo, avg, preferred_element_type=F32))
    return z * lax.rsqrt(ms + EPS) * gain


def _tile_lanes(t, width):
    reps = width // t.shape[1]
    return t if reps == 1 else jnp.concatenate([t] * reps, axis=1)


def _rope(z, cos, sin_signed):
    width = z.shape[1]
    lane = lax.broadcasted_iota(jnp.int32, z.shape, 1)
    first_half = (lane & (HEAD_DIM - 1)) < HEAD_DIM // 2
    rot = jnp.where(first_half,
                    pltpu.roll(z, width - HEAD_DIM // 2, 1),
                    pltpu.roll(z, HEAD_DIM // 2, 1))
    return z * _tile_lanes(cos, width) + rot * _tile_lanes(sin_signed, width)


def _dup_halves(z):
    lane = lax.broadcasted_iota(jnp.int32, z.shape, 1)
    low = lane < HEAD_DIM
    sw = pltpu.roll(z, HEAD_DIM, 1)
    return jnp.where(low, z, sw), jnp.where(low, sw, z)


def _ada_kernel(c_ref, w_ref, b_ref, o_ref):
    a = jax.nn.silu(c_ref[...])
    o_ref[0] = jnp.dot(a, w_ref[0], preferred_element_type=F32,
                       precision=lax.Precision.HIGHEST) + b_ref[0]


def _ada_mods(cc, w_ada, b_ada):
    depth, _, n = w_ada.shape
    tn = ADA_COLS
    assert n % tn == 0
    return pl.pallas_call(
        _ada_kernel,
        grid=(depth, n // tn),
        in_specs=[pl.BlockSpec((SUBLANES, D_MODEL), lambda l, j: (0, 0)),
                  pl.BlockSpec((1, D_MODEL, tn), lambda l, j: (l, 0, j)),
                  pl.BlockSpec((1, 1, tn), lambda l, j: (l, 0, j))],
        out_specs=pl.BlockSpec((1, SUBLANES, tn), lambda l, j: (l, 0, j)),
        out_shape=jax.ShapeDtypeStruct((depth, SUBLANES, n), F32),
        compiler_params=_cparams(2),
        name="ada_mods",
    )(cc, w_ada, b_ada.reshape(depth, 1, n))


def _stream_specs(x_lat, x_ctx, nblk):
    blk = (TM, D_MODEL)
    if x_ctx is None:
        return ([pl.BlockSpec(blk, lambda i: (i, 0)),
                 pl.BlockSpec(blk, lambda i: ((i // nblk) * nblk, 0))], [x_lat, x_lat])
    lat_map = lambda i: ((i // nblk) * (nblk - 1) + jnp.maximum(i % nblk - 1, 0), 0)
    return ([pl.BlockSpec(blk, lat_map), pl.BlockSpec(blk, lambda i: (i // nblk, 0))],
            [x_lat, x_ctx])


def _stream_tile(x_ref, xc_ref, nblk):
    return jnp.where(pl.program_id(0) % nblk == 0, xc_ref[...], x_ref[...])


def _inproj_kernel(x_ref, xc_ref, mod_ref, g_ref, w_ref, qk_ref, cos_ref, sin_ref,
                   qa_ref, kta_ref, va_ref, u_ref, qc_ref, ktc_ref, vc_ref, *, nblk):
    mods = mod_ref[0]
    x = _stream_tile(x_ref, xc_ref, nblk)
    h = _rms_mod(x, g_ref[...], mods[0:1], mods[1:2]).astype(BF16)
    cos = cos_ref[...]
    sin = sin_ref[...]
    avg = _head_avg_matrix(BRANCH_WIDTH)

    def seg(k):
        return jnp.dot(h, w_ref[:, IN_OFFS[k]:IN_OFFS[k + 1]], preferred_element_type=F32)

    def gain(k, width=BRANCH_WIDTH):
        return qk_ref[k:k + 1, :width]

    q = _rope(_head_rms(seg(0), gain(0), avg), cos, sin)
    qa_ref[...] = (q * SCORE_SCALE).astype(BF16)
    k = _rope(_head_rms(seg(1), gain(1, LANES), avg[:LANES, :LANES]), cos, sin)
    for hh, kd in enumerate(_dup_halves(k)):
        kta_ref[0, hh] = kd.T.astype(BF16)
    for hh, vd in enumerate(_dup_halves(seg(2))):
        va_ref[0, hh] = vd.astype(BF16)
    u_ref[...] = seg(3)
    q = _rope(_head_rms(seg(4), gain(2), avg), cos, sin)
    qc_ref[...] = (q * SCORE_SCALE).astype(BF16)
    k = _rope(_head_rms(seg(5), gain(3), avg), cos, sin)
    v = seg(6)
    for hh in range(C_HEADS):
        ktc_ref[0, hh] = k[:, hh * LANES:(hh + 1) * LANES].T.astype(BF16)
        vc_ref[0, hh] = v[:, hh * LANES:(hh + 1) * LANES].astype(BF16)


def _inproj(x_lat, x_ctx, mods, norm_g, w_qkvu, qk_gain, cos_t, sin_t, batch, nblk):
    s_len = nblk * TM
    t_rows = batch * s_len
    row = lambda i: (i, 0)
    mod_map = lambda i: (jnp.where(i % nblk == 0, batch, i // nblk), 0, 0)
    pos = lambda i: (i % nblk, 0)
    kt_map = lambda i: (i // nblk, 0, 0, i % nblk)
    v_map = lambda i: (i // nblk, 0, i % nblk, 0)
    x_specs, x_args = _stream_specs(x_lat, x_ctx, nblk)
    return pl.pallas_call(
        functools.partial(_inproj_kernel, nblk=nblk),
        grid=(t_rows // TM,),
        in_specs=x_specs + [
                  pl.BlockSpec((1, 6, D_MODEL), mod_map),
                  pl.BlockSpec((1, D_MODEL), lambda i: (0, 0)),
                  pl.BlockSpec((D_MODEL, N_QKVU), lambda i: (0, 0)),
                  pl.BlockSpec((4, BRANCH_WIDTH), lambda i: (0, 0)),
                  pl.BlockSpec((TM, LANES), pos),
                  pl.BlockSpec((TM, LANES), pos)],
        out_specs=[pl.BlockSpec((TM, BRANCH_WIDTH), row),
                   pl.BlockSpec((1, A_KV_HEADS, LANES, TM), kt_map),
                   pl.BlockSpec((1, A_KV_HEADS, TM, LANES), v_map),
                   pl.BlockSpec((TM, BRANCH_WIDTH), row),
                   pl.BlockSpec((TM, BRANCH_WIDTH), row),
                   pl.BlockSpec((1, C_HEADS, LANES, TM), kt_map),
                   pl.BlockSpec((1, C_HEADS, TM, LANES), v_map)],
        out_shape=[jax.ShapeDtypeStruct((t_rows, BRANCH_WIDTH), BF16),
                   jax.ShapeDtypeStruct((batch, A_KV_HEADS, LANES, s_len), BF16),
                   jax.ShapeDtypeStruct((batch, A_KV_HEADS, s_len, LANES), BF16),
                   jax.ShapeDtypeStruct((t_rows, BRANCH_WIDTH), F32),
                   jax.ShapeDtypeStruct((t_rows, BRANCH_WIDTH), BF16),
                   jax.ShapeDtypeStruct((batch, C_HEADS, LANES, s_len), BF16),
                   jax.ShapeDtypeStruct((batch, C_HEADS, s_len, LANES), BF16)],
        compiler_params=_cparams(1),
        name="in_proj",
    )(*x_args, mods, norm_g, w_qkvu, qk_gain, cos_t, sin_t)


def _attn_kernel(*refs, diff, lam_init, n_ctx, bounds):
    if diff:
        q_ref, kt_ref, v_ref, lam_ref, gout_ref, o_ref = refs[:6]
    else:
        q_ref, kt_ref, v_ref, o_ref = refs[:4]
    lhs_sc, m_sc, l_sc, acc_sc, s0_sc, s1_sc = refs[-6:]
    bufs = (s0_sc, s1_sc)
    n_chunks = len(bounds) - 1
    j = pl.program_id(2)
    rows = 2 * TM
    lane = lax.broadcasted_iota(jnp.int32, (TM, LANES), 1)
    low = lane < HEAD_DIM
    qt = q_ref[...].astype(F32)
    lhs_sc[0:TM, :] = jnp.where(low, qt, 0.0).astype(BF16)
    lhs_sc[TM:rows, :] = jnp.where(low, 0.0, qt).astype(BF16)

    def chunk_keys(c):
        return slice(bounds[c], bounds[c + 1])

    def scores(keys):
        return jnp.dot(lhs_sc[...], kt_ref[0, 0, :, keys], preferred_element_type=F32)

    def softmax_pv(s, keys, first):
        width = s.shape[1]
        mx = s[:, 0:LANES]
        for t in range(1, width // LANES):
            mx = jnp.maximum(mx, s[:, t * LANES:(t + 1) * LANES])
        m_cur = jnp.max(mx, axis=1, keepdims=True)
        if first:
            m_next = jnp.broadcast_to(m_cur, (rows, LANES))
        else:
            m_prev = m_sc[...]
            m_next = jnp.maximum(m_prev, m_cur)
            alpha = jnp.exp2(m_prev - m_next)
        p = jnp.exp2(s - _tile_lanes(m_next, width))
        psum = p[:, 0:LANES]
        for t in range(1, width // LANES):
            psum = psum + p[:, t * LANES:(t + 1) * LANES]
        pv = jnp.dot(p.astype(BF16), v_ref[0, 0, keys, :], preferred_element_type=F32)
        if first:
            l_sc[...] = psum
            acc_sc[...] = pv
        else:
            l_sc[...] = alpha * l_sc[...] + psum
            acc_sc[...] = alpha * acc_sc[...] + pv
        m_sc[...] = m_next

    @pl.when(j == 0)
    def _():
        ctx_keys = slice(0, n_ctx)
        softmax_pv(scores(ctx_keys), ctx_keys, True)

    @pl.when(j > 0)
    def _():
        def put(c):
            keys = chunk_keys(c)
            bufs[c % 2][:, 0:keys.stop - keys.start] = scores(keys)

        def take(c):
            keys = chunk_keys(c)
            softmax_pv(bufs[c % 2][:, 0:keys.stop - keys.start], keys, c == 0)

        put(0)
        for c in range(n_chunks):
            if c + 1 < n_chunks:
                put(c + 1)
            take(c)

    o = acc_sc[...] / jnp.sum(l_sc[...], axis=1, keepdims=True)
    if diff:
        lv = lam_ref[...]
        lam = (jnp.exp(jnp.sum(lv[0:1] * lv[1:2], axis=1, keepdims=True))
               - jnp.exp(jnp.sum(lv[2:3] * lv[3:4], axis=1, keepdims=True)) + lam_init)
        od = o[0:TM] - lam * o[TM:rows]
        od = od * lax.rsqrt(jnp.mean(od * od, axis=-1, keepdims=True) + EPS)
        o_ref[...] = ((od * gout_ref[...]) * (1.0 - lam_init)).astype(o_ref.dtype)
    else:
        o_ref[...] = jnp.where(low, o[0:TM], o[TM:rows]).astype(o_ref.dtype)


def _attention(q, kt, v, batch, nblk, *, diff, lam_vecs=None, g_out=None, lam_init=0.0):
    kv_heads = kt.shape[1]
    s_len = kt.shape[3]
    tiles = BRANCH_WIDTH // LANES
    rows = 2 * TM
    q_map = lambda b, h, j: (b * nblk + j, h)
    kv_map = lambda b, h, j: (b, h * kv_heads // tiles, 0, 0)
    const = lambda b, h, j: (0, 0)
    in_specs = [pl.BlockSpec((TM, LANES), q_map),
                pl.BlockSpec((1, 1, LANES, s_len), kv_map),
                pl.BlockSpec((1, 1, s_len, LANES), kv_map)]
    args = [q, kt, v]
    if diff:
        in_specs += [pl.BlockSpec((4, HEAD_DIM), const), pl.BlockSpec((1, LANES), const)]
        args += [lam_vecs, g_out]
    n_ctx = TM
    tkl = min(SCORE_KEYS, s_len - n_ctx)
    assert (s_len - n_ctx) % tkl == 0
    bounds = [0] + list(range(n_ctx + tkl, s_len + 1, tkl))
    kern = functools.partial(_attn_kernel, diff=diff, lam_init=lam_init,
                             n_ctx=n_ctx, bounds=tuple(bounds))
    return pl.pallas_call(
        kern,
        grid=(batch, tiles, nblk),
        in_specs=in_specs,
        out_specs=pl.BlockSpec((TM, LANES), q_map),
        out_shape=jax.ShapeDtypeStruct((batch * s_len, BRANCH_WIDTH), BF16),
        scratch_shapes=([pltpu.VMEM((rows, LANES), BF16)]
                        + [pltpu.VMEM((rows, LANES), F32)] * 3
                        + [pltpu.VMEM((rows, n_ctx + tkl), F32)] * 2),
        compiler_params=_cparams(3),
        name="diff_attn" if diff else "gqa_attn",
    )(*args)


def _s5_kernel(uf_ref, ub_ref, pin_ref, pout_ref, wd_ref, a_ref, cm_ref, yf_ref, yb_ref,
               bu_sc, st_sc):
    n_tiles = BRANCH_WIDTH // LANES
    half = SUBLANES * S5_STATE
    rows = TC * SUBLANES
    tok = (SUBLANES // 2) * TC

    @pl.when(pl.program_id(0) == 0)
    def _():
        st_sc[...] = jnp.zeros_like(st_sc)

    seq = lax.broadcasted_iota(jnp.int32, (rows, LANES), 0) & (SUBLANES - 1)
    fwd = seq < SUBLANES // 2
    lhs_dir = []
    for d, ref in enumerate((uf_ref, ub_ref)):
        ud = ref[...].reshape(tok, BRANCH_WIDTH).astype(BF16)
        lhs_dir.append(jnp.dot(pin_ref[d], ud, preferred_element_type=F32).astype(BF16))
    for j in range(n_tiles):
        cs = slice(j * LANES, (j + 1) * LANES)
        lhs = jnp.concatenate([lhs_dir[0][:, cs], lhs_dir[1][:, cs]], axis=1)
        bu_sc[:, 2 * half * j:2 * half * (j + 1)] = jnp.dot(
            lhs, wd_ref[j], preferred_element_type=F32)

    for j in range(n_tiles):
        re = slice(2 * half * j, 2 * half * j + half)
        im = slice(2 * half * j + half, 2 * half * (j + 1))
        ar = a_ref[0, :, half * j:half * (j + 1)]
        ai = a_ref[1, :, half * j:half * (j + 1)]

        def step(t, carry, re=re, im=im, ar=ar, ai=ai):
            xr, xi = carry
            r = pl.multiple_of(t * SUBLANES, SUBLANES)
            nxr = ar * xr - ai * xi + bu_sc[pl.ds(r, SUBLANES), re]
            nxi = ar * xi + ai * xr + bu_sc[pl.ds(r, SUBLANES), im]
            bu_sc[pl.ds(r, SUBLANES), re] = nxr
            bu_sc[pl.ds(r, SUBLANES), im] = nxi
            return nxr, nxi

        xr, xi = lax.fori_loop(0, TC, step, (st_sc[:, re], st_sc[:, im]), unroll=True)
        st_sc[:, re] = xr
        st_sc[:, im] = xi

    ys = []
    for j in range(n_tiles):
        x = bu_sc[:, 2 * half * j:2 * half * (j + 1)].astype(BF16)
        yy = jnp.dot(x, cm_ref[j], preferred_element_type=F32)
        ys.append(jnp.where(fwd, yy[:, :LANES], yy[:, LANES:]))
    y = jnp.concatenate(ys, axis=1)
    parts = []
    rest = y
    for _ in range(2):
        part = rest.astype(BF16)
        parts.append(part)
        rest = rest - part.astype(F32)
    for d, ref in enumerate((yf_ref, yb_ref)):
        out = None
        for part in parts:
            term = jnp.dot(pout_ref[d], part, preferred_element_type=F32)
            out = term if out is None else out + term
        ref[...] = out.reshape(ref.shape)


def _scan_row_placement(n_seq):
    p = np.zeros((2, TC * 2 * n_seq, n_seq * TC), np.float32)
    for b in range(n_seq):
        for k in range(TC):
            p[0, 2 * n_seq * k + b, b * TC + k] = 1.0
            p[1, 2 * n_seq * (TC - 1 - k) + n_seq + b, b * TC + k] = 1.0
    return jnp.asarray(p, BF16), jnp.asarray(p.transpose(0, 2, 1), BF16)


def _s5_scan(u3, wd, a8, cm, n_ctx):
    batch, s_len, _ = u3.shape
    rows = TC * SUBLANES
    n_tiles = BRANCH_WIDTH // LANES
    n_state = 2 * SUBLANES * S5_STATE * n_tiles
    n_steps = s_len // TC
    ctx_steps = n_ctx // TC
    p_in, p_out = _scan_row_placement(batch)
    fwd_map = lambda g: (0, g, 0)
    bwd_map = lambda g: (0, jnp.where(g < ctx_steps, ctx_steps - 1 - g,
                                      n_steps + ctx_steps - 1 - g), 0)
    c3 = lambda g: (0, 0, 0)
    blk = (batch, TC, BRANCH_WIDTH)
    out = jax.ShapeDtypeStruct(u3.shape, F32)
    return pl.pallas_call(
        _s5_kernel,
        grid=(n_steps,),
        in_specs=[pl.BlockSpec(blk, fwd_map),
                  pl.BlockSpec(blk, bwd_map),
                  pl.BlockSpec(p_in.shape, c3),
                  pl.BlockSpec(p_out.shape, c3),
                  pl.BlockSpec(wd.shape, c3),
                  pl.BlockSpec(a8.shape, c3),
                  pl.BlockSpec(cm.shape, c3)],
        out_specs=[pl.BlockSpec(blk, fwd_map), pl.BlockSpec(blk, bwd_map)],
        out_shape=[out, out],
        scratch_shapes=[pltpu.VMEM((rows, n_state), F32),
                        pltpu.VMEM((SUBLANES, n_state), F32)],
        compiler_params=_cparams(1),
        name="s5_scan",
    )(u3, u3, p_in, p_out, wd, a8, cm)


def _zoh(lam_re, lam_im, log_dt, b_re, b_im):
    dt = jnp.exp(log_dt)[..., None]
    mag = jnp.exp(lam_re * dt)
    a_re = mag * jnp.cos(lam_im * dt)
    a_im = mag * jnp.sin(lam_im * dt)
    den = lam_re * lam_re + lam_im * lam_im
    f_re = ((a_re - 1.0) * lam_re + a_im * lam_im) / den
    f_im = (a_im * lam_re - (a_re - 1.0) * lam_im) / den
    bb_re = f_re[..., None] * b_re - f_im[..., None] * b_im
    bb_im = f_re[..., None] * b_im + f_im[..., None] * b_re
    return a_re, a_im, bb_re, bb_im


def _s5_weights(lam_re, lam_im, log_dt, b_re, b_im, c_re, c_im, batch):
    n_tiles = BRANCH_WIDTH // LANES
    gpt = S5_GROUPS // n_tiles
    a_re, a_im, bb_re, bb_im = _zoh(lam_re, lam_im, log_dt, b_re, b_im)
    eye = jnp.eye(gpt, dtype=F32)

    def drive(bb):
        t = bb.reshape(2, n_tiles, gpt, S5_STATE, S5_CH)
        w = jnp.einsum('djgpc,gh->jdgchp', t, eye)
        return w.reshape(n_tiles, 2 * gpt * S5_CH, gpt * S5_STATE)

    def read(cc):
        t = cc.reshape(2, n_tiles, gpt, S5_CH, S5_STATE)
        w = jnp.einsum('djgcp,gh->jgpdhc', t, eye)
        return w.reshape(n_tiles, gpt * S5_STATE, 2 * gpt * S5_CH)

    wd = jnp.concatenate([drive(bb_re), drive(bb_im)], axis=2).astype(BF16)
    cm = jnp.concatenate([read(c_re), read(-c_im)], axis=1).astype(BF16)

    def per_seq(a):
        return jnp.repeat(a.reshape(2, 1, S5_GROUPS * S5_STATE), batch, axis=1).reshape(
            2 * batch, S5_GROUPS * S5_STATE)

    a8 = jnp.stack([per_seq(a_re), per_seq(a_im)])
    return wd, a8, cm


def _merge_kernel(x_ref, xc_ref, mod_ref, g1_ref, g2_ref, ya_ref, yd_ref, yf_ref, yr_ref, u_ref,
                  dsk_ref, wglu_ref, bglu_ref, wgate_ref, wbr_ref, wout_ref, x1_ref, h2_ref,
                  *, nblk):
    mods = mod_ref[0]
    x = _stream_tile(x_ref, xc_ref, nblk)
    h = _rms_mod(x, g1_ref[...], mods[0:1], mods[1:2]).astype(BF16)
    ys = yf_ref[...] + yr_ref[...] + dsk_ref[...] * u_ref[...]
    g = jax.nn.gelu(ys)
    yb = g * jax.nn.sigmoid(
        jnp.dot(g.astype(BF16), wglu_ref[...], preferred_element_type=F32) + bglu_ref[...])
    branches = (ya_ref[...], yb.astype(BF16), yd_ref[...])
    m = None
    for k, y in enumerate(branches):
        gate = jax.nn.sigmoid(jnp.dot(h, wgate_ref[:, k * D_MODEL:(k + 1) * D_MODEL],
                                      preferred_element_type=F32))
        term = gate * jnp.dot(y, wbr_ref[k], preferred_element_type=F32)
        m = term if m is None else m + term
    y = jnp.dot(m.astype(BF16), wout_ref[...], preferred_element_type=F32)
    x1 = x + mods[2:3] * y
    x1_ref[...] = x1
    h2_ref[...] = _rms_mod(x1, g2_ref[...], mods[3:4], mods[4:5]).astype(BF16)


def _merge(x_lat, x_ctx, mods, g1, g2, ya, yd, yf, yr, u, d_skip, w_glu, b_glu, w_gate,
           w_branch, w_out, batch, nblk):
    t_rows = ya.shape[0]
    row = lambda i: (i, 0)
    mod_map = lambda i: (jnp.where(i % nblk == 0, batch, i // nblk), 0, 0)
    c2 = lambda i: (0, 0)
    c3 = lambda i: (0, 0, 0)
    x_specs, x_args = _stream_specs(x_lat, x_ctx, nblk)
    return pl.pallas_call(
        functools.partial(_merge_kernel, nblk=nblk),
        grid=(t_rows // TM,),
        in_specs=x_specs + [
                  pl.BlockSpec((1, 6, D_MODEL), mod_map),
                  pl.BlockSpec((1, D_MODEL), c2),
                  pl.BlockSpec((1, D_MODEL), c2),
                  pl.BlockSpec((TM, BRANCH_WIDTH), row),
                  pl.BlockSpec((TM, BRANCH_WIDTH), row),
                  pl.BlockSpec((TM, BRANCH_WIDTH), row),
                  pl.BlockSpec((TM, BRANCH_WIDTH), row),
                  pl.BlockSpec((TM, BRANCH_WIDTH), row),
                  pl.BlockSpec((1, BRANCH_WIDTH), c2),
                  pl.BlockSpec((BRANCH_WIDTH, BRANCH_WIDTH), c2),
                  pl.BlockSpec((1, BRANCH_WIDTH), c2),
                  pl.BlockSpec((D_MODEL, 3 * D_MODEL), c2),
                  pl.BlockSpec((3, BRANCH_WIDTH, D_MODEL), c3),
                  pl.BlockSpec((D_MODEL, D_MODEL), c2)],
        out_specs=[pl.BlockSpec((TM, D_MODEL), row), pl.BlockSpec((TM, D_MODEL), row)],
        out_shape=[jax.ShapeDtypeStruct((t_rows, D_MODEL), F32),
                   jax.ShapeDtypeStruct((t_rows, D_MODEL), BF16)],
        compiler_params=_cparams(1),
        name="merge",
    )(*x_args, mods, g1, g2, ya, yd, yf, yr, u, d_skip, w_glu, b_glu, w_gate, w_branch, w_out)


def _ffn_kernel(h_ref, hp_ref, hn_ref, x1_ref, mod_ref, wup_ref, cw_ref, wdn_ref, o_ref,
                lhs_sc, u0_sc, u1_sc, act_sc, *, nblk, n_chunks):
    pj = pl.program_id(0) % nblk
    left_ok = (pj >= 2).astype(F32)
    right_ok = jnp.logical_and(pj != 0, pj != nblk - 1).astype(F32)
    lhs_sc[0:HALO, :] = (hp_ref[...].astype(F32) * left_ok).astype(BF16)
    lhs_sc[HALO:HALO + TM, :] = h_ref[...]
    lhs_sc[HALO + TM:2 * HALO + TM, :] = (hn_ref[...].astype(F32) * right_ok).astype(BF16)

    def up(k, buf):
        buf[...] = jnp.dot(lhs_sc[...], wup_ref[k], preferred_element_type=F32)

    def activate(k, buf):
        cw = cw_ref[k]
        c = (buf[HALO - 1:HALO - 1 + TM, :] * cw[0:1]
             + buf[HALO:HALO + TM, :] * cw[1:2]
             + buf[HALO + 1:HALO + 1 + TM, :] * cw[2:3]
             + cw[3:4])
        act = jax.nn.silu(c[:, FF_CHUNK:]) * c[:, :FF_CHUNK]
        act_sc[:, k * FF_CHUNK:(k + 1) * FF_CHUNK] = act.astype(BF16)

    bufs = (u0_sc, u1_sc)
    up(0, bufs[0])
    for k in range(n_chunks):
        if k + 1 < n_chunks:
            up(k + 1, bufs[(k + 1) % 2])
        activate(k, bufs[k % 2])
    y = jnp.dot(act_sc[...], wdn_ref[...], preferred_element_type=F32)
    o_ref[...] = x1_ref[...] + mod_ref[0][5:6] * y


def _ffn(h2, x1, mods, wup, cw, wdn, batch, nblk, latent_only):
    t_rows = x1.shape[0]
    n_chunks = wup.shape[0]
    per = TM // HALO
    last = t_rows // HALO - 1
    row = lambda i: (i, 0)
    mod_map = lambda i: (jnp.where(i % nblk == 0, batch, i // nblk), 0, 0)
    kern = functools.partial(_ffn_kernel, nblk=nblk, n_chunks=n_chunks)
    if latent_only:
        out_map = lambda i: ((i // nblk) * (nblk - 1) + jnp.maximum(i % nblk - 1, 0), 0)
        out_rows = t_rows - batch * TM
    else:
        out_map, out_rows = row, t_rows
    return pl.pallas_call(
        kern,
        grid=(t_rows // TM,),
        in_specs=[pl.BlockSpec((TM, D_MODEL), row),
                  pl.BlockSpec((HALO, D_MODEL), lambda i: (jnp.maximum(i * per - 1, 0), 0)),
                  pl.BlockSpec((HALO, D_MODEL), lambda i: (jnp.minimum((i + 1) * per, last), 0)),
                  pl.BlockSpec((TM, D_MODEL), row),
                  pl.BlockSpec((1, 6, D_MODEL), mod_map),
                  pl.BlockSpec(wup.shape, lambda i: (0, 0, 0)),
                  pl.BlockSpec(cw.shape, lambda i: (0, 0, 0)),
                  pl.BlockSpec(wdn.shape, lambda i: (0, 0))],
        out_specs=pl.BlockSpec((TM, D_MODEL), out_map),
        out_shape=jax.ShapeDtypeStruct((out_rows, D_MODEL), F32),
        scratch_shapes=([pltpu.VMEM((TM + 2 * HALO, D_MODEL), BF16)]
                        + [pltpu.VMEM((TM + 2 * HALO, 2 * FF_CHUNK), F32)] * 2
                        + [pltpu.VMEM((TM, D_FF), BF16)]),
        compiler_params=_cparams(1),
        name="conv_ffn",
    )(h2, h2, h2, x1, mods, wup, cw, wdn)


def _rope_tables(seq, ctx):
    rows = seq // GRID_W
    row = jnp.repeat(jnp.arange(rows, dtype=F32), GRID_W)
    col = jnp.tile(jnp.arange(GRID_W, dtype=F32), rows)
    n_freq = HEAD_DIM // 4
    inv_freq = ROPE_BASE ** (-jnp.arange(n_freq, dtype=F32) / n_freq)
    ang = jnp.concatenate([row[:, None] * inv_freq, col[:, None] * inv_freq], axis=-1)
    ang = jnp.concatenate([ang, ang], axis=-1)
    cos = jnp.concatenate([jnp.ones((ctx, HEAD_DIM), F32), jnp.cos(ang)], axis=0)
    sin = jnp.concatenate([jnp.zeros((ctx, HEAD_DIM), F32), jnp.sin(ang)], axis=0)
    sign = jnp.where(jnp.arange(HEAD_DIM) < HEAD_DIM // 2, -1.0, 1.0).astype(F32)
    two = lambda t: jnp.concatenate([t, t], axis=1)
    return two(cos), two(sin * sign)


def kernel(x, c, ctx, c_ctx, w_ada, b_ada, norm_g, w_in, qk_gain, ssm_lam_re, ssm_lam_im, ssm_log_dt, ssm_b_re, ssm_b_im, ssm_c_re, ssm_c_im, ssm_d, w_glu, b_glu, diff_lam, diff_norm_g, w_branch, w_out, w_up, conv_w, conv_b, w_down):
    batch, seq, _ = x.shape
    n_ctx = ctx.shape[1]
    depth = w_in.shape[0]
    assert n_ctx == TM and seq % TM == 0 and 2 * batch == SUBLANES
    s_len = n_ctx + seq
    nblk = s_len // TM
    assert s_len % TC == 0 and D_FF % FF_CHUNK == 0
    n_chunks = D_FF // FF_CHUNK

    cos_t, sin_t = _rope_tables(seq, n_ctx)
    cc = jnp.concatenate([c, c_ctx[None, :], jnp.zeros((SUBLANES - batch - 1, D_MODEL), F32)], axis=0)
    mods_all = _ada_mods(cc, w_ada, b_ada).reshape(depth, SUBLANES, 6, D_MODEL)

    xa = x.reshape(batch * seq, D_MODEL)
    xc = ctx.reshape(batch * n_ctx, D_MODEL)
    for i in range(depth):
        lam_init = 0.8 - 0.6 * math.exp(-0.3 * i)
        mods = mods_all[i]
        g1 = norm_g[i, 0:1]
        g2 = norm_g[i, 1:2]
        w_i = w_in[i].astype(BF16)
        qa, kta, va, u, qc, ktc, vc = _inproj(xa, xc, mods, g1, w_i[:, :N_QKVU],
                                              jnp.tile(qk_gain[i], (1, BRANCH_WIDTH // HEAD_DIM)),
                                              cos_t, sin_t, batch, nblk)
        ya = _attention(qa, kta, va, batch, nblk, diff=False)
        yd = _attention(qc, ktc, vc, batch, nblk, diff=True, lam_vecs=diff_lam[i],
                        g_out=diff_norm_g[i][None, :], lam_init=lam_init)

        wd, a8, cm = _s5_weights(ssm_lam_re[i], ssm_lam_im[i], ssm_log_dt[i], ssm_b_re[i],
                                 ssm_b_im[i], ssm_c_re[i], ssm_c_im[i], batch)
        yf, yr = _s5_scan(u.reshape(batch, s_len, BRANCH_WIDTH), wd, a8, cm, n_ctx)
        yf = yf.reshape(batch * s_len, BRANCH_WIDTH)
        yr = yr.reshape(batch * s_len, BRANCH_WIDTH)

        x1, h2 = _merge(xa, xc, mods, g1, g2, ya, yd, yf, yr, u, ssm_d[i][None, :],
                        w_glu[i].astype(BF16), b_glu[i][None, :], w_i[:, N_QKVU:],
                        w_branch[i].astype(BF16), w_out[i].astype(BF16), batch, nblk)

        wup = w_up[i].astype(BF16).reshape(D_MODEL, 2, n_chunks, FF_CHUNK).transpose(
            2, 0, 1, 3).reshape(n_chunks, D_MODEL, 2 * FF_CHUNK)
        cwb = jnp.concatenate([conv_w[i], conv_b[i][None, :]], axis=0)
        cw = cwb.reshape(4, 2, n_chunks, FF_CHUNK).transpose(2, 0, 1, 3).reshape(
            n_chunks, 4, 2 * FF_CHUNK)
        cw = jnp.pad(cw, ((0, 0), (0, SUBLANES - 4), (0, 0)))
        wdn = w_down[i].astype(BF16)
        xa = _ffn(h2, x1, mods, wup, cw, wdn, batch, nblk, latent_only=(i == depth - 1))
        xc = None

    return xa.reshape(batch, seq, D_MODEL)
```

```python
import functools
import math

import jax
import jax.numpy as jnp
import numpy as np
from jax import lax
from jax.experimental import pallas as pl
from jax.experimental.pallas import tpu as pltpu

F32 = jnp.float32
BF16 = jnp.bfloat16

D_MODEL = 1024
HEAD_DIM = 64
GRID_W = 64
ROPE_BASE = 10000.0
EPS = 1e-6
BRANCH_WIDTH = D_MODEL // 2
A_HEADS = BRANCH_WIDTH // HEAD_DIM
A_KV_HEADS = A_HEADS // 4
C_HEADS = BRANCH_WIDTH // (2 * HEAD_DIM)
S5_CH = 16
S5_STATE = 64
S5_GROUPS = BRANCH_WIDTH // S5_CH
D_FF = 2816
IN_SIZES = (512, 128, 128, 512, 512, 512, 512, 3 * D_MODEL)
IN_OFFS = tuple(int(v) for v in np.cumsum((0,) + IN_SIZES))
N_QKVU = IN_OFFS[7]

LANES = 128
SUBLANES = 8
TM = 256
SCORE_KEYS = 2048
TC = 64
FF_CHUNK = 128
HALO = 16
ADA_COLS = 1536
V7X_VMEM_BYTES = 64 * 1024 * 1024
VMEM_LIMIT = V7X_VMEM_BYTES * 7 // 8
SCORE_SCALE = HEAD_DIM ** -0.5 * math.log2(math.e)


def _cparams(n_axes):
    return pltpu.CompilerParams(dimension_semantics=("arbitrary",) * n_axes,
                                vmem_limit_bytes=VMEM_LIMIT)


def _rms_mod(x, gain, shift, scale):
    y = x * lax.rsqrt(jnp.mean(x * x, axis=-1, keepdims=True) + EPS)
    return (y * gain) * (1.0 + scale) + shift


def _head_avg_matrix(width):
    shift = HEAD_DIM.bit_length() - 1
    r = lax.broadcasted_iota(jnp.int32, (width, width), 0) >> shift
    c = lax.broadcasted_iota(jnp.int32, (width, width), 1) >> shift
    return jnp.where(r == c, 1.0 / HEAD_DIM, 0.0).astype(BF16)


def _head_rms(z, gain, avg):
    sq = z * z
    hi = sq.astype(BF16)
    lo = (sq - hi.astype(F32)).astype(BF16)
    ms = (jnp.dot(hi, avg, preferred_element_type=F32)
          + jnp.dot(lo, avg, preferred_element_type=F32))
    return z * lax.rsqrt(ms + EPS) * gain


def _tile_lanes(t, width):
    reps = width // t.shape[1]
    return t if reps == 1 else jnp.concatenate([t] * reps, axis=1)


def _rope(z, cos, sin_signed):
    width = z.shape[1]
    lane = lax.broadcasted_iota(jnp.int32, z.shape, 1)
    first_half = (lane & (HEAD_DIM - 1)) < HEAD_DIM // 2
    rot = jnp.where(first_half,
                    pltpu.roll(z, width - HEAD_DIM // 2, 1),
                    pltpu.roll(z, HEAD_DIM // 2, 1))
    return z * _tile_lanes(cos, width) + rot * _tile_lanes(sin_signed, width)


def _dup_halves(z):
    lane = lax.broadcasted_iota(jnp.int32, z.shape, 1)
    low = lane < HEAD_DIM
    sw = pltpu.roll(z, HEAD_DIM, 1)
    return jnp.where(low, z, sw), jnp.where(low, sw, z)


def _ada_kernel(c_ref, w_ref, b_ref, o_ref):
    a = jax.nn.silu(c_ref[...])
    o_ref[0] = jnp.dot(a, w_ref[0], preferred_element_type=F32,
                       precision=lax.Precision.HIGHEST) + b_ref[0]


def _ada_mods(cc, w_ada, b_ada):
    depth, _, n = w_ada.shape
    tn = ADA_COLS
    assert n % tn == 0
    return pl.pallas_call(
        _ada_kernel,
        grid=(depth, n // tn),
        in_specs=[pl.BlockSpec((SUBLANES, D_MODEL), lambda l, j: (0, 0)),
                  pl.BlockSpec((1, D_MODEL, tn), lambda l, j: (l, 0, j)),
                  pl.BlockSpec((1, 1, tn), lambda l, j: (l, 0, j))],
        out_specs=pl.BlockSpec((1, SUBLANES, tn), lambda l, j: (l, 0, j)),
        out_shape=jax.ShapeDtypeStruct((depth, SUBLANES, n), F32),
        compiler_params=_cparams(2),
        name="ada_mods",
    )(cc, w_ada, b_ada.reshape(depth, 1, n))


def _stream_specs(x_lat, x_ctx, nblk):
    blk = (TM, D_MODEL)
    if x_ctx is None:
        return ([pl.BlockSpec(blk, lambda i: (i, 0)),
                 pl.BlockSpec(blk, lambda i: ((i // nblk) * nblk, 0))], [x_lat, x_lat])
    lat_map = lambda i: ((i // nblk) * (nblk - 1) + jnp.maximum(i % nblk - 1, 0), 0)
    return ([pl.BlockSpec(blk, lat_map), pl.BlockSpec(blk, lambda i: (i // nblk, 0))],
            [x_lat, x_ctx])


def _stream_tile(x_ref, xc_ref, nblk):
    return jnp.where(pl.program_id(0) % nblk == 0, xc_ref[...], x_ref[...])


def _inproj_kernel(x_ref, xc_ref, mod_ref, g_ref, w_ref, qk_ref, cos_ref, sin_ref,
                   qa_ref, kta_ref, va_ref, u_ref, qc_ref, ktc_ref, vc_ref, *, nblk):
    mods = mod_ref[0]
    x = _stream_tile(x_ref, xc_ref, nblk)
    h = _rms_mod(x, g_ref[...], mods[0:1], mods[1:2]).astype(BF16)
    cos = cos_ref[...]
    sin = sin_ref[...]
    avg = _head_avg_matrix(BRANCH_WIDTH)

    def seg(k):
        return jnp.dot(h, w_ref[:, IN_OFFS[k]:IN_OFFS[k + 1]], preferred_element_type=F32)

    def gain(k, width=BRANCH_WIDTH):
        return qk_ref[k:k + 1, :width]

    k = _rope(_head_rms(seg(5), gain(3), avg), cos, sin)
    for hh in range(C_HEADS):
        ktc_ref[0, hh] = k[:, hh * LANES:(hh + 1) * LANES].T.astype(BF16)
    k = _rope(_head_rms(seg(1), gain(1, LANES), avg[:LANES, :LANES]), cos, sin)
    for hh, kd in enumerate(_dup_halves(k)):
        kta_ref[0, hh] = kd.T.astype(BF16)
    q = _rope(_head_rms(seg(4), gain(2), avg), cos, sin)
    qc_ref[...] = (q * SCORE_SCALE).astype(BF16)
    q = _rope(_head_rms(seg(0), gain(0), avg), cos, sin)
    qa_ref[...] = (q * SCORE_SCALE).astype(BF16)
    for hh, vd in enumerate(_dup_halves(seg(2))):
        va_ref[0, hh] = vd.astype(BF16)
    v = seg(6)
    for hh in range(C_HEADS):
        vc_ref[0, hh] = v[:, hh * LANES:(hh + 1) * LANES].astype(BF16)
    u_ref[...] = seg(3)


def _inproj(x_lat, x_ctx, mods, norm_g, w_qkvu, qk_gain, cos_t, sin_t, batch, nblk):
    s_len = nblk * TM
    t_rows = batch * s_len
    row = lambda i: (i, 0)
    mod_map = lambda i: (jnp.where(i % nblk == 0, batch, i // nblk), 0, 0)
    pos = lambda i: (i % nblk, 0)
    kt_map = lambda i: (i // nblk, 0, 0, i % nblk)
    v_map = lambda i: (i // nblk, 0, i % nblk, 0)
    x_specs, x_args = _stream_specs(x_lat, x_ctx, nblk)
    return pl.pallas_call(
        functools.partial(_inproj_kernel, nblk=nblk),
        grid=(t_rows // TM,),
        in_specs=x_specs + [
                  pl.BlockSpec((1, 6, D_MODEL), mod_map),
                  pl.BlockSpec((1, D_MODEL), lambda i: (0, 0)),
                  pl.BlockSpec((D_MODEL, N_QKVU), lambda i: (0, 0)),
                  pl.BlockSpec((4, BRANCH_WIDTH), lambda i: (0, 0)),
                  pl.BlockSpec((TM, LANES), pos),
                  pl.BlockSpec((TM, LANES), pos)],
        out_specs=[pl.BlockSpec((TM, BRANCH_WIDTH), row),
                   pl.BlockSpec((1, A_KV_HEADS, LANES, TM), kt_map),
                   pl.BlockSpec((1, A_KV_HEADS, TM, LANES), v_map),
                   pl.BlockSpec((TM, BRANCH_WIDTH), row),
                   pl.BlockSpec((TM, BRANCH_WIDTH), row),
                   pl.BlockSpec((1, C_HEADS, LANES, TM), kt_map),
                   pl.BlockSpec((1, C_HEADS, TM, LANES), v_map)],
        out_shape=[jax.ShapeDtypeStruct((t_rows, BRANCH_WIDTH), BF16),
                   jax.ShapeDtypeStruct((batch, A_KV_HEADS, LANES, s_len), BF16),
                   jax.ShapeDtypeStruct((batch, A_KV_HEADS, s_len, LANES), BF16),
                   jax.ShapeDtypeStruct((t_rows, BRANCH_WIDTH), F32),
                   jax.ShapeDtypeStruct((t_rows, BRANCH_WIDTH), BF16),
                   jax.ShapeDtypeStruct((batch, C_HEADS, LANES, s_len), BF16),
                   jax.ShapeDtypeStruct((batch, C_HEADS, s_len, LANES), BF16)],
        compiler_params=_cparams(1),
        name="in_proj",
    )(*x_args, mods, norm_g, w_qkvu, qk_gain, cos_t, sin_t)


def _attn_kernel(*refs, diff, lam_init, n_ctx, bounds):
    if diff:
        q_ref, kt_ref, v_ref, lam_ref, gout_ref, o_ref = refs[:6]
    else:
        q_ref, kt_ref, v_ref, o_ref = refs[:4]
    lhs_sc, m_sc, l_sc, acc_sc, s0_sc, s1_sc = refs[-6:]
    bufs = (s0_sc, s1_sc)
    n_chunks = len(bounds) - 1
    j = pl.program_id(2)
    rows = 2 * TM
    lane = lax.broadcasted_iota(jnp.int32, (TM, LANES), 1)
    low = lane < HEAD_DIM
    qt = q_ref[...].astype(F32)
    lhs_sc[0:TM, :] = jnp.where(low, qt, 0.0).astype(BF16)
    lhs_sc[TM:rows, :] = jnp.where(low, 0.0, qt).astype(BF16)

    def chunk_keys(c):
        return slice(bounds[c], bounds[c + 1])

    def scores(keys):
        return jnp.dot(lhs_sc[...], kt_ref[0, 0, :, keys], preferred_element_type=F32)

    def softmax_pv(s, keys, first):
        width = s.shape[1]
        mx = s[:, 0:LANES]
        for t in range(1, width // LANES):
            mx = jnp.maximum(mx, s[:, t * LANES:(t + 1) * LANES])
        m_cur = jnp.max(mx, axis=1, keepdims=True)
        if first:
            m_next = jnp.broadcast_to(m_cur, (rows, LANES))
        else:
            m_prev = m_sc[...]
            m_next = jnp.maximum(m_prev, m_cur)
            alpha = jnp.exp2(m_prev - m_next)
        p = jnp.exp2(s - _tile_lanes(m_next, width))
        psum = p[:, 0:LANES]
        for t in range(1, width // LANES):
            psum = psum + p[:, t * LANES:(t + 1) * LANES]
        pv = jnp.dot(p.astype(BF16), v_ref[0, 0, keys, :], preferred_element_type=F32)
        if first:
            l_sc[...] = psum
            acc_sc[...] = pv
        else:
            l_sc[...] = alpha * l_sc[...] + psum
            acc_sc[...] = alpha * acc_sc[...] + pv
        m_sc[...] = m_next

    @pl.when(j == 0)
    def _():
        ctx_keys = slice(0, n_ctx)
        softmax_pv(scores(ctx_keys), ctx_keys, True)

    @pl.when(j > 0)
    def _():
        def put(c):
            keys = chunk_keys(c)
            bufs[c % 2][:, 0:keys.stop - keys.start] = scores(keys)

        def take(c):
            keys = chunk_keys(c)
            softmax_pv(bufs[c % 2][:, 0:keys.stop - keys.start], keys, c == 0)

        put(0)
        for c in range(n_chunks):
            if c + 1 < n_chunks:
                put(c + 1)
            take(c)

    o = acc_sc[...] / jnp.sum(l_sc[...], axis=1, keepdims=True)
    if diff:
        lv = lam_ref[...]
        lam = (jnp.exp(jnp.sum(lv[0:1] * lv[1:2], axis=1, keepdims=True))
               - jnp.exp(jnp.sum(lv[2:3] * lv[3:4], axis=1, keepdims=True)) + lam_init)
        od = o[0:TM] - lam * o[TM:rows]
        od = od * lax.rsqrt(jnp.mean(od * od, axis=-1, keepdims=True) + EPS)
        o_ref[...] = ((od * gout_ref[...]) * (1.0 - lam_init)).astype(o_ref.dtype)
    else:
        o_ref[...] = jnp.where(low, o[0:TM], o[TM:rows]).astype(o_ref.dtype)


def _attention(q, kt, v, batch, nblk, *, diff, lam_vecs=None, g_out=None, lam_init=0.0):
    kv_heads = kt.shape[1]
    s_len = kt.shape[3]
    tiles = BRANCH_WIDTH // LANES
    rows = 2 * TM
    q_map = lambda b, h, j: (b * nblk + j, h)
    kv_map = lambda b, h, j: (b, h * kv_heads // tiles, 0, 0)
    const = lambda b, h, j: (0, 0)
    in_specs = [pl.BlockSpec((TM, LANES), q_map),
                pl.BlockSpec((1, 1, LANES, s_len), kv_map),
                pl.BlockSpec((1, 1, s_len, LANES), kv_map)]
    args = [q, kt, v]
    if diff:
        in_specs += [pl.BlockSpec((4, HEAD_DIM), const), pl.BlockSpec((1, LANES), const)]
        args += [lam_vecs, g_out]
    n_ctx = TM
    tkl = min(SCORE_KEYS, s_len - n_ctx)
    assert (s_len - n_ctx) % tkl == 0
    bounds = [0] + list(range(n_ctx + tkl, s_len + 1, tkl))
    kern = functools.partial(_attn_kernel, diff=diff, lam_init=lam_init,
                             n_ctx=n_ctx, bounds=tuple(bounds))
    return pl.pallas_call(
        kern,
        grid=(batch, tiles, nblk),
        in_specs=in_specs,
        out_specs=pl.BlockSpec((TM, LANES), q_map),
        out_shape=jax.ShapeDtypeStruct((batch * s_len, BRANCH_WIDTH), BF16),
        scratch_shapes=([pltpu.VMEM((rows, LANES), BF16)]
                        + [pltpu.VMEM((rows, LANES), F32)] * 3
                        + [pltpu.VMEM((rows, n_ctx + tkl), F32)] * 2),
        compiler_params=_cparams(3),
        name="diff_attn" if diff else "gqa_attn",
    )(*args)


def _s5_kernel(uf_ref, ub_ref, pin_ref, pout_ref, wd_ref, a_ref, cm_ref, yf_ref, yb_ref,
               bu_sc, st_sc):
    n_tiles = BRANCH_WIDTH // LANES
    half = SUBLANES * S5_STATE
    rows = TC * SUBLANES
    tok = (SUBLANES // 2) * TC

    @pl.when(pl.program_id(0) == 0)
    def _():
        st_sc[...] = jnp.zeros_like(st_sc)

    seq = lax.broadcasted_iota(jnp.int32, (rows, LANES), 0) & (SUBLANES - 1)
    fwd = seq < SUBLANES // 2
    lhs_dir = []
    for d, ref in enumerate((uf_ref, ub_ref)):
        ud = ref[...].reshape(tok, BRANCH_WIDTH).astype(BF16)
        lhs_dir.append(jnp.dot(pin_ref[d], ud, preferred_element_type=F32).astype(BF16))
    for j in range(n_tiles):
        cs = slice(j * LANES, (j + 1) * LANES)
        lhs = jnp.concatenate([lhs_dir[0][:, cs], lhs_dir[1][:, cs]], axis=1)
        bu_sc[:, 2 * half * j:2 * half * (j + 1)] = jnp.dot(
            lhs, wd_ref[j], preferred_element_type=F32)

    for j in range(n_tiles):
        re = slice(2 * half * j, 2 * half * j + half)
        im = slice(2 * half * j + half, 2 * half * (j + 1))
        ar = a_ref[0, :, half * j:half * (j + 1)]
        ai = a_ref[1, :, half * j:half * (j + 1)]

        def step(t, carry, re=re, im=im, ar=ar, ai=ai):
            xr, xi = carry
            r = pl.multiple_of(t * SUBLANES, SUBLANES)
            nxr = ar * xr - ai * xi + bu_sc[pl.ds(r, SUBLANES), re]
            nxi = ar * xi + ai * xr + bu_sc[pl.ds(r, SUBLANES), im]
            bu_sc[pl.ds(r, SUBLANES), re] = nxr
            bu_sc[pl.ds(r, SUBLANES), im] = nxi
            return nxr, nxi

        xr, xi = lax.fori_loop(0, TC, step, (st_sc[:, re], st_sc[:, im]), unroll=True)
        st_sc[:, re] = xr
        st_sc[:, im] = xi

    ys = []
    for j in range(n_tiles):
        x = bu_sc[:, 2 * half * j:2 * half * (j + 1)].astype(BF16)
        yy = jnp.dot(x, cm_ref[j], preferred_element_type=F32)
        ys.append(jnp.where(fwd, yy[:, :LANES], yy[:, LANES:]))
    y = jnp.concatenate(ys, axis=1)
    parts = []
    rest = y
    for _ in range(2):
        part = rest.astype(BF16)
        parts.append(part)
        rest = rest - part.astype(F32)
    for d, ref in enumerate((yf_ref, yb_ref)):
        out = None
        for part in parts:
            term = jnp.dot(pout_ref[d], part, preferred_element_type=F32)
            out = term if out is None else out + term
        ref[...] = out.reshape(ref.shape)


def _scan_row_placement(n_seq):
    p = np.zeros((2, TC * 2 * n_seq, n_seq * TC), np.float32)
    for b in range(n_seq):
        for k in range(TC):
            p[0, 2 * n_seq * k + b, b * TC + k] = 1.0
            p[1, 2 * n_seq * (TC - 1 - k) + n_seq + b, b * TC + k] = 1.0
    return jnp.asarray(p, BF16), jnp.asarray(p.transpose(0, 2, 1), BF16)


def _s5_scan(u3, wd, a8, cm, n_ctx):
    batch, s_len, _ = u3.shape
    rows = TC * SUBLANES
    n_tiles = BRANCH_WIDTH // LANES
    n_state = 2 * SUBLANES * S5_STATE * n_tiles
    n_steps = s_len // TC
    ctx_steps = n_ctx // TC
    p_in, p_out = _scan_row_placement(batch)
    fwd_map = lambda g: (0, g, 0)
    bwd_map = lambda g: (0, jnp.where(g < ctx_steps, ctx_steps - 1 - g,
                                      n_steps + ctx_steps - 1 - g), 0)
    c3 = lambda g: (0, 0, 0)
    blk = (batch, TC, BRANCH_WIDTH)
    out = jax.ShapeDtypeStruct(u3.shape, F32)
    return pl.pallas_call(
        _s5_kernel,
        grid=(n_steps,),
        in_specs=[pl.BlockSpec(blk, fwd_map),
                  pl.BlockSpec(blk, bwd_map),
                  pl.BlockSpec(p_in.shape, c3),
                  pl.BlockSpec(p_out.shape, c3),
                  pl.BlockSpec(wd.shape, c3),
                  pl.BlockSpec(a8.shape, c3),
                  pl.BlockSpec(cm.shape, c3)],
        out_specs=[pl.BlockSpec(blk, fwd_map), pl.BlockSpec(blk, bwd_map)],
        out_shape=[out, out],
        scratch_shapes=[pltpu.VMEM((rows, n_state), F32),
                        pltpu.VMEM((SUBLANES, n_state), F32)],
        compiler_params=_cparams(1),
        name="s5_scan",
    )(u3, u3, p_in, p_out, wd, a8, cm)


def _zoh(lam_re, lam_im, log_dt, b_re, b_im):
    dt = jnp.exp(log_dt)[..., None]
    mag = jnp.exp(lam_re * dt)
    a_re = mag * jnp.cos(lam_im * dt)
    a_im = mag * jnp.sin(lam_im * dt)
    den = lam_re * lam_re + lam_im * lam_im
    f_re = ((a_re - 1.0) * lam_re + a_im * lam_im) / den
    f_im = (a_im * lam_re - (a_re - 1.0) * lam_im) / den
    bb_re = f_re[..., None] * b_re - f_im[..., None] * b_im
    bb_im = f_re[..., None] * b_im + f_im[..., None] * b_re
    return a_re, a_im, bb_re, bb_im


def _s5_weights(lam_re, lam_im, log_dt, b_re, b_im, c_re, c_im, batch):
    n_tiles = BRANCH_WIDTH // LANES
    gpt = S5_GROUPS // n_tiles
    a_re, a_im, bb_re, bb_im = _zoh(lam_re, lam_im, log_dt, b_re, b_im)
    eye = jnp.eye(gpt, dtype=F32)

    def drive(bb):
        t = bb.reshape(2, n_tiles, gpt, S5_STATE, S5_CH)
        w = jnp.einsum('djgpc,gh->jdgchp', t, eye)
        return w.reshape(n_tiles, 2 * gpt * S5_CH, gpt * S5_STATE)

    def read(cc):
        t = cc.reshape(2, n_tiles, gpt, S5_CH, S5_STATE)
        w = jnp.einsum('djgcp,gh->jgpdhc', t, eye)
        return w.reshape(n_tiles, gpt * S5_STATE, 2 * gpt * S5_CH)

    wd = jnp.concatenate([drive(bb_re), drive(bb_im)], axis=2).astype(BF16)
    cm = jnp.concatenate([read(c_re), read(-c_im)], axis=1).astype(BF16)

    def per_seq(a):
        return jnp.repeat(a.reshape(2, 1, S5_GROUPS * S5_STATE), batch, axis=1).reshape(
            2 * batch, S5_GROUPS * S5_STATE)

    a8 = jnp.stack([per_seq(a_re), per_seq(a_im)])
    return wd, a8, cm


def _merge_kernel(x_ref, xc_ref, mod_ref, g1_ref, g2_ref, ya_ref, yd_ref, yf_ref, yr_ref, u_ref,
                  dsk_ref, wglu_ref, bglu_ref, wgate_ref, wbr_ref, wout_ref, x1_ref, h2_ref,
                  *, nblk):
    mods = mod_ref[0]
    x = _stream_tile(x_ref, xc_ref, nblk)
    h = _rms_mod(x, g1_ref[...], mods[0:1], mods[1:2]).astype(BF16)
    ys = yf_ref[...] + yr_ref[...] + dsk_ref[...] * u_ref[...]
    g = jax.nn.gelu(ys)
    yb = g * jax.nn.sigmoid(
        jnp.dot(g.astype(BF16), wglu_ref[...], preferred_element_type=F32) + bglu_ref[...])
    branches = (ya_ref[...], yb.astype(BF16), yd_ref[...])
    m = None
    for k, y in enumerate(branches):
        gate = jax.nn.sigmoid(jnp.dot(h, wgate_ref[:, k * D_MODEL:(k + 1) * D_MODEL],
                                      preferred_element_type=F32))
        term = gate * jnp.dot(y, wbr_ref[k], preferred_element_type=F32)
        m = term if m is None else m + term
    y = jnp.dot(m.astype(BF16), wout_ref[...], preferred_element_type=F32)
    x1 = x + mods[2:3] * y
    x1_ref[...] = x1
    h2_ref[...] = _rms_mod(x1, g2_ref[...], mods[3:4], mods[4:5]).astype(BF16)


def _merge(x_lat, x_ctx, mods, g1, g2, ya, yd, yf, yr, u, d_skip, w_glu, b_glu, w_gate,
           w_branch, w_out, batch, nblk):
    t_rows = ya.shape[0]
    row = lambda i: (i, 0)
    mod_map = lambda i: (jnp.where(i % nblk == 0, batch, i // nblk), 0, 0)
    c2 = lambda i: (0, 0)
    c3 = lambda i: (0, 0, 0)
    x_specs, x_args = _stream_specs(x_lat, x_ctx, nblk)
    return pl.pallas_call(
        functools.partial(_merge_kernel, nblk=nblk),
        grid=(t_rows // TM,),
        in_specs=x_specs + [
                  pl.BlockSpec((1, 6, D_MODEL), mod_map),
                  pl.BlockSpec((1, D_MODEL), c2),
                  pl.BlockSpec((1, D_MODEL), c2),
                  pl.BlockSpec((TM, BRANCH_WIDTH), row),
                  pl.BlockSpec((TM, BRANCH_WIDTH), row),
                  pl.BlockSpec((TM, BRANCH_WIDTH), row),
                  pl.BlockSpec((TM, BRANCH_WIDTH), row),
                  pl.BlockSpec((TM, BRANCH_WIDTH), row),
                  pl.BlockSpec((1, BRANCH_WIDTH), c2),
                  pl.BlockSpec((BRANCH_WIDTH, BRANCH_WIDTH), c2),
                  pl.BlockSpec((1, BRANCH_WIDTH), c2),
                  pl.BlockSpec((D_MODEL, 3 * D_MODEL), c2),
                  pl.BlockSpec((3, BRANCH_WIDTH, D_MODEL), c3),
                  pl.BlockSpec((D_MODEL, D_MODEL), c2)],
        out_specs=[pl.BlockSpec((TM, D_MODEL), row), pl.BlockSpec((TM, D_MODEL), row)],
        out_shape=[jax.ShapeDtypeStruct((t_rows, D_MODEL), F32),
                   jax.ShapeDtypeStruct((t_rows, D_MODEL), BF16)],
        compiler_params=_cparams(1),
        name="merge",
    )(*x_args, mods, g1, g2, ya, yd, yf, yr, u, d_skip, w_glu, b_glu, w_gate, w_branch, w_out)


def _ffn_kernel(h_ref, hp_ref, hn_ref, x1_ref, mod_ref, wup_ref, cw_ref, wdn_ref, o_ref,
                lhs_sc, u0_sc, u1_sc, act_sc, *, nblk, n_chunks):
    pj = pl.program_id(0) % nblk
    left_ok = (pj >= 2).astype(F32)
    right_ok = jnp.logical_and(pj != 0, pj != nblk - 1).astype(F32)
    lhs_sc[0:HALO, :] = (hp_ref[...].astype(F32) * left_ok).astype(BF16)
    lhs_sc[HALO:HALO + TM, :] = h_ref[...]
    lhs_sc[HALO + TM:2 * HALO + TM, :] = (hn_ref[...].astype(F32) * right_ok).astype(BF16)

    def up(k, buf):
        buf[...] = jnp.dot(lhs_sc[...], wup_ref[k], preferred_element_type=F32)

    def activate(k, buf):
        cw = cw_ref[k]
        c = (buf[HALO - 1:HALO - 1 + TM, :] * cw[0:1]
             + buf[HALO:HALO + TM, :] * cw[1:2]
             + buf[HALO + 1:HALO + 1 + TM, :] * cw[2:3]
             + cw[3:4])
        act = jax.nn.silu(c[:, FF_CHUNK:]) * c[:, :FF_CHUNK]
        act_sc[:, k * FF_CHUNK:(k + 1) * FF_CHUNK] = act.astype(BF16)

    bufs = (u0_sc, u1_sc)
    up(0, bufs[0])
    for k in range(n_chunks):
        if k + 1 < n_chunks:
            up(k + 1, bufs[(k + 1) % 2])
        activate(k, bufs[k % 2])
    y = jnp.dot(act_sc[...], wdn_ref[...], preferred_element_type=F32)
    o_ref[...] = x1_ref[...] + mod_ref[0][5:6] * y


def _ffn(h2, x1, mods, wup, cw, wdn, batch, nblk, latent_only):
    t_rows = x1.shape[0]
    n_chunks = wup.shape[0]
    per = TM // HALO
    last = t_rows // HALO - 1
    row = lambda i: (i, 0)
    mod_map = lambda i: (jnp.where(i % nblk == 0, batch, i // nblk), 0, 0)
    kern = functools.partial(_ffn_kernel, nblk=nblk, n_chunks=n_chunks)
    if latent_only:
        out_map = lambda i: ((i // nblk) * (nblk - 1) + jnp.maximum(i % nblk - 1, 0), 0)
        out_rows = t_rows - batch * TM
    else:
        out_map, out_rows = row, t_rows
    return pl.pallas_call(
        kern,
        grid=(t_rows // TM,),
        in_specs=[pl.BlockSpec((TM, D_MODEL), row),
                  pl.BlockSpec((HALO, D_MODEL), lambda i: (jnp.maximum(i * per - 1, 0), 0)),
                  pl.BlockSpec((HALO, D_MODEL), lambda i: (jnp.minimum((i + 1) * per, last), 0)),
                  pl.BlockSpec((TM, D_MODEL), row),
                  pl.BlockSpec((1, 6, D_MODEL), mod_map),
                  pl.BlockSpec(wup.shape, lambda i: (0, 0, 0)),
                  pl.BlockSpec(cw.shape, lambda i: (0, 0, 0)),
                  pl.BlockSpec(wdn.shape, lambda i: (0, 0))],
        out_specs=pl.BlockSpec((TM, D_MODEL), out_map),
        out_shape=jax.ShapeDtypeStruct((out_rows, D_MODEL), F32),
        scratch_shapes=([pltpu.VMEM((TM + 2 * HALO, D_MODEL), BF16)]
                        + [pltpu.VMEM((TM + 2 * HALO, 2 * FF_CHUNK), F32)] * 2
                        + [pltpu.VMEM((TM, D_FF), BF16)]),
        compiler_params=_cparams(1),
        name="conv_ffn",
    )(h2, h2, h2, x1, mods, wup, cw, wdn)


def _rope_tables(seq, ctx):
    rows = seq // GRID_W
    row = jnp.repeat(jnp.arange(rows, dtype=F32), GRID_W)
    col = jnp.tile(jnp.arange(GRID_W, dtype=F32), rows)
    n_freq = HEAD_DIM // 4
    inv_freq = ROPE_BASE ** (-jnp.arange(n_freq, dtype=F32) / n_freq)
    ang = jnp.concatenate([row[:, None] * inv_freq, col[:, None] * inv_freq], axis=-1)
    ang = jnp.concatenate([ang, ang], axis=-1)
    cos = jnp.concatenate([jnp.ones((ctx, HEAD_DIM), F32), jnp.cos(ang)], axis=0)
    sin = jnp.concatenate([jnp.zeros((ctx, HEAD_DIM), F32), jnp.sin(ang)], axis=0)
    sign = jnp.where(jnp.arange(HEAD_DIM) < HEAD_DIM // 2, -1.0, 1.0).astype(F32)
    two = lambda t: jnp.concatenate([t, t], axis=1)
    return two(cos), two(sin * sign)


def kernel(x, c, ctx, c_ctx, w_ada, b_ada, norm_g, w_in, qk_gain, ssm_lam_re, ssm_lam_im, ssm_log_dt, ssm_b_re, ssm_b_im, ssm_c_re, ssm_c_im, ssm_d, w_glu, b_glu, diff_lam, diff_norm_g, w_branch, w_out, w_up, conv_w, conv_b, w_down):
    batch, seq, _ = x.shape
    n_ctx = ctx.shape[1]
    depth = w_in.shape[0]
    assert n_ctx == TM and seq % TM == 0 and 2 * batch == SUBLANES
    s_len = n_ctx + seq
    nblk = s_len // TM
    assert s_len % TC == 0 and D_FF % FF_CHUNK == 0
    n_chunks = D_FF // FF_CHUNK

    cos_t, sin_t = _rope_tables(seq, n_ctx)
    cc = jnp.concatenate([c, c_ctx[None, :], jnp.zeros((SUBLANES - batch - 1, D_MODEL), F32)], axis=0)
    mods_all = _ada_mods(cc, w_ada, b_ada).reshape(depth, SUBLANES, 6, D_MODEL)

    xa = x.reshape(batch * seq, D_MODEL)
    xc = ctx.reshape(batch * n_ctx, D_MODEL)
    for i in range(depth):
        lam_init = 0.8 - 0.6 * math.exp(-0.3 * i)
        mods = mods_all[i]
        g1 = norm_g[i, 0:1]
        g2 = norm_g[i, 1:2]
        w_i = w_in[i].astype(BF16)
        qa, kta, va, u, qc, ktc, vc = _inproj(xa, xc, mods, g1, w_i[:, :N_QKVU],
                                              jnp.tile(qk_gain[i], (1, BRANCH_WIDTH // HEAD_DIM)),
                                              cos_t, sin_t, batch, nblk)
        ya = _attention(qa, kta, va, batch, nblk, diff=False)
        yd = _attention(qc, ktc, vc, batch, nblk, diff=True, lam_vecs=diff_lam[i],
                        g_out=diff_norm_g[i][None, :], lam_init=lam_init)

        wd, a8, cm = _s5_weights(ssm_lam_re[i], ssm_lam_im[i], ssm_log_dt[i], ssm_b_re[i],
                                 ssm_b_im[i], ssm_c_re[i], ssm_c_im[i], batch)
        yf, yr = _s5_scan(u.reshape(batch, s_len, BRANCH_WIDTH), wd, a8, cm, n_ctx)
        yf = yf.reshape(batch * s_len, BRANCH_WIDTH)
        yr = yr.reshape(batch * s_len, BRANCH_WIDTH)

        x1, h2 = _merge(xa, xc, mods, g1, g2, ya, yd, yf, yr, u, ssm_d[i][None, :],
                        w_glu[i].astype(BF16), b_glu[i][None, :], w_i[:, N_QKVU:],
                        w_branch[i].astype(BF16), w_out[i].astype(BF16), batch, nblk)

        def chunked(t):
            a = t[:, :D_FF].reshape(t.shape[0], n_chunks, FF_CHUNK)
            g = t[:, D_FF:].reshape(t.shape[0], n_chunks, FF_CHUNK)
            return jnp.concatenate([a, g], axis=-1).transpose(1, 0, 2)

        wup = chunked(w_up[i].astype(BF16))
        cw = chunked(jnp.concatenate(
            [conv_w[i], conv_b[i][None, :], jnp.zeros((SUBLANES - 4, 2 * D_FF), F32)], axis=0))
        wdn = w_down[i].astype(BF16)
        xa = _ffn(h2, x1, mods, wup, cw, wdn, batch, nblk, latent_only=(i == depth - 1))
        xc = None

    return xa.reshape(batch, seq, D_MODEL)
```

```python
import functools
import math

import jax
import jax.numpy as jnp
import numpy as np
from jax import lax
from jax.experimental import pallas as pl
from jax.experimental.pallas import tpu as pltpu

F32 = jnp.float32
BF16 = jnp.bfloat16

D_MODEL = 1024
HEAD_DIM = 64
GRID_W = 64
ROPE_BASE = 10000.0
EPS = 1e-6
BRANCH_WIDTH = D_MODEL // 2
A_HEADS = BRANCH_WIDTH // HEAD_DIM
A_KV_HEADS = A_HEADS // 4
C_HEADS = BRANCH_WIDTH // (2 * HEAD_DIM)
S5_CH = 16
S5_STATE = 64
S5_GROUPS = BRANCH_WIDTH // S5_CH
D_FF = 2816
IN_SIZES = (512, 128, 128, 512, 512, 512, 512, 3 * D_MODEL)
IN_OFFS = tuple(int(v) for v in np.cumsum((0,) + IN_SIZES))
N_QKVU = IN_OFFS[7]

LANES = 128
SUBLANES = 8
TM = 256
SCORE_KEYS = 2048
TC = 64
FF_CHUNK = 128
HALO = 16
ADA_COLS = 1536
V7X_VMEM_BYTES = 64 * 1024 * 1024
VMEM_LIMIT = V7X_VMEM_BYTES * 7 // 8
SCORE_SCALE = HEAD_DIM ** -0.5 * math.log2(math.e)


def _cparams(n_axes):
    return pltpu.CompilerParams(dimension_semantics=("arbitrary",) * n_axes,
                                vmem_limit_bytes=VMEM_LIMIT)


def _rms_mod(x, gain, shift, scale):
    y = x * lax.rsqrt(jnp.mean(x * x, axis=-1, keepdims=True) + EPS)
    return (y * gain) * (1.0 + scale) + shift


def _head_avg_matrix(width):
    shift = HEAD_DIM.bit_length() - 1
    r = lax.broadcasted_iota(jnp.int32, (width, width), 0) >> shift
    c = lax.broadcasted_iota(jnp.int32, (width, width), 1) >> shift
    return jnp.where(r == c, 1.0 / HEAD_DIM, 0.0).astype(BF16)


def _head_rms(z, gain, avg):
    sq = z * z
    hi = sq.astype(BF16)
    lo = (sq - hi.astype(F32)).astype(BF16)
    ms = (jnp.dot(hi, avg, preferred_element_type=F32)
          + jnp.dot(lo, avg, preferred_element_type=F32))
    return z * lax.rsqrt(ms + EPS) * gain


def _tile_lanes(t, width):
    reps = width // t.shape[1]
    return t if reps == 1 else jnp.concatenate([t] * reps, axis=1)


def _rope(z, cos, sin_signed):
    width = z.shape[1]
    lane = lax.broadcasted_iota(jnp.int32, z.shape, 1)
    first_half = (lane & (HEAD_DIM - 1)) < HEAD_DIM // 2
    rot = jnp.where(first_half,
                    pltpu.roll(z, width - HEAD_DIM // 2, 1),
                    pltpu.roll(z, HEAD_DIM // 2, 1))
    return z * _tile_lanes(cos, width) + rot * _tile_lanes(sin_signed, width)


def _dup_halves(z):
    lane = lax.broadcasted_iota(jnp.int32, z.shape, 1)
    low = lane < HEAD_DIM
    sw = pltpu.roll(z, HEAD_DIM, 1)
    return jnp.where(low, z, sw), jnp.where(low, sw, z)


def _ada_kernel(c_ref, w_ref, b_ref, o_ref):
    a = jax.nn.silu(c_ref[...])
    o_ref[0] = jnp.dot(a, w_ref[0], preferred_element_type=F32,
                       precision=lax.Precision.HIGHEST) + b_ref[0]


def _ada_mods(cc, w_ada, b_ada):
    depth, _, n = w_ada.shape
    tn = ADA_COLS
    assert n % tn == 0
    return pl.pallas_call(
        _ada_kernel,
        grid=(depth, n // tn),
        in_specs=[pl.BlockSpec((SUBLANES, D_MODEL), lambda l, j: (0, 0)),
                  pl.BlockSpec((1, D_MODEL, tn), lambda l, j: (l, 0, j)),
                  pl.BlockSpec((1, 1, tn), lambda l, j: (l, 0, j))],
        out_specs=pl.BlockSpec((1, SUBLANES, tn), lambda l, j: (l, 0, j)),
        out_shape=jax.ShapeDtypeStruct((depth, SUBLANES, n), F32),
        compiler_params=_cparams(2),
        name="ada_mods",
    )(cc, w_ada, b_ada.reshape(depth, 1, n))


def _stream_specs(x_lat, x_ctx, nblk):
    blk = (TM, D_MODEL)
    if x_ctx is None:
        return ([pl.BlockSpec(blk, lambda i: (i, 0)),
                 pl.BlockSpec(blk, lambda i: ((i // nblk) * nblk, 0))], [x_lat, x_lat])
    lat_map = lambda i: ((i // nblk) * (nblk - 1) + jnp.maximum(i % nblk - 1, 0), 0)
    return ([pl.BlockSpec(blk, lat_map), pl.BlockSpec(blk, lambda i: (i // nblk, 0))],
            [x_lat, x_ctx])


def _stream_tile(x_ref, xc_ref, nblk):
    return jnp.where(pl.program_id(0) % nblk == 0, xc_ref[...], x_ref[...])


def _inproj_kernel(x_ref, xc_ref, mod_ref, g_ref, w_ref, qk_ref, cos_ref, sin_ref,
                   qa_ref, kta_ref, va_ref, u_ref, qc_ref, ktc_ref, vc_ref, *, nblk):
    mods = mod_ref[0]
    x = _stream_tile(x_ref, xc_ref, nblk)
    h = _rms_mod(x, g_ref[...], mods[0:1], mods[1:2]).astype(BF16)
    cos = cos_ref[...]
    sin = sin_ref[...]
    avg = _head_avg_matrix(BRANCH_WIDTH)

    def seg(k):
        return jnp.dot(h, w_ref[:, IN_OFFS[k]:IN_OFFS[k + 1]], preferred_element_type=F32)

    def gain(k, width=BRANCH_WIDTH):
        return qk_ref[k:k + 1, :width]

    k = _rope(_head_rms(seg(5), gain(3), avg), cos, sin)
    for hh in range(C_HEADS):
        ktc_ref[0, hh] = k[:, hh * LANES:(hh + 1) * LANES].T.astype(BF16)
    k = _rope(_head_rms(seg(1), gain(1, LANES), avg[:LANES, :LANES]), cos, sin)
    for hh, kd in enumerate(_dup_halves(k)):
        kta_ref[0, hh] = kd.T.astype(BF16)
    q = _rope(_head_rms(seg(4), gain(2), avg), cos, sin)
    qc_ref[...] = (q * SCORE_SCALE).astype(BF16)
    q = _rope(_head_rms(seg(0), gain(0), avg), cos, sin)
    qa_ref[...] = (q * SCORE_SCALE).astype(BF16)
    for hh, vd in enumerate(_dup_halves(seg(2))):
        va_ref[0, hh] = vd.astype(BF16)
    v = seg(6)
    for hh in range(C_HEADS):
        vc_ref[0, hh] = v[:, hh * LANES:(hh + 1) * LANES].astype(BF16)
    u_ref[...] = seg(3)


def _inproj(x_lat, x_ctx, mods, norm_g, w_qkvu, qk_gain, cos_t, sin_t, batch, nblk):
    s_len = nblk * TM
    t_rows = batch * s_len
    row = lambda i: (i, 0)
    mod_map = lambda i: (jnp.where(i % nblk == 0, batch, i // nblk), 0, 0)
    pos = lambda i: (i % nblk, 0)
    kt_map = lambda i: (i // nblk, 0, 0, i % nblk)
    v_map = lambda i: (i // nblk, 0, i % nblk, 0)
    x_specs, x_args = _stream_specs(x_lat, x_ctx, nblk)
    return pl.pallas_call(
        functools.partial(_inproj_kernel, nblk=nblk),
        grid=(t_rows // TM,),
        in_specs=x_specs + [
                  pl.BlockSpec((1, 6, D_MODEL), mod_map),
                  pl.BlockSpec((1, D_MODEL), lambda i: (0, 0)),
                  pl.BlockSpec((D_MODEL, N_QKVU), lambda i: (0, 0)),
                  pl.BlockSpec((4, BRANCH_WIDTH), lambda i: (0, 0)),
                  pl.BlockSpec((TM, LANES), pos),
                  pl.BlockSpec((TM, LANES), pos)],
        out_specs=[pl.BlockSpec((TM, BRANCH_WIDTH), row),
                   pl.BlockSpec((1, A_KV_HEADS, LANES, TM), kt_map),
                   pl.BlockSpec((1, A_KV_HEADS, TM, LANES), v_map),
                   pl.BlockSpec((TM, BRANCH_WIDTH), row),
                   pl.BlockSpec((TM, BRANCH_WIDTH), row),
                   pl.BlockSpec((1, C_HEADS, LANES, TM), kt_map),
                   pl.BlockSpec((1, C_HEADS, TM, LANES), v_map)],
        out_shape=[jax.ShapeDtypeStruct((t_rows, BRANCH_WIDTH), BF16),
                   jax.ShapeDtypeStruct((batch, A_KV_HEADS, LANES, s_len), BF16),
                   jax.ShapeDtypeStruct((batch, A_KV_HEADS, s_len, LANES), BF16),
                   jax.ShapeDtypeStruct((t_rows, BRANCH_WIDTH), F32),
                   jax.ShapeDtypeStruct((t_rows, BRANCH_WIDTH), BF16),
                   jax.ShapeDtypeStruct((batch, C_HEADS, LANES, s_len), BF16),
                   jax.ShapeDtypeStruct((batch, C_HEADS, s_len, LANES), BF16)],
        compiler_params=_cparams(1),
        name="in_proj",
    )(*x_args, mods, norm_g, w_qkvu, qk_gain, cos_t, sin_t)


def _attn_kernel(*refs, diff, lam_init, n_ctx, bounds):
    if diff:
        q_ref, kt_ref, v_ref, lam_ref, gout_ref, o_ref = refs[:6]
    else:
        q_ref, kt_ref, v_ref, o_ref = refs[:4]
    lhs_sc, m_sc, l_sc, acc_sc, s0_sc, s1_sc = refs[-6:]
    bufs = (s0_sc, s1_sc)
    n_chunks = len(bounds) - 1
    j = pl.program_id(2)
    rows = 2 * TM
    lane = lax.broadcasted_iota(jnp.int32, (TM, LANES), 1)
    low = lane < HEAD_DIM
    qt = q_ref[...].astype(F32)
    lhs_sc[0:TM, :] = jnp.where(low, qt, 0.0).astype(BF16)
    lhs_sc[TM:rows, :] = jnp.where(low, 0.0, qt).astype(BF16)

    def chunk_keys(c):
        return slice(bounds[c], bounds[c + 1])

    def scores(keys):
        return jnp.dot(lhs_sc[...], kt_ref[0, 0, :, keys], preferred_element_type=F32)

    def softmax_pv(s, keys, first):
        width = s.shape[1]
        mx = s[:, 0:LANES]
        for t in range(1, width // LANES):
            mx = jnp.maximum(mx, s[:, t * LANES:(t + 1) * LANES])
        m_cur = jnp.max(mx, axis=1, keepdims=True)
        if first:
            m_next = jnp.broadcast_to(m_cur, (rows, LANES))
        else:
            m_prev = m_sc[...]
            m_next = jnp.maximum(m_prev, m_cur)
            alpha = jnp.exp2(m_prev - m_next)
        p = jnp.exp2(s - _tile_lanes(m_next, width))
        psum = p[:, 0:LANES]
        for t in range(1, width // LANES):
            psum = psum + p[:, t * LANES:(t + 1) * LANES]
        pv = jnp.dot(p.astype(BF16), v_ref[0, 0, keys, :], preferred_element_type=F32)
        if first:
            l_sc[...] = psum
            acc_sc[...] = pv
        else:
            l_sc[...] = alpha * l_sc[...] + psum
            acc_sc[...] = alpha * acc_sc[...] + pv
        m_sc[...] = m_next

    @pl.when(j == 0)
    def _():
        ctx_keys = slice(0, n_ctx)
        softmax_pv(scores(ctx_keys), ctx_keys, True)

    @pl.when(j > 0)
    def _():
        def put(c):
            keys = chunk_keys(c)
            bufs[c % 2][:, 0:keys.stop - keys.start] = scores(keys)

        def take(c):
            keys = chunk_keys(c)
            softmax_pv(bufs[c % 2][:, 0:keys.stop - keys.start], keys, c == 0)

        put(0)
        for c in range(n_chunks):
            if c + 1 < n_chunks:
                put(c + 1)
            take(c)

    o = acc_sc[...] / jnp.sum(l_sc[...], axis=1, keepdims=True)
    if diff:
        lv = lam_ref[...]
        lam = (jnp.exp(jnp.sum(lv[0:1] * lv[1:2], axis=1, keepdims=True))
               - jnp.exp(jnp.sum(lv[2:3] * lv[3:4], axis=1, keepdims=True)) + lam_init)
        od = o[0:TM] - lam * o[TM:rows]
        od = od * lax.rsqrt(jnp.mean(od * od, axis=-1, keepdims=True) + EPS)
        o_ref[...] = ((od * gout_ref[...]) * (1.0 - lam_init)).astype(o_ref.dtype)
    else:
        o_ref[...] = jnp.where(low, o[0:TM], o[TM:rows]).astype(o_ref.dtype)


def _attention(q, kt, v, batch, nblk, *, diff, lam_vecs=None, g_out=None, lam_init=0.0):
    kv_heads = kt.shape[1]
    s_len = kt.shape[3]
    tiles = BRANCH_WIDTH // LANES
    rows = 2 * TM
    q_map = lambda b, h, j: (b * nblk + j, h)
    kv_map = lambda b, h, j: (b, h * kv_heads // tiles, 0, 0)
    const = lambda b, h, j: (0, 0)
    in_specs = [pl.BlockSpec((TM, LANES), q_map),
                pl.BlockSpec((1, 1, LANES, s_len), kv_map),
                pl.BlockSpec((1, 1, s_len, LANES), kv_map)]
    args = [q, kt, v]
    if diff:
        in_specs += [pl.BlockSpec((4, HEAD_DIM), const), pl.BlockSpec((1, LANES), const)]
        args += [lam_vecs, g_out]
    n_ctx = TM
    tkl = min(SCORE_KEYS, s_len - n_ctx)
    assert (s_len - n_ctx) % tkl == 0
    bounds = [0] + list(range(n_ctx + tkl, s_len + 1, tkl))
    kern = functools.partial(_attn_kernel, diff=diff, lam_init=lam_init,
                             n_ctx=n_ctx, bounds=tuple(bounds))
    return pl.pallas_call(
        kern,
        grid=(batch, tiles, nblk),
        in_specs=in_specs,
        out_specs=pl.BlockSpec((TM, LANES), q_map),
        out_shape=jax.ShapeDtypeStruct((batch * s_len, BRANCH_WIDTH), BF16),
        scratch_shapes=([pltpu.VMEM((rows, LANES), BF16)]
                        + [pltpu.VMEM((rows, LANES), F32)] * 3
                        + [pltpu.VMEM((rows, n_ctx + tkl), F32)] * 2),
        compiler_params=_cparams(3),
        name="diff_attn" if diff else "gqa_attn",
    )(*args)


def _s5_kernel(uf_ref, ub_ref, pin_ref, pout_ref, wd_ref, a_ref, cm_ref, yf_ref, yb_ref,
               bu_sc, st_sc):
    n_tiles = BRANCH_WIDTH // LANES
    half = SUBLANES * S5_STATE
    rows = TC * SUBLANES
    tok = (SUBLANES // 2) * TC

    @pl.when(pl.program_id(0) == 0)
    def _():
        st_sc[...] = jnp.zeros_like(st_sc)

    seq = lax.broadcasted_iota(jnp.int32, (rows, LANES), 0) & (SUBLANES - 1)
    fwd = seq < SUBLANES // 2
    lhs_dir = []
    for d, ref in enumerate((uf_ref, ub_ref)):
        ud = ref[...].reshape(tok, BRANCH_WIDTH).astype(BF16)
        lhs_dir.append(jnp.dot(pin_ref[d], ud, preferred_element_type=F32).astype(BF16))
    for j in range(n_tiles):
        cs = slice(j * LANES, (j + 1) * LANES)
        lhs = jnp.concatenate([lhs_dir[0][:, cs], lhs_dir[1][:, cs]], axis=1)
        bu_sc[:, 2 * half * j:2 * half * (j + 1)] = jnp.dot(
            lhs, wd_ref[j], preferred_element_type=F32)

    for j in range(n_tiles):
        re = slice(2 * half * j, 2 * half * j + half)
        im = slice(2 * half * j + half, 2 * half * (j + 1))
        ar = a_ref[0, :, half * j:half * (j + 1)]
        ai = a_ref[1, :, half * j:half * (j + 1)]

        def step(t, carry, re=re, im=im, ar=ar, ai=ai):
            xr, xi = carry
            r = pl.multiple_of(t * SUBLANES, SUBLANES)
            nxr = ar * xr - ai * xi + bu_sc[pl.ds(r, SUBLANES), re]
            nxi = ar * xi + ai * xr + bu_sc[pl.ds(r, SUBLANES), im]
            bu_sc[pl.ds(r, SUBLANES), re] = nxr
            bu_sc[pl.ds(r, SUBLANES), im] = nxi
            return nxr, nxi

        xr, xi = lax.fori_loop(0, TC, step, (st_sc[:, re], st_sc[:, im]), unroll=True)
        st_sc[:, re] = xr
        st_sc[:, im] = xi

    ys = []
    for j in range(n_tiles):
        x = bu_sc[:, 2 * half * j:2 * half * (j + 1)].astype(BF16)
        yy = jnp.dot(x, cm_ref[j], preferred_element_type=F32)
        ys.append(jnp.where(fwd, yy[:, :LANES], yy[:, LANES:]))
    y = jnp.concatenate(ys, axis=1)
    parts = []
    rest = y
    for _ in range(2):
        part = rest.astype(BF16)
        parts.append(part)
        rest = rest - part.astype(F32)
    for d, ref in enumerate((yf_ref, yb_ref)):
        out = None
        for part in parts:
            term = jnp.dot(pout_ref[d], part, preferred_element_type=F32)
            out = term if out is None else out + term
        ref[...] = out.reshape(ref.shape)


def _scan_row_placement(n_seq):
    p = np.zeros((2, TC * 2 * n_seq, n_seq * TC), np.float32)
    for b in range(n_seq):
        for k in range(TC):
            p[0, 2 * n_seq * k + b, b * TC + k] = 1.0
            p[1, 2 * n_seq * (TC - 1 - k) + n_seq + b, b * TC + k] = 1.0
    return jnp.asarray(p, BF16), jnp.asarray(p.transpose(0, 2, 1), BF16)


def _s5_scan(u3, wd, a8, cm, n_ctx):
    batch, s_len, _ = u3.shape
    rows = TC * SUBLANES
    n_tiles = BRANCH_WIDTH // LANES
    n_state = 2 * SUBLANES * S5_STATE * n_tiles
    n_steps = s_len // TC
    ctx_steps = n_ctx // TC
    p_in, p_out = _scan_row_placement(batch)
    fwd_map = lambda g: (0, g, 0)
    bwd_map = lambda g: (0, jnp.where(g < ctx_steps, ctx_steps - 1 - g,
                                      n_steps + ctx_steps - 1 - g), 0)
    c3 = lambda g: (0, 0, 0)
    blk = (batch, TC, BRANCH_WIDTH)
    out = jax.ShapeDtypeStruct(u3.shape, F32)
    return pl.pallas_call(
        _s5_kernel,
        grid=(n_steps,),
        in_specs=[pl.BlockSpec(blk, fwd_map),
                  pl.BlockSpec(blk, bwd_map),
                  pl.BlockSpec(p_in.shape, c3),
                  pl.BlockSpec(p_out.shape, c3),
                  pl.BlockSpec(wd.shape, c3),
                  pl.BlockSpec(a8.shape, c3),
                  pl.BlockSpec(cm.shape, c3)],
        out_specs=[pl.BlockSpec(blk, fwd_map), pl.BlockSpec(blk, bwd_map)],
        out_shape=[out, out],
        scratch_shapes=[pltpu.VMEM((rows, n_state), F32),
                        pltpu.VMEM((SUBLANES, n_state), F32)],
        compiler_params=_cparams(1),
        name="s5_scan",
    )(u3, u3, p_in, p_out, wd, a8, cm)


def _zoh(lam_re, lam_im, log_dt, b_re, b_im):
    dt = jnp.exp(log_dt)[..., None]
    mag = jnp.exp(lam_re * dt)
    a_re = mag * jnp.cos(lam_im * dt)
    a_im = mag * jnp.sin(lam_im * dt)
    den = lam_re * lam_re + lam_im * lam_im
    f_re = ((a_re - 1.0) * lam_re + a_im * lam_im) / den
    f_im = (a_im * lam_re - (a_re - 1.0) * lam_im) / den
    bb_re = f_re[..., None] * b_re - f_im[..., None] * b_im
    bb_im = f_re[..., None] * b_im + f_im[..., None] * b_re
    return a_re, a_im, bb_re, bb_im


def _s5_weights(lam_re, lam_im, log_dt, b_re, b_im, c_re, c_im, batch):
    n_tiles = BRANCH_WIDTH // LANES
    gpt = S5_GROUPS // n_tiles
    a_re, a_im, bb_re, bb_im = _zoh(lam_re, lam_im, log_dt, b_re, b_im)
    eye = jnp.eye(gpt, dtype=F32)

    def drive(bb):
        t = bb.reshape(2, n_tiles, gpt, S5_STATE, S5_CH)
        w = jnp.einsum('djgpc,gh->jdgchp', t, eye)
        return w.reshape(n_tiles, 2 * gpt * S5_CH, gpt * S5_STATE)

    def read(cc):
        t = cc.reshape(2, n_tiles, gpt, S5_CH, S5_STATE)
        w = jnp.einsum('djgcp,gh->jgpdhc', t, eye)
        return w.reshape(n_tiles, gpt * S5_STATE, 2 * gpt * S5_CH)

    wd = jnp.concatenate([drive(bb_re), drive(bb_im)], axis=2).astype(BF16)
    cm = jnp.concatenate([read(c_re), read(-c_im)], axis=1).astype(BF16)

    def per_seq(a):
        return jnp.repeat(a.reshape(2, 1, S5_GROUPS * S5_STATE), batch, axis=1).reshape(
            2 * batch, S5_GROUPS * S5_STATE)

    a8 = jnp.stack([per_seq(a_re), per_seq(a_im)])
    return wd, a8, cm


def _merge_kernel(x_ref, xc_ref, mod_ref, g1_ref, g2_ref, ya_ref, yd_ref, yf_ref, yr_ref, u_ref,
                  dsk_ref, wglu_ref, bglu_ref, wgate_ref, wbr_ref, wout_ref, x1_ref, h2_ref,
                  *, nblk):
    mods = mod_ref[0]
    x = _stream_tile(x_ref, xc_ref, nblk)
    h = _rms_mod(x, g1_ref[...], mods[0:1], mods[1:2]).astype(BF16)
    ys = yf_ref[...] + yr_ref[...] + dsk_ref[...] * u_ref[...]
    g = jax.nn.gelu(ys)
    yb = g * jax.nn.sigmoid(
        jnp.dot(g.astype(BF16), wglu_ref[...], preferred_element_type=F32) + bglu_ref[...])
    branches = (ya_ref[...], yb.astype(BF16), yd_ref[...])
    m = None
    for k, y in enumerate(branches):
        gate = jax.nn.sigmoid(jnp.dot(h, wgate_ref[:, k * D_MODEL:(k + 1) * D_MODEL],
                                      preferred_element_type=F32))
        term = gate * jnp.dot(y, wbr_ref[k], preferred_element_type=F32)
        m = term if m is None else m + term
    y = jnp.dot(m.astype(BF16), wout_ref[...], preferred_element_type=F32)
    x1 = x + mods[2:3] * y
    x1_ref[...] = x1
    h2_ref[...] = _rms_mod(x1, g2_ref[...], mods[3:4], mods[4:5]).astype(BF16)


def _merge(x_lat, x_ctx, mods, g1, g2, ya, yd, yf, yr, u, d_skip, w_glu, b_glu, w_gate,
           w_branch, w_out, batch, nblk):
    t_rows = ya.shape[0]
    row = lambda i: (i, 0)
    mod_map = lambda i: (jnp.where(i % nblk == 0, batch, i // nblk), 0, 0)
    c2 = lambda i: (0, 0)
    c3 = lambda i: (0, 0, 0)
    x_specs, x_args = _stream_specs(x_lat, x_ctx, nblk)
    return pl.pallas_call(
        functools.partial(_merge_kernel, nblk=nblk),
        grid=(t_rows // TM,),
        in_specs=x_specs + [
                  pl.BlockSpec((1, 6, D_MODEL), mod_map),
                  pl.BlockSpec((1, D_MODEL), c2),
                  pl.BlockSpec((1, D_MODEL), c2),
                  pl.BlockSpec((TM, BRANCH_WIDTH), row),
                  pl.BlockSpec((TM, BRANCH_WIDTH), row),
                  pl.BlockSpec((TM, BRANCH_WIDTH), row),
                  pl.BlockSpec((TM, BRANCH_WIDTH), row),
                  pl.BlockSpec((TM, BRANCH_WIDTH), row),
                  pl.BlockSpec((1, BRANCH_WIDTH), c2),
                  pl.BlockSpec((BRANCH_WIDTH, BRANCH_WIDTH), c2),
                  pl.BlockSpec((1, BRANCH_WIDTH), c2),
                  pl.BlockSpec((D_MODEL, 3 * D_MODEL), c2),
                  pl.BlockSpec((3, BRANCH_WIDTH, D_MODEL), c3),
                  pl.BlockSpec((D_MODEL, D_MODEL), c2)],
        out_specs=[pl.BlockSpec((TM, D_MODEL), row), pl.BlockSpec((TM, D_MODEL), row)],
        out_shape=[jax.ShapeDtypeStruct((t_rows, D_MODEL), F32),
                   jax.ShapeDtypeStruct((t_rows, D_MODEL), BF16)],
        compiler_params=_cparams(1),
        name="merge",
    )(*x_args, mods, g1, g2, ya, yd, yf, yr, u, d_skip, w_glu, b_glu, w_gate, w_branch, w_out)


def _ffn_kernel(h_ref, hp_ref, hn_ref, x1_ref, mod_ref, wup_ref, cw_ref, wdn_ref, o_ref,
                lhs_sc, u0_sc, u1_sc, act_sc, *, nblk, n_chunks):
    pj = pl.program_id(0) % nblk
    left_ok = (pj >= 2).astype(F32)
    right_ok = jnp.logical_and(pj != 0, pj != nblk - 1).astype(F32)
    lhs_sc[0:HALO, :] = (hp_ref[...].astype(F32) * left_ok).astype(BF16)
    lhs_sc[HALO:HALO + TM, :] = h_ref[...]
    lhs_sc[HALO + TM:2 * HALO + TM, :] = (hn_ref[...].astype(F32) * right_ok).astype(BF16)

    def up(k, buf):
        buf[...] = jnp.dot(lhs_sc[...], wup_ref[k], preferred_element_type=F32)

    def activate(k, buf):
        cw = cw_ref[k]
        ext = buf[HALO - SUBLANES:HALO + TM + SUBLANES, :]
        n_ext = TM + 2 * SUBLANES
        prev = pltpu.roll(ext, 1, 0)[SUBLANES:SUBLANES + TM]
        nxt = pltpu.roll(ext, n_ext - 1, 0)[SUBLANES:SUBLANES + TM]
        c = (prev * cw[0:1] + ext[SUBLANES:SUBLANES + TM] * cw[1:2] + nxt * cw[2:3] + cw[3:4])
        act = jax.nn.silu(c[:, FF_CHUNK:]) * c[:, :FF_CHUNK]
        act_sc[:, k * FF_CHUNK:(k + 1) * FF_CHUNK] = act.astype(BF16)

    bufs = (u0_sc, u1_sc)
    up(0, bufs[0])
    for k in range(n_chunks):
        if k + 1 < n_chunks:
            up(k + 1, bufs[(k + 1) % 2])
        activate(k, bufs[k % 2])
    y = jnp.dot(act_sc[...], wdn_ref[...], preferred_element_type=F32)
    o_ref[...] = x1_ref[...] + mod_ref[0][5:6] * y


def _ffn(h2, x1, mods, wup, cw, wdn, batch, nblk, latent_only):
    t_rows = x1.shape[0]
    n_chunks = wup.shape[0]
    per = TM // HALO
    last = t_rows // HALO - 1
    row = lambda i: (i, 0)
    mod_map = lambda i: (jnp.where(i % nblk == 0, batch, i // nblk), 0, 0)
    kern = functools.partial(_ffn_kernel, nblk=nblk, n_chunks=n_chunks)
    if latent_only:
        out_map = lambda i: ((i // nblk) * (nblk - 1) + jnp.maximum(i % nblk - 1, 0), 0)
        out_rows = t_rows - batch * TM
    else:
        out_map, out_rows = row, t_rows
    return pl.pallas_call(
        kern,
        grid=(t_rows // TM,),
        in_specs=[pl.BlockSpec((TM, D_MODEL), row),
                  pl.BlockSpec((HALO, D_MODEL), lambda i: (jnp.maximum(i * per - 1, 0), 0)),
                  pl.BlockSpec((HALO, D_MODEL), lambda i: (jnp.minimum((i + 1) * per, last), 0)),
                  pl.BlockSpec((TM, D_MODEL), row),
                  pl.BlockSpec((1, 6, D_MODEL), mod_map),
                  pl.BlockSpec(wup.shape, lambda i: (0, 0, 0)),
                  pl.BlockSpec(cw.shape, lambda i: (0, 0, 0)),
                  pl.BlockSpec(wdn.shape, lambda i: (0, 0))],
        out_specs=pl.BlockSpec((TM, D_MODEL), out_map),
        out_shape=jax.ShapeDtypeStruct((out_rows, D_MODEL), F32),
        scratch_shapes=([pltpu.VMEM((TM + 2 * HALO, D_MODEL), BF16)]
                        + [pltpu.VMEM((TM + 2 * HALO, 2 * FF_CHUNK), F32)] * 2
                        + [pltpu.VMEM((TM, D_FF), BF16)]),
        compiler_params=_cparams(1),
        name="conv_ffn",
    )(h2, h2, h2, x1, mods, wup, cw, wdn)


def _rope_tables(seq, ctx):
    rows = seq // GRID_W
    row = jnp.repeat(jnp.arange(rows, dtype=F32), GRID_W)
    col = jnp.tile(jnp.arange(GRID_W, dtype=F32), rows)
    n_freq = HEAD_DIM // 4
    inv_freq = ROPE_BASE ** (-jnp.arange(n_freq, dtype=F32) / n_freq)
    ang = jnp.concatenate([row[:, None] * inv_freq, col[:, None] * inv_freq], axis=-1)
    ang = jnp.concatenate([ang, ang], axis=-1)
    cos = jnp.concatenate([jnp.ones((ctx, HEAD_DIM), F32), jnp.cos(ang)], axis=0)
    sin = jnp.concatenate([jnp.zeros((ctx, HEAD_DIM), F32), jnp.sin(ang)], axis=0)
    sign = jnp.where(jnp.arange(HEAD_DIM) < HEAD_DIM // 2, -1.0, 1.0).astype(F32)
    two = lambda t: jnp.concatenate([t, t], axis=1)
    return two(cos), two(sin * sign)


def kernel(x, c, ctx, c_ctx, w_ada, b_ada, norm_g, w_in, qk_gain, ssm_lam_re, ssm_lam_im, ssm_log_dt, ssm_b_re, ssm_b_im, ssm_c_re, ssm_c_im, ssm_d, w_glu, b_glu, diff_lam, diff_norm_g, w_branch, w_out, w_up, conv_w, conv_b, w_down):
    batch, seq, _ = x.shape
    n_ctx = ctx.shape[1]
    depth = w_in.shape[0]
    assert n_ctx == TM and seq % TM == 0 and 2 * batch == SUBLANES
    s_len = n_ctx + seq
    nblk = s_len // TM
    assert s_len % TC == 0 and D_FF % FF_CHUNK == 0
    n_chunks = D_FF // FF_CHUNK

    cos_t, sin_t = _rope_tables(seq, n_ctx)
    cc = jnp.concatenate([c, c_ctx[None, :], jnp.zeros((SUBLANES - batch - 1, D_MODEL), F32)], axis=0)
    mods_all = _ada_mods(cc, w_ada, b_ada).reshape(depth, SUBLANES, 6, D_MODEL)

    xa = x.reshape(batch * seq, D_MODEL)
    xc = ctx.reshape(batch * n_ctx, D_MODEL)
    for i in range(depth):
        lam_init = 0.8 - 0.6 * math.exp(-0.3 * i)
        mods = mods_all[i]
        g1 = norm_g[i, 0:1]
        g2 = norm_g[i, 1:2]
        w_i = w_in[i].astype(BF16)
        qa, kta, va, u, qc, ktc, vc = _inproj(xa, xc, mods, g1, w_i[:, :N_QKVU],
                                              jnp.tile(qk_gain[i], (1, BRANCH_WIDTH // HEAD_DIM)),
                                              cos_t, sin_t, batch, nblk)
        ya = _attention(qa, kta, va, batch, nblk, diff=False)
        yd = _attention(qc, ktc, vc, batch, nblk, diff=True, lam_vecs=diff_lam[i],
                        g_out=diff_norm_g[i][None, :], lam_init=lam_init)

        wd, a8, cm = _s5_weights(ssm_lam_re[i], ssm_lam_im[i], ssm_log_dt[i], ssm_b_re[i],
                                 ssm_b_im[i], ssm_c_re[i], ssm_c_im[i], batch)
        yf, yr = _s5_scan(u.reshape(batch, s_len, BRANCH_WIDTH), wd, a8, cm, n_ctx)
        yf = yf.reshape(batch * s_len, BRANCH_WIDTH)
        yr = yr.reshape(batch * s_len, BRANCH_WIDTH)

        x1, h2 = _merge(xa, xc, mods, g1, g2, ya, yd, yf, yr, u, ssm_d[i][None, :],
                        w_glu[i].astype(BF16), b_glu[i][None, :], w_i[:, N_QKVU:],
                        w_branch[i].astype(BF16), w_out[i].astype(BF16), batch, nblk)

        def chunked(t):
            a = t[:, :D_FF].reshape(t.shape[0], n_chunks, FF_CHUNK)
            g = t[:, D_FF:].reshape(t.shape[0], n_chunks, FF_CHUNK)
            return jnp.concatenate([a, g], axis=-1).transpose(1, 0, 2)

        wup = chunked(w_up[i].astype(BF16))
        cw = chunked(jnp.concatenate(
            [conv_w[i], conv_b[i][None, :], jnp.zeros((SUBLANES - 4, 2 * D_FF), F32)], axis=0))
        wdn = w_down[i].astype(BF16)
        xa = _ffn(h2, x1, mods, wup, cw, wdn, batch, nblk, latent_only=(i == depth - 1))
        xc = None

    return xa.reshape(batch, seq, D_MODEL)
```

```python
import functools
import math

import jax
import jax.numpy as jnp
import numpy as np
from jax import lax
from jax.experimental import pallas as pl
from jax.experimental.pallas import tpu as pltpu

F32 = jnp.float32
BF16 = jnp.bfloat16

D_MODEL = 1024
HEAD_DIM = 64
GRID_W = 64
ROPE_BASE = 10000.0
EPS = 1e-6
BRANCH_WIDTH = D_MODEL // 2
A_HEADS = BRANCH_WIDTH // HEAD_DIM
A_KV_HEADS = A_HEADS // 4
C_HEADS = BRANCH_WIDTH // (2 * HEAD_DIM)
S5_CH = 16
S5_STATE = 64
S5_GROUPS = BRANCH_WIDTH // S5_CH
D_FF = 2816
IN_SIZES = (512, 128, 128, 512, 512, 512, 512, 3 * D_MODEL)
IN_OFFS = tuple(int(v) for v in np.cumsum((0,) + IN_SIZES))
N_QKVU = IN_OFFS[7]

LANES = 128
SUBLANES = 8
TM = 256
SCORE_KEYS = 2048
TC = 64
FF_CHUNK = 128
HALO = 16
ADA_COLS = 1536
V7X_VMEM_BYTES = 64 * 1024 * 1024
VMEM_LIMIT = V7X_VMEM_BYTES * 7 // 8
SCORE_SCALE = HEAD_DIM ** -0.5 * math.log2(math.e)


def _cparams(n_axes):
    return pltpu.CompilerParams(dimension_semantics=("arbitrary",) * n_axes,
                                vmem_limit_bytes=VMEM_LIMIT)


def _rms_mod(x, gain, shift, scale):
    y = x * lax.rsqrt(jnp.mean(x * x, axis=-1, keepdims=True) + EPS)
    return (y * gain) * (1.0 + scale) + shift


def _head_avg_matrix(width):
    shift = HEAD_DIM.bit_length() - 1
    r = lax.broadcasted_iota(jnp.int32, (width, width), 0) >> shift
    c = lax.broadcasted_iota(jnp.int32, (width, width), 1) >> shift
    return jnp.where(r == c, 1.0 / HEAD_DIM, 0.0).astype(BF16)


def _head_rms(z, gain, avg):
    sq = z * z
    hi = sq.astype(BF16)
    lo = (sq - hi.astype(F32)).astype(BF16)
    ms = (jnp.dot(hi, avg, preferred_element_type=F32)
          + jnp.dot(lo, avg, preferred_element_type=F32))
    return z * lax.rsqrt(ms + EPS) * gain


def _tile_lanes(t, width):
    reps = width // t.shape[1]
    return t if reps == 1 else jnp.concatenate([t] * reps, axis=1)


def _rope(z, cos, sin_signed):
    width = z.shape[1]
    lane = lax.broadcasted_iota(jnp.int32, z.shape, 1)
    first_half = (lane & (HEAD_DIM - 1)) < HEAD_DIM // 2
    rot = jnp.where(first_half,
                    pltpu.roll(z, width - HEAD_DIM // 2, 1),
                    pltpu.roll(z, HEAD_DIM // 2, 1))
    return z * _tile_lanes(cos, width) + rot * _tile_lanes(sin_signed, width)


def _dup_halves(z):
    lane = lax.broadcasted_iota(jnp.int32, z.shape, 1)
    low = lane < HEAD_DIM
    sw = pltpu.roll(z, HEAD_DIM, 1)
    return jnp.where(low, z, sw), jnp.where(low, sw, z)


def _ada_kernel(c_ref, w_ref, b_ref, o_ref):
    a = jax.nn.silu(c_ref[...])
    o_ref[0] = jnp.dot(a, w_ref[0], preferred_element_type=F32,
                       precision=lax.Precision.HIGHEST) + b_ref[0]


def _ada_mods(cc, w_ada, b_ada):
    depth, _, n = w_ada.shape
    tn = ADA_COLS
    assert n % tn == 0
    return pl.pallas_call(
        _ada_kernel,
        grid=(depth, n // tn),
        in_specs=[pl.BlockSpec((SUBLANES, D_MODEL), lambda l, j: (0, 0)),
                  pl.BlockSpec((1, D_MODEL, tn), lambda l, j: (l, 0, j)),
                  pl.BlockSpec((1, 1, tn), lambda l, j: (l, 0, j))],
        out_specs=pl.BlockSpec((1, SUBLANES, tn), lambda l, j: (l, 0, j)),
        out_shape=jax.ShapeDtypeStruct((depth, SUBLANES, n), F32),
        compiler_params=_cparams(2),
        name="ada_mods",
    )(cc, w_ada, b_ada.reshape(depth, 1, n))


def _stream_specs(x_lat, x_ctx, nblk):
    blk = (TM, D_MODEL)
    if x_ctx is None:
        return ([pl.BlockSpec(blk, lambda i: (i, 0)),
                 pl.BlockSpec(blk, lambda i: ((i // nblk) * nblk, 0))], [x_lat, x_lat])
    lat_map = lambda i: ((i // nblk) * (nblk - 1) + jnp.maximum(i % nblk - 1, 0), 0)
    return ([pl.BlockSpec(blk, lat_map), pl.BlockSpec(blk, lambda i: (i // nblk, 0))],
            [x_lat, x_ctx])


def _stream_tile(x_ref, xc_ref, nblk):
    return jnp.where(pl.program_id(0) % nblk == 0, xc_ref[...], x_ref[...])


def _inproj_kernel(x_ref, xc_ref, mod_ref, g_ref, w_ref, qk_ref, cos_ref, sin_ref,
                   qa_ref, kta_ref, va_ref, u_ref, qc_ref, ktc_ref, vc_ref, *, nblk):
    mods = mod_ref[0]
    x = _stream_tile(x_ref, xc_ref, nblk)
    h = _rms_mod(x, g_ref[...], mods[0:1], mods[1:2]).astype(BF16)
    cos = cos_ref[...]
    sin = sin_ref[...]
    avg = _head_avg_matrix(BRANCH_WIDTH)

    def seg(k):
        return jnp.dot(h, w_ref[:, IN_OFFS[k]:IN_OFFS[k + 1]], preferred_element_type=F32)

    def gain(k, width=BRANCH_WIDTH):
        return qk_ref[k:k + 1, :width]

    k = _rope(_head_rms(seg(5), gain(3), avg), cos, sin)
    for hh in range(C_HEADS):
        ktc_ref[0, hh] = k[:, hh * LANES:(hh + 1) * LANES].T.astype(BF16)
    k = _rope(_head_rms(seg(1), gain(1, LANES), avg[:LANES, :LANES]), cos, sin)
    for hh, kd in enumerate(_dup_halves(k)):
        kta_ref[0, hh] = kd.T.astype(BF16)
    q = _rope(_head_rms(seg(4), gain(2), avg), cos, sin)
    qc_ref[...] = (q * SCORE_SCALE).astype(BF16)
    q = _rope(_head_rms(seg(0), gain(0), avg), cos, sin)
    qa_ref[...] = (q * SCORE_SCALE).astype(BF16)
    for hh, vd in enumerate(_dup_halves(seg(2))):
        va_ref[0, hh] = vd.astype(BF16)
    v = seg(6)
    for hh in range(C_HEADS):
        vc_ref[0, hh] = v[:, hh * LANES:(hh + 1) * LANES].astype(BF16)
    u_ref[...] = seg(3)


def _inproj(x_lat, x_ctx, mods, norm_g, w_qkvu, qk_gain, cos_t, sin_t, batch, nblk):
    s_len = nblk * TM
    t_rows = batch * s_len
    row = lambda i: (i, 0)
    mod_map = lambda i: (jnp.where(i % nblk == 0, batch, i // nblk), 0, 0)
    pos = lambda i: (i % nblk, 0)
    kt_map = lambda i: (i // nblk, 0, 0, i % nblk)
    v_map = lambda i: (i // nblk, 0, i % nblk, 0)
    x_specs, x_args = _stream_specs(x_lat, x_ctx, nblk)
    return pl.pallas_call(
        functools.partial(_inproj_kernel, nblk=nblk),
        grid=(t_rows // TM,),
        in_specs=x_specs + [
                  pl.BlockSpec((1, 6, D_MODEL), mod_map),
                  pl.BlockSpec((1, D_MODEL), lambda i: (0, 0)),
                  pl.BlockSpec((D_MODEL, N_QKVU), lambda i: (0, 0)),
                  pl.BlockSpec((4, BRANCH_WIDTH), lambda i: (0, 0)),
                  pl.BlockSpec((TM, LANES), pos),
                  pl.BlockSpec((TM, LANES), pos)],
        out_specs=[pl.BlockSpec((TM, BRANCH_WIDTH), row),
                   pl.BlockSpec((1, A_KV_HEADS, LANES, TM), kt_map),
                   pl.BlockSpec((1, A_KV_HEADS, TM, LANES), v_map),
                   pl.BlockSpec((TM, BRANCH_WIDTH), row),
                   pl.BlockSpec((TM, BRANCH_WIDTH), row),
                   pl.BlockSpec((1, C_HEADS, LANES, TM), kt_map),
                   pl.BlockSpec((1, C_HEADS, TM, LANES), v_map)],
        out_shape=[jax.ShapeDtypeStruct((t_rows, BRANCH_WIDTH), BF16),
                   jax.ShapeDtypeStruct((batch, A_KV_HEADS, LANES, s_len), BF16),
                   jax.ShapeDtypeStruct((batch, A_KV_HEADS, s_len, LANES), BF16),
                   jax.ShapeDtypeStruct((t_rows, BRANCH_WIDTH), F32),
                   jax.ShapeDtypeStruct((t_rows, BRANCH_WIDTH), BF16),
                   jax.ShapeDtypeStruct((batch, C_HEADS, LANES, s_len), BF16),
                   jax.ShapeDtypeStruct((batch, C_HEADS, s_len, LANES), BF16)],
        compiler_params=_cparams(1),
        name="in_proj",
    )(*x_args, mods, norm_g, w_qkvu, qk_gain, cos_t, sin_t)


def _attn_kernel(*refs, diff, lam_init, n_ctx, bounds):
    if diff:
        q_ref, kt_ref, v_ref, lam_ref, gout_ref, o_ref = refs[:6]
    else:
        q_ref, kt_ref, v_ref, o_ref = refs[:4]
    lhs_sc, m_sc, l_sc, acc_sc, s0_sc, s1_sc = refs[-6:]
    bufs = (s0_sc, s1_sc)
    n_chunks = len(bounds) - 1
    j = pl.program_id(2)
    rows = 2 * TM
    lane = lax.broadcasted_iota(jnp.int32, (TM, LANES), 1)
    low = lane < HEAD_DIM
    qt = q_ref[...].astype(F32)
    lhs_sc[0:TM, :] = jnp.where(low, qt, 0.0).astype(BF16)
    lhs_sc[TM:rows, :] = jnp.where(low, 0.0, qt).astype(BF16)

    def chunk_keys(c):
        return slice(bounds[c], bounds[c + 1])

    def scores(keys):
        return jnp.dot(lhs_sc[...], kt_ref[0, 0, :, keys], preferred_element_type=F32)

    def softmax_pv(s, keys, first):
        width = s.shape[1]
        mx = s[:, 0:LANES]
        for t in range(1, width // LANES):
            mx = jnp.maximum(mx, s[:, t * LANES:(t + 1) * LANES])
        m_cur = jnp.max(mx, axis=1, keepdims=True)
        if first:
            m_next = jnp.broadcast_to(m_cur, (rows, LANES))
        else:
            m_prev = m_sc[...]
            m_next = jnp.maximum(m_prev, m_cur)
            alpha = jnp.exp2(m_prev - m_next)
        p = jnp.exp2(s - _tile_lanes(m_next, width))
        psum = p[:, 0:LANES]
        for t in range(1, width // LANES):
            psum = psum + p[:, t * LANES:(t + 1) * LANES]
        pv = jnp.dot(p.astype(BF16), v_ref[0, 0, keys, :], preferred_element_type=F32)
        if first:
            l_sc[...] = psum
            acc_sc[...] = pv
        else:
            l_sc[...] = alpha * l_sc[...] + psum
            acc_sc[...] = alpha * acc_sc[...] + pv
        m_sc[...] = m_next

    @pl.when(j == 0)
    def _():
        ctx_keys = slice(0, n_ctx)
        softmax_pv(scores(ctx_keys), ctx_keys, True)

    @pl.when(j > 0)
    def _():
        def put(c):
            keys = chunk_keys(c)
            bufs[c % 2][:, 0:keys.stop - keys.start] = scores(keys)

        def take(c):
            keys = chunk_keys(c)
            softmax_pv(bufs[c % 2][:, 0:keys.stop - keys.start], keys, c == 0)

        put(0)
        for c in range(n_chunks):
            if c + 1 < n_chunks:
                put(c + 1)
            take(c)

    o = acc_sc[...] / jnp.sum(l_sc[...], axis=1, keepdims=True)
    if diff:
        lv = lam_ref[...]
        lam = (jnp.exp(jnp.sum(lv[0:1] * lv[1:2], axis=1, keepdims=True))
               - jnp.exp(jnp.sum(lv[2:3] * lv[3:4], axis=1, keepdims=True)) + lam_init)
        od = o[0:TM] - lam * o[TM:rows]
        od = od * lax.rsqrt(jnp.mean(od * od, axis=-1, keepdims=True) + EPS)
        o_ref[...] = ((od * gout_ref[...]) * (1.0 - lam_init)).astype(o_ref.dtype)
    else:
        o_ref[...] = jnp.where(low, o[0:TM], o[TM:rows]).astype(o_ref.dtype)


def _attention(q, kt, v, batch, nblk, *, diff, lam_vecs=None, g_out=None, lam_init=0.0):
    kv_heads = kt.shape[1]
    s_len = kt.shape[3]
    tiles = BRANCH_WIDTH // LANES
    rows = 2 * TM
    q_map = lambda b, h, j: (b * nblk + j, h)
    kv_map = lambda b, h, j: (b, h * kv_heads // tiles, 0, 0)
    const = lambda b, h, j: (0, 0)
    in_specs = [pl.BlockSpec((TM, LANES), q_map),
                pl.BlockSpec((1, 1, LANES, s_len), kv_map),
                pl.BlockSpec((1, 1, s_len, LANES), kv_map)]
    args = [q, kt, v]
    if diff:
        in_specs += [pl.BlockSpec((4, HEAD_DIM), const), pl.BlockSpec((1, LANES), const)]
        args += [lam_vecs, g_out]
    n_ctx = TM
    tkl = min(SCORE_KEYS, s_len - n_ctx)
    assert (s_len - n_ctx) % tkl == 0
    bounds = [0] + list(range(n_ctx + tkl, s_len + 1, tkl))
    kern = functools.partial(_attn_kernel, diff=diff, lam_init=lam_init,
                             n_ctx=n_ctx, bounds=tuple(bounds))
    return pl.pallas_call(
        kern,
        grid=(batch, tiles, nblk),
        in_specs=in_specs,
        out_specs=pl.BlockSpec((TM, LANES), q_map),
        out_shape=jax.ShapeDtypeStruct((batch * s_len, BRANCH_WIDTH), BF16),
        scratch_shapes=([pltpu.VMEM((rows, LANES), BF16)]
                        + [pltpu.VMEM((rows, LANES), F32)] * 3
                        + [pltpu.VMEM((rows, n_ctx + tkl), F32)] * 2),
        compiler_params=_cparams(3),
        name="diff_attn" if diff else "gqa_attn",
    )(*args)


def _s5_kernel(uf_ref, ub_ref, pin_ref, pout_ref, wd_ref, a_ref, cm_ref, yf_ref, yb_ref,
               bu_sc, st_sc):
    n_tiles = BRANCH_WIDTH // LANES
    half = SUBLANES * S5_STATE
    rows = TC * SUBLANES
    tok = (SUBLANES // 2) * TC

    @pl.when(pl.program_id(0) == 0)
    def _():
        st_sc[...] = jnp.zeros_like(st_sc)

    seq = lax.broadcasted_iota(jnp.int32, (rows, LANES), 0) & (SUBLANES - 1)
    fwd = seq < SUBLANES // 2
    lhs_dir = []
    for d, ref in enumerate((uf_ref, ub_ref)):
        ud = ref[...].reshape(tok, BRANCH_WIDTH).astype(BF16)
        lhs_dir.append(jnp.dot(pin_ref[d], ud, preferred_element_type=F32).astype(BF16))
    for j in range(n_tiles):
        cs = slice(j * LANES, (j + 1) * LANES)
        lhs = jnp.concatenate([lhs_dir[0][:, cs], lhs_dir[1][:, cs]], axis=1)
        bu_sc[:, 2 * half * j:2 * half * (j + 1)] = jnp.dot(
            lhs, wd_ref[j], preferred_element_type=F32)

    for j in range(n_tiles):
        re = slice(2 * half * j, 2 * half * j + half)
        im = slice(2 * half * j + half, 2 * half * (j + 1))
        ar = a_ref[0, :, half * j:half * (j + 1)]
        ai = a_ref[1, :, half * j:half * (j + 1)]

        def step(t, carry, re=re, im=im, ar=ar, ai=ai):
            xr, xi = carry
            r = pl.multiple_of(t * SUBLANES, SUBLANES)
            nxr = ar * xr - ai * xi + bu_sc[pl.ds(r, SUBLANES), re]
            nxi = ar * xi + ai * xr + bu_sc[pl.ds(r, SUBLANES), im]
            bu_sc[pl.ds(r, SUBLANES), re] = nxr
            bu_sc[pl.ds(r, SUBLANES), im] = nxi
            return nxr, nxi

        xr, xi = lax.fori_loop(0, TC, step, (st_sc[:, re], st_sc[:, im]), unroll=True)
        st_sc[:, re] = xr
        st_sc[:, im] = xi

    ys = []
    for j in range(n_tiles):
        x = bu_sc[:, 2 * half * j:2 * half * (j + 1)].astype(BF16)
        yy = jnp.dot(x, cm_ref[j], preferred_element_type=F32)
        ys.append(jnp.where(fwd, yy[:, :LANES], yy[:, LANES:]))
    y = jnp.concatenate(ys, axis=1)
    parts = []
    rest = y
    for _ in range(2):
        part = rest.astype(BF16)
        parts.append(part)
        rest = rest - part.astype(F32)
    for d, ref in enumerate((yf_ref, yb_ref)):
        out = None
        for part in parts:
            term = jnp.dot(pout_ref[d], part, preferred_element_type=F32)
            out = term if out is None else out + term
        ref[...] = out.reshape(ref.shape)


def _scan_row_placement(n_seq):
    p = np.zeros((2, TC * 2 * n_seq, n_seq * TC), np.float32)
    for b in range(n_seq):
        for k in range(TC):
            p[0, 2 * n_seq * k + b, b * TC + k] = 1.0
            p[1, 2 * n_seq * (TC - 1 - k) + n_seq + b, b * TC + k] = 1.0
    return jnp.asarray(p, BF16), jnp.asarray(p.transpose(0, 2, 1), BF16)


def _s5_scan(u3, wd, a8, cm, n_ctx):
    batch, s_len, _ = u3.shape
    rows = TC * SUBLANES
    n_tiles = BRANCH_WIDTH // LANES
    n_state = 2 * SUBLANES * S5_STATE * n_tiles
    n_steps = s_len // TC
    ctx_steps = n_ctx // TC
    p_in, p_out = _scan_row_placement(batch)
    fwd_map = lambda g: (0, g, 0)
    bwd_map = lambda g: (0, jnp.where(g < ctx_steps, ctx_steps - 1 - g,
                                      n_steps + ctx_steps - 1 - g), 0)
    c3 = lambda g: (0, 0, 0)
    blk = (batch, TC, BRANCH_WIDTH)
    out = jax.ShapeDtypeStruct(u3.shape, F32)
    return pl.pallas_call(
        _s5_kernel,
        grid=(n_steps,),
        in_specs=[pl.BlockSpec(blk, fwd_map),
                  pl.BlockSpec(blk, bwd_map),
                  pl.BlockSpec(p_in.shape, c3),
                  pl.BlockSpec(p_out.shape, c3),
                  pl.BlockSpec(wd.shape, c3),
                  pl.BlockSpec(a8.shape, c3),
                  pl.BlockSpec(cm.shape, c3)],
        out_specs=[pl.BlockSpec(blk, fwd_map), pl.BlockSpec(blk, bwd_map)],
        out_shape=[out, out],
        scratch_shapes=[pltpu.VMEM((rows, n_state), F32),
                        pltpu.VMEM((SUBLANES, n_state), F32)],
        compiler_params=_cparams(1),
        name="s5_scan",
    )(u3, u3, p_in, p_out, wd, a8, cm)


def _zoh(lam_re, lam_im, log_dt, b_re, b_im):
    dt = jnp.exp(log_dt)[..., None]
    mag = jnp.exp(lam_re * dt)
    a_re = mag * jnp.cos(lam_im * dt)
    a_im = mag * jnp.sin(lam_im * dt)
    den = lam_re * lam_re + lam_im * lam_im
    f_re = ((a_re - 1.0) * lam_re + a_im * lam_im) / den
    f_im = (a_im * lam_re - (a_re - 1.0) * lam_im) / den
    bb_re = f_re[..., None] * b_re - f_im[..., None] * b_im
    bb_im = f_re[..., None] * b_im + f_im[..., None] * b_re
    return a_re, a_im, bb_re, bb_im


def _s5_weights(lam_re, lam_im, log_dt, b_re, b_im, c_re, c_im, batch):
    n_tiles = BRANCH_WIDTH // LANES
    gpt = S5_GROUPS // n_tiles
    a_re, a_im, bb_re, bb_im = _zoh(lam_re, lam_im, log_dt, b_re, b_im)
    eye = jnp.eye(gpt, dtype=F32)

    def drive(bb):
        t = bb.reshape(2, n_tiles, gpt, S5_STATE, S5_CH)
        w = jnp.einsum('djgpc,gh->jdgchp', t, eye)
        return w.reshape(n_tiles, 2 * gpt * S5_CH, gpt * S5_STATE)

    def read(cc):
        t = cc.reshape(2, n_tiles, gpt, S5_CH, S5_STATE)
        w = jnp.einsum('djgcp,gh->jgpdhc', t, eye)
        return w.reshape(n_tiles, gpt * S5_STATE, 2 * gpt * S5_CH)

    wd = jnp.concatenate([drive(bb_re), drive(bb_im)], axis=2).astype(BF16)
    cm = jnp.concatenate([read(c_re), read(-c_im)], axis=1).astype(BF16)

    def per_seq(a):
        return jnp.repeat(a.reshape(2, 1, S5_GROUPS * S5_STATE), batch, axis=1).reshape(
            2 * batch, S5_GROUPS * S5_STATE)

    a8 = jnp.stack([per_seq(a_re), per_seq(a_im)])
    return wd, a8, cm


def _merge_kernel(x_ref, xc_ref, mod_ref, g1_ref, g2_ref, ya_ref, yd_ref, yf_ref, yr_ref, u_ref,
                  dsk_ref, wglu_ref, bglu_ref, wgate_ref, wbr_ref, wout_ref, x1_ref, h2_ref,
                  *, nblk):
    mods = mod_ref[0]
    x = _stream_tile(x_ref, xc_ref, nblk)
    h = _rms_mod(x, g1_ref[...], mods[0:1], mods[1:2]).astype(BF16)
    ys = yf_ref[...] + yr_ref[...] + dsk_ref[...] * u_ref[...]
    g = jax.nn.gelu(ys)
    yb = g * jax.nn.sigmoid(
        jnp.dot(g.astype(BF16), wglu_ref[...], preferred_element_type=F32) + bglu_ref[...])
    branches = (ya_ref[...], yb.astype(BF16), yd_ref[...])
    m = None
    for k, y in enumerate(branches):
        gate = jax.nn.sigmoid(jnp.dot(h, wgate_ref[:, k * D_MODEL:(k + 1) * D_MODEL],
                                      preferred_element_type=F32))
        term = gate * jnp.dot(y, wbr_ref[k], preferred_element_type=F32)
        m = term if m is None else m + term
    y = jnp.dot(m.astype(BF16), wout_ref[...], preferred_element_type=F32)
    x1 = x + mods[2:3] * y
    x1_ref[...] = x1
    h2_ref[...] = _rms_mod(x1, g2_ref[...], mods[3:4], mods[4:5]).astype(BF16)


def _merge(x_lat, x_ctx, mods, g1, g2, ya, yd, yf, yr, u, d_skip, w_glu, b_glu, w_gate,
           w_branch, w_out, batch, nblk):
    t_rows = ya.shape[0]
    row = lambda i: (i, 0)
    mod_map = lambda i: (jnp.where(i % nblk == 0, batch, i // nblk), 0, 0)
    c2 = lambda i: (0, 0)
    c3 = lambda i: (0, 0, 0)
    x_specs, x_args = _stream_specs(x_lat, x_ctx, nblk)
    return pl.pallas_call(
        functools.partial(_merge_kernel, nblk=nblk),
        grid=(t_rows // TM,),
        in_specs=x_specs + [
                  pl.BlockSpec((1, 6, D_MODEL), mod_map),
                  pl.BlockSpec((1, D_MODEL), c2),
                  pl.BlockSpec((1, D_MODEL), c2),
                  pl.BlockSpec((TM, BRANCH_WIDTH), row),
                  pl.BlockSpec((TM, BRANCH_WIDTH), row),
                  pl.BlockSpec((TM, BRANCH_WIDTH), row),
                  pl.BlockSpec((TM, BRANCH_WIDTH), row),
                  pl.BlockSpec((TM, BRANCH_WIDTH), row),
                  pl.BlockSpec((1, BRANCH_WIDTH), c2),
                  pl.BlockSpec((BRANCH_WIDTH, BRANCH_WIDTH), c2),
                  pl.BlockSpec((1, BRANCH_WIDTH), c2),
                  pl.BlockSpec((D_MODEL, 3 * D_MODEL), c2),
                  pl.BlockSpec((3, BRANCH_WIDTH, D_MODEL), c3),
                  pl.BlockSpec((D_MODEL, D_MODEL), c2)],
        out_specs=[pl.BlockSpec((TM, D_MODEL), row), pl.BlockSpec((TM, D_MODEL), row)],
        out_shape=[jax.ShapeDtypeStruct((t_rows, D_MODEL), F32),
                   jax.ShapeDtypeStruct((t_rows, D_MODEL), BF16)],
        compiler_params=_cparams(1),
        name="merge",
    )(*x_args, mods, g1, g2, ya, yd, yf, yr, u, d_skip, w_glu, b_glu, w_gate, w_branch, w_out)


def _ffn_kernel(h_ref, hp_ref, hn_ref, x1_ref, mod_ref, perm_ref, wup_ref, cw_ref, wdn_ref, o_ref,
                lhs_sc, u0_sc, u1_sc, act_sc, *, nblk, n_chunks):
    pj = pl.program_id(0) % nblk
    left_ok = (pj >= 2).astype(F32)
    right_ok = jnp.logical_and(pj != 0, pj != nblk - 1).astype(F32)
    lhs_sc[0:TM, :] = jnp.dot(perm_ref[0], h_ref[...], preferred_element_type=F32).astype(BF16)
    lhs_sc[TM:TM + HALO, :] = (hp_ref[...].astype(F32) * left_ok).astype(BF16)
    lhs_sc[TM + HALO:TM + 2 * HALO, :] = (hn_ref[...].astype(F32) * right_ok).astype(BF16)
    sub = lax.broadcasted_iota(jnp.int32, (SUBLANES, 2 * FF_CHUNK), 0)

    def up(k, buf):
        buf[...] = jnp.dot(lhs_sc[...], wup_ref[k], preferred_element_type=F32)

    def activate(k, buf):
        cw = cw_ref[k]
        before = buf[TM + HALO - 1:TM + HALO, :]
        after = buf[TM + HALO:TM + HALO + 1, :]
        first = jnp.where(sub == 0, before, pltpu.roll(buf[TM - SUBLANES:TM, :], 1, 0))
        last = jnp.where(sub == SUBLANES - 1, after,
                         pltpu.roll(buf[0:SUBLANES, :], SUBLANES - 1, 0))
        prev = jnp.concatenate([first, buf[0:TM - SUBLANES, :]], axis=0)
        nxt = jnp.concatenate([buf[SUBLANES:TM, :], last], axis=0)
        c = prev * cw[0:1] + buf[0:TM, :] * cw[1:2] + nxt * cw[2:3] + cw[3:4]
        act = jax.nn.silu(c[:, FF_CHUNK:]) * c[:, :FF_CHUNK]
        act_sc[:, k * FF_CHUNK:(k + 1) * FF_CHUNK] = act.astype(BF16)

    bufs = (u0_sc, u1_sc)
    up(0, bufs[0])
    for k in range(n_chunks):
        if k + 1 < n_chunks:
            up(k + 1, bufs[(k + 1) % 2])
        activate(k, bufs[k % 2])
    act = jnp.dot(perm_ref[1], act_sc[...], preferred_element_type=F32).astype(BF16)
    y = jnp.dot(act, wdn_ref[...], preferred_element_type=F32)
    o_ref[...] = x1_ref[...] + mod_ref[0][5:6] * y


def _strided_row_order():
    p = np.zeros((TM, TM), np.float32)
    groups = TM // SUBLANES
    for r in range(groups):
        for s in range(SUBLANES):
            p[SUBLANES * r + s, groups * s + r] = 1.0
    return jnp.asarray(np.stack([p, p.T]), BF16)


def _ffn(h2, x1, mods, wup, cw, wdn, batch, nblk, latent_only):
    t_rows = x1.shape[0]
    n_chunks = wup.shape[0]
    per = TM // HALO
    last = t_rows // HALO - 1
    row = lambda i: (i, 0)
    mod_map = lambda i: (jnp.where(i % nblk == 0, batch, i // nblk), 0, 0)
    kern = functools.partial(_ffn_kernel, nblk=nblk, n_chunks=n_chunks)
    if latent_only:
        out_map = lambda i: ((i // nblk) * (nblk - 1) + jnp.maximum(i % nblk - 1, 0), 0)
        out_rows = t_rows - batch * TM
    else:
        out_map, out_rows = row, t_rows
    return pl.pallas_call(
        kern,
        grid=(t_rows // TM,),
        in_specs=[pl.BlockSpec((TM, D_MODEL), row),
                  pl.BlockSpec((HALO, D_MODEL), lambda i: (jnp.maximum(i * per - 1, 0), 0)),
                  pl.BlockSpec((HALO, D_MODEL), lambda i: (jnp.minimum((i + 1) * per, last), 0)),
                  pl.BlockSpec((TM, D_MODEL), row),
                  pl.BlockSpec((1, 6, D_MODEL), mod_map),
                  pl.BlockSpec((2, TM, TM), lambda i: (0, 0, 0)),
                  pl.BlockSpec(wup.shape, lambda i: (0, 0, 0)),
                  pl.BlockSpec(cw.shape, lambda i: (0, 0, 0)),
                  pl.BlockSpec(wdn.shape, lambda i: (0, 0))],
        out_specs=pl.BlockSpec((TM, D_MODEL), out_map),
        out_shape=jax.ShapeDtypeStruct((out_rows, D_MODEL), F32),
        scratch_shapes=([pltpu.VMEM((TM + 2 * HALO, D_MODEL), BF16)]
                        + [pltpu.VMEM((TM + 2 * HALO, 2 * FF_CHUNK), F32)] * 2
                        + [pltpu.VMEM((TM, D_FF), BF16)]),
        compiler_params=_cparams(1),
        name="conv_ffn",
    )(h2, h2, h2, x1, mods, _strided_row_order(), wup, cw, wdn)


def _rope_tables(seq, ctx):
    rows = seq // GRID_W
    row = jnp.repeat(jnp.arange(rows, dtype=F32), GRID_W)
    col = jnp.tile(jnp.arange(GRID_W, dtype=F32), rows)
    n_freq = HEAD_DIM // 4
    inv_freq = ROPE_BASE ** (-jnp.arange(n_freq, dtype=F32) / n_freq)
    ang = jnp.concatenate([row[:, None] * inv_freq, col[:, None] * inv_freq], axis=-1)
    ang = jnp.concatenate([ang, ang], axis=-1)
    cos = jnp.concatenate([jnp.ones((ctx, HEAD_DIM), F32), jnp.cos(ang)], axis=0)
    sin = jnp.concatenate([jnp.zeros((ctx, HEAD_DIM), F32), jnp.sin(ang)], axis=0)
    sign = jnp.where(jnp.arange(HEAD_DIM) < HEAD_DIM // 2, -1.0, 1.0).astype(F32)
    two = lambda t: jnp.concatenate([t, t], axis=1)
    return two(cos), two(sin * sign)


def kernel(x, c, ctx, c_ctx, w_ada, b_ada, norm_g, w_in, qk_gain, ssm_lam_re, ssm_lam_im, ssm_log_dt, ssm_b_re, ssm_b_im, ssm_c_re, ssm_c_im, ssm_d, w_glu, b_glu, diff_lam, diff_norm_g, w_branch, w_out, w_up, conv_w, conv_b, w_down):
    batch, seq, _ = x.shape
    n_ctx = ctx.shape[1]
    depth = w_in.shape[0]
    assert n_ctx == TM and seq % TM == 0 and 2 * batch == SUBLANES
    s_len = n_ctx + seq
    nblk = s_len // TM
    assert s_len % TC == 0 and D_FF % FF_CHUNK == 0
    n_chunks = D_FF // FF_CHUNK

    cos_t, sin_t = _rope_tables(seq, n_ctx)
    cc = jnp.concatenate([c, c_ctx[None, :], jnp.zeros((SUBLANES - batch - 1, D_MODEL), F32)], axis=0)
    mods_all = _ada_mods(cc, w_ada, b_ada).reshape(depth, SUBLANES, 6, D_MODEL)

    xa = x.reshape(batch * seq, D_MODEL)
    xc = ctx.reshape(batch * n_ctx, D_MODEL)
    for i in range(depth):
        lam_init = 0.8 - 0.6 * math.exp(-0.3 * i)
        mods = mods_all[i]
        g1 = norm_g[i, 0:1]
        g2 = norm_g[i, 1:2]
        w_i = w_in[i].astype(BF16)
        qa, kta, va, u, qc, ktc, vc = _inproj(xa, xc, mods, g1, w_i[:, :N_QKVU],
                                              jnp.tile(qk_gain[i], (1, BRANCH_WIDTH // HEAD_DIM)),
                                              cos_t, sin_t, batch, nblk)
        ya = _attention(qa, kta, va, batch, nblk, diff=False)
        yd = _attention(qc, ktc, vc, batch, nblk, diff=True, lam_vecs=diff_lam[i],
                        g_out=diff_norm_g[i][None, :], lam_init=lam_init)

        wd, a8, cm = _s5_weights(ssm_lam_re[i], ssm_lam_im[i], ssm_log_dt[i], ssm_b_re[i],
                                 ssm_b_im[i], ssm_c_re[i], ssm_c_im[i], batch)
        yf, yr = _s5_scan(u.reshape(batch, s_len, BRANCH_WIDTH), wd, a8, cm, n_ctx)
        yf = yf.reshape(batch * s_len, BRANCH_WIDTH)
        yr = yr.reshape(batch * s_len, BRANCH_WIDTH)

        x1, h2 = _merge(xa, xc, mods, g1, g2, ya, yd, yf, yr, u, ssm_d[i][None, :],
                        w_glu[i].astype(BF16), b_glu[i][None, :], w_i[:, N_QKVU:],
                        w_branch[i].astype(BF16), w_out[i].astype(BF16), batch, nblk)

        def chunked(t):
            a = t[:, :D_FF].reshape(t.shape[0], n_chunks, FF_CHUNK)
            g = t[:, D_FF:].reshape(t.shape[0], n_chunks, FF_CHUNK)
            return jnp.concatenate([a, g], axis=-1).transpose(1, 0, 2)

        wup = chunked(w_up[i].astype(BF16))
        cw = chunked(jnp.concatenate(
            [conv_w[i], conv_b[i][None, :], jnp.zeros((SUBLANES - 4, 2 * D_FF), F32)], axis=0))
        wdn = w_down[i].astype(BF16)
        xa = _ffn(h2, x1, mods, wup, cw, wdn, batch, nblk, latent_only=(i == depth - 1))
        xc = None

    return xa.reshape(batch, seq, D_MODEL)
```

```python
import functools
import math

import jax
import jax.numpy as jnp
import numpy as np
from jax import lax
from jax.experimental import pallas as pl
from jax.experimental.pallas import tpu as pltpu

F32 = jnp.float32
BF16 = jnp.bfloat16

D_MODEL = 1024
HEAD_DIM = 64
GRID_W = 64
ROPE_BASE = 10000.0
EPS = 1e-6
BRANCH_WIDTH = D_MODEL // 2
A_HEADS = BRANCH_WIDTH // HEAD_DIM
A_KV_HEADS = A_HEADS // 4
C_HEADS = BRANCH_WIDTH // (2 * HEAD_DIM)
S5_CH = 16
S5_STATE = 64
S5_GROUPS = BRANCH_WIDTH // S5_CH
D_FF = 2816
IN_SIZES = (512, 128, 128, 512, 512, 512, 512, 3 * D_MODEL)
IN_OFFS = tuple(int(v) for v in np.cumsum((0,) + IN_SIZES))
N_QKVU = IN_OFFS[7]

LANES = 128
SUBLANES = 8
TM = 256
SCORE_KEYS = 2048
TC = 64
FF_CHUNK = 128
HALO = 16
ADA_COLS = 1536
V7X_VMEM_BYTES = 64 * 1024 * 1024
VMEM_LIMIT = V7X_VMEM_BYTES * 7 // 8
SCORE_SCALE = HEAD_DIM ** -0.5 * math.log2(math.e)


def _cparams(n_axes):
    return pltpu.CompilerParams(dimension_semantics=("arbitrary",) * n_axes,
                                vmem_limit_bytes=VMEM_LIMIT)


def _rms_mod(x, gain, shift, scale):
    y = x * lax.rsqrt(jnp.mean(x * x, axis=-1, keepdims=True) + EPS)
    return (y * gain) * (1.0 + scale) + shift


def _head_avg_matrix(width):
    shift = HEAD_DIM.bit_length() - 1
    r = lax.broadcasted_iota(jnp.int32, (width, width), 0) >> shift
    c = lax.broadcasted_iota(jnp.int32, (width, width), 1) >> shift
    return jnp.where(r == c, 1.0 / HEAD_DIM, 0.0).astype(BF16)


def _head_rms(z, gain, avg):
    sq = z * z
    hi = sq.astype(BF16)
    lo = (sq - hi.astype(F32)).astype(BF16)
    ms = (jnp.dot(hi, avg, preferred_element_type=F32)
          + jnp.dot(lo, avg, preferred_element_type=F32))
    return z * lax.rsqrt(ms + EPS) * gain


def _tile_lanes(t, width):
    reps = width // t.shape[1]
    return t if reps == 1 else jnp.concatenate([t] * reps, axis=1)


def _rope(z, cos, sin_signed):
    width = z.shape[1]
    lane = lax.broadcasted_iota(jnp.int32, z.shape, 1)
    first_half = (lane & (HEAD_DIM - 1)) < HEAD_DIM // 2
    rot = jnp.where(first_half,
                    pltpu.roll(z, width - HEAD_DIM // 2, 1),
                    pltpu.roll(z, HEAD_DIM // 2, 1))
    return z * _tile_lanes(cos, width) + rot * _tile_lanes(sin_signed, width)


def _dup_halves(z):
    lane = lax.broadcasted_iota(jnp.int32, z.shape, 1)
    low = lane < HEAD_DIM
    sw = pltpu.roll(z, HEAD_DIM, 1)
    return jnp.where(low, z, sw), jnp.where(low, sw, z)


def _ada_kernel(c_ref, w_ref, b_ref, o_ref):
    a = jax.nn.silu(c_ref[...])
    o_ref[0] = jnp.dot(a, w_ref[0], preferred_element_type=F32,
                       precision=lax.Precision.HIGHEST) + b_ref[0]


def _ada_mods(cc, w_ada, b_ada):
    depth, _, n = w_ada.shape
    tn = ADA_COLS
    assert n % tn == 0
    return pl.pallas_call(
        _ada_kernel,
        grid=(depth, n // tn),
        in_specs=[pl.BlockSpec((SUBLANES, D_MODEL), lambda l, j: (0, 0)),
                  pl.BlockSpec((1, D_MODEL, tn), lambda l, j: (l, 0, j)),
                  pl.BlockSpec((1, 1, tn), lambda l, j: (l, 0, j))],
        out_specs=pl.BlockSpec((1, SUBLANES, tn), lambda l, j: (l, 0, j)),
        out_shape=jax.ShapeDtypeStruct((depth, SUBLANES, n), F32),
        compiler_params=_cparams(2),
        name="ada_mods",
    )(cc, w_ada, b_ada.reshape(depth, 1, n))


def _stream_specs(x_lat, x_ctx, nblk):
    blk = (TM, D_MODEL)
    if x_ctx is None:
        return ([pl.BlockSpec(blk, lambda i: (i, 0)),
                 pl.BlockSpec(blk, lambda i: ((i // nblk) * nblk, 0))], [x_lat, x_lat])
    lat_map = lambda i: ((i // nblk) * (nblk - 1) + jnp.maximum(i % nblk - 1, 0), 0)
    return ([pl.BlockSpec(blk, lat_map), pl.BlockSpec(blk, lambda i: (i // nblk, 0))],
            [x_lat, x_ctx])


def _stream_tile(x_ref, xc_ref, nblk):
    return jnp.where(pl.program_id(0) % nblk == 0, xc_ref[...], x_ref[...])


def _inproj_kernel(x_ref, xc_ref, mod_ref, g_ref, w_ref, qk_ref, cos_ref, sin_ref,
                   qa_ref, kta_ref, va_ref, u_ref, qc_ref, ktc_ref, vc_ref, *, nblk):
    mods = mod_ref[0]
    x = _stream_tile(x_ref, xc_ref, nblk)
    h = _rms_mod(x, g_ref[...], mods[0:1], mods[1:2]).astype(BF16)
    cos = cos_ref[...]
    sin = sin_ref[...]
    avg = _head_avg_matrix(BRANCH_WIDTH)

    def seg(k):
        return jnp.dot(h, w_ref[:, IN_OFFS[k]:IN_OFFS[k + 1]], preferred_element_type=F32)

    def gain(k, width=BRANCH_WIDTH):
        return qk_ref[k:k + 1, :width]

    k = _rope(_head_rms(seg(5), gain(3), avg), cos, sin)
    for hh in range(C_HEADS):
        ktc_ref[0, hh] = k[:, hh * LANES:(hh + 1) * LANES].T.astype(BF16)
    k = _rope(_head_rms(seg(1), gain(1, LANES), avg[:LANES, :LANES]), cos, sin)
    for hh, kd in enumerate(_dup_halves(k)):
        kta_ref[0, hh] = kd.T.astype(BF16)
    q = _rope(_head_rms(seg(4), gain(2), avg), cos, sin)
    qc_ref[...] = (q * SCORE_SCALE).astype(BF16)
    q = _rope(_head_rms(seg(0), gain(0), avg), cos, sin)
    qa_ref[...] = (q * SCORE_SCALE).astype(BF16)
    for hh, vd in enumerate(_dup_halves(seg(2))):
        va_ref[0, hh] = vd.astype(BF16)
    v = seg(6)
    for hh in range(C_HEADS):
        vc_ref[0, hh] = v[:, hh * LANES:(hh + 1) * LANES].astype(BF16)
    u_ref[...] = seg(3)


def _inproj(x_lat, x_ctx, mods, norm_g, w_qkvu, qk_gain, cos_t, sin_t, batch, nblk):
    s_len = nblk * TM
    t_rows = batch * s_len
    row = lambda i: (i, 0)
    mod_map = lambda i: (jnp.where(i % nblk == 0, batch, i // nblk), 0, 0)
    pos = lambda i: (i % nblk, 0)
    kt_map = lambda i: (i // nblk, 0, 0, i % nblk)
    v_map = lambda i: (i // nblk, 0, i % nblk, 0)
    x_specs, x_args = _stream_specs(x_lat, x_ctx, nblk)
    return pl.pallas_call(
        functools.partial(_inproj_kernel, nblk=nblk),
        grid=(t_rows // TM,),
        in_specs=x_specs + [
                  pl.BlockSpec((1, 6, D_MODEL), mod_map),
                  pl.BlockSpec((1, D_MODEL), lambda i: (0, 0)),
                  pl.BlockSpec((D_MODEL, N_QKVU), lambda i: (0, 0)),
                  pl.BlockSpec((4, BRANCH_WIDTH), lambda i: (0, 0)),
                  pl.BlockSpec((TM, LANES), pos),
                  pl.BlockSpec((TM, LANES), pos)],
        out_specs=[pl.BlockSpec((TM, BRANCH_WIDTH), row),
                   pl.BlockSpec((1, A_KV_HEADS, LANES, TM), kt_map),
                   pl.BlockSpec((1, A_KV_HEADS, TM, LANES), v_map),
                   pl.BlockSpec((TM, BRANCH_WIDTH), row),
                   pl.BlockSpec((TM, BRANCH_WIDTH), row),
                   pl.BlockSpec((1, C_HEADS, LANES, TM), kt_map),
                   pl.BlockSpec((1, C_HEADS, TM, LANES), v_map)],
        out_shape=[jax.ShapeDtypeStruct((t_rows, BRANCH_WIDTH), BF16),
                   jax.ShapeDtypeStruct((batch, A_KV_HEADS, LANES, s_len), BF16),
                   jax.ShapeDtypeStruct((batch, A_KV_HEADS, s_len, LANES), BF16),
                   jax.ShapeDtypeStruct((t_rows, BRANCH_WIDTH), F32),
                   jax.ShapeDtypeStruct((t_rows, BRANCH_WIDTH), BF16),
                   jax.ShapeDtypeStruct((batch, C_HEADS, LANES, s_len), BF16),
                   jax.ShapeDtypeStruct((batch, C_HEADS, s_len, LANES), BF16)],
        compiler_params=_cparams(1),
        name="in_proj",
    )(*x_args, mods, norm_g, w_qkvu, qk_gain, cos_t, sin_t)


def _attn_kernel(*refs, diff, lam_init, n_ctx, bounds):
    if diff:
        q_ref, kt_ref, v_ref, lam_ref, gout_ref, o_ref = refs[:6]
    else:
        q_ref, kt_ref, v_ref, o_ref = refs[:4]
    lhs_sc, m_sc, l_sc, acc_sc, s0_sc, s1_sc = refs[-6:]
    bufs = (s0_sc, s1_sc)
    n_chunks = len(bounds) - 1
    j = pl.program_id(2)
    rows = 2 * TM
    lane = lax.broadcasted_iota(jnp.int32, (TM, LANES), 1)
    low = lane < HEAD_DIM
    qt = q_ref[...].astype(F32)
    lhs_sc[0:TM, :] = jnp.where(low, qt, 0.0).astype(BF16)
    lhs_sc[TM:rows, :] = jnp.where(low, 0.0, qt).astype(BF16)

    def chunk_keys(c):
        return slice(bounds[c], bounds[c + 1])

    def scores(keys):
        return jnp.dot(lhs_sc[...], kt_ref[0, 0, :, keys], preferred_element_type=F32)

    def softmax_pv(s, keys, first):
        width = s.shape[1]
        mx = s[:, 0:LANES]
        for t in range(1, width // LANES):
            mx = jnp.maximum(mx, s[:, t * LANES:(t + 1) * LANES])
        m_cur = jnp.max(mx, axis=1, keepdims=True)
        if first:
            m_next = jnp.broadcast_to(m_cur, (rows, LANES))
        else:
            m_prev = m_sc[...]
            m_next = jnp.maximum(m_prev, m_cur)
            alpha = jnp.exp2(m_prev - m_next)
        p = jnp.exp2(s - _tile_lanes(m_next, width))
        psum = p[:, 0:LANES]
        for t in range(1, width // LANES):
            psum = psum + p[:, t * LANES:(t + 1) * LANES]
        pv = jnp.dot(p.astype(BF16), v_ref[0, 0, keys, :], preferred_element_type=F32)
        if first:
            l_sc[...] = psum
            acc_sc[...] = pv
        else:
            l_sc[...] = alpha * l_sc[...] + psum
            acc_sc[...] = alpha * acc_sc[...] + pv
        m_sc[...] = m_next

    @pl.when(j == 0)
    def _():
        ctx_keys = slice(0, n_ctx)
        softmax_pv(scores(ctx_keys), ctx_keys, True)

    @pl.when(j > 0)
    def _():
        def put(c):
            keys = chunk_keys(c)
            bufs[c % 2][:, 0:keys.stop - keys.start] = scores(keys)

        def take(c):
            keys = chunk_keys(c)
            softmax_pv(bufs[c % 2][:, 0:keys.stop - keys.start], keys, c == 0)

        put(0)
        for c in range(n_chunks):
            if c + 1 < n_chunks:
                put(c + 1)
            take(c)

    o = acc_sc[...] / jnp.sum(l_sc[...], axis=1, keepdims=True)
    if diff:
        lv = lam_ref[...]
        lam = (jnp.exp(jnp.sum(lv[0:1] * lv[1:2], axis=1, keepdims=True))
               - jnp.exp(jnp.sum(lv[2:3] * lv[3:4], axis=1, keepdims=True)) + lam_init)
        od = o[0:TM] - lam * o[TM:rows]
        od = od * lax.rsqrt(jnp.mean(od * od, axis=-1, keepdims=True) + EPS)
        o_ref[...] = ((od * gout_ref[...]) * (1.0 - lam_init)).astype(o_ref.dtype)
    else:
        o_ref[...] = jnp.where(low, o[0:TM], o[TM:rows]).astype(o_ref.dtype)


def _attention(q, kt, v, batch, nblk, *, diff, lam_vecs=None, g_out=None, lam_init=0.0):
    kv_heads = kt.shape[1]
    s_len = kt.shape[3]
    tiles = BRANCH_WIDTH // LANES
    rows = 2 * TM
    q_map = lambda b, h, j: (b * nblk + j, h)
    kv_map = lambda b, h, j: (b, h * kv_heads // tiles, 0, 0)
    const = lambda b, h, j: (0, 0)
    in_specs = [pl.BlockSpec((TM, LANES), q_map),
                pl.BlockSpec((1, 1, LANES, s_len), kv_map),
                pl.BlockSpec((1, 1, s_len, LANES), kv_map)]
    args = [q, kt, v]
    if diff:
        in_specs += [pl.BlockSpec((4, HEAD_DIM), const), pl.BlockSpec((1, LANES), const)]
        args += [lam_vecs, g_out]
    n_ctx = TM
    tkl = min(SCORE_KEYS, s_len - n_ctx)
    assert (s_len - n_ctx) % tkl == 0
    bounds = [0] + list(range(n_ctx + tkl, s_len + 1, tkl))
    kern = functools.partial(_attn_kernel, diff=diff, lam_init=lam_init,
                             n_ctx=n_ctx, bounds=tuple(bounds))
    return pl.pallas_call(
        kern,
        grid=(batch, tiles, nblk),
        in_specs=in_specs,
        out_specs=pl.BlockSpec((TM, LANES), q_map),
        out_shape=jax.ShapeDtypeStruct((batch * s_len, BRANCH_WIDTH), BF16),
        scratch_shapes=([pltpu.VMEM((rows, LANES), BF16)]
                        + [pltpu.VMEM((rows, LANES), F32)] * 3
                        + [pltpu.VMEM((rows, n_ctx + tkl), F32)] * 2),
        compiler_params=_cparams(3),
        name="diff_attn" if diff else "gqa_attn",
    )(*args)


def _s5_kernel(uf_ref, ub_ref, pin_ref, pout_ref, wd_ref, a_ref, cm_ref, yf_ref, yb_ref,
               bu_sc, st_sc):
    n_tiles = BRANCH_WIDTH // LANES
    half = SUBLANES * S5_STATE
    rows = TC * SUBLANES
    tok = (SUBLANES // 2) * TC

    @pl.when(pl.program_id(0) == 0)
    def _():
        st_sc[...] = jnp.zeros_like(st_sc)

    seq = lax.broadcasted_iota(jnp.int32, (rows, LANES), 0) & (SUBLANES - 1)
    fwd = seq < SUBLANES // 2
    lhs_dir = []
    for d, ref in enumerate((uf_ref, ub_ref)):
        ud = ref[...].reshape(tok, BRANCH_WIDTH).astype(BF16)
        lhs_dir.append(jnp.dot(pin_ref[d], ud, preferred_element_type=F32).astype(BF16))
    for j in range(n_tiles):
        cs = slice(j * LANES, (j + 1) * LANES)
        lhs = jnp.concatenate([lhs_dir[0][:, cs], lhs_dir[1][:, cs]], axis=1)
        bu_sc[:, 2 * half * j:2 * half * (j + 1)] = jnp.dot(
            lhs, wd_ref[j], preferred_element_type=F32)

    for j in range(n_tiles):
        re = slice(2 * half * j, 2 * half * j + half)
        im = slice(2 * half * j + half, 2 * half * (j + 1))
        ar = a_ref[0, :, half * j:half * (j + 1)]
        ai = a_ref[1, :, half * j:half * (j + 1)]

        def step(t, carry, re=re, im=im, ar=ar, ai=ai):
            xr, xi = carry
            r = pl.multiple_of(t * SUBLANES, SUBLANES)
            nxr = ar * xr - ai * xi + bu_sc[pl.ds(r, SUBLANES), re]
            nxi = ar * xi + ai * xr + bu_sc[pl.ds(r, SUBLANES), im]
            bu_sc[pl.ds(r, SUBLANES), re] = nxr
            bu_sc[pl.ds(r, SUBLANES), im] = nxi
            return nxr, nxi

        xr, xi = lax.fori_loop(0, TC, step, (st_sc[:, re], st_sc[:, im]), unroll=True)
        st_sc[:, re] = xr
        st_sc[:, im] = xi

    ys = []
    for j in range(n_tiles):
        x = bu_sc[:, 2 * half * j:2 * half * (j + 1)].astype(BF16)
        yy = jnp.dot(x, cm_ref[j], preferred_element_type=F32)
        ys.append(jnp.where(fwd, yy[:, :LANES], yy[:, LANES:]))
    y = jnp.concatenate(ys, axis=1)
    parts = []
    rest = y
    for _ in range(2):
        part = rest.astype(BF16)
        parts.append(part)
        rest = rest - part.astype(F32)
    for d, ref in enumerate((yf_ref, yb_ref)):
        out = None
        for part in parts:
            term = jnp.dot(pout_ref[d], part, preferred_element_type=F32)
            out = term if out is None else out + term
        ref[...] = out.reshape(ref.shape)


def _scan_row_placement(n_seq):
    p = np.zeros((2, TC * 2 * n_seq, n_seq * TC), np.float32)
    for b in range(n_seq):
        for k in range(TC):
            p[0, 2 * n_seq * k + b, b * TC + k] = 1.0
            p[1, 2 * n_seq * (TC - 1 - k) + n_seq + b, b * TC + k] = 1.0
    return jnp.asarray(p, BF16), jnp.asarray(p.transpose(0, 2, 1), BF16)


def _s5_scan(u3, wd, a8, cm, n_ctx):
    batch, s_len, _ = u3.shape
    rows = TC * SUBLANES
    n_tiles = BRANCH_WIDTH // LANES
    n_state = 2 * SUBLANES * S5_STATE * n_tiles
    n_steps = s_len // TC
    ctx_steps = n_ctx // TC
    p_in, p_out = _scan_row_placement(batch)
    fwd_map = lambda g: (0, g, 0)
    bwd_map = lambda g: (0, jnp.where(g < ctx_steps, ctx_steps - 1 - g,
                                      n_steps + ctx_steps - 1 - g), 0)
    c3 = lambda g: (0, 0, 0)
    blk = (batch, TC, BRANCH_WIDTH)
    out = jax.ShapeDtypeStruct(u3.shape, F32)
    return pl.pallas_call(
        _s5_kernel,
        grid=(n_steps,),
        in_specs=[pl.BlockSpec(blk, fwd_map),
                  pl.BlockSpec(blk, bwd_map),
                  pl.BlockSpec(p_in.shape, c3),
                  pl.BlockSpec(p_out.shape, c3),
                  pl.BlockSpec(wd.shape, c3),
                  pl.BlockSpec(a8.shape, c3),
                  pl.BlockSpec(cm.shape, c3)],
        out_specs=[pl.BlockSpec(blk, fwd_map), pl.BlockSpec(blk, bwd_map)],
        out_shape=[out, out],
        scratch_shapes=[pltpu.VMEM((rows, n_state), F32),
                        pltpu.VMEM((SUBLANES, n_state), F32)],
        compiler_params=_cparams(1),
        name="s5_scan",
    )(u3, u3, p_in, p_out, wd, a8, cm)


def _zoh(lam_re, lam_im, log_dt, b_re, b_im):
    dt = jnp.exp(log_dt)[..., None]
    mag = jnp.exp(lam_re * dt)
    a_re = mag * jnp.cos(lam_im * dt)
    a_im = mag * jnp.sin(lam_im * dt)
    den = lam_re * lam_re + lam_im * lam_im
    f_re = ((a_re - 1.0) * lam_re + a_im * lam_im) / den
    f_im = (a_im * lam_re - (a_re - 1.0) * lam_im) / den
    bb_re = f_re[..., None] * b_re - f_im[..., None] * b_im
    bb_im = f_re[..., None] * b_im + f_im[..., None] * b_re
    return a_re, a_im, bb_re, bb_im


def _s5_weights(lam_re, lam_im, log_dt, b_re, b_im, c_re, c_im, batch):
    n_tiles = BRANCH_WIDTH // LANES
    gpt = S5_GROUPS // n_tiles
    a_re, a_im, bb_re, bb_im = _zoh(lam_re, lam_im, log_dt, b_re, b_im)
    eye = jnp.eye(gpt, dtype=F32)

    def drive(bb):
        t = bb.reshape(2, n_tiles, gpt, S5_STATE, S5_CH)
        w = jnp.einsum('djgpc,gh->jdgchp', t, eye)
        return w.reshape(n_tiles, 2 * gpt * S5_CH, gpt * S5_STATE)

    def read(cc):
        t = cc.reshape(2, n_tiles, gpt, S5_CH, S5_STATE)
        w = jnp.einsum('djgcp,gh->jgpdhc', t, eye)
        return w.reshape(n_tiles, gpt * S5_STATE, 2 * gpt * S5_CH)

    wd = jnp.concatenate([drive(bb_re), drive(bb_im)], axis=2).astype(BF16)
    cm = jnp.concatenate([read(c_re), read(-c_im)], axis=1).astype(BF16)

    def per_seq(a):
        return jnp.repeat(a.reshape(2, 1, S5_GROUPS * S5_STATE), batch, axis=1).reshape(
            2 * batch, S5_GROUPS * S5_STATE)

    a8 = jnp.stack([per_seq(a_re), per_seq(a_im)])
    return wd, a8, cm


def _merge_kernel(x_ref, xc_ref, mod_ref, g1_ref, g2_ref, ya_ref, yd_ref, yf_ref, yr_ref, u_ref,
                  dsk_ref, wglu_ref, bglu_ref, wgate_ref, wbr_ref, wout_ref, x1_ref, h2_ref,
                  *, nblk):
    mods = mod_ref[0]
    x = _stream_tile(x_ref, xc_ref, nblk)
    h = _rms_mod(x, g1_ref[...], mods[0:1], mods[1:2]).astype(BF16)
    ys = yf_ref[...] + yr_ref[...] + dsk_ref[...] * u_ref[...]
    g = jax.nn.gelu(ys)
    yb = g * jax.nn.sigmoid(
        jnp.dot(g.astype(BF16), wglu_ref[...], preferred_element_type=F32) + bglu_ref[...])
    branches = (ya_ref[...], yb.astype(BF16), yd_ref[...])
    m = None
    for k, y in enumerate(branches):
        gate = jax.nn.sigmoid(jnp.dot(h, wgate_ref[:, k * D_MODEL:(k + 1) * D_MODEL],
                                      preferred_element_type=F32))
        term = gate * jnp.dot(y, wbr_ref[k], preferred_element_type=F32)
        m = term if m is None else m + term
    y = jnp.dot(m.astype(BF16), wout_ref[...], preferred_element_type=F32)
    x1 = x + mods[2:3] * y
    x1_ref[...] = x1
    h2_ref[...] = _rms_mod(x1, g2_ref[...], mods[3:4], mods[4:5]).astype(BF16)


def _merge(x_lat, x_ctx, mods, g1, g2, ya, yd, yf, yr, u, d_skip, w_glu, b_glu, w_gate,
           w_branch, w_out, batch, nblk):
    t_rows = ya.shape[0]
    row = lambda i: (i, 0)
    mod_map = lambda i: (jnp.where(i % nblk == 0, batch, i // nblk), 0, 0)
    c2 = lambda i: (0, 0)
    c3 = lambda i: (0, 0, 0)
    x_specs, x_args = _stream_specs(x_lat, x_ctx, nblk)
    return pl.pallas_call(
        functools.partial(_merge_kernel, nblk=nblk),
        grid=(t_rows // TM,),
        in_specs=x_specs + [
                  pl.BlockSpec((1, 6, D_MODEL), mod_map),
                  pl.BlockSpec((1, D_MODEL), c2),
                  pl.BlockSpec((1, D_MODEL), c2),
                  pl.BlockSpec((TM, BRANCH_WIDTH), row),
                  pl.BlockSpec((TM, BRANCH_WIDTH), row),
                  pl.BlockSpec((TM, BRANCH_WIDTH), row),
                  pl.BlockSpec((TM, BRANCH_WIDTH), row),
                  pl.BlockSpec((TM, BRANCH_WIDTH), row),
                  pl.BlockSpec((1, BRANCH_WIDTH), c2),
                  pl.BlockSpec((BRANCH_WIDTH, BRANCH_WIDTH), c2),
                  pl.BlockSpec((1, BRANCH_WIDTH), c2),
                  pl.BlockSpec((D_MODEL, 3 * D_MODEL), c2),
                  pl.BlockSpec((3, BRANCH_WIDTH, D_MODEL), c3),
                  pl.BlockSpec((D_MODEL, D_MODEL), c2)],
        out_specs=[pl.BlockSpec((TM, D_MODEL), row), pl.BlockSpec((TM, D_MODEL), row)],
        out_shape=[jax.ShapeDtypeStruct((t_rows, D_MODEL), F32),
                   jax.ShapeDtypeStruct((t_rows, D_MODEL), BF16)],
        compiler_params=_cparams(1),
        name="merge",
    )(*x_args, mods, g1, g2, ya, yd, yf, yr, u, d_skip, w_glu, b_glu, w_gate, w_branch, w_out)


def _ffn_kernel(h_ref, hp_ref, hn_ref, x1_ref, mod_ref, perm_ref, wup_ref, cw_ref, wdn_ref, o_ref,
                lhs_sc, u0_sc, u1_sc, act_sc, *, nblk, n_chunks):
    pj = pl.program_id(0) % nblk
    left_ok = (pj >= 2).astype(F32)
    right_ok = jnp.logical_and(pj != 0, pj != nblk - 1).astype(F32)
    lhs_sc[0:TM, :] = jnp.dot(perm_ref[0], h_ref[...], preferred_element_type=F32).astype(BF16)
    hrow = lax.broadcasted_iota(jnp.int32, (HALO, D_MODEL), 0)
    halo = (jnp.where(hrow == HALO - 1, hp_ref[...].astype(F32) * left_ok, 0.0)
            + jnp.where(hrow == 0, hn_ref[...].astype(F32) * right_ok, 0.0))
    lhs_sc[TM:TM + HALO, :] = halo.astype(BF16)
    sub = lax.broadcasted_iota(jnp.int32, (SUBLANES, 2 * FF_CHUNK), 0)

    def up(k, buf):
        buf[...] = jnp.dot(lhs_sc[...], wup_ref[k], preferred_element_type=F32)

    def activate(k, buf):
        cw = cw_ref[k]
        before = buf[TM + HALO - 1:TM + HALO, :]
        after = buf[TM:TM + 1, :]
        first = jnp.where(sub == 0, before, pltpu.roll(buf[TM - SUBLANES:TM, :], 1, 0))
        last = jnp.where(sub == SUBLANES - 1, after,
                         pltpu.roll(buf[0:SUBLANES, :], SUBLANES - 1, 0))
        prev = jnp.concatenate([first, buf[0:TM - SUBLANES, :]], axis=0)
        nxt = jnp.concatenate([buf[SUBLANES:TM, :], last], axis=0)
        c = prev * cw[0:1] + buf[0:TM, :] * cw[1:2] + nxt * cw[2:3] + cw[3:4]
        act = jax.nn.silu(c[:, FF_CHUNK:]) * c[:, :FF_CHUNK]
        act_sc[:, k * FF_CHUNK:(k + 1) * FF_CHUNK] = act.astype(BF16)

    bufs = (u0_sc, u1_sc)
    up(0, bufs[0])
    for k in range(n_chunks):
        if k + 1 < n_chunks:
            up(k + 1, bufs[(k + 1) % 2])
        activate(k, bufs[k % 2])
    act = jnp.dot(perm_ref[1], act_sc[...], preferred_element_type=F32).astype(BF16)
    y = jnp.dot(act, wdn_ref[...], preferred_element_type=F32)
    o_ref[...] = x1_ref[...] + mod_ref[0][5:6] * y


def _strided_row_order():
    p = np.zeros((TM, TM), np.float32)
    groups = TM // SUBLANES
    for r in range(groups):
        for s in range(SUBLANES):
            p[SUBLANES * r + s, groups * s + r] = 1.0
    return jnp.asarray(np.stack([p, p.T]), BF16)


def _ffn(h2, x1, mods, wup, cw, wdn, batch, nblk, latent_only):
    t_rows = x1.shape[0]
    n_chunks = wup.shape[0]
    per = TM // HALO
    last = t_rows // HALO - 1
    row = lambda i: (i, 0)
    mod_map = lambda i: (jnp.where(i % nblk == 0, batch, i // nblk), 0, 0)
    kern = functools.partial(_ffn_kernel, nblk=nblk, n_chunks=n_chunks)
    if latent_only:
        out_map = lambda i: ((i // nblk) * (nblk - 1) + jnp.maximum(i % nblk - 1, 0), 0)
        out_rows = t_rows - batch * TM
    else:
        out_map, out_rows = row, t_rows
    return pl.pallas_call(
        kern,
        grid=(t_rows // TM,),
        in_specs=[pl.BlockSpec((TM, D_MODEL), row),
                  pl.BlockSpec((HALO, D_MODEL), lambda i: (jnp.maximum(i * per - 1, 0), 0)),
                  pl.BlockSpec((HALO, D_MODEL), lambda i: (jnp.minimum((i + 1) * per, last), 0)),
                  pl.BlockSpec((TM, D_MODEL), row),
                  pl.BlockSpec((1, 6, D_MODEL), mod_map),
                  pl.BlockSpec((2, TM, TM), lambda i: (0, 0, 0)),
                  pl.BlockSpec(wup.shape, lambda i: (0, 0, 0)),
                  pl.BlockSpec(cw.shape, lambda i: (0, 0, 0)),
                  pl.BlockSpec(wdn.shape, lambda i: (0, 0))],
        out_specs=pl.BlockSpec((TM, D_MODEL), out_map),
        out_shape=jax.ShapeDtypeStruct((out_rows, D_MODEL), F32),
        scratch_shapes=([pltpu.VMEM((TM + HALO, D_MODEL), BF16)]
                        + [pltpu.VMEM((TM + HALO, 2 * FF_CHUNK), F32)] * 2
                        + [pltpu.VMEM((TM, D_FF), BF16)]),
        compiler_params=_cparams(1),
        name="conv_ffn",
    )(h2, h2, h2, x1, mods, _strided_row_order(), wup, cw, wdn)


def _rope_tables(seq, ctx):
    rows = seq // GRID_W
    row = jnp.repeat(jnp.arange(rows, dtype=F32), GRID_W)
    col = jnp.tile(jnp.arange(GRID_W, dtype=F32), rows)
    n_freq = HEAD_DIM // 4
    inv_freq = ROPE_BASE ** (-jnp.arange(n_freq, dtype=F32) / n_freq)
    ang = jnp.concatenate([row[:, None] * inv_freq, col[:, None] * inv_freq], axis=-1)
    ang = jnp.concatenate([ang, ang], axis=-1)
    cos = jnp.concatenate([jnp.ones((ctx, HEAD_DIM), F32), jnp.cos(ang)], axis=0)
    sin = jnp.concatenate([jnp.zeros((ctx, HEAD_DIM), F32), jnp.sin(ang)], axis=0)
    sign = jnp.where(jnp.arange(HEAD_DIM) < HEAD_DIM // 2, -1.0, 1.0).astype(F32)
    two = lambda t: jnp.concatenate([t, t], axis=1)
    return two(cos), two(sin * sign)


def kernel(x, c, ctx, c_ctx, w_ada, b_ada, norm_g, w_in, qk_gain, ssm_lam_re, ssm_lam_im, ssm_log_dt, ssm_b_re, ssm_b_im, ssm_c_re, ssm_c_im, ssm_d, w_glu, b_glu, diff_lam, diff_norm_g, w_branch, w_out, w_up, conv_w, conv_b, w_down):
    batch, seq, _ = x.shape
    n_ctx = ctx.shape[1]
    depth = w_in.shape[0]
    assert n_ctx == TM and seq % TM == 0 and 2 * batch == SUBLANES
    s_len = n_ctx + seq
    nblk = s_len // TM
    assert s_len % TC == 0 and D_FF % FF_CHUNK == 0
    n_chunks = D_FF // FF_CHUNK

    cos_t, sin_t = _rope_tables(seq, n_ctx)
    cc = jnp.concatenate([c, c_ctx[None, :], jnp.zeros((SUBLANES - batch - 1, D_MODEL), F32)], axis=0)
    mods_all = _ada_mods(cc, w_ada, b_ada).reshape(depth, SUBLANES, 6, D_MODEL)

    xa = x.reshape(batch * seq, D_MODEL)
    xc = ctx.reshape(batch * n_ctx, D_MODEL)
    for i in range(depth):
        lam_init = 0.8 - 0.6 * math.exp(-0.3 * i)
        mods = mods_all[i]
        g1 = norm_g[i, 0:1]
        g2 = norm_g[i, 1:2]
        w_i = w_in[i].astype(BF16)
        qa, kta, va, u, qc, ktc, vc = _inproj(xa, xc, mods, g1, w_i[:, :N_QKVU],
                                              jnp.tile(qk_gain[i], (1, BRANCH_WIDTH // HEAD_DIM)),
                                              cos_t, sin_t, batch, nblk)
        ya = _attention(qa, kta, va, batch, nblk, diff=False)
        yd = _attention(qc, ktc, vc, batch, nblk, diff=True, lam_vecs=diff_lam[i],
                        g_out=diff_norm_g[i][None, :], lam_init=lam_init)

        wd, a8, cm = _s5_weights(ssm_lam_re[i], ssm_lam_im[i], ssm_log_dt[i], ssm_b_re[i],
                                 ssm_b_im[i], ssm_c_re[i], ssm_c_im[i], batch)
        yf, yr = _s5_scan(u.reshape(batch, s_len, BRANCH_WIDTH), wd, a8, cm, n_ctx)
        yf = yf.reshape(batch * s_len, BRANCH_WIDTH)
        yr = yr.reshape(batch * s_len, BRANCH_WIDTH)

        x1, h2 = _merge(xa, xc, mods, g1, g2, ya, yd, yf, yr, u, ssm_d[i][None, :],
                        w_glu[i].astype(BF16), b_glu[i][None, :], w_i[:, N_QKVU:],
                        w_branch[i].astype(BF16), w_out[i].astype(BF16), batch, nblk)

        def chunked(t):
            a = t[:, :D_FF].reshape(t.shape[0], n_chunks, FF_CHUNK)
            g = t[:, D_FF:].reshape(t.shape[0], n_chunks, FF_CHUNK)
            return jnp.concatenate([a, g], axis=-1).transpose(1, 0, 2)

        wup = chunked(w_up[i].astype(BF16))
        cw = chunked(jnp.concatenate(
            [conv_w[i], conv_b[i][None, :], jnp.zeros((SUBLANES - 4, 2 * D_FF), F32)], axis=0))
        wdn = w_down[i].astype(BF16)
        xa = _ffn(h2, x1, mods, wup, cw, wdn, batch, nblk, latent_only=(i == depth - 1))
        xc = None

    return xa.reshape(batch, seq, D_MODEL)
```

```python
import functools
import math

import jax
import jax.numpy as jnp
import numpy as np
from jax import lax
from jax.experimental import pallas as pl
from jax.experimental.pallas import tpu as pltpu

F32 = jnp.float32
BF16 = jnp.bfloat16

D_MODEL = 1024
HEAD_DIM = 64
GRID_W = 64
ROPE_BASE = 10000.0
EPS = 1e-6
BRANCH_WIDTH = D_MODEL // 2
A_HEADS = BRANCH_WIDTH // HEAD_DIM
A_KV_HEADS = A_HEADS // 4
C_HEADS = BRANCH_WIDTH // (2 * HEAD_DIM)
S5_CH = 16
S5_STATE = 64
S5_GROUPS = BRANCH_WIDTH // S5_CH
D_FF = 2816
IN_SIZES = (512, 128, 128, 512, 512, 512, 512, 3 * D_MODEL)
IN_OFFS = tuple(int(v) for v in np.cumsum((0,) + IN_SIZES))
N_QKVU = IN_OFFS[7]

LANES = 128
SUBLANES = 8
TM = 256
SCORE_KEYS = 2048
TC = 64
FF_CHUNK = 128
HALO = 16
ADA_COLS = 1536
V7X_VMEM_BYTES = 64 * 1024 * 1024
VMEM_LIMIT = V7X_VMEM_BYTES * 7 // 8
SCORE_SCALE = HEAD_DIM ** -0.5 * math.log2(math.e)


def _cparams(n_axes):
    return pltpu.CompilerParams(dimension_semantics=("arbitrary",) * n_axes,
                                vmem_limit_bytes=VMEM_LIMIT)


def _rms_mod(x, gain, shift, scale):
    y = x * lax.rsqrt(jnp.mean(x * x, axis=-1, keepdims=True) + EPS)
    return (y * gain) * (1.0 + scale) + shift


def _head_avg_matrix(width):
    shift = HEAD_DIM.bit_length() - 1
    r = lax.broadcasted_iota(jnp.int32, (width, width), 0) >> shift
    c = lax.broadcasted_iota(jnp.int32, (width, width), 1) >> shift
    return jnp.where(r == c, 1.0 / HEAD_DIM, 0.0).astype(BF16)


def _head_rms(z, gain, avg):
    sq = z * z
    hi = sq.astype(BF16)
    lo = (sq - hi.astype(F32)).astype(BF16)
    ms = (jnp.dot(hi, avg, preferred_element_type=F32)
          + jnp.dot(lo, avg, preferred_element_type=F32))
    return z * lax.rsqrt(ms + EPS) * gain


def _tile_lanes(t, width):
    reps = width // t.shape[1]
    return t if reps == 1 else jnp.concatenate([t] * reps, axis=1)


def _rope(z, cos, sin_signed):
    width = z.shape[1]
    lane = lax.broadcasted_iota(jnp.int32, z.shape, 1)
    first_half = (lane & (HEAD_DIM - 1)) < HEAD_DIM // 2
    rot = jnp.where(first_half,
                    pltpu.roll(z, width - HEAD_DIM // 2, 1),
                    pltpu.roll(z, HEAD_DIM // 2, 1))
    return z * _tile_lanes(cos, width) + rot * _tile_lanes(sin_signed, width)


def _dup_halves(z):
    lane = lax.broadcasted_iota(jnp.int32, z.shape, 1)
    low = lane < HEAD_DIM
    sw = pltpu.roll(z, HEAD_DIM, 1)
    return jnp.where(low, z, sw), jnp.where(low, sw, z)


def _ada_kernel(c_ref, w_ref, b_ref, o_ref):
    a = jax.nn.silu(c_ref[...])
    o_ref[0] = jnp.dot(a, w_ref[0], preferred_element_type=F32,
                       precision=lax.Precision.HIGHEST) + b_ref[0]


def _ada_mods(cc, w_ada, b_ada):
    depth, _, n = w_ada.shape
    tn = ADA_COLS
    assert n % tn == 0
    return pl.pallas_call(
        _ada_kernel,
        grid=(depth, n // tn),
        in_specs=[pl.BlockSpec((SUBLANES, D_MODEL), lambda l, j: (0, 0)),
                  pl.BlockSpec((1, D_MODEL, tn), lambda l, j: (l, 0, j)),
                  pl.BlockSpec((1, 1, tn), lambda l, j: (l, 0, j))],
        out_specs=pl.BlockSpec((1, SUBLANES, tn), lambda l, j: (l, 0, j)),
        out_shape=jax.ShapeDtypeStruct((depth, SUBLANES, n), F32),
        compiler_params=_cparams(2),
        name="ada_mods",
    )(cc, w_ada, b_ada.reshape(depth, 1, n))


def _stream_specs(x_lat, x_ctx, nblk):
    blk = (TM, D_MODEL)
    if x_ctx is None:
        return ([pl.BlockSpec(blk, lambda i: (i, 0)),
                 pl.BlockSpec(blk, lambda i: ((i // nblk) * nblk, 0))], [x_lat, x_lat])
    lat_map = lambda i: ((i // nblk) * (nblk - 1) + jnp.maximum(i % nblk - 1, 0), 0)
    return ([pl.BlockSpec(blk, lat_map), pl.BlockSpec(blk, lambda i: (i // nblk, 0))],
            [x_lat, x_ctx])


def _stream_tile(x_ref, xc_ref, nblk):
    return jnp.where(pl.program_id(0) % nblk == 0, xc_ref[...], x_ref[...])


def _inproj_kernel(x_ref, xc_ref, mod_ref, g_ref, w_ref, qk_ref, cos_ref, sin_ref,
                   qa_ref, kta_ref, va_ref, u_ref, qc_ref, ktc_ref, vc_ref, *, nblk):
    mods = mod_ref[0]
    x = _stream_tile(x_ref, xc_ref, nblk)
    h = _rms_mod(x, g_ref[...], mods[0:1], mods[1:2]).astype(BF16)
    cos = cos_ref[...]
    sin = sin_ref[...]
    avg = _head_avg_matrix(BRANCH_WIDTH)

    def seg(k):
        return jnp.dot(h, w_ref[:, IN_OFFS[k]:IN_OFFS[k + 1]], preferred_element_type=F32)

    def gain(k, width=BRANCH_WIDTH):
        return qk_ref[k:k + 1, :width]

    k = _rope(_head_rms(seg(5), gain(3), avg), cos, sin)
    for hh in range(C_HEADS):
        ktc_ref[0, hh] = k[:, hh * LANES:(hh + 1) * LANES].T.astype(BF16)
    k = _rope(_head_rms(seg(1), gain(1, LANES), avg[:LANES, :LANES]), cos, sin)
    for hh, kd in enumerate(_dup_halves(k)):
        kta_ref[0, hh] = kd.T.astype(BF16)
    q = _rope(_head_rms(seg(4), gain(2), avg), cos, sin)
    qc_ref[...] = (q * SCORE_SCALE).astype(BF16)
    q = _rope(_head_rms(seg(0), gain(0), avg), cos, sin)
    qa_ref[...] = (q * SCORE_SCALE).astype(BF16)
    for hh, vd in enumerate(_dup_halves(seg(2))):
        va_ref[0, hh] = vd.astype(BF16)
    v = seg(6)
    for hh in range(C_HEADS):
        vc_ref[0, hh] = v[:, hh * LANES:(hh + 1) * LANES].astype(BF16)
    u_ref[...] = seg(3)


def _inproj(x_lat, x_ctx, mods, norm_g, w_qkvu, qk_gain, cos_t, sin_t, batch, nblk):
    s_len = nblk * TM
    t_rows = batch * s_len
    row = lambda i: (i, 0)
    mod_map = lambda i: (jnp.where(i % nblk == 0, batch, i // nblk), 0, 0)
    pos = lambda i: (i % nblk, 0)
    kt_map = lambda i: (i // nblk, 0, 0, i % nblk)
    v_map = lambda i: (i // nblk, 0, i % nblk, 0)
    x_specs, x_args = _stream_specs(x_lat, x_ctx, nblk)
    return pl.pallas_call(
        functools.partial(_inproj_kernel, nblk=nblk),
        grid=(t_rows // TM,),
        in_specs=x_specs + [
                  pl.BlockSpec((1, 6, D_MODEL), mod_map),
                  pl.BlockSpec((1, D_MODEL), lambda i: (0, 0)),
                  pl.BlockSpec((D_MODEL, N_QKVU), lambda i: (0, 0)),
                  pl.BlockSpec((4, BRANCH_WIDTH), lambda i: (0, 0)),
                  pl.BlockSpec((TM, LANES), pos),
                  pl.BlockSpec((TM, LANES), pos)],
        out_specs=[pl.BlockSpec((TM, BRANCH_WIDTH), row),
                   pl.BlockSpec((1, A_KV_HEADS, LANES, TM), kt_map),
                   pl.BlockSpec((1, A_KV_HEADS, TM, LANES), v_map),
                   pl.BlockSpec((TM, BRANCH_WIDTH), row),
                   pl.BlockSpec((TM, BRANCH_WIDTH), row),
                   pl.BlockSpec((1, C_HEADS, LANES, TM), kt_map),
                   pl.BlockSpec((1, C_HEADS, TM, LANES), v_map)],
        out_shape=[jax.ShapeDtypeStruct((t_rows, BRANCH_WIDTH), BF16),
                   jax.ShapeDtypeStruct((batch, A_KV_HEADS, LANES, s_len), BF16),
                   jax.ShapeDtypeStruct((batch, A_KV_HEADS, s_len, LANES), BF16),
                   jax.ShapeDtypeStruct((t_rows, BRANCH_WIDTH), F32),
                   jax.ShapeDtypeStruct((t_rows, BRANCH_WIDTH), BF16),
                   jax.ShapeDtypeStruct((batch, C_HEADS, LANES, s_len), BF16),
                   jax.ShapeDtypeStruct((batch, C_HEADS, s_len, LANES), BF16)],
        compiler_params=_cparams(1),
        name="in_proj",
    )(*x_args, mods, norm_g, w_qkvu, qk_gain, cos_t, sin_t)


def _attn_kernel(*refs, diff, lam_init, n_ctx, bounds):
    if diff:
        q_ref, kt_ref, v_ref, lam_ref, gout_ref, o_ref = refs[:6]
    else:
        q_ref, kt_ref, v_ref, o_ref = refs[:4]
    lhs_sc, m_sc, l_sc, acc_sc, s0_sc, s1_sc = refs[-6:]
    bufs = (s0_sc, s1_sc)
    n_chunks = len(bounds) - 1
    j = pl.program_id(2)
    rows = 2 * TM
    lane = lax.broadcasted_iota(jnp.int32, (TM, LANES), 1)
    low = lane < HEAD_DIM
    qt = q_ref[...].astype(F32)
    lhs_sc[0:TM, :] = jnp.where(low, qt, 0.0).astype(BF16)
    lhs_sc[TM:rows, :] = jnp.where(low, 0.0, qt).astype(BF16)

    def chunk_keys(c):
        return slice(bounds[c], bounds[c + 1])

    def scores(keys):
        return jnp.dot(lhs_sc[...], kt_ref[0, 0, :, keys], preferred_element_type=F32)

    def softmax_pv(s, keys, first):
        width = s.shape[1]
        mx = s[:, 0:LANES]
        for t in range(1, width // LANES):
            mx = jnp.maximum(mx, s[:, t * LANES:(t + 1) * LANES])
        m_cur = jnp.max(mx, axis=1, keepdims=True)
        if first:
            m_next = jnp.broadcast_to(m_cur, (rows, LANES))
        else:
            m_prev = m_sc[...]
            m_next = jnp.maximum(m_prev, m_cur)
            alpha = jnp.exp2(m_prev - m_next)
        p = jnp.exp2(s - _tile_lanes(m_next, width))
        psum = p[:, 0:LANES]
        for t in range(1, width // LANES):
            psum = psum + p[:, t * LANES:(t + 1) * LANES]
        pv = jnp.dot(p.astype(BF16), v_ref[0, 0, keys, :], preferred_element_type=F32)
        if first:
            l_sc[...] = psum
            acc_sc[...] = pv
        else:
            l_sc[...] = alpha * l_sc[...] + psum
            acc_sc[...] = alpha * acc_sc[...] + pv
        m_sc[...] = m_next

    @pl.when(j == 0)
    def _():
        ctx_keys = slice(0, n_ctx)
        softmax_pv(scores(ctx_keys), ctx_keys, True)

    @pl.when(j > 0)
    def _():
        def put(c):
            keys = chunk_keys(c)
            bufs[c % 2][:, 0:keys.stop - keys.start] = scores(keys)

        def take(c):
            keys = chunk_keys(c)
            softmax_pv(bufs[c % 2][:, 0:keys.stop - keys.start], keys, c == 0)

        put(0)
        for c in range(n_chunks):
            if c + 1 < n_chunks:
                put(c + 1)
            take(c)

    o = acc_sc[...] / jnp.sum(l_sc[...], axis=1, keepdims=True)
    if diff:
        lv = lam_ref[...]
        lam = (jnp.exp(jnp.sum(lv[0:1] * lv[1:2], axis=1, keepdims=True))
               - jnp.exp(jnp.sum(lv[2:3] * lv[3:4], axis=1, keepdims=True)) + lam_init)
        od = o[0:TM] - lam * o[TM:rows]
        od = od * lax.rsqrt(jnp.mean(od * od, axis=-1, keepdims=True) + EPS)
        o_ref[...] = ((od * gout_ref[...]) * (1.0 - lam_init)).astype(o_ref.dtype)
    else:
        o_ref[...] = jnp.where(low, o[0:TM], o[TM:rows]).astype(o_ref.dtype)


def _attention(q, kt, v, batch, nblk, *, diff, lam_vecs=None, g_out=None, lam_init=0.0):
    kv_heads = kt.shape[1]
    s_len = kt.shape[3]
    tiles = BRANCH_WIDTH // LANES
    rows = 2 * TM
    q_map = lambda b, h, j: (b * nblk + j, h)
    kv_map = lambda b, h, j: (b, h * kv_heads // tiles, 0, 0)
    const = lambda b, h, j: (0, 0)
    in_specs = [pl.BlockSpec((TM, LANES), q_map),
                pl.BlockSpec((1, 1, LANES, s_len), kv_map),
                pl.BlockSpec((1, 1, s_len, LANES), kv_map)]
    args = [q, kt, v]
    if diff:
        in_specs += [pl.BlockSpec((4, HEAD_DIM), const), pl.BlockSpec((1, LANES), const)]
        args += [lam_vecs, g_out]
    n_ctx = TM
    tkl = min(SCORE_KEYS, s_len - n_ctx)
    assert (s_len - n_ctx) % tkl == 0
    bounds = [0] + list(range(n_ctx + tkl, s_len + 1, tkl))
    kern = functools.partial(_attn_kernel, diff=diff, lam_init=lam_init,
                             n_ctx=n_ctx, bounds=tuple(bounds))
    return pl.pallas_call(
        kern,
        grid=(batch, tiles, nblk),
        in_specs=in_specs,
        out_specs=pl.BlockSpec((TM, LANES), q_map),
        out_shape=jax.ShapeDtypeStruct((batch * s_len, BRANCH_WIDTH), BF16),
        scratch_shapes=([pltpu.VMEM((rows, LANES), BF16)]
                        + [pltpu.VMEM((rows, LANES), F32)] * 3
                        + [pltpu.VMEM((rows, n_ctx + tkl), F32)] * 2),
        compiler_params=_cparams(3),
        name="diff_attn" if diff else "gqa_attn",
    )(*args)


def _s5_kernel(uf_ref, ub_ref, pin_ref, pout_ref, wd_ref, a_ref, cm_ref, yf_ref, yb_ref,
               bu_sc, st_sc):
    n_tiles = BRANCH_WIDTH // LANES
    half = SUBLANES * S5_STATE
    rows = TC * SUBLANES
    tok = (SUBLANES // 2) * TC

    @pl.when(pl.program_id(0) == 0)
    def _():
        st_sc[...] = jnp.zeros_like(st_sc)

    seq = lax.broadcasted_iota(jnp.int32, (rows, LANES), 0) & (SUBLANES - 1)
    fwd = seq < SUBLANES // 2
    lhs_dir = []
    for d, ref in enumerate((uf_ref, ub_ref)):
        ud = ref[...].reshape(tok, BRANCH_WIDTH).astype(BF16)
        lhs_dir.append(jnp.dot(pin_ref[d], ud, preferred_element_type=F32).astype(BF16))
    def drive(j):
        cs = slice(j * LANES, (j + 1) * LANES)
        lhs = jnp.concatenate([lhs_dir[0][:, cs], lhs_dir[1][:, cs]], axis=1)
        bu_sc[:, 2 * half * j:2 * half * (j + 1)] = jnp.dot(
            lhs, wd_ref[j], preferred_element_type=F32)

    def scan(j):
        re = slice(2 * half * j, 2 * half * j + half)
        im = slice(2 * half * j + half, 2 * half * (j + 1))
        ar = a_ref[0, :, half * j:half * (j + 1)]
        ai = a_ref[1, :, half * j:half * (j + 1)]

        def step(t, carry):
            xr, xi = carry
            r = pl.multiple_of(t * SUBLANES, SUBLANES)
            nxr = ar * xr - ai * xi + bu_sc[pl.ds(r, SUBLANES), re]
            nxi = ar * xi + ai * xr + bu_sc[pl.ds(r, SUBLANES), im]
            bu_sc[pl.ds(r, SUBLANES), re] = nxr
            bu_sc[pl.ds(r, SUBLANES), im] = nxi
            return nxr, nxi

        xr, xi = lax.fori_loop(0, TC, step, (st_sc[:, re], st_sc[:, im]), unroll=True)
        st_sc[:, re] = xr
        st_sc[:, im] = xi

    def readout(j):
        x = bu_sc[:, 2 * half * j:2 * half * (j + 1)].astype(BF16)
        yy = jnp.dot(x, cm_ref[j], preferred_element_type=F32)
        return jnp.where(fwd, yy[:, :LANES], yy[:, LANES:])

    ys = []
    drive(0)
    for j in range(n_tiles):
        if j + 1 < n_tiles:
            drive(j + 1)
        scan(j)
        ys.append(readout(j))
    y = jnp.concatenate(ys, axis=1)
    parts = []
    rest = y
    for _ in range(2):
        part = rest.astype(BF16)
        parts.append(part)
        rest = rest - part.astype(F32)
    for d, ref in enumerate((yf_ref, yb_ref)):
        out = None
        for part in parts:
            term = jnp.dot(pout_ref[d], part, preferred_element_type=F32)
            out = term if out is None else out + term
        ref[...] = out.reshape(ref.shape)


def _scan_row_placement(n_seq):
    p = np.zeros((2, TC * 2 * n_seq, n_seq * TC), np.float32)
    for b in range(n_seq):
        for k in range(TC):
            p[0, 2 * n_seq * k + b, b * TC + k] = 1.0
            p[1, 2 * n_seq * (TC - 1 - k) + n_seq + b, b * TC + k] = 1.0
    return jnp.asarray(p, BF16), jnp.asarray(p.transpose(0, 2, 1), BF16)


def _s5_scan(u3, wd, a8, cm, n_ctx):
    batch, s_len, _ = u3.shape
    rows = TC * SUBLANES
    n_tiles = BRANCH_WIDTH // LANES
    n_state = 2 * SUBLANES * S5_STATE * n_tiles
    n_steps = s_len // TC
    ctx_steps = n_ctx // TC
    p_in, p_out = _scan_row_placement(batch)
    fwd_map = lambda g: (0, g, 0)
    bwd_map = lambda g: (0, jnp.where(g < ctx_steps, ctx_steps - 1 - g,
                                      n_steps + ctx_steps - 1 - g), 0)
    c3 = lambda g: (0, 0, 0)
    blk = (batch, TC, BRANCH_WIDTH)
    out = jax.ShapeDtypeStruct(u3.shape, F32)
    return pl.pallas_call(
        _s5_kernel,
        grid=(n_steps,),
        in_specs=[pl.BlockSpec(blk, fwd_map),
                  pl.BlockSpec(blk, bwd_map),
                  pl.BlockSpec(p_in.shape, c3),
                  pl.BlockSpec(p_out.shape, c3),
                  pl.BlockSpec(wd.shape, c3),
                  pl.BlockSpec(a8.shape, c3),
                  pl.BlockSpec(cm.shape, c3)],
        out_specs=[pl.BlockSpec(blk, fwd_map), pl.BlockSpec(blk, bwd_map)],
        out_shape=[out, out],
        scratch_shapes=[pltpu.VMEM((rows, n_state), F32),
                        pltpu.VMEM((SUBLANES, n_state), F32)],
        compiler_params=_cparams(1),
        name="s5_scan",
    )(u3, u3, p_in, p_out, wd, a8, cm)


def _zoh(lam_re, lam_im, log_dt, b_re, b_im):
    dt = jnp.exp(log_dt)[..., None]
    mag = jnp.exp(lam_re * dt)
    a_re = mag * jnp.cos(lam_im * dt)
    a_im = mag * jnp.sin(lam_im * dt)
    den = lam_re * lam_re + lam_im * lam_im
    f_re = ((a_re - 1.0) * lam_re + a_im * lam_im) / den
    f_im = (a_im * lam_re - (a_re - 1.0) * lam_im) / den
    bb_re = f_re[..., None] * b_re - f_im[..., None] * b_im
    bb_im = f_re[..., None] * b_im + f_im[..., None] * b_re
    return a_re, a_im, bb_re, bb_im


def _s5_weights(lam_re, lam_im, log_dt, b_re, b_im, c_re, c_im, batch):
    n_tiles = BRANCH_WIDTH // LANES
    gpt = S5_GROUPS // n_tiles
    a_re, a_im, bb_re, bb_im = _zoh(lam_re, lam_im, log_dt, b_re, b_im)
    eye = jnp.eye(gpt, dtype=F32)

    def drive(bb):
        t = bb.reshape(2, n_tiles, gpt, S5_STATE, S5_CH)
        w = jnp.einsum('djgpc,gh->jdgchp', t, eye)
        return w.reshape(n_tiles, 2 * gpt * S5_CH, gpt * S5_STATE)

    def read(cc):
        t = cc.reshape(2, n_tiles, gpt, S5_CH, S5_STATE)
        w = jnp.einsum('djgcp,gh->jgpdhc', t, eye)
        return w.reshape(n_tiles, gpt * S5_STATE, 2 * gpt * S5_CH)

    wd = jnp.concatenate([drive(bb_re), drive(bb_im)], axis=2).astype(BF16)
    cm = jnp.concatenate([read(c_re), read(-c_im)], axis=1).astype(BF16)

    def per_seq(a):
        return jnp.repeat(a.reshape(2, 1, S5_GROUPS * S5_STATE), batch, axis=1).reshape(
            2 * batch, S5_GROUPS * S5_STATE)

    a8 = jnp.stack([per_seq(a_re), per_seq(a_im)])
    return wd, a8, cm


def _merge_kernel(x_ref, xc_ref, mod_ref, g1_ref, g2_ref, ya_ref, yd_ref, yf_ref, yr_ref, u_ref,
                  dsk_ref, wglu_ref, bglu_ref, wgate_ref, wbr_ref, wout_ref, x1_ref, h2_ref,
                  *, nblk):
    mods = mod_ref[0]
    proj = {k: jnp.dot(ref[...], wbr_ref[k], preferred_element_type=F32)
            for k, ref in ((0, ya_ref), (2, yd_ref))}
    x = _stream_tile(x_ref, xc_ref, nblk)
    h = _rms_mod(x, g1_ref[...], mods[0:1], mods[1:2]).astype(BF16)
    ys = yf_ref[...] + yr_ref[...] + dsk_ref[...] * u_ref[...]
    g = jax.nn.gelu(ys)
    yb = g * jax.nn.sigmoid(
        jnp.dot(g.astype(BF16), wglu_ref[...], preferred_element_type=F32) + bglu_ref[...])
    proj[1] = jnp.dot(yb.astype(BF16), wbr_ref[1], preferred_element_type=F32)
    m = None
    for k in range(3):
        gate = jax.nn.sigmoid(jnp.dot(h, wgate_ref[:, k * D_MODEL:(k + 1) * D_MODEL],
                                      preferred_element_type=F32))
        term = gate * proj[k]
        m = term if m is None else m + term
    y = jnp.dot(m.astype(BF16), wout_ref[...], preferred_element_type=F32)
    x1 = x + mods[2:3] * y
    x1_ref[...] = x1
    h2_ref[...] = _rms_mod(x1, g2_ref[...], mods[3:4], mods[4:5]).astype(BF16)


def _merge(x_lat, x_ctx, mods, g1, g2, ya, yd, yf, yr, u, d_skip, w_glu, b_glu, w_gate,
           w_branch, w_out, batch, nblk):
    t_rows = ya.shape[0]
    row = lambda i: (i, 0)
    mod_map = lambda i: (jnp.where(i % nblk == 0, batch, i // nblk), 0, 0)
    c2 = lambda i: (0, 0)
    c3 = lambda i: (0, 0, 0)
    x_specs, x_args = _stream_specs(x_lat, x_ctx, nblk)
    return pl.pallas_call(
        functools.partial(_merge_kernel, nblk=nblk),
        grid=(t_rows // TM,),
        in_specs=x_specs + [
                  pl.BlockSpec((1, 6, D_MODEL), mod_map),
                  pl.BlockSpec((1, D_MODEL), c2),
                  pl.BlockSpec((1, D_MODEL), c2),
                  pl.BlockSpec((TM, BRANCH_WIDTH), row),
                  pl.BlockSpec((TM, BRANCH_WIDTH), row),
                  pl.BlockSpec((TM, BRANCH_WIDTH), row),
                  pl.BlockSpec((TM, BRANCH_WIDTH), row),
                  pl.BlockSpec((TM, BRANCH_WIDTH), row),
                  pl.BlockSpec((1, BRANCH_WIDTH), c2),
                  pl.BlockSpec((BRANCH_WIDTH, BRANCH_WIDTH), c2),
                  pl.BlockSpec((1, BRANCH_WIDTH), c2),
                  pl.BlockSpec((D_MODEL, 3 * D_MODEL), c2),
                  pl.BlockSpec((3, BRANCH_WIDTH, D_MODEL), c3),
                  pl.BlockSpec((D_MODEL, D_MODEL), c2)],
        out_specs=[pl.BlockSpec((TM, D_MODEL), row), pl.BlockSpec((TM, D_MODEL), row)],
        out_shape=[jax.ShapeDtypeStruct((t_rows, D_MODEL), F32),
                   jax.ShapeDtypeStruct((t_rows, D_MODEL), BF16)],
        compiler_params=_cparams(1),
        name="merge",
    )(*x_args, mods, g1, g2, ya, yd, yf, yr, u, d_skip, w_glu, b_glu, w_gate, w_branch, w_out)


def _ffn_kernel(h_ref, hp_ref, hn_ref, x1_ref, mod_ref, perm_ref, wup_ref, cw_ref, wdn_ref, o_ref,
                lhs_sc, u0_sc, u1_sc, act_sc, *, nblk, n_chunks):
    pj = pl.program_id(0) % nblk
    left_ok = (pj >= 2).astype(F32)
    right_ok = jnp.logical_and(pj != 0, pj != nblk - 1).astype(F32)
    lhs_sc[0:TM, :] = jnp.dot(perm_ref[0], h_ref[...], preferred_element_type=F32).astype(BF16)
    hrow = lax.broadcasted_iota(jnp.int32, (HALO, D_MODEL), 0)
    halo = (jnp.where(hrow == HALO - 1, hp_ref[...].astype(F32) * left_ok, 0.0)
            + jnp.where(hrow == 0, hn_ref[...].astype(F32) * right_ok, 0.0))
    lhs_sc[TM:TM + HALO, :] = halo.astype(BF16)
    sub = lax.broadcasted_iota(jnp.int32, (SUBLANES, 2 * FF_CHUNK), 0)

    def up(k, buf):
        buf[...] = jnp.dot(lhs_sc[...], wup_ref[k], preferred_element_type=F32)

    def activate(k, buf):
        cw = cw_ref[k]
        before = buf[TM + HALO - 1:TM + HALO, :]
        after = buf[TM:TM + 1, :]
        first = jnp.where(sub == 0, before, pltpu.roll(buf[TM - SUBLANES:TM, :], 1, 0))
        last = jnp.where(sub == SUBLANES - 1, after,
                         pltpu.roll(buf[0:SUBLANES, :], SUBLANES - 1, 0))
        prev = jnp.concatenate([first, buf[0:TM - SUBLANES, :]], axis=0)
        nxt = jnp.concatenate([buf[SUBLANES:TM, :], last], axis=0)
        c = prev * cw[0:1] + buf[0:TM, :] * cw[1:2] + nxt * cw[2:3] + cw[3:4]
        act = jax.nn.silu(c[:, FF_CHUNK:]) * c[:, :FF_CHUNK]
        act_sc[:, k * FF_CHUNK:(k + 1) * FF_CHUNK] = act.astype(BF16)

    bufs = (u0_sc, u1_sc)
    up(0, bufs[0])
    for k in range(n_chunks):
        if k + 1 < n_chunks:
            up(k + 1, bufs[(k + 1) % 2])
        activate(k, bufs[k % 2])
    act = jnp.dot(perm_ref[1], act_sc[...], preferred_element_type=F32).astype(BF16)
    y = jnp.dot(act, wdn_ref[...], preferred_element_type=F32)
    o_ref[...] = x1_ref[...] + mod_ref[0][5:6] * y


def _strided_row_order():
    p = np.zeros((TM, TM), np.float32)
    groups = TM // SUBLANES
    for r in range(groups):
        for s in range(SUBLANES):
            p[SUBLANES * r + s, groups * s + r] = 1.0
    return jnp.asarray(np.stack([p, p.T]), BF16)


def _ffn(h2, x1, mods, wup, cw, wdn, batch, nblk, latent_only):
    t_rows = x1.shape[0]
    n_chunks = wup.shape[0]
    per = TM // HALO
    last = t_rows // HALO - 1
    row = lambda i: (i, 0)
    mod_map = lambda i: (jnp.where(i % nblk == 0, batch, i // nblk), 0, 0)
    kern = functools.partial(_ffn_kernel, nblk=nblk, n_chunks=n_chunks)
    if latent_only:
        out_map = lambda i: ((i // nblk) * (nblk - 1) + jnp.maximum(i % nblk - 1, 0), 0)
        out_rows = t_rows - batch * TM
    else:
        out_map, out_rows = row, t_rows
    return pl.pallas_call(
        kern,
        grid=(t_rows // TM,),
        in_specs=[pl.BlockSpec((TM, D_MODEL), row),
                  pl.BlockSpec((HALO, D_MODEL), lambda i: (jnp.maximum(i * per - 1, 0), 0)),
                  pl.BlockSpec((HALO, D_MODEL), lambda i: (jnp.minimum((i + 1) * per, last), 0)),
                  pl.BlockSpec((TM, D_MODEL), row),
                  pl.BlockSpec((1, 6, D_MODEL), mod_map),
                  pl.BlockSpec((2, TM, TM), lambda i: (0, 0, 0)),
                  pl.BlockSpec(wup.shape, lambda i: (0, 0, 0)),
                  pl.BlockSpec(cw.shape, lambda i: (0, 0, 0)),
                  pl.BlockSpec(wdn.shape, lambda i: (0, 0))],
        out_specs=pl.BlockSpec((TM, D_MODEL), out_map),
        out_shape=jax.ShapeDtypeStruct((out_rows, D_MODEL), F32),
        scratch_shapes=([pltpu.VMEM((TM + HALO, D_MODEL), BF16)]
                        + [pltpu.VMEM((TM + HALO, 2 * FF_CHUNK), F32)] * 2
                        + [pltpu.VMEM((TM, D_FF), BF16)]),
        compiler_params=_cparams(1),
        name="conv_ffn",
    )(h2, h2, h2, x1, mods, _strided_row_order(), wup, cw, wdn)


def _rope_tables(seq, ctx):
    rows = seq // GRID_W
    row = jnp.repeat(jnp.arange(rows, dtype=F32), GRID_W)
    col = jnp.tile(jnp.arange(GRID_W, dtype=F32), rows)
    n_freq = HEAD_DIM // 4
    inv_freq = ROPE_BASE ** (-jnp.arange(n_freq, dtype=F32) / n_freq)
    ang = jnp.concatenate([row[:, None] * inv_freq, col[:, None] * inv_freq], axis=-1)
    ang = jnp.concatenate([ang, ang], axis=-1)
    cos = jnp.concatenate([jnp.ones((ctx, HEAD_DIM), F32), jnp.cos(ang)], axis=0)
    sin = jnp.concatenate([jnp.zeros((ctx, HEAD_DIM), F32), jnp.sin(ang)], axis=0)
    sign = jnp.where(jnp.arange(HEAD_DIM) < HEAD_DIM // 2, -1.0, 1.0).astype(F32)
    two = lambda t: jnp.concatenate([t, t], axis=1)
    return two(cos), two(sin * sign)


def kernel(x, c, ctx, c_ctx, w_ada, b_ada, norm_g, w_in, qk_gain, ssm_lam_re, ssm_lam_im, ssm_log_dt, ssm_b_re, ssm_b_im, ssm_c_re, ssm_c_im, ssm_d, w_glu, b_glu, diff_lam, diff_norm_g, w_branch, w_out, w_up, conv_w, conv_b, w_down):
    batch, seq, _ = x.shape
    n_ctx = ctx.shape[1]
    depth = w_in.shape[0]
    assert n_ctx == TM and seq % TM == 0 and 2 * batch == SUBLANES
    s_len = n_ctx + seq
    nblk = s_len // TM
    assert s_len % TC == 0 and D_FF % FF_CHUNK == 0
    n_chunks = D_FF // FF_CHUNK

    cos_t, sin_t = _rope_tables(seq, n_ctx)
    cc = jnp.concatenate([c, c_ctx[None, :], jnp.zeros((SUBLANES - batch - 1, D_MODEL), F32)], axis=0)
    mods_all = _ada_mods(cc, w_ada, b_ada).reshape(depth, SUBLANES, 6, D_MODEL)

    xa = x.reshape(batch * seq, D_MODEL)
    xc = ctx.reshape(batch * n_ctx, D_MODEL)
    for i in range(depth):
        lam_init = 0.8 - 0.6 * math.exp(-0.3 * i)
        mods = mods_all[i]
        g1 = norm_g[i, 0:1]
        g2 = norm_g[i, 1:2]
        w_i = w_in[i].astype(BF16)
        qa, kta, va, u, qc, ktc, vc = _inproj(xa, xc, mods, g1, w_i[:, :N_QKVU],
                                              jnp.tile(qk_gain[i], (1, BRANCH_WIDTH // HEAD_DIM)),
                                              cos_t, sin_t, batch, nblk)
        ya = _attention(qa, kta, va, batch, nblk, diff=False)
        yd = _attention(qc, ktc, vc, batch, nblk, diff=True, lam_vecs=diff_lam[i],
                        g_out=diff_norm_g[i][None, :], lam_init=lam_init)

        wd, a8, cm = _s5_weights(ssm_lam_re[i], ssm_lam_im[i], ssm_log_dt[i], ssm_b_re[i],
                                 ssm_b_im[i], ssm_c_re[i], ssm_c_im[i], batch)
        yf, yr = _s5_scan(u.reshape(batch, s_len, BRANCH_WIDTH), wd, a8, cm, n_ctx)
        yf = yf.reshape(batch * s_len, BRANCH_WIDTH)
        yr = yr.reshape(batch * s_len, BRANCH_WIDTH)

        x1, h2 = _merge(xa, xc, mods, g1, g2, ya, yd, yf, yr, u, ssm_d[i][None, :],
                        w_glu[i].astype(BF16), b_glu[i][None, :], w_i[:, N_QKVU:],
                        w_branch[i].astype(BF16), w_out[i].astype(BF16), batch, nblk)

        def chunked(t):
            a = t[:, :D_FF].reshape(t.shape[0], n_chunks, FF_CHUNK)
            g = t[:, D_FF:].reshape(t.shape[0], n_chunks, FF_CHUNK)
            return jnp.concatenate([a, g], axis=-1).transpose(1, 0, 2)

        wup = chunked(w_up[i].astype(BF16))
        cw = chunked(jnp.concatenate(
            [conv_w[i], conv_b[i][None, :], jnp.zeros((SUBLANES - 4, 2 * D_FF), F32)], axis=0))
        wdn = w_down[i].astype(BF16)
        xa = _ffn(h2, x1, mods, wup, cw, wdn, batch, nblk, latent_only=(i == depth - 1))
        xc = None

    return xa.reshape(batch, seq, D_MODEL)
```

```python
import functools
import math

import jax
import jax.numpy as jnp
import numpy as np
from jax import lax
from jax.experimental import pallas as pl
from jax.experimental.pallas import tpu as pltpu

F32 = jnp.float32
BF16 = jnp.bfloat16

D_MODEL = 1024
HEAD_DIM = 64
GRID_W = 64
ROPE_BASE = 10000.0
EPS = 1e-6
BRANCH_WIDTH = D_MODEL // 2
A_HEADS = BRANCH_WIDTH // HEAD_DIM
A_KV_HEADS = A_HEADS // 4
C_HEADS = BRANCH_WIDTH // (2 * HEAD_DIM)
S5_CH = 16
S5_STATE = 64
S5_GROUPS = BRANCH_WIDTH // S5_CH
D_FF = 2816
IN_SIZES = (512, 128, 128, 512, 512, 512, 512, 3 * D_MODEL)
IN_OFFS = tuple(int(v) for v in np.cumsum((0,) + IN_SIZES))
N_QKVU = IN_OFFS[7]

LANES = 128
SUBLANES = 8
TM = 256
SCORE_KEYS = 2048
TC = 64
FF_CHUNK = 128
HALO = 16
ADA_COLS = 1536
V7X_VMEM_BYTES = 64 * 1024 * 1024
VMEM_LIMIT = V7X_VMEM_BYTES * 7 // 8
SCORE_SCALE = HEAD_DIM ** -0.5 * math.log2(math.e)


def _cparams(n_axes):
    return pltpu.CompilerParams(dimension_semantics=("arbitrary",) * n_axes,
                                vmem_limit_bytes=VMEM_LIMIT)


def _rms_mod(x, gain, shift, scale):
    y = x * lax.rsqrt(jnp.mean(x * x, axis=-1, keepdims=True) + EPS)
    return (y * gain) * (1.0 + scale) + shift


def _head_avg_matrix(width):
    shift = HEAD_DIM.bit_length() - 1
    r = lax.broadcasted_iota(jnp.int32, (width, width), 0) >> shift
    c = lax.broadcasted_iota(jnp.int32, (width, width), 1) >> shift
    return jnp.where(r == c, 1.0 / HEAD_DIM, 0.0).astype(BF16)


def _head_rms(z, gain, avg):
    sq = z * z
    hi = sq.astype(BF16)
    lo = (sq - hi.astype(F32)).astype(BF16)
    ms = (jnp.dot(hi, avg, preferred_element_type=F32)
          + jnp.dot(lo, avg, preferred_element_type=F32))
    return z * lax.rsqrt(ms + EPS) * gain


def _tile_lanes(t, width):
    reps = width // t.shape[1]
    return t if reps == 1 else jnp.concatenate([t] * reps, axis=1)


def _rope(z, cos, sin_signed):
    width = z.shape[1]
    lane = lax.broadcasted_iota(jnp.int32, z.shape, 1)
    first_half = (lane & (HEAD_DIM - 1)) < HEAD_DIM // 2
    rot = jnp.where(first_half,
                    pltpu.roll(z, width - HEAD_DIM // 2, 1),
                    pltpu.roll(z, HEAD_DIM // 2, 1))
    return z * _tile_lanes(cos, width) + rot * _tile_lanes(sin_signed, width)


def _dup_halves(z):
    lane = lax.broadcasted_iota(jnp.int32, z.shape, 1)
    low = lane < HEAD_DIM
    sw = pltpu.roll(z, HEAD_DIM, 1)
    return jnp.where(low, z, sw), jnp.where(low, sw, z)


def _ada_kernel(c_ref, w_ref, b_ref, o_ref):
    a = jax.nn.silu(c_ref[...])
    o_ref[0] = jnp.dot(a, w_ref[0], preferred_element_type=F32,
                       precision=lax.Precision.HIGHEST) + b_ref[0]


def _ada_mods(cc, w_ada, b_ada):
    depth, _, n = w_ada.shape
    tn = ADA_COLS
    assert n % tn == 0
    return pl.pallas_call(
        _ada_kernel,
        grid=(depth, n // tn),
        in_specs=[pl.BlockSpec((SUBLANES, D_MODEL), lambda l, j: (0, 0)),
                  pl.BlockSpec((1, D_MODEL, tn), lambda l, j: (l, 0, j)),
                  pl.BlockSpec((1, 1, tn), lambda l, j: (l, 0, j))],
        out_specs=pl.BlockSpec((1, SUBLANES, tn), lambda l, j: (l, 0, j)),
        out_shape=jax.ShapeDtypeStruct((depth, SUBLANES, n), F32),
        compiler_params=_cparams(2),
        name="ada_mods",
    )(cc, w_ada, b_ada.reshape(depth, 1, n))


def _stream_specs(x_lat, x_ctx, nblk):
    blk = (TM, D_MODEL)
    if x_ctx is None:
        return ([pl.BlockSpec(blk, lambda i: (i, 0)),
                 pl.BlockSpec(blk, lambda i: ((i // nblk) * nblk, 0))], [x_lat, x_lat])
    lat_map = lambda i: ((i // nblk) * (nblk - 1) + jnp.maximum(i % nblk - 1, 0), 0)
    return ([pl.BlockSpec(blk, lat_map), pl.BlockSpec(blk, lambda i: (i // nblk, 0))],
            [x_lat, x_ctx])


def _stream_tile(x_ref, xc_ref, nblk):
    return jnp.where(pl.program_id(0) % nblk == 0, xc_ref[...], x_ref[...])


def _inproj_kernel(x_ref, xc_ref, mod_ref, g_ref, w_ref, qk_ref, cos_ref, sin_ref,
                   qa_ref, kta_ref, va_ref, u_ref, qc_ref, ktc_ref, vc_ref, *, nblk):
    mods = mod_ref[0]
    x = _stream_tile(x_ref, xc_ref, nblk)
    h = _rms_mod(x, g_ref[...], mods[0:1], mods[1:2]).astype(BF16)
    cos = cos_ref[...]
    sin = sin_ref[...]
    avg = _head_avg_matrix(BRANCH_WIDTH)

    def seg(k):
        return jnp.dot(h, w_ref[:, IN_OFFS[k]:IN_OFFS[k + 1]], preferred_element_type=F32)

    def gain(k, width=BRANCH_WIDTH):
        return qk_ref[k:k + 1, :width]

    k = _rope(_head_rms(seg(5), gain(3), avg), cos, sin)
    for hh in range(C_HEADS):
        ktc_ref[0, hh] = k[:, hh * LANES:(hh + 1) * LANES].T.astype(BF16)
    k = _rope(_head_rms(seg(1), gain(1, LANES), avg[:LANES, :LANES]), cos, sin)
    for hh, kd in enumerate(_dup_halves(k)):
        kta_ref[0, hh] = kd.T.astype(BF16)
    q = _rope(_head_rms(seg(4), gain(2), avg), cos, sin)
    qc_ref[...] = (q * SCORE_SCALE).astype(BF16)
    q = _rope(_head_rms(seg(0), gain(0), avg), cos, sin)
    qa_ref[...] = (q * SCORE_SCALE).astype(BF16)
    for hh, vd in enumerate(_dup_halves(seg(2))):
        va_ref[0, hh] = vd.astype(BF16)
    v = seg(6)
    for hh in range(C_HEADS):
        vc_ref[0, hh] = v[:, hh * LANES:(hh + 1) * LANES].astype(BF16)
    u_ref[...] = seg(3)


def _inproj(x_lat, x_ctx, mods, norm_g, w_in, qk_gain, cos_t, sin_t, batch, nblk):
    s_len = nblk * TM
    t_rows = batch * s_len
    row = lambda i: (i, 0)
    mod_map = lambda i: (jnp.where(i % nblk == 0, batch, i // nblk), 0, 0)
    pos = lambda i: (i % nblk, 0)
    kt_map = lambda i: (i // nblk, 0, 0, i % nblk)
    v_map = lambda i: (i // nblk, 0, i % nblk, 0)
    x_specs, x_args = _stream_specs(x_lat, x_ctx, nblk)
    return pl.pallas_call(
        functools.partial(_inproj_kernel, nblk=nblk),
        grid=(t_rows // TM,),
        in_specs=x_specs + [
                  pl.BlockSpec((1, 6, D_MODEL), mod_map),
                  pl.BlockSpec((1, D_MODEL), lambda i: (0, 0)),
                  pl.BlockSpec((D_MODEL, N_QKVU), lambda i: (0, 0)),
                  pl.BlockSpec((4, BRANCH_WIDTH), lambda i: (0, 0)),
                  pl.BlockSpec((TM, LANES), pos),
                  pl.BlockSpec((TM, LANES), pos)],
        out_specs=[pl.BlockSpec((TM, BRANCH_WIDTH), row),
                   pl.BlockSpec((1, A_KV_HEADS, LANES, TM), kt_map),
                   pl.BlockSpec((1, A_KV_HEADS, TM, LANES), v_map),
                   pl.BlockSpec((TM, BRANCH_WIDTH), row),
                   pl.BlockSpec((TM, BRANCH_WIDTH), row),
                   pl.BlockSpec((1, C_HEADS, LANES, TM), kt_map),
                   pl.BlockSpec((1, C_HEADS, TM, LANES), v_map)],
        out_shape=[jax.ShapeDtypeStruct((t_rows, BRANCH_WIDTH), BF16),
                   jax.ShapeDtypeStruct((batch, A_KV_HEADS, LANES, s_len), BF16),
                   jax.ShapeDtypeStruct((batch, A_KV_HEADS, s_len, LANES), BF16),
                   jax.ShapeDtypeStruct((t_rows, BRANCH_WIDTH), F32),
                   jax.ShapeDtypeStruct((t_rows, BRANCH_WIDTH), BF16),
                   jax.ShapeDtypeStruct((batch, C_HEADS, LANES, s_len), BF16),
                   jax.ShapeDtypeStruct((batch, C_HEADS, s_len, LANES), BF16)],
        compiler_params=_cparams(1),
        name="in_proj",
    )(*x_args, mods, norm_g, w_in, qk_gain, cos_t, sin_t)


def _attn_kernel(*refs, diff, lam_init, n_ctx, bounds):
    if diff:
        q_ref, kt_ref, v_ref, lam_ref, gout_ref, o_ref = refs[:6]
    else:
        q_ref, kt_ref, v_ref, o_ref = refs[:4]
    lhs_sc, m_sc, l_sc, acc_sc, s0_sc, s1_sc = refs[-6:]
    bufs = (s0_sc, s1_sc)
    n_chunks = len(bounds) - 1
    j = pl.program_id(2)
    rows = 2 * TM
    lane = lax.broadcasted_iota(jnp.int32, (TM, LANES), 1)
    low = lane < HEAD_DIM
    qt = q_ref[...].astype(F32)
    lhs_sc[0:TM, :] = jnp.where(low, qt, 0.0).astype(BF16)
    lhs_sc[TM:rows, :] = jnp.where(low, 0.0, qt).astype(BF16)

    def chunk_keys(c):
        return slice(bounds[c], bounds[c + 1])

    def scores(keys):
        return jnp.dot(lhs_sc[...], kt_ref[0, 0, :, keys], preferred_element_type=F32)

    def softmax_pv(s, keys, first):
        width = s.shape[1]
        mx = s[:, 0:LANES]
        for t in range(1, width // LANES):
            mx = jnp.maximum(mx, s[:, t * LANES:(t + 1) * LANES])
        m_cur = jnp.max(mx, axis=1, keepdims=True)
        if first:
            m_next = jnp.broadcast_to(m_cur, (rows, LANES))
        else:
            m_prev = m_sc[...]
            m_next = jnp.maximum(m_prev, m_cur)
            alpha = jnp.exp2(m_prev - m_next)
        p = jnp.exp2(s - _tile_lanes(m_next, width))
        psum = p[:, 0:LANES]
        for t in range(1, width // LANES):
            psum = psum + p[:, t * LANES:(t + 1) * LANES]
        pv = jnp.dot(p.astype(BF16), v_ref[0, 0, keys, :], preferred_element_type=F32)
        if first:
            l_sc[...] = psum
            acc_sc[...] = pv
        else:
            l_sc[...] = alpha * l_sc[...] + psum
            acc_sc[...] = alpha * acc_sc[...] + pv
        m_sc[...] = m_next

    @pl.when(j == 0)
    def _():
        ctx_keys = slice(0, n_ctx)
        softmax_pv(scores(ctx_keys), ctx_keys, True)

    @pl.when(j > 0)
    def _():
        def put(c):
            keys = chunk_keys(c)
            bufs[c % 2][:, 0:keys.stop - keys.start] = scores(keys)

        def take(c):
            keys = chunk_keys(c)
            softmax_pv(bufs[c % 2][:, 0:keys.stop - keys.start], keys, c == 0)

        put(0)
        for c in range(n_chunks):
            if c + 1 < n_chunks:
                put(c + 1)
            take(c)

    o = acc_sc[...] / jnp.sum(l_sc[...], axis=1, keepdims=True)
    if diff:
        lv = lam_ref[...]
        lam = (jnp.exp(jnp.sum(lv[0:1] * lv[1:2], axis=1, keepdims=True))
               - jnp.exp(jnp.sum(lv[2:3] * lv[3:4], axis=1, keepdims=True)) + lam_init)
        od = o[0:TM] - lam * o[TM:rows]
        od = od * lax.rsqrt(jnp.mean(od * od, axis=-1, keepdims=True) + EPS)
        o_ref[...] = ((od * gout_ref[...]) * (1.0 - lam_init)).astype(o_ref.dtype)
    else:
        o_ref[...] = jnp.where(low, o[0:TM], o[TM:rows]).astype(o_ref.dtype)


def _attention(q, kt, v, batch, nblk, *, diff, lam_vecs=None, g_out=None, lam_init=0.0):
    kv_heads = kt.shape[1]
    s_len = kt.shape[3]
    tiles = BRANCH_WIDTH // LANES
    rows = 2 * TM
    q_map = lambda b, h, j: (b * nblk + j, h)
    kv_map = lambda b, h, j: (b, h * kv_heads // tiles, 0, 0)
    const = lambda b, h, j: (0, 0)
    in_specs = [pl.BlockSpec((TM, LANES), q_map),
                pl.BlockSpec((1, 1, LANES, s_len), kv_map),
                pl.BlockSpec((1, 1, s_len, LANES), kv_map)]
    args = [q, kt, v]
    if diff:
        in_specs += [pl.BlockSpec((4, HEAD_DIM), const), pl.BlockSpec((1, LANES), const)]
        args += [lam_vecs, g_out]
    n_ctx = TM
    tkl = min(SCORE_KEYS, s_len - n_ctx)
    assert (s_len - n_ctx) % tkl == 0
    bounds = [0] + list(range(n_ctx + tkl, s_len + 1, tkl))
    kern = functools.partial(_attn_kernel, diff=diff, lam_init=lam_init,
                             n_ctx=n_ctx, bounds=tuple(bounds))
    return pl.pallas_call(
        kern,
        grid=(batch, tiles, nblk),
        in_specs=in_specs,
        out_specs=pl.BlockSpec((TM, LANES), q_map),
        out_shape=jax.ShapeDtypeStruct((batch * s_len, BRANCH_WIDTH), BF16),
        scratch_shapes=([pltpu.VMEM((rows, LANES), BF16)]
                        + [pltpu.VMEM((rows, LANES), F32)] * 3
                        + [pltpu.VMEM((rows, n_ctx + tkl), F32)] * 2),
        compiler_params=_cparams(3),
        name="diff_attn" if diff else "gqa_attn",
    )(*args)


def _s5_kernel(uf_ref, ub_ref, pin_ref, pout_ref, wd_ref, a_ref, cm_ref, yf_ref, yb_ref,
               bu_sc, st_sc):
    n_tiles = BRANCH_WIDTH // LANES
    half = SUBLANES * S5_STATE
    rows = TC * SUBLANES
    tok = (SUBLANES // 2) * TC

    @pl.when(pl.program_id(0) == 0)
    def _():
        st_sc[...] = jnp.zeros_like(st_sc)

    seq = lax.broadcasted_iota(jnp.int32, (rows, LANES), 0) & (SUBLANES - 1)
    fwd = seq < SUBLANES // 2
    lhs_dir = []
    for d, ref in enumerate((uf_ref, ub_ref)):
        ud = ref[...].reshape(tok, BRANCH_WIDTH).astype(BF16)
        lhs_dir.append(jnp.dot(pin_ref[d], ud, preferred_element_type=F32).astype(BF16))
    def drive(j):
        cs = slice(j * LANES, (j + 1) * LANES)
        lhs = jnp.concatenate([lhs_dir[0][:, cs], lhs_dir[1][:, cs]], axis=1)
        bu_sc[:, 2 * half * j:2 * half * (j + 1)] = jnp.dot(
            lhs, wd_ref[j], preferred_element_type=F32)

    def scan(j):
        re = slice(2 * half * j, 2 * half * j + half)
        im = slice(2 * half * j + half, 2 * half * (j + 1))
        ar = a_ref[0, :, half * j:half * (j + 1)]
        ai = a_ref[1, :, half * j:half * (j + 1)]

        def step(t, carry):
            xr, xi = carry
            r = pl.multiple_of(t * SUBLANES, SUBLANES)
            nxr = ar * xr - ai * xi + bu_sc[pl.ds(r, SUBLANES), re]
            nxi = ar * xi + ai * xr + bu_sc[pl.ds(r, SUBLANES), im]
            bu_sc[pl.ds(r, SUBLANES), re] = nxr
            bu_sc[pl.ds(r, SUBLANES), im] = nxi
            return nxr, nxi

        xr, xi = lax.fori_loop(0, TC, step, (st_sc[:, re], st_sc[:, im]), unroll=True)
        st_sc[:, re] = xr
        st_sc[:, im] = xi

    def readout(j):
        x = bu_sc[:, 2 * half * j:2 * half * (j + 1)].astype(BF16)
        yy = jnp.dot(x, cm_ref[j], preferred_element_type=F32)
        return jnp.where(fwd, yy[:, :LANES], yy[:, LANES:])

    ys = []
    drive(0)
    for j in range(n_tiles):
        if j + 1 < n_tiles:
            drive(j + 1)
        scan(j)
        ys.append(readout(j))
    y = jnp.concatenate(ys, axis=1)
    parts = []
    rest = y
    for _ in range(2):
        part = rest.astype(BF16)
        parts.append(part)
        rest = rest - part.astype(F32)
    stacked = jnp.concatenate(parts, axis=0)
    for d, ref in enumerate((yf_ref, yb_ref)):
        out = jnp.dot(pout_ref[d], stacked, preferred_element_type=F32)
        ref[...] = out.reshape(ref.shape)


def _scan_row_placement(n_seq):
    p = np.zeros((2, TC * 2 * n_seq, n_seq * TC), np.float32)
    for b in range(n_seq):
        for k in range(TC):
            p[0, 2 * n_seq * k + b, b * TC + k] = 1.0
            p[1, 2 * n_seq * (TC - 1 - k) + n_seq + b, b * TC + k] = 1.0
    pt = p.transpose(0, 2, 1)
    return jnp.asarray(p, BF16), jnp.asarray(np.concatenate([pt, pt], axis=2), BF16)


def _s5_scan(u3, wd, a8, cm, n_ctx):
    batch, s_len, _ = u3.shape
    rows = TC * SUBLANES
    n_tiles = BRANCH_WIDTH // LANES
    n_state = 2 * SUBLANES * S5_STATE * n_tiles
    n_steps = s_len // TC
    ctx_steps = n_ctx // TC
    p_in, p_out = _scan_row_placement(batch)
    fwd_map = lambda g: (0, g, 0)
    bwd_map = lambda g: (0, jnp.where(g < ctx_steps, ctx_steps - 1 - g,
                                      n_steps + ctx_steps - 1 - g), 0)
    c3 = lambda g: (0, 0, 0)
    blk = (batch, TC, BRANCH_WIDTH)
    out = jax.ShapeDtypeStruct(u3.shape, F32)
    return pl.pallas_call(
        _s5_kernel,
        grid=(n_steps,),
        in_specs=[pl.BlockSpec(blk, fwd_map),
                  pl.BlockSpec(blk, bwd_map),
                  pl.BlockSpec(p_in.shape, c3),
                  pl.BlockSpec(p_out.shape, c3),
                  pl.BlockSpec(wd.shape, c3),
                  pl.BlockSpec(a8.shape, c3),
                  pl.BlockSpec(cm.shape, c3)],
        out_specs=[pl.BlockSpec(blk, fwd_map), pl.BlockSpec(blk, bwd_map)],
        out_shape=[out, out],
        scratch_shapes=[pltpu.VMEM((rows, n_state), F32),
                        pltpu.VMEM((SUBLANES, n_state), F32)],
        compiler_params=_cparams(1),
        name="s5_scan",
    )(u3, u3, p_in, p_out, wd, a8, cm)


def _zoh(lam_re, lam_im, log_dt, b_re, b_im):
    dt = jnp.exp(log_dt)[..., None]
    mag = jnp.exp(lam_re * dt)
    a_re = mag * jnp.cos(lam_im * dt)
    a_im = mag * jnp.sin(lam_im * dt)
    den = lam_re * lam_re + lam_im * lam_im
    f_re = ((a_re - 1.0) * lam_re + a_im * lam_im) / den
    f_im = (a_im * lam_re - (a_re - 1.0) * lam_im) / den
    bb_re = f_re[..., None] * b_re - f_im[..., None] * b_im
    bb_im = f_re[..., None] * b_im + f_im[..., None] * b_re
    return a_re, a_im, bb_re, bb_im


def _s5_weights(lam_re, lam_im, log_dt, b_re, b_im, c_re, c_im, batch):
    n_tiles = BRANCH_WIDTH // LANES
    gpt = S5_GROUPS // n_tiles
    a_re, a_im, bb_re, bb_im = _zoh(lam_re, lam_im, log_dt, b_re, b_im)
    eye = jnp.eye(gpt, dtype=F32)

    def drive(bb):
        t = bb.reshape(2, n_tiles, gpt, S5_STATE, S5_CH)
        w = jnp.einsum('djgpc,gh->jdgchp', t, eye)
        return w.reshape(n_tiles, 2 * gpt * S5_CH, gpt * S5_STATE)

    def read(cc):
        t = cc.reshape(2, n_tiles, gpt, S5_CH, S5_STATE)
        w = jnp.einsum('djgcp,gh->jgpdhc', t, eye)
        return w.reshape(n_tiles, gpt * S5_STATE, 2 * gpt * S5_CH)

    wd = jnp.concatenate([drive(bb_re), drive(bb_im)], axis=2).astype(BF16)
    cm = jnp.concatenate([read(c_re), read(-c_im)], axis=1).astype(BF16)

    def per_seq(a):
        return jnp.repeat(a.reshape(2, 1, S5_GROUPS * S5_STATE), batch, axis=1).reshape(
            2 * batch, S5_GROUPS * S5_STATE)

    a8 = jnp.stack([per_seq(a_re), per_seq(a_im)])
    return wd, a8, cm


def _merge_kernel(x_ref, xc_ref, mod_ref, g1_ref, g2_ref, ya_ref, yd_ref, yf_ref, yr_ref, u_ref,
                  dsk_ref, wglu_ref, bglu_ref, wgate_ref, wbr_ref, wout_ref, x1_ref, h2_ref,
                  *, nblk):
    mods = mod_ref[0]
    proj = {k: jnp.dot(ref[...], wbr_ref[k], preferred_element_type=F32)
            for k, ref in ((0, ya_ref), (2, yd_ref))}
    x = _stream_tile(x_ref, xc_ref, nblk)
    h = _rms_mod(x, g1_ref[...], mods[0:1], mods[1:2]).astype(BF16)
    ys = yf_ref[...] + yr_ref[...] + dsk_ref[...] * u_ref[...]
    g = jax.nn.gelu(ys)
    yb = g * jax.nn.sigmoid(
        jnp.dot(g.astype(BF16), wglu_ref[...], preferred_element_type=F32) + bglu_ref[...])
    proj[1] = jnp.dot(yb.astype(BF16), wbr_ref[1], preferred_element_type=F32)
    m = None
    for k in range(3):
        gate = jax.nn.sigmoid(jnp.dot(h, wgate_ref[:, k * D_MODEL:(k + 1) * D_MODEL],
                                      preferred_element_type=F32))
        term = gate * proj[k]
        m = term if m is None else m + term
    y = jnp.dot(m.astype(BF16), wout_ref[...], preferred_element_type=F32)
    x1 = x + mods[2:3] * y
    x1_ref[...] = x1
    h2_ref[...] = _rms_mod(x1, g2_ref[...], mods[3:4], mods[4:5]).astype(BF16)


def _merge(x_lat, x_ctx, mods, g1, g2, ya, yd, yf, yr, u, d_skip, w_glu, b_glu, w_gate,
           w_branch, w_out, batch, nblk):
    t_rows = ya.shape[0]
    row = lambda i: (i, 0)
    mod_map = lambda i: (jnp.where(i % nblk == 0, batch, i // nblk), 0, 0)
    c2 = lambda i: (0, 0)
    c3 = lambda i: (0, 0, 0)
    x_specs, x_args = _stream_specs(x_lat, x_ctx, nblk)
    return pl.pallas_call(
        functools.partial(_merge_kernel, nblk=nblk),
        grid=(t_rows // TM,),
        in_specs=x_specs + [
                  pl.BlockSpec((1, 6, D_MODEL), mod_map),
                  pl.BlockSpec((1, D_MODEL), c2),
                  pl.BlockSpec((1, D_MODEL), c2),
                  pl.BlockSpec((TM, BRANCH_WIDTH), row),
                  pl.BlockSpec((TM, BRANCH_WIDTH), row),
                  pl.BlockSpec((TM, BRANCH_WIDTH), row),
                  pl.BlockSpec((TM, BRANCH_WIDTH), row),
                  pl.BlockSpec((TM, BRANCH_WIDTH), row),
                  pl.BlockSpec((1, BRANCH_WIDTH), c2),
                  pl.BlockSpec((BRANCH_WIDTH, BRANCH_WIDTH), c2),
                  pl.BlockSpec((1, BRANCH_WIDTH), c2),
                  pl.BlockSpec((D_MODEL, 3 * D_MODEL), c2),
                  pl.BlockSpec((3, BRANCH_WIDTH, D_MODEL), c3),
                  pl.BlockSpec((D_MODEL, D_MODEL), c2)],
        out_specs=[pl.BlockSpec((TM, D_MODEL), row), pl.BlockSpec((TM, D_MODEL), row)],
        out_shape=[jax.ShapeDtypeStruct((t_rows, D_MODEL), F32),
                   jax.ShapeDtypeStruct((t_rows, D_MODEL), BF16)],
        compiler_params=_cparams(1),
        name="merge",
    )(*x_args, mods, g1, g2, ya, yd, yf, yr, u, d_skip, w_glu, b_glu, w_gate, w_branch, w_out)


def _ffn_kernel(h_ref, hp_ref, hn_ref, x1_ref, mod_ref, perm_ref, wup_ref, cw_ref, wdn_ref, o_ref,
                lhs_sc, u0_sc, u1_sc, act_sc, *, nblk, n_chunks):
    pj = pl.program_id(0) % nblk
    left_ok = (pj >= 2).astype(F32)
    right_ok = jnp.logical_and(pj != 0, pj != nblk - 1).astype(F32)
    lhs_sc[0:TM, :] = jnp.dot(perm_ref[0], h_ref[...], preferred_element_type=F32).astype(BF16)
    hrow = lax.broadcasted_iota(jnp.int32, (HALO, D_MODEL), 0)
    halo = (jnp.where(hrow == HALO - 1, hp_ref[...].astype(F32) * left_ok, 0.0)
            + jnp.where(hrow == 0, hn_ref[...].astype(F32) * right_ok, 0.0))
    lhs_sc[TM:TM + HALO, :] = halo.astype(BF16)
    sub = lax.broadcasted_iota(jnp.int32, (SUBLANES, 2 * FF_CHUNK), 0)

    def up(k, buf):
        buf[...] = jnp.dot(lhs_sc[...], wup_ref[k], preferred_element_type=F32)

    def activate(k, buf):
        cw = cw_ref[k]
        before = buf[TM + HALO - 1:TM + HALO, :]
        after = buf[TM:TM + 1, :]
        first = jnp.where(sub == 0, before, pltpu.roll(buf[TM - SUBLANES:TM, :], 1, 0))
        last = jnp.where(sub == SUBLANES - 1, after,
                         pltpu.roll(buf[0:SUBLANES, :], SUBLANES - 1, 0))
        prev = jnp.concatenate([first, buf[0:TM - SUBLANES, :]], axis=0)
        nxt = jnp.concatenate([buf[SUBLANES:TM, :], last], axis=0)
        c = prev * cw[0:1] + buf[0:TM, :] * cw[1:2] + nxt * cw[2:3] + cw[3:4]
        act = jax.nn.silu(c[:, FF_CHUNK:]) * c[:, :FF_CHUNK]
        act_sc[:, k * FF_CHUNK:(k + 1) * FF_CHUNK] = act.astype(BF16)

    bufs = (u0_sc, u1_sc)
    up(0, bufs[0])
    for k in range(n_chunks):
        if k + 1 < n_chunks:
            up(k + 1, bufs[(k + 1) % 2])
        activate(k, bufs[k % 2])
    act = jnp.dot(perm_ref[1], act_sc[...], preferred_element_type=F32).astype(BF16)
    y = jnp.dot(act, wdn_ref[...], preferred_element_type=F32)
    o_ref[...] = x1_ref[...] + mod_ref[0][5:6] * y


def _strided_row_order():
    p = np.zeros((TM, TM), np.float32)
    groups = TM // SUBLANES
    for r in range(groups):
        for s in range(SUBLANES):
            p[SUBLANES * r + s, groups * s + r] = 1.0
    return jnp.asarray(np.stack([p, p.T]), BF16)


def _ffn(h2, x1, mods, wup, cw, wdn, batch, nblk, latent_only):
    t_rows = x1.shape[0]
    n_chunks = wup.shape[0]
    per = TM // HALO
    last = t_rows // HALO - 1
    row = lambda i: (i, 0)
    mod_map = lambda i: (jnp.where(i % nblk == 0, batch, i // nblk), 0, 0)
    kern = functools.partial(_ffn_kernel, nblk=nblk, n_chunks=n_chunks)
    if latent_only:
        out_map = lambda i: ((i // nblk) * (nblk - 1) + jnp.maximum(i % nblk - 1, 0), 0)
        out_rows = t_rows - batch * TM
    else:
        out_map, out_rows = row, t_rows
    return pl.pallas_call(
        kern,
        grid=(t_rows // TM,),
        in_specs=[pl.BlockSpec((TM, D_MODEL), row),
                  pl.BlockSpec((HALO, D_MODEL), lambda i: (jnp.maximum(i * per - 1, 0), 0)),
                  pl.BlockSpec((HALO, D_MODEL), lambda i: (jnp.minimum((i + 1) * per, last), 0)),
                  pl.BlockSpec((TM, D_MODEL), row),
                  pl.BlockSpec((1, 6, D_MODEL), mod_map),
                  pl.BlockSpec((2, TM, TM), lambda i: (0, 0, 0)),
                  pl.BlockSpec(wup.shape, lambda i: (0, 0, 0)),
                  pl.BlockSpec(cw.shape, lambda i: (0, 0, 0)),
                  pl.BlockSpec(wdn.shape, lambda i: (0, 0))],
        out_specs=pl.BlockSpec((TM, D_MODEL), out_map),
        out_shape=jax.ShapeDtypeStruct((out_rows, D_MODEL), F32),
        scratch_shapes=([pltpu.VMEM((TM + HALO, D_MODEL), BF16)]
                        + [pltpu.VMEM((TM + HALO, 2 * FF_CHUNK), F32)] * 2
                        + [pltpu.VMEM((TM, D_FF), BF16)]),
        compiler_params=_cparams(1),
        name="conv_ffn",
    )(h2, h2, h2, x1, mods, _strided_row_order(), wup, cw, wdn)


def _rope_tables(seq, ctx):
    rows = seq // GRID_W
    row = jnp.repeat(jnp.arange(rows, dtype=F32), GRID_W)
    col = jnp.tile(jnp.arange(GRID_W, dtype=F32), rows)
    n_freq = HEAD_DIM // 4
    inv_freq = ROPE_BASE ** (-jnp.arange(n_freq, dtype=F32) / n_freq)
    ang = jnp.concatenate([row[:, None] * inv_freq, col[:, None] * inv_freq], axis=-1)
    ang = jnp.concatenate([ang, ang], axis=-1)
    cos = jnp.concatenate([jnp.ones((ctx, HEAD_DIM), F32), jnp.cos(ang)], axis=0)
    sin = jnp.concatenate([jnp.zeros((ctx, HEAD_DIM), F32), jnp.sin(ang)], axis=0)
    sign = jnp.where(jnp.arange(HEAD_DIM) < HEAD_DIM // 2, -1.0, 1.0).astype(F32)
    two = lambda t: jnp.concatenate([t, t], axis=1)
    return two(cos), two(sin * sign)


def kernel(x, c, ctx, c_ctx, w_ada, b_ada, norm_g, w_in, qk_gain, ssm_lam_re, ssm_lam_im, ssm_log_dt, ssm_b_re, ssm_b_im, ssm_c_re, ssm_c_im, ssm_d, w_glu, b_glu, diff_lam, diff_norm_g, w_branch, w_out, w_up, conv_w, conv_b, w_down):
    batch, seq, _ = x.shape
    n_ctx = ctx.shape[1]
    depth = w_in.shape[0]
    assert n_ctx == TM and seq % TM == 0 and 2 * batch == SUBLANES
    s_len = n_ctx + seq
    nblk = s_len // TM
    assert s_len % TC == 0 and D_FF % FF_CHUNK == 0
    n_chunks = D_FF // FF_CHUNK

    cos_t, sin_t = _rope_tables(seq, n_ctx)
    cc = jnp.concatenate([c, c_ctx[None, :], jnp.zeros((SUBLANES - batch - 1, D_MODEL), F32)], axis=0)
    mods_all = _ada_mods(cc, w_ada, b_ada).reshape(depth, SUBLANES, 6, D_MODEL)

    xa = x.reshape(batch * seq, D_MODEL)
    xc = ctx.reshape(batch * n_ctx, D_MODEL)
    for i in range(depth):
        lam_init = 0.8 - 0.6 * math.exp(-0.3 * i)
        mods = mods_all[i]
        g1 = norm_g[i, 0:1]
        g2 = norm_g[i, 1:2]
        w_i = w_in[i].astype(BF16)
        qa, kta, va, u, qc, ktc, vc = _inproj(xa, xc, mods, g1, w_i,
                                              jnp.tile(qk_gain[i], (1, BRANCH_WIDTH // HEAD_DIM)),
                                              cos_t, sin_t, batch, nblk)
        ya = _attention(qa, kta, va, batch, nblk, diff=False)
        yd = _attention(qc, ktc, vc, batch, nblk, diff=True, lam_vecs=diff_lam[i],
                        g_out=diff_norm_g[i][None, :], lam_init=lam_init)

        wd, a8, cm = _s5_weights(ssm_lam_re[i], ssm_lam_im[i], ssm_log_dt[i], ssm_b_re[i],
                                 ssm_b_im[i], ssm_c_re[i], ssm_c_im[i], batch)
        yf, yr = _s5_scan(u.reshape(batch, s_len, BRANCH_WIDTH), wd, a8, cm, n_ctx)
        yf = yf.reshape(batch * s_len, BRANCH_WIDTH)
        yr = yr.reshape(batch * s_len, BRANCH_WIDTH)

        x1, h2 = _merge(xa, xc, mods, g1, g2, ya, yd, yf, yr, u, ssm_d[i][None, :],
                        w_glu[i].astype(BF16), b_glu[i][None, :], w_i[:, N_QKVU:],
                        w_branch[i].astype(BF16), w_out[i].astype(BF16), batch, nblk)

        def chunked(t):
            a = t[:, :D_FF].reshape(t.shape[0], n_chunks, FF_CHUNK)
            g = t[:, D_FF:].reshape(t.shape[0], n_chunks, FF_CHUNK)
            return jnp.concatenate([a, g], axis=-1).transpose(1, 0, 2)

        wup = chunked(w_up[i].astype(BF16))
        cw = chunked(jnp.concatenate(
            [conv_w[i], conv_b[i][None, :], jnp.zeros((SUBLANES - 4, 2 * D_FF), F32)], axis=0))
        wdn = w_down[i].astype(BF16)
        xa = _ffn(h2, x1, mods, wup, cw, wdn, batch, nblk, latent_only=(i == depth - 1))
        xc = None

    return xa.reshape(batch, seq, D_MODEL)
```

```python
import functools
import math

import jax
import jax.numpy as jnp
import numpy as np
from jax import lax
from jax.experimental import pallas as pl
from jax.experimental.pallas import tpu as pltpu

F32 = jnp.float32
BF16 = jnp.bfloat16

D_MODEL = 1024
HEAD_DIM = 64
GRID_W = 64
ROPE_BASE = 10000.0
EPS = 1e-6
BRANCH_WIDTH = D_MODEL // 2
A_HEADS = BRANCH_WIDTH // HEAD_DIM
A_KV_HEADS = A_HEADS // 4
C_HEADS = BRANCH_WIDTH // (2 * HEAD_DIM)
S5_CH = 16
S5_STATE = 64
S5_GROUPS = BRANCH_WIDTH // S5_CH
D_FF = 2816
IN_SIZES = (512, 128, 128, 512, 512, 512, 512, 3 * D_MODEL)
IN_OFFS = tuple(int(v) for v in np.cumsum((0,) + IN_SIZES))
N_QKVU = IN_OFFS[7]

LANES = 128
SUBLANES = 8
TM = 256
SCORE_KEYS = 2048
TC = 64
FF_CHUNK = 128
HALO = 16
ADA_COLS = 1536
V7X_VMEM_BYTES = 64 * 1024 * 1024
VMEM_LIMIT = V7X_VMEM_BYTES * 7 // 8
SCORE_SCALE = HEAD_DIM ** -0.5 * math.log2(math.e)


def _cparams(n_axes):
    return pltpu.CompilerParams(dimension_semantics=("arbitrary",) * n_axes,
                                vmem_limit_bytes=VMEM_LIMIT)


def _rms_mod(x, gain, shift, scale):
    y = x * lax.rsqrt(jnp.mean(x * x, axis=-1, keepdims=True) + EPS)
    return (y * gain) * (1.0 + scale) + shift


def _head_avg_matrix(width):
    shift = HEAD_DIM.bit_length() - 1
    r = lax.broadcasted_iota(jnp.int32, (width, width), 0) >> shift
    c = lax.broadcasted_iota(jnp.int32, (width, width), 1) >> shift
    return jnp.where(r == c, 1.0 / HEAD_DIM, 0.0).astype(BF16)


def _head_rms(z, gain, avg):
    sq = z * z
    hi = sq.astype(BF16)
    lo = (sq - hi.astype(F32)).astype(BF16)
    ms = (jnp.dot(hi, avg, preferred_element_type=F32)
          + jnp.dot(lo, avg, preferred_element_type=F32))
    return z * lax.rsqrt(ms + EPS) * gain


def _tile_lanes(t, width):
    reps = width // t.shape[1]
    return t if reps == 1 else jnp.concatenate([t] * reps, axis=1)


def _rope(z, cos, sin_signed):
    width = z.shape[1]
    lane = lax.broadcasted_iota(jnp.int32, z.shape, 1)
    first_half = (lane & (HEAD_DIM - 1)) < HEAD_DIM // 2
    rot = jnp.where(first_half,
                    pltpu.roll(z, width - HEAD_DIM // 2, 1),
                    pltpu.roll(z, HEAD_DIM // 2, 1))
    return z * _tile_lanes(cos, width) + rot * _tile_lanes(sin_signed, width)


def _dup_halves(z):
    lane = lax.broadcasted_iota(jnp.int32, z.shape, 1)
    low = lane < HEAD_DIM
    sw = pltpu.roll(z, HEAD_DIM, 1)
    return jnp.where(low, z, sw), jnp.where(low, sw, z)


def _ada_kernel(c_ref, w_ref, b_ref, o_ref):
    a = jax.nn.silu(c_ref[...])
    o_ref[0] = jnp.dot(a, w_ref[0], preferred_element_type=F32,
                       precision=lax.Precision.HIGHEST) + b_ref[0]


def _ada_mods(cc, w_ada, b_ada):
    depth, _, n = w_ada.shape
    tn = ADA_COLS
    assert n % tn == 0
    return pl.pallas_call(
        _ada_kernel,
        grid=(depth, n // tn),
        in_specs=[pl.BlockSpec((SUBLANES, D_MODEL), lambda l, j: (0, 0)),
                  pl.BlockSpec((1, D_MODEL, tn), lambda l, j: (l, 0, j)),
                  pl.BlockSpec((1, 1, tn), lambda l, j: (l, 0, j))],
        out_specs=pl.BlockSpec((1, SUBLANES, tn), lambda l, j: (l, 0, j)),
        out_shape=jax.ShapeDtypeStruct((depth, SUBLANES, n), F32),
        compiler_params=_cparams(2),
        name="ada_mods",
    )(cc, w_ada, b_ada.reshape(depth, 1, n))


def _stream_specs(x_lat, x_ctx, nblk):
    blk = (TM, D_MODEL)
    if x_ctx is None:
        return ([pl.BlockSpec(blk, lambda i: (i, 0)),
                 pl.BlockSpec(blk, lambda i: ((i // nblk) * nblk, 0))], [x_lat, x_lat])
    lat_map = lambda i: ((i // nblk) * (nblk - 1) + jnp.maximum(i % nblk - 1, 0), 0)
    return ([pl.BlockSpec(blk, lat_map), pl.BlockSpec(blk, lambda i: (i // nblk, 0))],
            [x_lat, x_ctx])


def _stream_tile(x_ref, xc_ref, nblk):
    return jnp.where(pl.program_id(0) % nblk == 0, xc_ref[...], x_ref[...])


def _inproj_kernel(x_ref, xc_ref, mod_ref, g_ref, w_ref, qk_ref, cos_ref, sin_ref,
                   qa_ref, kta_ref, va_ref, u_ref, qc_ref, ktc_ref, vc_ref, *, nblk):
    mods = mod_ref[0]
    x = _stream_tile(x_ref, xc_ref, nblk)
    h = _rms_mod(x, g_ref[...], mods[0:1], mods[1:2]).astype(BF16)
    cos = cos_ref[...]
    sin = sin_ref[...]
    avg = _head_avg_matrix(BRANCH_WIDTH)

    def seg(k):
        return jnp.dot(h, w_ref[:, IN_OFFS[k]:IN_OFFS[k + 1]], preferred_element_type=F32)

    def gain(k, width=BRANCH_WIDTH):
        return qk_ref[k:k + 1, :width]

    k = _rope(_head_rms(seg(5), gain(3), avg), cos, sin)
    for hh in range(C_HEADS):
        ktc_ref[0, hh] = k[:, hh * LANES:(hh + 1) * LANES].T.astype(BF16)
    k = _rope(_head_rms(seg(1), gain(1, LANES), avg[:LANES, :LANES]), cos, sin)
    for hh, kd in enumerate(_dup_halves(k)):
        kta_ref[0, hh] = kd.T.astype(BF16)
    q = _rope(_head_rms(seg(4), gain(2), avg), cos, sin)
    qc_ref[...] = (q * SCORE_SCALE).astype(BF16)
    q = _rope(_head_rms(seg(0), gain(0), avg), cos, sin)
    qa_ref[...] = (q * SCORE_SCALE).astype(BF16)
    for hh, vd in enumerate(_dup_halves(seg(2))):
        va_ref[0, hh] = vd.astype(BF16)
    v = seg(6)
    for hh in range(C_HEADS):
        vc_ref[0, hh] = v[:, hh * LANES:(hh + 1) * LANES].astype(BF16)
    u_ref[...] = seg(3)


def _inproj(x_lat, x_ctx, mods, norm_g, w_in, qk_gain, cos_t, sin_t, batch, nblk):
    s_len = nblk * TM
    t_rows = batch * s_len
    row = lambda i: (i, 0)
    mod_map = lambda i: (jnp.where(i % nblk == 0, batch, i // nblk), 0, 0)
    pos = lambda i: (i % nblk, 0)
    kt_map = lambda i: (i // nblk, 0, 0, i % nblk)
    v_map = lambda i: (i // nblk, 0, i % nblk, 0)
    x_specs, x_args = _stream_specs(x_lat, x_ctx, nblk)
    return pl.pallas_call(
        functools.partial(_inproj_kernel, nblk=nblk),
        grid=(t_rows // TM,),
        in_specs=x_specs + [
                  pl.BlockSpec((1, 6, D_MODEL), mod_map),
                  pl.BlockSpec((1, D_MODEL), lambda i: (0, 0)),
                  pl.BlockSpec((D_MODEL, N_QKVU), lambda i: (0, 0)),
                  pl.BlockSpec((4, BRANCH_WIDTH), lambda i: (0, 0)),
                  pl.BlockSpec((TM, LANES), pos),
                  pl.BlockSpec((TM, LANES), pos)],
        out_specs=[pl.BlockSpec((TM, BRANCH_WIDTH), row),
                   pl.BlockSpec((1, A_KV_HEADS, LANES, TM), kt_map),
                   pl.BlockSpec((1, A_KV_HEADS, TM, LANES), v_map),
                   pl.BlockSpec((TM, BRANCH_WIDTH), row),
                   pl.BlockSpec((TM, BRANCH_WIDTH), row),
                   pl.BlockSpec((1, C_HEADS, LANES, TM), kt_map),
                   pl.BlockSpec((1, C_HEADS, TM, LANES), v_map)],
        out_shape=[jax.ShapeDtypeStruct((t_rows, BRANCH_WIDTH), BF16),
                   jax.ShapeDtypeStruct((batch, A_KV_HEADS, LANES, s_len), BF16),
                   jax.ShapeDtypeStruct((batch, A_KV_HEADS, s_len, LANES), BF16),
                   jax.ShapeDtypeStruct((t_rows, BRANCH_WIDTH), F32),
                   jax.ShapeDtypeStruct((t_rows, BRANCH_WIDTH), BF16),
                   jax.ShapeDtypeStruct((batch, C_HEADS, LANES, s_len), BF16),
                   jax.ShapeDtypeStruct((batch, C_HEADS, s_len, LANES), BF16)],
        compiler_params=_cparams(1),
        name="in_proj",
    )(*x_args, mods, norm_g, w_in, qk_gain, cos_t, sin_t)


def _attn_kernel(*refs, diff, lam_init, n_ctx, bounds):
    if diff:
        q_ref, kt_ref, v_ref, lam_ref, gout_ref, o_ref = refs[:6]
    else:
        q_ref, kt_ref, v_ref, o_ref = refs[:4]
    lhs_sc, m_sc, l_sc, acc_sc, s0_sc, s1_sc = refs[-6:]
    bufs = (s0_sc, s1_sc)
    n_chunks = len(bounds) - 1
    j = pl.program_id(2)
    rows = 2 * TM
    lane = lax.broadcasted_iota(jnp.int32, (TM, LANES), 1)
    low = lane < HEAD_DIM
    qt = q_ref[...].astype(F32)
    lhs_sc[0:TM, :] = jnp.where(low, qt, 0.0).astype(BF16)
    lhs_sc[TM:rows, :] = jnp.where(low, 0.0, qt).astype(BF16)

    def chunk_keys(c):
        return slice(bounds[c], bounds[c + 1])

    def scores(keys):
        return jnp.dot(lhs_sc[...], kt_ref[0, 0, :, keys], preferred_element_type=F32)

    def softmax_pv(s, keys, first):
        width = s.shape[1]
        mx = s[:, 0:LANES]
        for t in range(1, width // LANES):
            mx = jnp.maximum(mx, s[:, t * LANES:(t + 1) * LANES])
        m_cur = jnp.max(mx, axis=1, keepdims=True)
        if first:
            m_next = jnp.broadcast_to(m_cur, (rows, LANES))
        else:
            m_prev = m_sc[...]
            m_next = jnp.maximum(m_prev, m_cur)
            alpha = jnp.exp2(m_prev - m_next)
        p = jnp.exp2(s - _tile_lanes(m_next, width))
        psum = p[:, 0:LANES]
        for t in range(1, width // LANES):
            psum = psum + p[:, t * LANES:(t + 1) * LANES]
        pv = jnp.dot(p.astype(BF16), v_ref[0, 0, keys, :], preferred_element_type=F32)
        if first:
            l_sc[...] = psum
            acc_sc[...] = pv
        else:
            l_sc[...] = alpha * l_sc[...] + psum
            acc_sc[...] = alpha * acc_sc[...] + pv
        m_sc[...] = m_next

    @pl.when(j == 0)
    def _():
        ctx_keys = slice(0, n_ctx)
        softmax_pv(scores(ctx_keys), ctx_keys, True)

    @pl.when(j > 0)
    def _():
        def put(c):
            keys = chunk_keys(c)
            bufs[c % 2][:, 0:keys.stop - keys.start] = scores(keys)

        def take(c):
            keys = chunk_keys(c)
            softmax_pv(bufs[c % 2][:, 0:keys.stop - keys.start], keys, c == 0)

        put(0)
        for c in range(n_chunks):
            if c + 1 < n_chunks:
                put(c + 1)
            take(c)

    o = acc_sc[...] / jnp.sum(l_sc[...], axis=1, keepdims=True)
    if diff:
        lv = lam_ref[...]
        lam = (jnp.exp(jnp.sum(lv[0:1] * lv[1:2], axis=1, keepdims=True))
               - jnp.exp(jnp.sum(lv[2:3] * lv[3:4], axis=1, keepdims=True)) + lam_init)
        od = o[0:TM] - lam * o[TM:rows]
        od = od * lax.rsqrt(jnp.mean(od * od, axis=-1, keepdims=True) + EPS)
        o_ref[...] = ((od * gout_ref[...]) * (1.0 - lam_init)).astype(o_ref.dtype)
    else:
        o_ref[...] = jnp.where(low, o[0:TM], o[TM:rows]).astype(o_ref.dtype)


def _attention(q, kt, v, batch, nblk, *, diff, lam_vecs=None, g_out=None, lam_init=0.0):
    kv_heads = kt.shape[1]
    s_len = kt.shape[3]
    tiles = BRANCH_WIDTH // LANES
    rows = 2 * TM
    q_map = lambda b, h, j: (b * nblk + j, h)
    kv_map = lambda b, h, j: (b, h * kv_heads // tiles, 0, 0)
    const = lambda b, h, j: (0, 0)
    in_specs = [pl.BlockSpec((TM, LANES), q_map),
                pl.BlockSpec((1, 1, LANES, s_len), kv_map),
                pl.BlockSpec((1, 1, s_len, LANES), kv_map)]
    args = [q, kt, v]
    if diff:
        in_specs += [pl.BlockSpec((4, HEAD_DIM), const), pl.BlockSpec((1, LANES), const)]
        args += [lam_vecs, g_out]
    n_ctx = TM
    tkl = min(SCORE_KEYS, s_len - n_ctx)
    assert (s_len - n_ctx) % tkl == 0
    bounds = [0] + list(range(n_ctx + tkl, s_len + 1, tkl))
    kern = functools.partial(_attn_kernel, diff=diff, lam_init=lam_init,
                             n_ctx=n_ctx, bounds=tuple(bounds))
    return pl.pallas_call(
        kern,
        grid=(batch, tiles, nblk),
        in_specs=in_specs,
        out_specs=pl.BlockSpec((TM, LANES), q_map),
        out_shape=jax.ShapeDtypeStruct((batch * s_len, BRANCH_WIDTH), BF16),
        scratch_shapes=([pltpu.VMEM((rows, LANES), BF16)]
                        + [pltpu.VMEM((rows, LANES), F32)] * 3
                        + [pltpu.VMEM((rows, n_ctx + tkl), F32)] * 2),
        compiler_params=_cparams(3),
        name="diff_attn" if diff else "gqa_attn",
    )(*args)


def _s5_kernel(uf_ref, ub_ref, pin_ref, pout_ref, wd_ref, a_ref, cm_ref, yf_ref, yb_ref,
               bu_sc, st_sc):
    n_tiles = BRANCH_WIDTH // LANES
    half = SUBLANES * S5_STATE
    rows = TC * SUBLANES
    tok = (SUBLANES // 2) * TC

    @pl.when(pl.program_id(0) == 0)
    def _():
        st_sc[...] = jnp.zeros_like(st_sc)

    seq = lax.broadcasted_iota(jnp.int32, (rows, LANES), 0) & (SUBLANES - 1)
    fwd = seq < SUBLANES // 2
    lhs_dir = []
    for d, ref in enumerate((uf_ref, ub_ref)):
        ud = ref[...].reshape(tok, BRANCH_WIDTH).astype(BF16)
        lhs_dir.append(jnp.dot(pin_ref[d], ud, preferred_element_type=F32).astype(BF16))
    def drive(j):
        cs = slice(j * LANES, (j + 1) * LANES)
        lhs = jnp.concatenate([lhs_dir[0][:, cs], lhs_dir[1][:, cs]], axis=1)
        bu_sc[:, 2 * half * j:2 * half * (j + 1)] = jnp.dot(
            lhs, wd_ref[j], preferred_element_type=F32)

    def scan(j):
        re = slice(2 * half * j, 2 * half * j + half)
        im = slice(2 * half * j + half, 2 * half * (j + 1))
        ar = a_ref[0, :, half * j:half * (j + 1)]
        ai = a_ref[1, :, half * j:half * (j + 1)]

        def step(t, carry):
            xr, xi = carry
            r = pl.multiple_of(t * SUBLANES, SUBLANES)
            nxr = ar * xr - ai * xi + bu_sc[pl.ds(r, SUBLANES), re]
            nxi = ar * xi + ai * xr + bu_sc[pl.ds(r, SUBLANES), im]
            bu_sc[pl.ds(r, SUBLANES), re] = nxr
            bu_sc[pl.ds(r, SUBLANES), im] = nxi
            return nxr, nxi

        xr, xi = lax.fori_loop(0, TC, step, (st_sc[:, re], st_sc[:, im]), unroll=True)
        st_sc[:, re] = xr
        st_sc[:, im] = xi

    def readout(j):
        x = bu_sc[:, 2 * half * j:2 * half * (j + 1)].astype(BF16)
        yy = jnp.dot(x, cm_ref[j], preferred_element_type=F32)
        return jnp.where(fwd, yy[:, :LANES], yy[:, LANES:])

    ys = []
    drive(0)
    for j in range(n_tiles):
        if j + 1 < n_tiles:
            drive(j + 1)
        scan(j)
        ys.append(readout(j))
    y = jnp.concatenate(ys, axis=1)
    parts = []
    rest = y
    for _ in range(2):
        part = rest.astype(BF16)
        parts.append(part)
        rest = rest - part.astype(F32)
    stacked = jnp.concatenate(parts, axis=0)
    for d, ref in enumerate((yf_ref, yb_ref)):
        out = jnp.dot(pout_ref[d], stacked, preferred_element_type=F32)
        ref[...] = out.reshape(ref.shape)


def _scan_row_placement(n_seq):
    p = np.zeros((2, TC * 2 * n_seq, n_seq * TC), np.float32)
    for b in range(n_seq):
        for k in range(TC):
            p[0, 2 * n_seq * k + b, b * TC + k] = 1.0
            p[1, 2 * n_seq * (TC - 1 - k) + n_seq + b, b * TC + k] = 1.0
    pt = p.transpose(0, 2, 1)
    return jnp.asarray(p, BF16), jnp.asarray(np.concatenate([pt, pt], axis=2), BF16)


def _s5_scan(u3, wd, a8, cm, n_ctx):
    batch, s_len, _ = u3.shape
    rows = TC * SUBLANES
    n_tiles = BRANCH_WIDTH // LANES
    n_state = 2 * SUBLANES * S5_STATE * n_tiles
    n_steps = s_len // TC
    ctx_steps = n_ctx // TC
    p_in, p_out = _scan_row_placement(batch)
    fwd_map = lambda g: (0, g, 0)
    bwd_map = lambda g: (0, jnp.where(g < ctx_steps, ctx_steps - 1 - g,
                                      n_steps + ctx_steps - 1 - g), 0)
    c3 = lambda g: (0, 0, 0)
    blk = (batch, TC, BRANCH_WIDTH)
    out = jax.ShapeDtypeStruct(u3.shape, F32)
    return pl.pallas_call(
        _s5_kernel,
        grid=(n_steps,),
        in_specs=[pl.BlockSpec(blk, fwd_map),
                  pl.BlockSpec(blk, bwd_map),
                  pl.BlockSpec(p_in.shape, c3),
                  pl.BlockSpec(p_out.shape, c3),
                  pl.BlockSpec(wd.shape, c3),
                  pl.BlockSpec(a8.shape, c3),
                  pl.BlockSpec(cm.shape, c3)],
        out_specs=[pl.BlockSpec(blk, fwd_map), pl.BlockSpec(blk, bwd_map)],
        out_shape=[out, out],
        scratch_shapes=[pltpu.VMEM((rows, n_state), F32),
                        pltpu.VMEM((SUBLANES, n_state), F32)],
        compiler_params=_cparams(1),
        name="s5_scan",
    )(u3, u3, p_in, p_out, wd, a8, cm)


def _zoh(lam_re, lam_im, log_dt, b_re, b_im):
    dt = jnp.exp(log_dt)[..., None]
    mag = jnp.exp(lam_re * dt)
    a_re = mag * jnp.cos(lam_im * dt)
    a_im = mag * jnp.sin(lam_im * dt)
    den = lam_re * lam_re + lam_im * lam_im
    f_re = ((a_re - 1.0) * lam_re + a_im * lam_im) / den
    f_im = (a_im * lam_re - (a_re - 1.0) * lam_im) / den
    bb_re = f_re[..., None] * b_re - f_im[..., None] * b_im
    bb_im = f_re[..., None] * b_im + f_im[..., None] * b_re
    return a_re, a_im, bb_re, bb_im


def _s5_weights(lam_re, lam_im, log_dt, b_re, b_im, c_re, c_im, batch):
    n_tiles = BRANCH_WIDTH // LANES
    gpt = S5_GROUPS // n_tiles
    a_re, a_im, bb_re, bb_im = _zoh(lam_re, lam_im, log_dt, b_re, b_im)
    eye = jnp.eye(gpt, dtype=F32)

    def drive(bb):
        t = bb.reshape(2, n_tiles, gpt, S5_STATE, S5_CH)
        w = jnp.einsum('djgpc,gh->jdgchp', t, eye)
        return w.reshape(n_tiles, 2 * gpt * S5_CH, gpt * S5_STATE)

    def read(cc):
        t = cc.reshape(2, n_tiles, gpt, S5_CH, S5_STATE)
        w = jnp.einsum('djgcp,gh->jgpdhc', t, eye)
        return w.reshape(n_tiles, gpt * S5_STATE, 2 * gpt * S5_CH)

    wd = jnp.concatenate([drive(bb_re), drive(bb_im)], axis=2).astype(BF16)
    cm = jnp.concatenate([read(c_re), read(-c_im)], axis=1).astype(BF16)

    def per_seq(a):
        return jnp.repeat(a.reshape(2, 1, S5_GROUPS * S5_STATE), batch, axis=1).reshape(
            2 * batch, S5_GROUPS * S5_STATE)

    a8 = jnp.stack([per_seq(a_re), per_seq(a_im)])
    return wd, a8, cm


def _merge_kernel(x_ref, xc_ref, mod_ref, g1_ref, g2_ref, ya_ref, yd_ref, yf_ref, yr_ref, u_ref,
                  dsk_ref, wglu_ref, bglu_ref, wgate_ref, wbr_ref, wout_ref, x1_ref, h2_ref,
                  *, nblk):
    mods = mod_ref[0]
    proj = {k: jnp.dot(ref[...], wbr_ref[k], preferred_element_type=F32)
            for k, ref in ((0, ya_ref), (2, yd_ref))}
    x = _stream_tile(x_ref, xc_ref, nblk)
    h = _rms_mod(x, g1_ref[...], mods[0:1], mods[1:2]).astype(BF16)
    ys = yf_ref[...] + yr_ref[...] + dsk_ref[...] * u_ref[...]
    g = jax.nn.gelu(ys)
    yb = g * jax.nn.sigmoid(
        jnp.dot(g.astype(BF16), wglu_ref[...], preferred_element_type=F32) + bglu_ref[...])
    proj[1] = jnp.dot(yb.astype(BF16), wbr_ref[1], preferred_element_type=F32)
    m = None
    for k in range(3):
        gate = jax.nn.sigmoid(jnp.dot(h, wgate_ref[:, k * D_MODEL:(k + 1) * D_MODEL],
                                      preferred_element_type=F32))
        term = gate * proj[k]
        m = term if m is None else m + term
    y = jnp.dot(m.astype(BF16), wout_ref[...], preferred_element_type=F32)
    x1 = x + mods[2:3] * y
    x1_ref[...] = x1
    h2_ref[...] = _rms_mod(x1, g2_ref[...], mods[3:4], mods[4:5]).astype(BF16)


def _merge(x_lat, x_ctx, mods, g1, g2, ya, yd, yf, yr, u, d_skip, w_glu, b_glu, w_gate,
           w_branch, w_out, batch, nblk):
    t_rows = ya.shape[0]
    row = lambda i: (i, 0)
    mod_map = lambda i: (jnp.where(i % nblk == 0, batch, i // nblk), 0, 0)
    c2 = lambda i: (0, 0)
    c3 = lambda i: (0, 0, 0)
    x_specs, x_args = _stream_specs(x_lat, x_ctx, nblk)
    return pl.pallas_call(
        functools.partial(_merge_kernel, nblk=nblk),
        grid=(t_rows // TM,),
        in_specs=x_specs + [
                  pl.BlockSpec((1, 6, D_MODEL), mod_map),
                  pl.BlockSpec((1, D_MODEL), c2),
                  pl.BlockSpec((1, D_MODEL), c2),
                  pl.BlockSpec((TM, BRANCH_WIDTH), row),
                  pl.BlockSpec((TM, BRANCH_WIDTH), row),
                  pl.BlockSpec((TM, BRANCH_WIDTH), row),
                  pl.BlockSpec((TM, BRANCH_WIDTH), row),
                  pl.BlockSpec((TM, BRANCH_WIDTH), row),
                  pl.BlockSpec((1, BRANCH_WIDTH), c2),
                  pl.BlockSpec((BRANCH_WIDTH, BRANCH_WIDTH), c2),
                  pl.BlockSpec((1, BRANCH_WIDTH), c2),
                  pl.BlockSpec((D_MODEL, 3 * D_MODEL), c2),
                  pl.BlockSpec((3, BRANCH_WIDTH, D_MODEL), c3),
                  pl.BlockSpec((D_MODEL, D_MODEL), c2)],
        out_specs=[pl.BlockSpec((TM, D_MODEL), row), pl.BlockSpec((TM, D_MODEL), row)],
        out_shape=[jax.ShapeDtypeStruct((t_rows, D_MODEL), F32),
                   jax.ShapeDtypeStruct((t_rows, D_MODEL), BF16)],
        compiler_params=_cparams(1),
        name="merge",
    )(*x_args, mods, g1, g2, ya, yd, yf, yr, u, d_skip, w_glu, b_glu, w_gate, w_branch, w_out)


def _ffn_kernel(h_ref, hp_ref, hn_ref, x1_ref, mod_ref, perm_ref, wup_ref, cw_ref, wdn_ref, o_ref,
                lhs_sc, u0_sc, u1_sc, act_sc, *, nblk, n_chunks):
    pj = pl.program_id(0) % nblk
    left_ok = (pj >= 2).astype(F32)
    right_ok = jnp.logical_and(pj != 0, pj != nblk - 1).astype(F32)
    lhs_sc[0:TM, :] = jnp.dot(perm_ref[0], h_ref[...], preferred_element_type=F32).astype(BF16)
    hrow = lax.broadcasted_iota(jnp.int32, (HALO, D_MODEL), 0)
    halo = (jnp.where(hrow == HALO - 1, hp_ref[...].astype(F32) * left_ok, 0.0)
            + jnp.where(hrow == 0, hn_ref[...].astype(F32) * right_ok, 0.0))
    lhs_sc[TM:TM + HALO, :] = halo.astype(BF16)
    sub = lax.broadcasted_iota(jnp.int32, (SUBLANES, 2 * FF_CHUNK), 0)

    def up(k, buf):
        buf[...] = jnp.dot(lhs_sc[...], wup_ref[k], preferred_element_type=F32)

    def activate(k, buf):
        cw = cw_ref[k]
        before = buf[TM + HALO - 1:TM + HALO, :]
        after = buf[TM:TM + 1, :]
        first = jnp.where(sub == 0, before, pltpu.roll(buf[TM - SUBLANES:TM, :], 1, 0))
        last = jnp.where(sub == SUBLANES - 1, after,
                         pltpu.roll(buf[0:SUBLANES, :], SUBLANES - 1, 0))
        prev = jnp.concatenate([first, buf[0:TM - SUBLANES, :]], axis=0)
        nxt = jnp.concatenate([buf[SUBLANES:TM, :], last], axis=0)
        c = prev * cw[0:1] + buf[0:TM, :] * cw[1:2] + nxt * cw[2:3] + cw[3:4]
        act = jax.nn.silu(c[:, FF_CHUNK:]) * c[:, :FF_CHUNK]
        act_sc[:, k * FF_CHUNK:(k + 1) * FF_CHUNK] = act.astype(BF16)

    bufs = (u0_sc, u1_sc)
    up(0, bufs[0])
    for k in range(n_chunks):
        if k + 1 < n_chunks:
            up(k + 1, bufs[(k + 1) % 2])
        activate(k, bufs[k % 2])
    act = jnp.dot(perm_ref[1], act_sc[...], preferred_element_type=F32).astype(BF16)
    y = jnp.dot(act, wdn_ref[...], preferred_element_type=F32)
    o_ref[...] = x1_ref[...] + mod_ref[0][5:6] * y


def _strided_row_order():
    p = np.zeros((TM, TM), np.float32)
    groups = TM // SUBLANES
    for r in range(groups):
        for s in range(SUBLANES):
            p[SUBLANES * r + s, groups * s + r] = 1.0
    return jnp.asarray(np.stack([p, p.T]), BF16)


def _ffn(h2, x1, mods, wup, cw, wdn, batch, nblk, latent_only):
    t_rows = x1.shape[0]
    n_chunks = wup.shape[0]
    per = TM // HALO
    last = t_rows // HALO - 1
    row = lambda i: (i, 0)
    mod_map = lambda i: (jnp.where(i % nblk == 0, batch, i // nblk), 0, 0)
    kern = functools.partial(_ffn_kernel, nblk=nblk, n_chunks=n_chunks)
    if latent_only:
        out_map = lambda i: ((i // nblk) * (nblk - 1) + jnp.maximum(i % nblk - 1, 0), 0)
        out_rows = t_rows - batch * TM
    else:
        out_map, out_rows = row, t_rows
    return pl.pallas_call(
        kern,
        grid=(t_rows // TM,),
        in_specs=[pl.BlockSpec((TM, D_MODEL), row),
                  pl.BlockSpec((HALO, D_MODEL), lambda i: (jnp.maximum(i * per - 1, 0), 0)),
                  pl.BlockSpec((HALO, D_MODEL), lambda i: (jnp.minimum((i + 1) * per, last), 0)),
                  pl.BlockSpec((TM, D_MODEL), row),
                  pl.BlockSpec((1, 6, D_MODEL), mod_map),
                  pl.BlockSpec((2, TM, TM), lambda i: (0, 0, 0)),
                  pl.BlockSpec(wup.shape, lambda i: (0, 0, 0)),
                  pl.BlockSpec(cw.shape, lambda i: (0, 0, 0)),
                  pl.BlockSpec(wdn.shape, lambda i: (0, 0))],
        out_specs=pl.BlockSpec((TM, D_MODEL), out_map),
        out_shape=jax.ShapeDtypeStruct((out_rows, D_MODEL), F32),
        scratch_shapes=([pltpu.VMEM((TM + HALO, D_MODEL), BF16)]
                        + [pltpu.VMEM((TM + HALO, 2 * FF_CHUNK), F32)] * 2
                        + [pltpu.VMEM((TM, D_FF), BF16)]),
        compiler_params=_cparams(1),
        name="conv_ffn",
    )(h2, h2, h2, x1, mods, _strided_row_order(), wup, cw, wdn)


def _relayout_kernel(a_ref, g_ref, o_ref):
    o_ref[0] = jnp.concatenate([a_ref[...], g_ref[...]], axis=1).astype(o_ref.dtype)


def _chunk_major_bf16(w_up, n_chunks):
    rows = w_up.shape[0]
    return pl.pallas_call(
        _relayout_kernel,
        grid=(n_chunks,),
        in_specs=[pl.BlockSpec((rows, FF_CHUNK), lambda k: (0, k)),
                  pl.BlockSpec((rows, FF_CHUNK), lambda k: (0, n_chunks + k))],
        out_specs=pl.BlockSpec((1, rows, 2 * FF_CHUNK), lambda k: (k, 0, 0)),
        out_shape=jax.ShapeDtypeStruct((n_chunks, rows, 2 * FF_CHUNK), BF16),
        compiler_params=_cparams(1),
        name="ffn_weight_layout",
    )(w_up, w_up)


def _rope_tables(seq, ctx):
    rows = seq // GRID_W
    row = jnp.repeat(jnp.arange(rows, dtype=F32), GRID_W)
    col = jnp.tile(jnp.arange(GRID_W, dtype=F32), rows)
    n_freq = HEAD_DIM // 4
    inv_freq = ROPE_BASE ** (-jnp.arange(n_freq, dtype=F32) / n_freq)
    ang = jnp.concatenate([row[:, None] * inv_freq, col[:, None] * inv_freq], axis=-1)
    ang = jnp.concatenate([ang, ang], axis=-1)
    cos = jnp.concatenate([jnp.ones((ctx, HEAD_DIM), F32), jnp.cos(ang)], axis=0)
    sin = jnp.concatenate([jnp.zeros((ctx, HEAD_DIM), F32), jnp.sin(ang)], axis=0)
    sign = jnp.where(jnp.arange(HEAD_DIM) < HEAD_DIM // 2, -1.0, 1.0).astype(F32)
    two = lambda t: jnp.concatenate([t, t], axis=1)
    return two(cos), two(sin * sign)


def kernel(x, c, ctx, c_ctx, w_ada, b_ada, norm_g, w_in, qk_gain, ssm_lam_re, ssm_lam_im, ssm_log_dt, ssm_b_re, ssm_b_im, ssm_c_re, ssm_c_im, ssm_d, w_glu, b_glu, diff_lam, diff_norm_g, w_branch, w_out, w_up, conv_w, conv_b, w_down):
    batch, seq, _ = x.shape
    n_ctx = ctx.shape[1]
    depth = w_in.shape[0]
    assert n_ctx == TM and seq % TM == 0 and 2 * batch == SUBLANES
    s_len = n_ctx + seq
    nblk = s_len // TM
    assert s_len % TC == 0 and D_FF % FF_CHUNK == 0
    n_chunks = D_FF // FF_CHUNK

    cos_t, sin_t = _rope_tables(seq, n_ctx)
    cc = jnp.concatenate([c, c_ctx[None, :], jnp.zeros((SUBLANES - batch - 1, D_MODEL), F32)], axis=0)
    mods_all = _ada_mods(cc, w_ada, b_ada).reshape(depth, SUBLANES, 6, D_MODEL)

    xa = x.reshape(batch * seq, D_MODEL)
    xc = ctx.reshape(batch * n_ctx, D_MODEL)
    for i in range(depth):
        lam_init = 0.8 - 0.6 * math.exp(-0.3 * i)
        mods = mods_all[i]
        g1 = norm_g[i, 0:1]
        g2 = norm_g[i, 1:2]
        w_i = w_in[i].astype(BF16)
        qa, kta, va, u, qc, ktc, vc = _inproj(xa, xc, mods, g1, w_i,
                                              jnp.tile(qk_gain[i], (1, BRANCH_WIDTH // HEAD_DIM)),
                                              cos_t, sin_t, batch, nblk)
        ya = _attention(qa, kta, va, batch, nblk, diff=False)
        yd = _attention(qc, ktc, vc, batch, nblk, diff=True, lam_vecs=diff_lam[i],
                        g_out=diff_norm_g[i][None, :], lam_init=lam_init)

        wd, a8, cm = _s5_weights(ssm_lam_re[i], ssm_lam_im[i], ssm_log_dt[i], ssm_b_re[i],
                                 ssm_b_im[i], ssm_c_re[i], ssm_c_im[i], batch)
        yf, yr = _s5_scan(u.reshape(batch, s_len, BRANCH_WIDTH), wd, a8, cm, n_ctx)
        yf = yf.reshape(batch * s_len, BRANCH_WIDTH)
        yr = yr.reshape(batch * s_len, BRANCH_WIDTH)

        x1, h2 = _merge(xa, xc, mods, g1, g2, ya, yd, yf, yr, u, ssm_d[i][None, :],
                        w_glu[i].astype(BF16), b_glu[i][None, :], w_i[:, N_QKVU:],
                        w_branch[i].astype(BF16), w_out[i].astype(BF16), batch, nblk)

        def chunked(t):
            a = t[:, :D_FF].reshape(t.shape[0], n_chunks, FF_CHUNK)
            g = t[:, D_FF:].reshape(t.shape[0], n_chunks, FF_CHUNK)
            return jnp.concatenate([a, g], axis=-1).transpose(1, 0, 2)

        wup = _chunk_major_bf16(w_up[i], n_chunks)
        cw = chunked(jnp.concatenate(
            [conv_w[i], conv_b[i][None, :], jnp.zeros((SUBLANES - 4, 2 * D_FF), F32)], axis=0))
        wdn = w_down[i].astype(BF16)
        xa = _ffn(h2, x1, mods, wup, cw, wdn, batch, nblk, latent_only=(i == depth - 1))
        xc = None

    return xa.reshape(batch, seq, D_MODEL)
```

```python
import functools
import math

import jax
import jax.numpy as jnp
import numpy as np
from jax import lax
from jax.experimental import pallas as pl
from jax.experimental.pallas import tpu as pltpu

F32 = jnp.float32
BF16 = jnp.bfloat16

D_MODEL = 1024
HEAD_DIM = 64
GRID_W = 64
ROPE_BASE = 10000.0
EPS = 1e-6
BRANCH_WIDTH = D_MODEL // 2
A_HEADS = BRANCH_WIDTH // HEAD_DIM
A_KV_HEADS = A_HEADS // 4
C_HEADS = BRANCH_WIDTH // (2 * HEAD_DIM)
S5_CH = 16
S5_STATE = 64
S5_GROUPS = BRANCH_WIDTH // S5_CH
D_FF = 2816
IN_SIZES = (512, 128, 128, 512, 512, 512, 512, 3 * D_MODEL)
IN_OFFS = tuple(int(v) for v in np.cumsum((0,) + IN_SIZES))
N_QKVU = IN_OFFS[7]

LANES = 128
SUBLANES = 8
TM = 256
SCORE_KEYS = 2048
TC = 64
FF_CHUNK = 128
HALO = 16
ADA_COLS = 1536
V7X_VMEM_BYTES = 64 * 1024 * 1024
VMEM_LIMIT = V7X_VMEM_BYTES * 7 // 8
SCORE_SCALE = HEAD_DIM ** -0.5 * math.log2(math.e)


def _cparams(n_axes):
    return pltpu.CompilerParams(dimension_semantics=("arbitrary",) * n_axes,
                                vmem_limit_bytes=VMEM_LIMIT)


def _rms_mod(x, gain, shift, scale):
    y = x * lax.rsqrt(jnp.mean(x * x, axis=-1, keepdims=True) + EPS)
    return (y * gain) * (1.0 + scale) + shift


def _head_avg_matrix(width):
    shift = HEAD_DIM.bit_length() - 1
    r = lax.broadcasted_iota(jnp.int32, (width, width), 0) >> shift
    c = lax.broadcasted_iota(jnp.int32, (width, width), 1) >> shift
    return jnp.where(r == c, 1.0 / HEAD_DIM, 0.0).astype(BF16)


def _head_rms(z, gain, avg):
    sq = z * z
    hi = sq.astype(BF16)
    lo = (sq - hi.astype(F32)).astype(BF16)
    ms = (jnp.dot(hi, avg, preferred_element_type=F32)
          + jnp.dot(lo, avg, preferred_element_type=F32))
    return z * lax.rsqrt(ms + EPS) * gain


def _tile_lanes(t, width):
    reps = width // t.shape[1]
    return t if reps == 1 else jnp.concatenate([t] * reps, axis=1)


def _rope(z, cos, sin_signed):
    width = z.shape[1]
    lane = lax.broadcasted_iota(jnp.int32, z.shape, 1)
    first_half = (lane & (HEAD_DIM - 1)) < HEAD_DIM // 2
    rot = jnp.where(first_half,
                    pltpu.roll(z, width - HEAD_DIM // 2, 1),
                    pltpu.roll(z, HEAD_DIM // 2, 1))
    return z * _tile_lanes(cos, width) + rot * _tile_lanes(sin_signed, width)


def _dup_halves(z):
    lane = lax.broadcasted_iota(jnp.int32, z.shape, 1)
    low = lane < HEAD_DIM
    sw = pltpu.roll(z, HEAD_DIM, 1)
    return jnp.where(low, z, sw), jnp.where(low, sw, z)


def _ada_kernel(c_ref, w_ref, b_ref, o_ref):
    a = jax.nn.silu(c_ref[...])
    o_ref[0] = jnp.dot(a, w_ref[0], preferred_element_type=F32,
                       precision=lax.Precision.HIGHEST) + b_ref[0]


def _ada_mods(cc, w_ada, b_ada):
    depth, _, n = w_ada.shape
    tn = ADA_COLS
    assert n % tn == 0
    return pl.pallas_call(
        _ada_kernel,
        grid=(depth, n // tn),
        in_specs=[pl.BlockSpec((SUBLANES, D_MODEL), lambda l, j: (0, 0)),
                  pl.BlockSpec((1, D_MODEL, tn), lambda l, j: (l, 0, j)),
                  pl.BlockSpec((1, 1, tn), lambda l, j: (l, 0, j))],
        out_specs=pl.BlockSpec((1, SUBLANES, tn), lambda l, j: (l, 0, j)),
        out_shape=jax.ShapeDtypeStruct((depth, SUBLANES, n), F32),
        compiler_params=_cparams(2),
        name="ada_mods",
    )(cc, w_ada, b_ada.reshape(depth, 1, n))


def _stream_specs(x_lat, x_ctx, nblk):
    blk = (TM, D_MODEL)
    if x_ctx is None:
        return ([pl.BlockSpec(blk, lambda i: (i, 0)),
                 pl.BlockSpec(blk, lambda i: ((i // nblk) * nblk, 0))], [x_lat, x_lat])
    lat_map = lambda i: ((i // nblk) * (nblk - 1) + jnp.maximum(i % nblk - 1, 0), 0)
    return ([pl.BlockSpec(blk, lat_map), pl.BlockSpec(blk, lambda i: (i // nblk, 0))],
            [x_lat, x_ctx])


def _stream_tile(x_ref, xc_ref, nblk):
    return jnp.where(pl.program_id(0) % nblk == 0, xc_ref[...], x_ref[...])


def _inproj_kernel(x_ref, xc_ref, mod_ref, g_ref, w_ref, qk_ref, cos_ref, sin_ref,
                   qa_ref, kta_ref, va_ref, u_ref, qc_ref, ktc_ref, vc_ref, *, nblk):
    mods = mod_ref[0]
    x = _stream_tile(x_ref, xc_ref, nblk)
    h = _rms_mod(x, g_ref[...], mods[0:1], mods[1:2]).astype(BF16)
    cos = cos_ref[...]
    sin = sin_ref[...]
    avg = _head_avg_matrix(BRANCH_WIDTH)

    def seg(k):
        return jnp.dot(h, w_ref[:, IN_OFFS[k]:IN_OFFS[k + 1]], preferred_element_type=F32)

    def gain(k, width=BRANCH_WIDTH):
        return qk_ref[k:k + 1, :width]

    k = _rope(_head_rms(seg(5), gain(3), avg), cos, sin)
    for hh in range(C_HEADS):
        ktc_ref[0, hh] = k[:, hh * LANES:(hh + 1) * LANES].T.astype(BF16)
    k = _rope(_head_rms(seg(1), gain(1, LANES), avg[:LANES, :LANES]), cos, sin)
    for hh, kd in enumerate(_dup_halves(k)):
        kta_ref[0, hh] = kd.T.astype(BF16)
    q = _rope(_head_rms(seg(4), gain(2), avg), cos, sin)
    qc_ref[...] = (q * SCORE_SCALE).astype(BF16)
    q = _rope(_head_rms(seg(0), gain(0), avg), cos, sin)
    qa_ref[...] = (q * SCORE_SCALE).astype(BF16)
    for hh, vd in enumerate(_dup_halves(seg(2))):
        va_ref[0, hh] = vd.astype(BF16)
    v = seg(6)
    for hh in range(C_HEADS):
        vc_ref[0, hh] = v[:, hh * LANES:(hh + 1) * LANES].astype(BF16)
    u_ref[...] = seg(3)


def _inproj(x_lat, x_ctx, mods, norm_g, w_in, qk_gain, cos_t, sin_t, batch, nblk):
    s_len = nblk * TM
    t_rows = batch * s_len
    row = lambda i: (i, 0)
    mod_map = lambda i: (jnp.where(i % nblk == 0, batch, i // nblk), 0, 0)
    pos = lambda i: (i % nblk, 0)
    kt_map = lambda i: (i // nblk, 0, 0, i % nblk)
    v_map = lambda i: (i // nblk, 0, i % nblk, 0)
    x_specs, x_args = _stream_specs(x_lat, x_ctx, nblk)
    return pl.pallas_call(
        functools.partial(_inproj_kernel, nblk=nblk),
        grid=(t_rows // TM,),
        in_specs=x_specs + [
                  pl.BlockSpec((1, 6, D_MODEL), mod_map),
                  pl.BlockSpec((1, D_MODEL), lambda i: (0, 0)),
                  pl.BlockSpec((D_MODEL, N_QKVU), lambda i: (0, 0)),
                  pl.BlockSpec((4, BRANCH_WIDTH), lambda i: (0, 0)),
                  pl.BlockSpec((TM, LANES), pos),
                  pl.BlockSpec((TM, LANES), pos)],
        out_specs=[pl.BlockSpec((TM, BRANCH_WIDTH), row),
                   pl.BlockSpec((1, A_KV_HEADS, LANES, TM), kt_map),
                   pl.BlockSpec((1, A_KV_HEADS, TM, LANES), v_map),
                   pl.BlockSpec((TM, BRANCH_WIDTH), row),
                   pl.BlockSpec((TM, BRANCH_WIDTH), row),
                   pl.BlockSpec((1, C_HEADS, LANES, TM), kt_map),
                   pl.BlockSpec((1, C_HEADS, TM, LANES), v_map)],
        out_shape=[jax.ShapeDtypeStruct((t_rows, BRANCH_WIDTH), BF16),
                   jax.ShapeDtypeStruct((batch, A_KV_HEADS, LANES, s_len), BF16),
                   jax.ShapeDtypeStruct((batch, A_KV_HEADS, s_len, LANES), BF16),
                   jax.ShapeDtypeStruct((t_rows, BRANCH_WIDTH), F32),
                   jax.ShapeDtypeStruct((t_rows, BRANCH_WIDTH), BF16),
                   jax.ShapeDtypeStruct((batch, C_HEADS, LANES, s_len), BF16),
                   jax.ShapeDtypeStruct((batch, C_HEADS, s_len, LANES), BF16)],
        compiler_params=_cparams(1),
        name="in_proj",
    )(*x_args, mods, norm_g, w_in, qk_gain, cos_t, sin_t)


def _attn_kernel(*refs, diff, lam_init, n_ctx, bounds):
    if diff:
        q_ref, kt_ref, v_ref, lam_ref, gout_ref, o_ref = refs[:6]
    else:
        q_ref, kt_ref, v_ref, o_ref = refs[:4]
    lhs_sc, m_sc, l_sc, acc_sc, s0_sc, s1_sc = refs[-6:]
    bufs = (s0_sc, s1_sc)
    n_chunks = len(bounds) - 1
    j = pl.program_id(2)
    rows = 2 * TM
    lane = lax.broadcasted_iota(jnp.int32, (TM, LANES), 1)
    low = lane < HEAD_DIM
    qt = q_ref[...].astype(F32)
    lhs_sc[0:TM, :] = jnp.where(low, qt, 0.0).astype(BF16)
    lhs_sc[TM:rows, :] = jnp.where(low, 0.0, qt).astype(BF16)

    def chunk_keys(c):
        return slice(bounds[c], bounds[c + 1])

    def scores(keys):
        return jnp.dot(lhs_sc[...], kt_ref[0, 0, :, keys], preferred_element_type=F32)

    def softmax_pv(s, keys, first):
        width = s.shape[1]
        mx = s[:, 0:LANES]
        for t in range(1, width // LANES):
            mx = jnp.maximum(mx, s[:, t * LANES:(t + 1) * LANES])
        m_cur = jnp.max(mx, axis=1, keepdims=True)
        if first:
            m_next = jnp.broadcast_to(m_cur, (rows, LANES))
        else:
            m_prev = m_sc[...]
            m_next = jnp.maximum(m_prev, m_cur)
            alpha = jnp.exp2(m_prev - m_next)
        p = jnp.exp2(s - _tile_lanes(m_next, width))
        psum = p[:, 0:LANES]
        for t in range(1, width // LANES):
            psum = psum + p[:, t * LANES:(t + 1) * LANES]
        pv = jnp.dot(p.astype(BF16), v_ref[0, 0, keys, :], preferred_element_type=F32)
        if first:
            l_sc[...] = psum
            acc_sc[...] = pv
        else:
            l_sc[...] = alpha * l_sc[...] + psum
            acc_sc[...] = alpha * acc_sc[...] + pv
        m_sc[...] = m_next

    @pl.when(j == 0)
    def _():
        ctx_keys = slice(0, n_ctx)
        softmax_pv(scores(ctx_keys), ctx_keys, True)

    @pl.when(j > 0)
    def _():
        def put(c):
            keys = chunk_keys(c)
            bufs[c % 2][:, 0:keys.stop - keys.start] = scores(keys)

        def take(c):
            keys = chunk_keys(c)
            softmax_pv(bufs[c % 2][:, 0:keys.stop - keys.start], keys, c == 0)

        put(0)
        for c in range(n_chunks):
            if c + 1 < n_chunks:
                put(c + 1)
            take(c)

    o = acc_sc[...] / jnp.sum(l_sc[...], axis=1, keepdims=True)
    if diff:
        lv = lam_ref[...]
        lam = (jnp.exp(jnp.sum(lv[0:1] * lv[1:2], axis=1, keepdims=True))
               - jnp.exp(jnp.sum(lv[2:3] * lv[3:4], axis=1, keepdims=True)) + lam_init)
        od = o[0:TM] - lam * o[TM:rows]
        od = od * lax.rsqrt(jnp.mean(od * od, axis=-1, keepdims=True) + EPS)
        o_ref[...] = ((od * gout_ref[...]) * (1.0 - lam_init)).astype(o_ref.dtype)
    else:
        o_ref[...] = jnp.where(low, o[0:TM], o[TM:rows]).astype(o_ref.dtype)


def _attention(q, kt, v, batch, nblk, *, diff, lam_vecs=None, g_out=None, lam_init=0.0):
    kv_heads = kt.shape[1]
    s_len = kt.shape[3]
    tiles = BRANCH_WIDTH // LANES
    rows = 2 * TM
    q_map = lambda b, h, j: (b * nblk + j, h)
    kv_map = lambda b, h, j: (b, h * kv_heads // tiles, 0, 0)
    const = lambda b, h, j: (0, 0)
    in_specs = [pl.BlockSpec((TM, LANES), q_map),
                pl.BlockSpec((1, 1, LANES, s_len), kv_map),
                pl.BlockSpec((1, 1, s_len, LANES), kv_map)]
    args = [q, kt, v]
    if diff:
        in_specs += [pl.BlockSpec((4, HEAD_DIM), const), pl.BlockSpec((1, LANES), const)]
        args += [lam_vecs, g_out]
    n_ctx = TM
    tkl = min(SCORE_KEYS, s_len - n_ctx)
    assert (s_len - n_ctx) % tkl == 0
    bounds = [0] + list(range(n_ctx + tkl, s_len + 1, tkl))
    kern = functools.partial(_attn_kernel, diff=diff, lam_init=lam_init,
                             n_ctx=n_ctx, bounds=tuple(bounds))
    return pl.pallas_call(
        kern,
        grid=(batch, tiles, nblk),
        in_specs=in_specs,
        out_specs=pl.BlockSpec((TM, LANES), q_map),
        out_shape=jax.ShapeDtypeStruct((batch * s_len, BRANCH_WIDTH), BF16),
        scratch_shapes=([pltpu.VMEM((rows, LANES), BF16)]
                        + [pltpu.VMEM((rows, LANES), F32)] * 3
                        + [pltpu.VMEM((rows, n_ctx + tkl), F32)] * 2),
        compiler_params=_cparams(3),
        name="diff_attn" if diff else "gqa_attn",
    )(*args)


def _s5_kernel(uf_ref, ub_ref, pin_ref, pout_ref, wd_ref, a_ref, cm_ref, yf_ref, yb_ref,
               bu_sc, st_sc):
    n_tiles = BRANCH_WIDTH // LANES
    half = SUBLANES * S5_STATE
    rows = TC * SUBLANES
    tok = (SUBLANES // 2) * TC

    @pl.when(pl.program_id(0) == 0)
    def _():
        st_sc[...] = jnp.zeros_like(st_sc)

    seq = lax.broadcasted_iota(jnp.int32, (rows, LANES), 0) & (SUBLANES - 1)
    fwd = seq < SUBLANES // 2
    lhs_dir = []
    for d, ref in enumerate((uf_ref, ub_ref)):
        ud = ref[...].reshape(tok, BRANCH_WIDTH).astype(BF16)
        lhs_dir.append(jnp.dot(pin_ref[d], ud, preferred_element_type=F32).astype(BF16))
    def drive(j):
        cs = slice(j * LANES, (j + 1) * LANES)
        lhs = jnp.concatenate([lhs_dir[0][:, cs], lhs_dir[1][:, cs]], axis=1)
        bu_sc[:, 2 * half * j:2 * half * (j + 1)] = jnp.dot(
            lhs, wd_ref[j], preferred_element_type=F32)

    def scan(j):
        re = slice(2 * half * j, 2 * half * j + half)
        im = slice(2 * half * j + half, 2 * half * (j + 1))
        ar = a_ref[0, :, half * j:half * (j + 1)]
        ai = a_ref[1, :, half * j:half * (j + 1)]

        def step(t, carry):
            xr, xi = carry
            r = pl.multiple_of(t * SUBLANES, SUBLANES)
            nxr = ar * xr - ai * xi + bu_sc[pl.ds(r, SUBLANES), re]
            nxi = ar * xi + ai * xr + bu_sc[pl.ds(r, SUBLANES), im]
            bu_sc[pl.ds(r, SUBLANES), re] = nxr
            bu_sc[pl.ds(r, SUBLANES), im] = nxi
            return nxr, nxi

        xr, xi = lax.fori_loop(0, TC, step, (st_sc[:, re], st_sc[:, im]), unroll=True)
        st_sc[:, re] = xr
        st_sc[:, im] = xi

    def readout(j):
        x = bu_sc[:, 2 * half * j:2 * half * (j + 1)].astype(BF16)
        yy = jnp.dot(x, cm_ref[j], preferred_element_type=F32)
        return jnp.where(fwd, yy[:, :LANES], yy[:, LANES:])

    ys = []
    drive(0)
    for j in range(n_tiles):
        if j + 1 < n_tiles:
            drive(j + 1)
        scan(j)
        ys.append(readout(j))
    y = jnp.concatenate(ys, axis=1)
    parts = []
    rest = y
    for _ in range(2):
        part = rest.astype(BF16)
        parts.append(part)
        rest = rest - part.astype(F32)
    stacked = jnp.concatenate(parts, axis=0)
    for d, ref in enumerate((yf_ref, yb_ref)):
        out = jnp.dot(pout_ref[d], stacked, preferred_element_type=F32)
        ref[...] = out.reshape(ref.shape)


def _scan_row_placement(n_seq):
    p = np.zeros((2, TC * 2 * n_seq, n_seq * TC), np.float32)
    for b in range(n_seq):
        for k in range(TC):
            p[0, 2 * n_seq * k + b, b * TC + k] = 1.0
            p[1, 2 * n_seq * (TC - 1 - k) + n_seq + b, b * TC + k] = 1.0
    pt = p.transpose(0, 2, 1)
    return jnp.asarray(p, BF16), jnp.asarray(np.concatenate([pt, pt], axis=2), BF16)


def _s5_scan(u3, wd, a8, cm, n_ctx):
    batch, s_len, _ = u3.shape
    rows = TC * SUBLANES
    n_tiles = BRANCH_WIDTH // LANES
    n_state = 2 * SUBLANES * S5_STATE * n_tiles
    n_steps = s_len // TC
    ctx_steps = n_ctx // TC
    p_in, p_out = _scan_row_placement(batch)
    fwd_map = lambda g: (0, g, 0)
    bwd_map = lambda g: (0, jnp.where(g < ctx_steps, ctx_steps - 1 - g,
                                      n_steps + ctx_steps - 1 - g), 0)
    c3 = lambda g: (0, 0, 0)
    blk = (batch, TC, BRANCH_WIDTH)
    out = jax.ShapeDtypeStruct(u3.shape, F32)
    return pl.pallas_call(
        _s5_kernel,
        grid=(n_steps,),
        in_specs=[pl.BlockSpec(blk, fwd_map),
                  pl.BlockSpec(blk, bwd_map),
                  pl.BlockSpec(p_in.shape, c3),
                  pl.BlockSpec(p_out.shape, c3),
                  pl.BlockSpec(wd.shape, c3),
                  pl.BlockSpec(a8.shape, c3),
                  pl.BlockSpec(cm.shape, c3)],
        out_specs=[pl.BlockSpec(blk, fwd_map), pl.BlockSpec(blk, bwd_map)],
        out_shape=[out, out],
        scratch_shapes=[pltpu.VMEM((rows, n_state), F32),
                        pltpu.VMEM((SUBLANES, n_state), F32)],
        compiler_params=_cparams(1),
        name="s5_scan",
    )(u3, u3, p_in, p_out, wd, a8, cm)


def _zoh(lam_re, lam_im, log_dt, b_re, b_im):
    dt = jnp.exp(log_dt)[..., None]
    mag = jnp.exp(lam_re * dt)
    a_re = mag * jnp.cos(lam_im * dt)
    a_im = mag * jnp.sin(lam_im * dt)
    den = lam_re * lam_re + lam_im * lam_im
    f_re = ((a_re - 1.0) * lam_re + a_im * lam_im) / den
    f_im = (a_im * lam_re - (a_re - 1.0) * lam_im) / den
    bb_re = f_re[..., None] * b_re - f_im[..., None] * b_im
    bb_im = f_re[..., None] * b_im + f_im[..., None] * b_re
    return a_re, a_im, bb_re, bb_im


def _s5_weights(lam_re, lam_im, log_dt, b_re, b_im, c_re, c_im, batch):
    n_tiles = BRANCH_WIDTH // LANES
    gpt = S5_GROUPS // n_tiles
    a_re, a_im, bb_re, bb_im = _zoh(lam_re, lam_im, log_dt, b_re, b_im)
    eye = jnp.eye(gpt, dtype=F32)

    def drive(bb):
        t = bb.reshape(2, n_tiles, gpt, S5_STATE, S5_CH)
        w = jnp.einsum('djgpc,gh->jdgchp', t, eye)
        return w.reshape(n_tiles, 2 * gpt * S5_CH, gpt * S5_STATE)

    def read(cc):
        t = cc.reshape(2, n_tiles, gpt, S5_CH, S5_STATE)
        w = jnp.einsum('djgcp,gh->jgpdhc', t, eye)
        return w.reshape(n_tiles, gpt * S5_STATE, 2 * gpt * S5_CH)

    wd = jnp.concatenate([drive(bb_re), drive(bb_im)], axis=2).astype(BF16)
    cm = jnp.concatenate([read(c_re), read(-c_im)], axis=1).astype(BF16)

    def per_seq(a):
        return jnp.repeat(a.reshape(2, 1, S5_GROUPS * S5_STATE), batch, axis=1).reshape(
            2 * batch, S5_GROUPS * S5_STATE)

    a8 = jnp.stack([per_seq(a_re), per_seq(a_im)])
    return wd, a8, cm


def _merge_kernel(x_ref, xc_ref, mod_ref, g1_ref, g2_ref, ya_ref, yd_ref, yf_ref, yr_ref, u_ref,
                  dsk_ref, wglu_ref, bglu_ref, wgate_ref, wbr_ref, wout_ref, x1_ref, h2_ref,
                  *, nblk):
    mods = mod_ref[0]
    proj = {k: jnp.dot(ref[...], wbr_ref[k], preferred_element_type=F32)
            for k, ref in ((0, ya_ref), (2, yd_ref))}
    x = _stream_tile(x_ref, xc_ref, nblk)
    h = _rms_mod(x, g1_ref[...], mods[0:1], mods[1:2]).astype(BF16)
    ys = yf_ref[...] + yr_ref[...] + dsk_ref[...] * u_ref[...]
    g = jax.nn.gelu(ys)
    yb = g * jax.nn.sigmoid(
        jnp.dot(g.astype(BF16), wglu_ref[...], preferred_element_type=F32) + bglu_ref[...])
    proj[1] = jnp.dot(yb.astype(BF16), wbr_ref[1], preferred_element_type=F32)
    m = None
    for k in range(3):
        gate = jax.nn.sigmoid(jnp.dot(h, wgate_ref[:, k * D_MODEL:(k + 1) * D_MODEL],
                                      preferred_element_type=F32))
        term = gate * proj[k]
        m = term if m is None else m + term
    y = jnp.dot(m.astype(BF16), wout_ref[...], preferred_element_type=F32)
    x1 = x + mods[2:3] * y
    x1_ref[...] = x1
    h2_ref[...] = _rms_mod(x1, g2_ref[...], mods[3:4], mods[4:5]).astype(BF16)


def _merge(x_lat, x_ctx, mods, g1, g2, ya, yd, yf, yr, u, d_skip, w_glu, b_glu, w_gate,
           w_branch, w_out, batch, nblk):
    t_rows = ya.shape[0]
    row = lambda i: (i, 0)
    mod_map = lambda i: (jnp.where(i % nblk == 0, batch, i // nblk), 0, 0)
    c2 = lambda i: (0, 0)
    c3 = lambda i: (0, 0, 0)
    x_specs, x_args = _stream_specs(x_lat, x_ctx, nblk)
    return pl.pallas_call(
        functools.partial(_merge_kernel, nblk=nblk),
        grid=(t_rows // TM,),
        in_specs=x_specs + [
                  pl.BlockSpec((1, 6, D_MODEL), mod_map),
                  pl.BlockSpec((1, D_MODEL), c2),
                  pl.BlockSpec((1, D_MODEL), c2),
                  pl.BlockSpec((TM, BRANCH_WIDTH), row),
                  pl.BlockSpec((TM, BRANCH_WIDTH), row),
                  pl.BlockSpec((TM, BRANCH_WIDTH), row),
                  pl.BlockSpec((TM, BRANCH_WIDTH), row),
                  pl.BlockSpec((TM, BRANCH_WIDTH), row),
                  pl.BlockSpec((1, BRANCH_WIDTH), c2),
                  pl.BlockSpec((BRANCH_WIDTH, BRANCH_WIDTH), c2),
                  pl.BlockSpec((1, BRANCH_WIDTH), c2),
                  pl.BlockSpec((D_MODEL, 3 * D_MODEL), c2),
                  pl.BlockSpec((3, BRANCH_WIDTH, D_MODEL), c3),
                  pl.BlockSpec((D_MODEL, D_MODEL), c2)],
        out_specs=[pl.BlockSpec((TM, D_MODEL), row), pl.BlockSpec((TM, D_MODEL), row)],
        out_shape=[jax.ShapeDtypeStruct((t_rows, D_MODEL), F32),
                   jax.ShapeDtypeStruct((t_rows, D_MODEL), BF16)],
        compiler_params=_cparams(1),
        name="merge",
    )(*x_args, mods, g1, g2, ya, yd, yf, yr, u, d_skip, w_glu, b_glu, w_gate, w_branch, w_out)


def _ffn_kernel(h_ref, hp_ref, hn_ref, x1_ref, mod_ref, perm_ref, wup_ref, cw_ref, wdn_ref, o_ref,
                lhs_sc, u0_sc, u1_sc, act_sc, *, nblk, n_chunks):
    pj = pl.program_id(0) % nblk
    left_ok = (pj >= 2).astype(F32)
    right_ok = jnp.logical_and(pj != 0, pj != nblk - 1).astype(F32)
    lhs_sc[0:TM, :] = jnp.dot(perm_ref[0], h_ref[...], preferred_element_type=F32).astype(BF16)
    hrow = lax.broadcasted_iota(jnp.int32, (HALO, D_MODEL), 0)
    halo = (jnp.where(hrow == HALO - 1, hp_ref[...].astype(F32) * left_ok, 0.0)
            + jnp.where(hrow == 0, hn_ref[...].astype(F32) * right_ok, 0.0))
    lhs_sc[TM:TM + HALO, :] = halo.astype(BF16)
    sub = lax.broadcasted_iota(jnp.int32, (SUBLANES, 2 * FF_CHUNK), 0)

    def up(k, buf):
        buf[...] = jnp.dot(lhs_sc[...], wup_ref[k], preferred_element_type=F32)

    def activate(k, buf):
        cw = cw_ref[k]
        before = buf[TM + HALO - 1:TM + HALO, :]
        after = buf[TM:TM + 1, :]
        first = jnp.where(sub == 0, before, pltpu.roll(buf[TM - SUBLANES:TM, :], 1, 0))
        last = jnp.where(sub == SUBLANES - 1, after,
                         pltpu.roll(buf[0:SUBLANES, :], SUBLANES - 1, 0))
        prev = jnp.concatenate([first, buf[0:TM - SUBLANES, :]], axis=0)
        nxt = jnp.concatenate([buf[SUBLANES:TM, :], last], axis=0)
        c = prev * cw[0:1] + buf[0:TM, :] * cw[1:2] + nxt * cw[2:3] + cw[3:4]
        act = jax.nn.silu(c[:, FF_CHUNK:]) * c[:, :FF_CHUNK]
        act_sc[:, k * FF_CHUNK:(k + 1) * FF_CHUNK] = act.astype(BF16)

    bufs = (u0_sc, u1_sc)
    up(0, bufs[0])
    for k in range(n_chunks):
        if k + 1 < n_chunks:
            up(k + 1, bufs[(k + 1) % 2])
        activate(k, bufs[k % 2])
    act = jnp.dot(perm_ref[1], act_sc[...], preferred_element_type=F32).astype(BF16)
    y = jnp.dot(act, wdn_ref[...], preferred_element_type=F32)
    o_ref[...] = x1_ref[...] + mod_ref[0][5:6] * y


def _strided_row_order():
    p = np.zeros((TM, TM), np.float32)
    groups = TM // SUBLANES
    for r in range(groups):
        for s in range(SUBLANES):
            p[SUBLANES * r + s, groups * s + r] = 1.0
    return jnp.asarray(np.stack([p, p.T]), BF16)


def _ffn(h2, x1, mods, wup, cw, wdn, batch, nblk, latent_only):
    t_rows = x1.shape[0]
    n_chunks = wup.shape[0]
    per = TM // HALO
    last = t_rows // HALO - 1
    row = lambda i: (i, 0)
    mod_map = lambda i: (jnp.where(i % nblk == 0, batch, i // nblk), 0, 0)
    kern = functools.partial(_ffn_kernel, nblk=nblk, n_chunks=n_chunks)
    if latent_only:
        out_map = lambda i: ((i // nblk) * (nblk - 1) + jnp.maximum(i % nblk - 1, 0), 0)
        out_rows = t_rows - batch * TM
    else:
        out_map, out_rows = row, t_rows
    return pl.pallas_call(
        kern,
        grid=(t_rows // TM,),
        in_specs=[pl.BlockSpec((TM, D_MODEL), row),
                  pl.BlockSpec((HALO, D_MODEL), lambda i: (jnp.maximum(i * per - 1, 0), 0)),
                  pl.BlockSpec((HALO, D_MODEL), lambda i: (jnp.minimum((i + 1) * per, last), 0)),
                  pl.BlockSpec((TM, D_MODEL), row),
                  pl.BlockSpec((1, 6, D_MODEL), mod_map),
                  pl.BlockSpec((2, TM, TM), lambda i: (0, 0, 0)),
                  pl.BlockSpec(wup.shape, lambda i: (0, 0, 0)),
                  pl.BlockSpec(cw.shape, lambda i: (0, 0, 0)),
                  pl.BlockSpec(wdn.shape, lambda i: (0, 0))],
        out_specs=pl.BlockSpec((TM, D_MODEL), out_map),
        out_shape=jax.ShapeDtypeStruct((out_rows, D_MODEL), F32),
        scratch_shapes=([pltpu.VMEM((TM + HALO, D_MODEL), BF16)]
                        + [pltpu.VMEM((TM + HALO, 2 * FF_CHUNK), F32)] * 2
                        + [pltpu.VMEM((TM, D_FF), BF16)]),
        compiler_params=_cparams(1),
        name="conv_ffn",
    )(h2, h2, h2, x1, mods, _strided_row_order(), wup, cw, wdn)


def _relayout_kernel(a_ref, g_ref, o_ref):
    o_ref[0] = jnp.concatenate([a_ref[0], g_ref[0]], axis=1).astype(o_ref.dtype)


def _chunk_major_bf16(w_up, layer, n_chunks):
    rows = w_up.shape[1]
    return pl.pallas_call(
        _relayout_kernel,
        grid=(n_chunks,),
        in_specs=[pl.BlockSpec((1, rows, FF_CHUNK), lambda k: (layer, 0, k)),
                  pl.BlockSpec((1, rows, FF_CHUNK), lambda k: (layer, 0, n_chunks + k))],
        out_specs=pl.BlockSpec((1, rows, 2 * FF_CHUNK), lambda k: (k, 0, 0)),
        out_shape=jax.ShapeDtypeStruct((n_chunks, rows, 2 * FF_CHUNK), BF16),
        compiler_params=_cparams(1),
        name="ffn_weight_layout",
    )(w_up, w_up)


def _cast_kernel(w_ref, o_ref):
    o_ref[...] = w_ref[0].astype(o_ref.dtype)


def _cast_cols_bf16(w, layer, first_col, n_cols):
    rows = w.shape[1]
    blk = 2 * LANES
    assert first_col % blk == 0 and n_cols % blk == 0
    return pl.pallas_call(
        _cast_kernel,
        grid=(n_cols // blk,),
        in_specs=[pl.BlockSpec((1, rows, blk), lambda k: (layer, 0, first_col // blk + k))],
        out_specs=pl.BlockSpec((rows, blk), lambda k: (0, k)),
        out_shape=jax.ShapeDtypeStruct((rows, n_cols), BF16),
        compiler_params=_cparams(1),
        name="weight_cast",
    )(w)


def _rope_tables(seq, ctx):
    rows = seq // GRID_W
    row = jnp.repeat(jnp.arange(rows, dtype=F32), GRID_W)
    col = jnp.tile(jnp.arange(GRID_W, dtype=F32), rows)
    n_freq = HEAD_DIM // 4
    inv_freq = ROPE_BASE ** (-jnp.arange(n_freq, dtype=F32) / n_freq)
    ang = jnp.concatenate([row[:, None] * inv_freq, col[:, None] * inv_freq], axis=-1)
    ang = jnp.concatenate([ang, ang], axis=-1)
    cos = jnp.concatenate([jnp.ones((ctx, HEAD_DIM), F32), jnp.cos(ang)], axis=0)
    sin = jnp.concatenate([jnp.zeros((ctx, HEAD_DIM), F32), jnp.sin(ang)], axis=0)
    sign = jnp.where(jnp.arange(HEAD_DIM) < HEAD_DIM // 2, -1.0, 1.0).astype(F32)
    two = lambda t: jnp.concatenate([t, t], axis=1)
    return two(cos), two(sin * sign)


def kernel(x, c, ctx, c_ctx, w_ada, b_ada, norm_g, w_in, qk_gain, ssm_lam_re, ssm_lam_im, ssm_log_dt, ssm_b_re, ssm_b_im, ssm_c_re, ssm_c_im, ssm_d, w_glu, b_glu, diff_lam, diff_norm_g, w_branch, w_out, w_up, conv_w, conv_b, w_down):
    batch, seq, _ = x.shape
    n_ctx = ctx.shape[1]
    depth = w_in.shape[0]
    assert n_ctx == TM and seq % TM == 0 and 2 * batch == SUBLANES
    s_len = n_ctx + seq
    nblk = s_len // TM
    assert s_len % TC == 0 and D_FF % FF_CHUNK == 0
    n_chunks = D_FF // FF_CHUNK

    cos_t, sin_t = _rope_tables(seq, n_ctx)
    cc = jnp.concatenate([c, c_ctx[None, :], jnp.zeros((SUBLANES - batch - 1, D_MODEL), F32)], axis=0)
    mods_all = _ada_mods(cc, w_ada, b_ada).reshape(depth, SUBLANES, 6, D_MODEL)

    xa = x.reshape(batch * seq, D_MODEL)
    xc = ctx.reshape(batch * n_ctx, D_MODEL)
    for i in range(depth):
        lam_init = 0.8 - 0.6 * math.exp(-0.3 * i)
        mods = mods_all[i]
        g1 = norm_g[i, 0:1]
        g2 = norm_g[i, 1:2]
        w_qkvu = _cast_cols_bf16(w_in, i, 0, N_QKVU)
        w_gate = _cast_cols_bf16(w_in, i, N_QKVU, 3 * D_MODEL)
        qa, kta, va, u, qc, ktc, vc = _inproj(xa, xc, mods, g1, w_qkvu,
                                              jnp.tile(qk_gain[i], (1, BRANCH_WIDTH // HEAD_DIM)),
                                              cos_t, sin_t, batch, nblk)
        ya = _attention(qa, kta, va, batch, nblk, diff=False)
        yd = _attention(qc, ktc, vc, batch, nblk, diff=True, lam_vecs=diff_lam[i],
                        g_out=diff_norm_g[i][None, :], lam_init=lam_init)

        wd, a8, cm = _s5_weights(ssm_lam_re[i], ssm_lam_im[i], ssm_log_dt[i], ssm_b_re[i],
                                 ssm_b_im[i], ssm_c_re[i], ssm_c_im[i], batch)
        yf, yr = _s5_scan(u.reshape(batch, s_len, BRANCH_WIDTH), wd, a8, cm, n_ctx)
        yf = yf.reshape(batch * s_len, BRANCH_WIDTH)
        yr = yr.reshape(batch * s_len, BRANCH_WIDTH)

        x1, h2 = _merge(xa, xc, mods, g1, g2, ya, yd, yf, yr, u, ssm_d[i][None, :],
                        w_glu[i].astype(BF16), b_glu[i][None, :], w_gate,
                        w_branch[i].astype(BF16), w_out[i].astype(BF16), batch, nblk)

        def chunked(t):
            a = t[:, :D_FF].reshape(t.shape[0], n_chunks, FF_CHUNK)
            g = t[:, D_FF:].reshape(t.shape[0], n_chunks, FF_CHUNK)
            return jnp.concatenate([a, g], axis=-1).transpose(1, 0, 2)

        wup = _chunk_major_bf16(w_up, i, n_chunks)
        cw = chunked(jnp.concatenate(
            [conv_w[i], conv_b[i][None, :], jnp.zeros((SUBLANES - 4, 2 * D_FF), F32)], axis=0))
        wdn = w_down[i].astype(BF16)
        xa = _ffn(h2, x1, mods, wup, cw, wdn, batch, nblk, latent_only=(i == depth - 1))
        xc = None

    return xa.reshape(batch, seq, D_MODEL)
```

```python
import functools
import math

import jax
import jax.numpy as jnp
import numpy as np
from jax import lax
from jax.experimental import pallas as pl
from jax.experimental.pallas import tpu as pltpu

F32 = jnp.float32
BF16 = jnp.bfloat16

D_MODEL = 1024
HEAD_DIM = 64
GRID_W = 64
ROPE_BASE = 10000.0
EPS = 1e-6
BRANCH_WIDTH = D_MODEL // 2
A_HEADS = BRANCH_WIDTH // HEAD_DIM
A_KV_HEADS = A_HEADS // 4
C_HEADS = BRANCH_WIDTH // (2 * HEAD_DIM)
S5_CH = 16
S5_STATE = 64
S5_GROUPS = BRANCH_WIDTH // S5_CH
D_FF = 2816
IN_SIZES = (512, 128, 128, 512, 512, 512, 512, 3 * D_MODEL)
IN_OFFS = tuple(int(v) for v in np.cumsum((0,) + IN_SIZES))
N_QKVU = IN_OFFS[7]

LANES = 128
SUBLANES = 8
TM = 256
SCORE_KEYS = 2048
TC = 64
FF_CHUNK = 128
HALO = 16
ADA_COLS = 1536
V7X_VMEM_BYTES = 64 * 1024 * 1024
VMEM_LIMIT = V7X_VMEM_BYTES * 7 // 8
SCORE_SCALE = HEAD_DIM ** -0.5 * math.log2(math.e)


def _cparams(n_axes):
    return pltpu.CompilerParams(dimension_semantics=("arbitrary",) * n_axes,
                                vmem_limit_bytes=VMEM_LIMIT)


def _rms_mod(x, gain, shift, scale):
    y = x * lax.rsqrt(jnp.mean(x * x, axis=-1, keepdims=True) + EPS)
    return (y * gain) * (1.0 + scale) + shift


def _head_avg_matrix(width):
    shift = HEAD_DIM.bit_length() - 1
    r = lax.broadcasted_iota(jnp.int32, (width, width), 0) >> shift
    c = lax.broadcasted_iota(jnp.int32, (width, width), 1) >> shift
    return jnp.where(r == c, 1.0 / HEAD_DIM, 0.0).astype(BF16)


def _head_rms(z, gain, avg):
    sq = z * z
    hi = sq.astype(BF16)
    lo = (sq - hi.astype(F32)).astype(BF16)
    ms = (jnp.dot(hi, avg, preferred_element_type=F32)
          + jnp.dot(lo, avg, preferred_element_type=F32))
    return z * lax.rsqrt(ms + EPS) * gain


def _tile_lanes(t, width):
    reps = width // t.shape[1]
    return t if reps == 1 else jnp.concatenate([t] * reps, axis=1)


def _rope(z, cos, sin_signed):
    width = z.shape[1]
    lane = lax.broadcasted_iota(jnp.int32, z.shape, 1)
    first_half = (lane & (HEAD_DIM - 1)) < HEAD_DIM // 2
    rot = jnp.where(first_half,
                    pltpu.roll(z, width - HEAD_DIM // 2, 1),
                    pltpu.roll(z, HEAD_DIM // 2, 1))
    return z * _tile_lanes(cos, width) + rot * _tile_lanes(sin_signed, width)


def _dup_halves(z):
    lane = lax.broadcasted_iota(jnp.int32, z.shape, 1)
    low = lane < HEAD_DIM
    sw = pltpu.roll(z, HEAD_DIM, 1)
    return jnp.where(low, z, sw), jnp.where(low, sw, z)


def _ada_kernel(c_ref, w_ref, b_ref, o_ref):
    a = jax.nn.silu(c_ref[...])
    o_ref[0] = jnp.dot(a, w_ref[0], preferred_element_type=F32,
                       precision=lax.Precision.HIGHEST) + b_ref[0]


def _ada_mods(cc, w_ada, b_ada):
    depth, _, n = w_ada.shape
    tn = ADA_COLS
    assert n % tn == 0
    return pl.pallas_call(
        _ada_kernel,
        grid=(depth, n // tn),
        in_specs=[pl.BlockSpec((SUBLANES, D_MODEL), lambda l, j: (0, 0)),
                  pl.BlockSpec((1, D_MODEL, tn), lambda l, j: (l, 0, j)),
                  pl.BlockSpec((1, 1, tn), lambda l, j: (l, 0, j))],
        out_specs=pl.BlockSpec((1, SUBLANES, tn), lambda l, j: (l, 0, j)),
        out_shape=jax.ShapeDtypeStruct((depth, SUBLANES, n), F32),
        compiler_params=_cparams(2),
        name="ada_mods",
    )(cc, w_ada, b_ada.reshape(depth, 1, n))


def _stream_specs(x_lat, x_ctx, nblk):
    blk = (TM, D_MODEL)
    if x_ctx is None:
        return ([pl.BlockSpec(blk, lambda i: (i, 0)),
                 pl.BlockSpec(blk, lambda i: ((i // nblk) * nblk, 0))], [x_lat, x_lat])
    lat_map = lambda i: ((i // nblk) * (nblk - 1) + jnp.maximum(i % nblk - 1, 0), 0)
    return ([pl.BlockSpec(blk, lat_map), pl.BlockSpec(blk, lambda i: (i // nblk, 0))],
            [x_lat, x_ctx])


def _stream_tile(x_ref, xc_ref, nblk):
    return jnp.where(pl.program_id(0) % nblk == 0, xc_ref[...], x_ref[...])


def _inproj_kernel(x_ref, xc_ref, mod_ref, g_ref, w_ref, qk_ref, cos_ref, sin_ref,
                   qa_ref, kta_ref, va_ref, u_ref, qc_ref, ktc_ref, vc_ref, *, nblk):
    mods = mod_ref[0]
    x = _stream_tile(x_ref, xc_ref, nblk)
    h = _rms_mod(x, g_ref[...], mods[0:1], mods[1:2]).astype(BF16)
    cos = cos_ref[...]
    sin = sin_ref[...]
    avg = _head_avg_matrix(BRANCH_WIDTH)

    def seg(k):
        return jnp.dot(h, w_ref[:, IN_OFFS[k]:IN_OFFS[k + 1]], preferred_element_type=F32)

    def gain(k, width=BRANCH_WIDTH):
        return qk_ref[k:k + 1, :width]

    z5 = seg(5)
    z1 = seg(1)
    z4 = seg(4)
    k = _rope(_head_rms(z5, gain(3), avg), cos, sin)
    for hh in range(C_HEADS):
        ktc_ref[0, hh] = k[:, hh * LANES:(hh + 1) * LANES].T.astype(BF16)
    z0 = seg(0)
    k = _rope(_head_rms(z1, gain(1, LANES), avg[:LANES, :LANES]), cos, sin)
    for hh, kd in enumerate(_dup_halves(k)):
        kta_ref[0, hh] = kd.T.astype(BF16)
    z2 = seg(2)
    q = _rope(_head_rms(z4, gain(2), avg), cos, sin)
    qc_ref[...] = (q * SCORE_SCALE).astype(BF16)
    q = _rope(_head_rms(z0, gain(0), avg), cos, sin)
    qa_ref[...] = (q * SCORE_SCALE).astype(BF16)
    for hh, vd in enumerate(_dup_halves(z2)):
        va_ref[0, hh] = vd.astype(BF16)
    v = seg(6)
    for hh in range(C_HEADS):
        vc_ref[0, hh] = v[:, hh * LANES:(hh + 1) * LANES].astype(BF16)
    u_ref[...] = seg(3)


def _inproj(x_lat, x_ctx, mods, norm_g, w_in, qk_gain, cos_t, sin_t, batch, nblk):
    s_len = nblk * TM
    t_rows = batch * s_len
    row = lambda i: (i, 0)
    mod_map = lambda i: (jnp.where(i % nblk == 0, batch, i // nblk), 0, 0)
    pos = lambda i: (i % nblk, 0)
    kt_map = lambda i: (i // nblk, 0, 0, i % nblk)
    v_map = lambda i: (i // nblk, 0, i % nblk, 0)
    x_specs, x_args = _stream_specs(x_lat, x_ctx, nblk)
    return pl.pallas_call(
        functools.partial(_inproj_kernel, nblk=nblk),
        grid=(t_rows // TM,),
        in_specs=x_specs + [
                  pl.BlockSpec((1, 6, D_MODEL), mod_map),
                  pl.BlockSpec((1, D_MODEL), lambda i: (0, 0)),
                  pl.BlockSpec((D_MODEL, N_QKVU), lambda i: (0, 0)),
                  pl.BlockSpec((4, BRANCH_WIDTH), lambda i: (0, 0)),
                  pl.BlockSpec((TM, LANES), pos),
                  pl.BlockSpec((TM, LANES), pos)],
        out_specs=[pl.BlockSpec((TM, BRANCH_WIDTH), row),
                   pl.BlockSpec((1, A_KV_HEADS, LANES, TM), kt_map),
                   pl.BlockSpec((1, A_KV_HEADS, TM, LANES), v_map),
                   pl.BlockSpec((TM, BRANCH_WIDTH), row),
                   pl.BlockSpec((TM, BRANCH_WIDTH), row),
                   pl.BlockSpec((1, C_HEADS, LANES, TM), kt_map),
                   pl.BlockSpec((1, C_HEADS, TM, LANES), v_map)],
        out_shape=[jax.ShapeDtypeStruct((t_rows, BRANCH_WIDTH), BF16),
                   jax.ShapeDtypeStruct((batch, A_KV_HEADS, LANES, s_len), BF16),
                   jax.ShapeDtypeStruct((batch, A_KV_HEADS, s_len, LANES), BF16),
                   jax.ShapeDtypeStruct((t_rows, BRANCH_WIDTH), F32),
                   jax.ShapeDtypeStruct((t_rows, BRANCH_WIDTH), BF16),
                   jax.ShapeDtypeStruct((batch, C_HEADS, LANES, s_len), BF16),
                   jax.ShapeDtypeStruct((batch, C_HEADS, s_len, LANES), BF16)],
        compiler_params=_cparams(1),
        name="in_proj",
    )(*x_args, mods, norm_g, w_in, qk_gain, cos_t, sin_t)


def _attn_kernel(*refs, diff, lam_init, n_ctx, bounds):
    if diff:
        q_ref, kt_ref, v_ref, lam_ref, gout_ref, o_ref = refs[:6]
    else:
        q_ref, kt_ref, v_ref, o_ref = refs[:4]
    lhs_sc, m_sc, l_sc, acc_sc, s0_sc, s1_sc = refs[-6:]
    bufs = (s0_sc, s1_sc)
    n_chunks = len(bounds) - 1
    j = pl.program_id(2)
    rows = 2 * TM
    lane = lax.broadcasted_iota(jnp.int32, (TM, LANES), 1)
    low = lane < HEAD_DIM
    qt = q_ref[...].astype(F32)
    lhs_sc[0:TM, :] = jnp.where(low, qt, 0.0).astype(BF16)
    lhs_sc[TM:rows, :] = jnp.where(low, 0.0, qt).astype(BF16)

    def chunk_keys(c):
        return slice(bounds[c], bounds[c + 1])

    def scores(keys):
        return jnp.dot(lhs_sc[...], kt_ref[0, 0, :, keys], preferred_element_type=F32)

    def softmax_pv(s, keys, first):
        width = s.shape[1]
        mx = s[:, 0:LANES]
        for t in range(1, width // LANES):
            mx = jnp.maximum(mx, s[:, t * LANES:(t + 1) * LANES])
        m_cur = jnp.max(mx, axis=1, keepdims=True)
        if first:
            m_next = jnp.broadcast_to(m_cur, (rows, LANES))
        else:
            m_prev = m_sc[...]
            m_next = jnp.maximum(m_prev, m_cur)
            alpha = jnp.exp2(m_prev - m_next)
        p = jnp.exp2(s - _tile_lanes(m_next, width))
        psum = p[:, 0:LANES]
        for t in range(1, width // LANES):
            psum = psum + p[:, t * LANES:(t + 1) * LANES]
        pv = jnp.dot(p.astype(BF16), v_ref[0, 0, keys, :], preferred_element_type=F32)
        if first:
            l_sc[...] = psum
            acc_sc[...] = pv
        else:
            l_sc[...] = alpha * l_sc[...] + psum
            acc_sc[...] = alpha * acc_sc[...] + pv
        m_sc[...] = m_next

    @pl.when(j == 0)
    def _():
        ctx_keys = slice(0, n_ctx)
        softmax_pv(scores(ctx_keys), ctx_keys, True)

    @pl.when(j > 0)
    def _():
        def put(c):
            keys = chunk_keys(c)
            bufs[c % 2][:, 0:keys.stop - keys.start] = scores(keys)

        def take(c):
            keys = chunk_keys(c)
            softmax_pv(bufs[c % 2][:, 0:keys.stop - keys.start], keys, c == 0)

        put(0)
        for c in range(n_chunks):
            if c + 1 < n_chunks:
                put(c + 1)
            take(c)

    o = acc_sc[...] / jnp.sum(l_sc[...], axis=1, keepdims=True)
    if diff:
        lv = lam_ref[...]
        lam = (jnp.exp(jnp.sum(lv[0:1] * lv[1:2], axis=1, keepdims=True))
               - jnp.exp(jnp.sum(lv[2:3] * lv[3:4], axis=1, keepdims=True)) + lam_init)
        od = o[0:TM] - lam * o[TM:rows]
        od = od * lax.rsqrt(jnp.mean(od * od, axis=-1, keepdims=True) + EPS)
        o_ref[...] = ((od * gout_ref[...]) * (1.0 - lam_init)).astype(o_ref.dtype)
    else:
        o_ref[...] = jnp.where(low, o[0:TM], o[TM:rows]).astype(o_ref.dtype)


def _attention(q, kt, v, batch, nblk, *, diff, lam_vecs=None, g_out=None, lam_init=0.0):
    kv_heads = kt.shape[1]
    s_len = kt.shape[3]
    tiles = BRANCH_WIDTH // LANES
    rows = 2 * TM
    q_map = lambda b, h, j: (b * nblk + j, h)
    kv_map = lambda b, h, j: (b, h * kv_heads // tiles, 0, 0)
    const = lambda b, h, j: (0, 0)
    in_specs = [pl.BlockSpec((TM, LANES), q_map),
                pl.BlockSpec((1, 1, LANES, s_len), kv_map),
                pl.BlockSpec((1, 1, s_len, LANES), kv_map)]
    args = [q, kt, v]
    if diff:
        in_specs += [pl.BlockSpec((4, HEAD_DIM), const), pl.BlockSpec((1, LANES), const)]
        args += [lam_vecs, g_out]
    n_ctx = TM
    tkl = min(SCORE_KEYS, s_len - n_ctx)
    assert (s_len - n_ctx) % tkl == 0
    bounds = [0] + list(range(n_ctx + tkl, s_len + 1, tkl))
    kern = functools.partial(_attn_kernel, diff=diff, lam_init=lam_init,
                             n_ctx=n_ctx, bounds=tuple(bounds))
    return pl.pallas_call(
        kern,
        grid=(batch, tiles, nblk),
        in_specs=in_specs,
        out_specs=pl.BlockSpec((TM, LANES), q_map),
        out_shape=jax.ShapeDtypeStruct((batch * s_len, BRANCH_WIDTH), BF16),
        scratch_shapes=([pltpu.VMEM((rows, LANES), BF16)]
                        + [pltpu.VMEM((rows, LANES), F32)] * 3
                        + [pltpu.VMEM((rows, n_ctx + tkl), F32)] * 2),
        compiler_params=_cparams(3),
        name="diff_attn" if diff else "gqa_attn",
    )(*args)


def _s5_kernel(uf_ref, ub_ref, pin_ref, pout_ref, wd_ref, a_ref, cm_ref, yf_ref, yb_ref,
               bu_sc, st_sc):
    n_tiles = BRANCH_WIDTH // LANES
    half = SUBLANES * S5_STATE
    rows = TC * SUBLANES
    tok = (SUBLANES // 2) * TC

    @pl.when(pl.program_id(0) == 0)
    def _():
        st_sc[...] = jnp.zeros_like(st_sc)

    seq = lax.broadcasted_iota(jnp.int32, (rows, LANES), 0) & (SUBLANES - 1)
    fwd = seq < SUBLANES // 2
    lhs_dir = []
    for d, ref in enumerate((uf_ref, ub_ref)):
        ud = ref[...].reshape(tok, BRANCH_WIDTH).astype(BF16)
        lhs_dir.append(jnp.dot(pin_ref[d], ud, preferred_element_type=F32).astype(BF16))
    def drive(j):
        cs = slice(j * LANES, (j + 1) * LANES)
        lhs = jnp.concatenate([lhs_dir[0][:, cs], lhs_dir[1][:, cs]], axis=1)
        bu_sc[:, 2 * half * j:2 * half * (j + 1)] = jnp.dot(
            lhs, wd_ref[j], preferred_element_type=F32)

    def scan(j):
        re = slice(2 * half * j, 2 * half * j + half)
        im = slice(2 * half * j + half, 2 * half * (j + 1))
        ar = a_ref[0, :, half * j:half * (j + 1)]
        ai = a_ref[1, :, half * j:half * (j + 1)]

        def step(t, carry):
            xr, xi = carry
            r = pl.multiple_of(t * SUBLANES, SUBLANES)
            nxr = ar * xr - ai * xi + bu_sc[pl.ds(r, SUBLANES), re]
            nxi = ar * xi + ai * xr + bu_sc[pl.ds(r, SUBLANES), im]
            bu_sc[pl.ds(r, SUBLANES), re] = nxr
            bu_sc[pl.ds(r, SUBLANES), im] = nxi
            return nxr, nxi

        xr, xi = lax.fori_loop(0, TC, step, (st_sc[:, re], st_sc[:, im]), unroll=True)
        st_sc[:, re] = xr
        st_sc[:, im] = xi

    def readout(j):
        x = bu_sc[:, 2 * half * j:2 * half * (j + 1)].astype(BF16)
        yy = jnp.dot(x, cm_ref[j], preferred_element_type=F32)
        return jnp.where(fwd, yy[:, :LANES], yy[:, LANES:])

    ys = []
    drive(0)
    for j in range(n_tiles):
        if j + 1 < n_tiles:
            drive(j + 1)
        scan(j)
        ys.append(readout(j))
    y = jnp.concatenate(ys, axis=1)
    parts = []
    rest = y
    for _ in range(2):
        part = rest.astype(BF16)
        parts.append(part)
        rest = rest - part.astype(F32)
    stacked = jnp.concatenate(parts, axis=0)
    for d, ref in enumerate((yf_ref, yb_ref)):
        out = jnp.dot(pout_ref[d], stacked, preferred_element_type=F32)
        ref[...] = out.reshape(ref.shape)


def _scan_row_placement(n_seq):
    p = np.zeros((2, TC * 2 * n_seq, n_seq * TC), np.float32)
    for b in range(n_seq):
        for k in range(TC):
            p[0, 2 * n_seq * k + b, b * TC + k] = 1.0
            p[1, 2 * n_seq * (TC - 1 - k) + n_seq + b, b * TC + k] = 1.0
    pt = p.transpose(0, 2, 1)
    return jnp.asarray(p, BF16), jnp.asarray(np.concatenate([pt, pt], axis=2), BF16)


def _s5_scan(u3, wd, a8, cm, n_ctx):
    batch, s_len, _ = u3.shape
    rows = TC * SUBLANES
    n_tiles = BRANCH_WIDTH // LANES
    n_state = 2 * SUBLANES * S5_STATE * n_tiles
    n_steps = s_len // TC
    ctx_steps = n_ctx // TC
    p_in, p_out = _scan_row_placement(batch)
    fwd_map = lambda g: (0, g, 0)
    bwd_map = lambda g: (0, jnp.where(g < ctx_steps, ctx_steps - 1 - g,
                                      n_steps + ctx_steps - 1 - g), 0)
    c3 = lambda g: (0, 0, 0)
    blk = (batch, TC, BRANCH_WIDTH)
    out = jax.ShapeDtypeStruct(u3.shape, F32)
    return pl.pallas_call(
        _s5_kernel,
        grid=(n_steps,),
        in_specs=[pl.BlockSpec(blk, fwd_map),
                  pl.BlockSpec(blk, bwd_map),
                  pl.BlockSpec(p_in.shape, c3),
                  pl.BlockSpec(p_out.shape, c3),
                  pl.BlockSpec(wd.shape, c3),
                  pl.BlockSpec(a8.shape, c3),
                  pl.BlockSpec(cm.shape, c3)],
        out_specs=[pl.BlockSpec(blk, fwd_map), pl.BlockSpec(blk, bwd_map)],
        out_shape=[out, out],
        scratch_shapes=[pltpu.VMEM((rows, n_state), F32),
                        pltpu.VMEM((SUBLANES, n_state), F32)],
        compiler_params=_cparams(1),
        name="s5_scan",
    )(u3, u3, p_in, p_out, wd, a8, cm)


def _zoh(lam_re, lam_im, log_dt, b_re, b_im):
    dt = jnp.exp(log_dt)[..., None]
    mag = jnp.exp(lam_re * dt)
    a_re = mag * jnp.cos(lam_im * dt)
    a_im = mag * jnp.sin(lam_im * dt)
    den = lam_re * lam_re + lam_im * lam_im
    f_re = ((a_re - 1.0) * lam_re + a_im * lam_im) / den
    f_im = (a_im * lam_re - (a_re - 1.0) * lam_im) / den
    bb_re = f_re[..., None] * b_re - f_im[..., None] * b_im
    bb_im = f_re[..., None] * b_im + f_im[..., None] * b_re
    return a_re, a_im, bb_re, bb_im


def _s5_weights(lam_re, lam_im, log_dt, b_re, b_im, c_re, c_im, batch):
    n_tiles = BRANCH_WIDTH // LANES
    gpt = S5_GROUPS // n_tiles
    a_re, a_im, bb_re, bb_im = _zoh(lam_re, lam_im, log_dt, b_re, b_im)
    eye = jnp.eye(gpt, dtype=F32)

    def drive(bb):
        t = bb.reshape(2, n_tiles, gpt, S5_STATE, S5_CH)
        w = jnp.einsum('djgpc,gh->jdgchp', t, eye)
        return w.reshape(n_tiles, 2 * gpt * S5_CH, gpt * S5_STATE)

    def read(cc):
        t = cc.reshape(2, n_tiles, gpt, S5_CH, S5_STATE)
        w = jnp.einsum('djgcp,gh->jgpdhc', t, eye)
        return w.reshape(n_tiles, gpt * S5_STATE, 2 * gpt * S5_CH)

    wd = jnp.concatenate([drive(bb_re), drive(bb_im)], axis=2).astype(BF16)
    cm = jnp.concatenate([read(c_re), read(-c_im)], axis=1).astype(BF16)

    def per_seq(a):
        return jnp.repeat(a.reshape(2, 1, S5_GROUPS * S5_STATE), batch, axis=1).reshape(
            2 * batch, S5_GROUPS * S5_STATE)

    a8 = jnp.stack([per_seq(a_re), per_seq(a_im)])
    return wd, a8, cm


def _merge_kernel(x_ref, xc_ref, mod_ref, g1_ref, g2_ref, ya_ref, yd_ref, yf_ref, yr_ref, u_ref,
                  dsk_ref, wglu_ref, bglu_ref, wgate_ref, wbr_ref, wout_ref, x1_ref, h2_ref,
                  *, nblk):
    mods = mod_ref[0]
    proj = {k: jnp.dot(ref[...], wbr_ref[k], preferred_element_type=F32)
            for k, ref in ((0, ya_ref), (2, yd_ref))}
    x = _stream_tile(x_ref, xc_ref, nblk)
    h = _rms_mod(x, g1_ref[...], mods[0:1], mods[1:2]).astype(BF16)
    ys = yf_ref[...] + yr_ref[...] + dsk_ref[...] * u_ref[...]
    g = jax.nn.gelu(ys)
    yb = g * jax.nn.sigmoid(
        jnp.dot(g.astype(BF16), wglu_ref[...], preferred_element_type=F32) + bglu_ref[...])
    proj[1] = jnp.dot(yb.astype(BF16), wbr_ref[1], preferred_element_type=F32)
    m = None
    for k in range(3):
        gate = jax.nn.sigmoid(jnp.dot(h, wgate_ref[:, k * D_MODEL:(k + 1) * D_MODEL],
                                      preferred_element_type=F32))
        term = gate * proj[k]
        m = term if m is None else m + term
    y = jnp.dot(m.astype(BF16), wout_ref[...], preferred_element_type=F32)
    x1 = x + mods[2:3] * y
    x1_ref[...] = x1
    h2_ref[...] = _rms_mod(x1, g2_ref[...], mods[3:4], mods[4:5]).astype(BF16)


def _merge(x_lat, x_ctx, mods, g1, g2, ya, yd, yf, yr, u, d_skip, w_glu, b_glu, w_gate,
           w_branch, w_out, batch, nblk):
    t_rows = ya.shape[0]
    row = lambda i: (i, 0)
    mod_map = lambda i: (jnp.where(i % nblk == 0, batch, i // nblk), 0, 0)
    c2 = lambda i: (0, 0)
    c3 = lambda i: (0, 0, 0)
    x_specs, x_args = _stream_specs(x_lat, x_ctx, nblk)
    return pl.pallas_call(
        functools.partial(_merge_kernel, nblk=nblk),
        grid=(t_rows // TM,),
        in_specs=x_specs + [
                  pl.BlockSpec((1, 6, D_MODEL), mod_map),
                  pl.BlockSpec((1, D_MODEL), c2),
                  pl.BlockSpec((1, D_MODEL), c2),
                  pl.BlockSpec((TM, BRANCH_WIDTH), row),
                  pl.BlockSpec((TM, BRANCH_WIDTH), row),
                  pl.BlockSpec((TM, BRANCH_WIDTH), row),
                  pl.BlockSpec((TM, BRANCH_WIDTH), row),
                  pl.BlockSpec((TM, BRANCH_WIDTH), row),
                  pl.BlockSpec((1, BRANCH_WIDTH), c2),
                  pl.BlockSpec((BRANCH_WIDTH, BRANCH_WIDTH), c2),
                  pl.BlockSpec((1, BRANCH_WIDTH), c2),
                  pl.BlockSpec((D_MODEL, 3 * D_MODEL), c2),
                  pl.BlockSpec((3, BRANCH_WIDTH, D_MODEL), c3),
                  pl.BlockSpec((D_MODEL, D_MODEL), c2)],
        out_specs=[pl.BlockSpec((TM, D_MODEL), row), pl.BlockSpec((TM, D_MODEL), row)],
        out_shape=[jax.ShapeDtypeStruct((t_rows, D_MODEL), F32),
                   jax.ShapeDtypeStruct((t_rows, D_MODEL), BF16)],
        compiler_params=_cparams(1),
        name="merge",
    )(*x_args, mods, g1, g2, ya, yd, yf, yr, u, d_skip, w_glu, b_glu, w_gate, w_branch, w_out)


def _ffn_kernel(h_ref, hp_ref, hn_ref, x1_ref, mod_ref, perm_ref, wup_ref, cw_ref, wdn_ref, o_ref,
                lhs_sc, u0_sc, u1_sc, act_sc, *, nblk, n_chunks):
    pj = pl.program_id(0) % nblk
    left_ok = (pj >= 2).astype(F32)
    right_ok = jnp.logical_and(pj != 0, pj != nblk - 1).astype(F32)
    lhs_sc[0:TM, :] = jnp.dot(perm_ref[0], h_ref[...], preferred_element_type=F32).astype(BF16)
    hrow = lax.broadcasted_iota(jnp.int32, (HALO, D_MODEL), 0)
    halo = (jnp.where(hrow == HALO - 1, hp_ref[...].astype(F32) * left_ok, 0.0)
            + jnp.where(hrow == 0, hn_ref[...].astype(F32) * right_ok, 0.0))
    lhs_sc[TM:TM + HALO, :] = halo.astype(BF16)
    sub = lax.broadcasted_iota(jnp.int32, (SUBLANES, 2 * FF_CHUNK), 0)

    def up(k, buf):
        buf[...] = jnp.dot(lhs_sc[...], wup_ref[k], preferred_element_type=F32)

    def activate(k, buf):
        cw = cw_ref[k]
        before = buf[TM + HALO - 1:TM + HALO, :]
        after = buf[TM:TM + 1, :]
        first = jnp.where(sub == 0, before, pltpu.roll(buf[TM - SUBLANES:TM, :], 1, 0))
        last = jnp.where(sub == SUBLANES - 1, after,
                         pltpu.roll(buf[0:SUBLANES, :], SUBLANES - 1, 0))
        prev = jnp.concatenate([first, buf[0:TM - SUBLANES, :]], axis=0)
        nxt = jnp.concatenate([buf[SUBLANES:TM, :], last], axis=0)
        c = prev * cw[0:1] + buf[0:TM, :] * cw[1:2] + nxt * cw[2:3] + cw[3:4]
        act = jax.nn.silu(c[:, FF_CHUNK:]) * c[:, :FF_CHUNK]
        act_sc[:, k * FF_CHUNK:(k + 1) * FF_CHUNK] = act.astype(BF16)

    bufs = (u0_sc, u1_sc)
    up(0, bufs[0])
    for k in range(n_chunks):
        if k + 1 < n_chunks:
            up(k + 1, bufs[(k + 1) % 2])
        activate(k, bufs[k % 2])
    act = jnp.dot(perm_ref[1], act_sc[...], preferred_element_type=F32).astype(BF16)
    y = jnp.dot(act, wdn_ref[...], preferred_element_type=F32)
    o_ref[...] = x1_ref[...] + mod_ref[0][5:6] * y


def _strided_row_order():
    p = np.zeros((TM, TM), np.float32)
    groups = TM // SUBLANES
    for r in range(groups):
        for s in range(SUBLANES):
            p[SUBLANES * r + s, groups * s + r] = 1.0
    return jnp.asarray(np.stack([p, p.T]), BF16)


def _ffn(h2, x1, mods, wup, cw, wdn, batch, nblk, latent_only):
    t_rows = x1.shape[0]
    n_chunks = wup.shape[0]
    per = TM // HALO
    last = t_rows // HALO - 1
    row = lambda i: (i, 0)
    mod_map = lambda i: (jnp.where(i % nblk == 0, batch, i // nblk), 0, 0)
    kern = functools.partial(_ffn_kernel, nblk=nblk, n_chunks=n_chunks)
    if latent_only:
        out_map = lambda i: ((i // nblk) * (nblk - 1) + jnp.maximum(i % nblk - 1, 0), 0)
        out_rows = t_rows - batch * TM
    else:
        out_map, out_rows = row, t_rows
    return pl.pallas_call(
        kern,
        grid=(t_rows // TM,),
        in_specs=[pl.BlockSpec((TM, D_MODEL), row),
                  pl.BlockSpec((HALO, D_MODEL), lambda i: (jnp.maximum(i * per - 1, 0), 0)),
                  pl.BlockSpec((HALO, D_MODEL), lambda i: (jnp.minimum((i + 1) * per, last), 0)),
                  pl.BlockSpec((TM, D_MODEL), row),
                  pl.BlockSpec((1, 6, D_MODEL), mod_map),
                  pl.BlockSpec((2, TM, TM), lambda i: (0, 0, 0)),
                  pl.BlockSpec(wup.shape, lambda i: (0, 0, 0)),
                  pl.BlockSpec(cw.shape, lambda i: (0, 0, 0)),
                  pl.BlockSpec(wdn.shape, lambda i: (0, 0))],
        out_specs=pl.BlockSpec((TM, D_MODEL), out_map),
        out_shape=jax.ShapeDtypeStruct((out_rows, D_MODEL), F32),
        scratch_shapes=([pltpu.VMEM((TM + HALO, D_MODEL), BF16)]
                        + [pltpu.VMEM((TM + HALO, 2 * FF_CHUNK), F32)] * 2
                        + [pltpu.VMEM((TM, D_FF), BF16)]),
        compiler_params=_cparams(1),
        name="conv_ffn",
    )(h2, h2, h2, x1, mods, _strided_row_order(), wup, cw, wdn)


def _relayout_kernel(a_ref, g_ref, o_ref):
    o_ref[0] = jnp.concatenate([a_ref[0], g_ref[0]], axis=1).astype(o_ref.dtype)


def _chunk_major_bf16(w_up, layer, n_chunks):
    rows = w_up.shape[1]
    return pl.pallas_call(
        _relayout_kernel,
        grid=(n_chunks,),
        in_specs=[pl.BlockSpec((1, rows, FF_CHUNK), lambda k: (layer, 0, k)),
                  pl.BlockSpec((1, rows, FF_CHUNK), lambda k: (layer, 0, n_chunks + k))],
        out_specs=pl.BlockSpec((1, rows, 2 * FF_CHUNK), lambda k: (k, 0, 0)),
        out_shape=jax.ShapeDtypeStruct((n_chunks, rows, 2 * FF_CHUNK), BF16),
        compiler_params=_cparams(1),
        name="ffn_weight_layout",
    )(w_up, w_up)


def _cast_kernel(w_ref, o_ref):
    o_ref[...] = w_ref[0].astype(o_ref.dtype)


def _cast_cols_bf16(w, layer, first_col, n_cols):
    rows = w.shape[1]
    blk = 2 * LANES
    assert first_col % blk == 0 and n_cols % blk == 0
    return pl.pallas_call(
        _cast_kernel,
        grid=(n_cols // blk,),
        in_specs=[pl.BlockSpec((1, rows, blk), lambda k: (layer, 0, first_col // blk + k))],
        out_specs=pl.BlockSpec((rows, blk), lambda k: (0, k)),
        out_shape=jax.ShapeDtypeStruct((rows, n_cols), BF16),
        compiler_params=_cparams(1),
        name="weight_cast",
    )(w)


def _rope_tables(seq, ctx):
    rows = seq // GRID_W
    row = jnp.repeat(jnp.arange(rows, dtype=F32), GRID_W)
    col = jnp.tile(jnp.arange(GRID_W, dtype=F32), rows)
    n_freq = HEAD_DIM // 4
    inv_freq = ROPE_BASE ** (-jnp.arange(n_freq, dtype=F32) / n_freq)
    ang = jnp.concatenate([row[:, None] * inv_freq, col[:, None] * inv_freq], axis=-1)
    ang = jnp.concatenate([ang, ang], axis=-1)
    cos = jnp.concatenate([jnp.ones((ctx, HEAD_DIM), F32), jnp.cos(ang)], axis=0)
    sin = jnp.concatenate([jnp.zeros((ctx, HEAD_DIM), F32), jnp.sin(ang)], axis=0)
    sign = jnp.where(jnp.arange(HEAD_DIM) < HEAD_DIM // 2, -1.0, 1.0).astype(F32)
    two = lambda t: jnp.concatenate([t, t], axis=1)
    return two(cos), two(sin * sign)


def kernel(x, c, ctx, c_ctx, w_ada, b_ada, norm_g, w_in, qk_gain, ssm_lam_re, ssm_lam_im, ssm_log_dt, ssm_b_re, ssm_b_im, ssm_c_re, ssm_c_im, ssm_d, w_glu, b_glu, diff_lam, diff_norm_g, w_branch, w_out, w_up, conv_w, conv_b, w_down):
    batch, seq, _ = x.shape
    n_ctx = ctx.shape[1]
    depth = w_in.shape[0]
    assert n_ctx == TM and seq % TM == 0 and 2 * batch == SUBLANES
    s_len = n_ctx + seq
    nblk = s_len // TM
    assert s_len % TC == 0 and D_FF % FF_CHUNK == 0
    n_chunks = D_FF // FF_CHUNK

    cos_t, sin_t = _rope_tables(seq, n_ctx)
    cc = jnp.concatenate([c, c_ctx[None, :], jnp.zeros((SUBLANES - batch - 1, D_MODEL), F32)], axis=0)
    mods_all = _ada_mods(cc, w_ada, b_ada).reshape(depth, SUBLANES, 6, D_MODEL)

    xa = x.reshape(batch * seq, D_MODEL)
    xc = ctx.reshape(batch * n_ctx, D_MODEL)
    for i in range(depth):
        lam_init = 0.8 - 0.6 * math.exp(-0.3 * i)
        mods = mods_all[i]
        g1 = norm_g[i, 0:1]
        g2 = norm_g[i, 1:2]
        w_qkvu = _cast_cols_bf16(w_in, i, 0, N_QKVU)
        w_gate = _cast_cols_bf16(w_in, i, N_QKVU, 3 * D_MODEL)
        qa, kta, va, u, qc, ktc, vc = _inproj(xa, xc, mods, g1, w_qkvu,
                                              jnp.tile(qk_gain[i], (1, BRANCH_WIDTH // HEAD_DIM)),
                                              cos_t, sin_t, batch, nblk)
        ya = _attention(qa, kta, va, batch, nblk, diff=False)
        yd = _attention(qc, ktc, vc, batch, nblk, diff=True, lam_vecs=diff_lam[i],
                        g_out=diff_norm_g[i][None, :], lam_init=lam_init)

        wd, a8, cm = _s5_weights(ssm_lam_re[i], ssm_lam_im[i], ssm_log_dt[i], ssm_b_re[i],
                                 ssm_b_im[i], ssm_c_re[i], ssm_c_im[i], batch)
        yf, yr = _s5_scan(u.reshape(batch, s_len, BRANCH_WIDTH), wd, a8, cm, n_ctx)
        yf = yf.reshape(batch * s_len, BRANCH_WIDTH)
        yr = yr.reshape(batch * s_len, BRANCH_WIDTH)

        x1, h2 = _merge(xa, xc, mods, g1, g2, ya, yd, yf, yr, u, ssm_d[i][None, :],
                        w_glu[i].astype(BF16), b_glu[i][None, :], w_gate,
                        w_branch[i].astype(BF16), w_out[i].astype(BF16), batch, nblk)

        def chunked(t):
            a = t[:, :D_FF].reshape(t.shape[0], n_chunks, FF_CHUNK)
            g = t[:, D_FF:].reshape(t.shape[0], n_chunks, FF_CHUNK)
            return jnp.concatenate([a, g], axis=-1).transpose(1, 0, 2)

        wup = _chunk_major_bf16(w_up, i, n_chunks)
        cw = chunked(jnp.concatenate(
            [conv_w[i], conv_b[i][None, :], jnp.zeros((SUBLANES - 4, 2 * D_FF), F32)], axis=0))
        wdn = w_down[i].astype(BF16)
        xa = _ffn(h2, x1, mods, wup, cw, wdn, batch, nblk, latent_only=(i == depth - 1))
        xc = None

    return xa.reshape(batch, seq, D_MODEL)
```

```python
import functools
import math

import jax
import jax.numpy as jnp
import numpy as np
from jax import lax
from jax.experimental import pallas as pl
from jax.experimental.pallas import tpu as pltpu

F32 = jnp.float32
BF16 = jnp.bfloat16

D_MODEL = 1024
HEAD_DIM = 64
GRID_W = 64
ROPE_BASE = 10000.0
EPS = 1e-6
BRANCH_WIDTH = D_MODEL // 2
A_HEADS = BRANCH_WIDTH // HEAD_DIM
A_KV_HEADS = A_HEADS // 4
C_HEADS = BRANCH_WIDTH // (2 * HEAD_DIM)
S5_CH = 16
S5_STATE = 64
S5_GROUPS = BRANCH_WIDTH // S5_CH
D_FF = 2816
IN_SIZES = (512, 128, 128, 512, 512, 512, 512, 3 * D_MODEL)
IN_OFFS = tuple(int(v) for v in np.cumsum((0,) + IN_SIZES))
N_QKVU = IN_OFFS[7]

LANES = 128
SUBLANES = 8
TM = 256
SCORE_KEYS = 2048
TC = 64
FF_CHUNK = 128
HALO = 16
ADA_COLS = 1536
V7X_VMEM_BYTES = 64 * 1024 * 1024
VMEM_LIMIT = V7X_VMEM_BYTES * 7 // 8
SCORE_SCALE = HEAD_DIM ** -0.5 * math.log2(math.e)


def _cparams(n_axes):
    return pltpu.CompilerParams(dimension_semantics=("arbitrary",) * n_axes,
                                vmem_limit_bytes=VMEM_LIMIT)


def _rms_mod(x, gain, shift, scale):
    y = x * lax.rsqrt(jnp.mean(x * x, axis=-1, keepdims=True) + EPS)
    return (y * gain) * (1.0 + scale) + shift


def _head_avg_matrix(width):
    shift = HEAD_DIM.bit_length() - 1
    r = lax.broadcasted_iota(jnp.int32, (width, width), 0) >> shift
    c = lax.broadcasted_iota(jnp.int32, (width, width), 1) >> shift
    return jnp.where(r == c, 1.0 / HEAD_DIM, 0.0).astype(BF16)


def _head_rms(z, gain, avg):
    sq = z * z
    hi = sq.astype(BF16)
    lo = (sq - hi.astype(F32)).astype(BF16)
    ms = (jnp.dot(hi, avg, preferred_element_type=F32)
          + jnp.dot(lo, avg, preferred_element_type=F32))
    return z * lax.rsqrt(ms + EPS) * gain


def _tile_lanes(t, width):
    reps = width // t.shape[1]
    return t if reps == 1 else jnp.concatenate([t] * reps, axis=1)


def _rope(z, cos, sin_signed):
    width = z.shape[1]
    lane = lax.broadcasted_iota(jnp.int32, z.shape, 1)
    first_half = (lane & (HEAD_DIM - 1)) < HEAD_DIM // 2
    rot = jnp.where(first_half,
                    pltpu.roll(z, width - HEAD_DIM // 2, 1),
                    pltpu.roll(z, HEAD_DIM // 2, 1))
    return z * _tile_lanes(cos, width) + rot * _tile_lanes(sin_signed, width)


def _dup_halves(z):
    lane = lax.broadcasted_iota(jnp.int32, z.shape, 1)
    low = lane < HEAD_DIM
    sw = pltpu.roll(z, HEAD_DIM, 1)
    return jnp.where(low, z, sw), jnp.where(low, sw, z)


def _ada_kernel(c_ref, w_ref, b_ref, o_ref):
    a = jax.nn.silu(c_ref[...])
    o_ref[0] = jnp.dot(a, w_ref[0], preferred_element_type=F32,
                       precision=lax.Precision.HIGHEST) + b_ref[0]


def _ada_mods(cc, w_ada, b_ada):
    depth, _, n = w_ada.shape
    tn = ADA_COLS
    assert n % tn == 0
    return pl.pallas_call(
        _ada_kernel,
        grid=(depth, n // tn),
        in_specs=[pl.BlockSpec((SUBLANES, D_MODEL), lambda l, j: (0, 0)),
                  pl.BlockSpec((1, D_MODEL, tn), lambda l, j: (l, 0, j)),
                  pl.BlockSpec((1, 1, tn), lambda l, j: (l, 0, j))],
        out_specs=pl.BlockSpec((1, SUBLANES, tn), lambda l, j: (l, 0, j)),
        out_shape=jax.ShapeDtypeStruct((depth, SUBLANES, n), F32),
        compiler_params=_cparams(2),
        name="ada_mods",
    )(cc, w_ada, b_ada.reshape(depth, 1, n))


def _stream_specs(x_lat, x_ctx, nblk):
    blk = (TM, D_MODEL)
    if x_ctx is None:
        return ([pl.BlockSpec(blk, lambda i: (i, 0)),
                 pl.BlockSpec(blk, lambda i: ((i // nblk) * nblk, 0))], [x_lat, x_lat])
    lat_map = lambda i: ((i // nblk) * (nblk - 1) + jnp.maximum(i % nblk - 1, 0), 0)
    return ([pl.BlockSpec(blk, lat_map), pl.BlockSpec(blk, lambda i: (i // nblk, 0))],
            [x_lat, x_ctx])


def _stream_tile(x_ref, xc_ref, nblk):
    return jnp.where(pl.program_id(0) % nblk == 0, xc_ref[...], x_ref[...])


def _inproj_kernel(x_ref, xc_ref, mod_ref, g_ref, w_ref, qk_ref, cos_ref, sin_ref,
                   qa_ref, kta_ref, va_ref, u_ref, qc_ref, ktc_ref, vc_ref, *, nblk):
    mods = mod_ref[0]
    x = _stream_tile(x_ref, xc_ref, nblk)
    h = _rms_mod(x, g_ref[...], mods[0:1], mods[1:2]).astype(BF16)
    cos = cos_ref[...]
    sin = sin_ref[...]
    avg = _head_avg_matrix(BRANCH_WIDTH)

    def seg(k):
        return jnp.dot(h, w_ref[:, IN_OFFS[k]:IN_OFFS[k + 1]], preferred_element_type=F32)

    def gain(k, width=BRANCH_WIDTH):
        return qk_ref[k:k + 1, :width]

    z5 = seg(5)
    z1 = seg(1)
    z4 = seg(4)
    k = _rope(_head_rms(z5, gain(3), avg), cos, sin)
    for hh in range(C_HEADS):
        ktc_ref[0, hh] = k[:, hh * LANES:(hh + 1) * LANES].T.astype(BF16)
    z0 = seg(0)
    k = _rope(_head_rms(z1, gain(1, LANES), avg[:LANES, :LANES]), cos, sin)
    for hh, kd in enumerate(_dup_halves(k)):
        kta_ref[0, hh] = kd.T.astype(BF16)
    z2 = seg(2)
    q = _rope(_head_rms(z4, gain(2), avg), cos, sin)
    qc_ref[...] = (q * SCORE_SCALE).astype(BF16)
    q = _rope(_head_rms(z0, gain(0), avg), cos, sin)
    qa_ref[...] = (q * SCORE_SCALE).astype(BF16)
    for hh, vd in enumerate(_dup_halves(z2)):
        va_ref[0, hh] = vd.astype(BF16)
    v = seg(6)
    for hh in range(C_HEADS):
        vc_ref[0, hh] = v[:, hh * LANES:(hh + 1) * LANES].astype(BF16)
    u_ref[...] = seg(3)


def _inproj(x_lat, x_ctx, mods, norm_g, w_in, qk_gain, cos_t, sin_t, batch, nblk):
    s_len = nblk * TM
    t_rows = batch * s_len
    row = lambda i: (i, 0)
    mod_map = lambda i: (jnp.where(i % nblk == 0, batch, i // nblk), 0, 0)
    pos = lambda i: (i % nblk, 0)
    kt_map = lambda i: (i // nblk, 0, 0, i % nblk)
    v_map = lambda i: (i // nblk, 0, i % nblk, 0)
    x_specs, x_args = _stream_specs(x_lat, x_ctx, nblk)
    return pl.pallas_call(
        functools.partial(_inproj_kernel, nblk=nblk),
        grid=(t_rows // TM,),
        in_specs=x_specs + [
                  pl.BlockSpec((1, 6, D_MODEL), mod_map),
                  pl.BlockSpec((1, D_MODEL), lambda i: (0, 0)),
                  pl.BlockSpec((D_MODEL, N_QKVU), lambda i: (0, 0)),
                  pl.BlockSpec((4, BRANCH_WIDTH), lambda i: (0, 0)),
                  pl.BlockSpec((TM, LANES), pos),
                  pl.BlockSpec((TM, LANES), pos)],
        out_specs=[pl.BlockSpec((TM, BRANCH_WIDTH), row),
                   pl.BlockSpec((1, A_KV_HEADS, LANES, TM), kt_map),
                   pl.BlockSpec((1, A_KV_HEADS, TM, LANES), v_map),
                   pl.BlockSpec((TM, BRANCH_WIDTH), row),
                   pl.BlockSpec((TM, BRANCH_WIDTH), row),
                   pl.BlockSpec((1, C_HEADS, LANES, TM), kt_map),
                   pl.BlockSpec((1, C_HEADS, TM, LANES), v_map)],
        out_shape=[jax.ShapeDtypeStruct((t_rows, BRANCH_WIDTH), BF16),
                   jax.ShapeDtypeStruct((batch, A_KV_HEADS, LANES, s_len), BF16),
                   jax.ShapeDtypeStruct((batch, A_KV_HEADS, s_len, LANES), BF16),
                   jax.ShapeDtypeStruct((t_rows, BRANCH_WIDTH), F32),
                   jax.ShapeDtypeStruct((t_rows, BRANCH_WIDTH), BF16),
                   jax.ShapeDtypeStruct((batch, C_HEADS, LANES, s_len), BF16),
                   jax.ShapeDtypeStruct((batch, C_HEADS, s_len, LANES), BF16)],
        compiler_params=_cparams(1),
        name="in_proj",
    )(*x_args, mods, norm_g, w_in, qk_gain, cos_t, sin_t)


def _attn_kernel(*refs, diff, lam_init, n_ctx, bounds, n_blocks):
    if diff:
        q_ref, kt_ref, v_ref, lam_ref, gout_ref, o_ref = refs[:6]
    else:
        q_ref, kt_ref, v_ref, o_ref = refs[:4]
    lhs_sc, m_sc, l_sc, acc_sc, s0_sc, s1_sc = refs[-6:]
    bufs = (s0_sc, s1_sc)
    n_chunks = len(bounds) - 1
    rows = 2 * TM
    lane = lax.broadcasted_iota(jnp.int32, (TM, LANES), 1)
    low = lane < HEAD_DIM

    def load_lhs(r0):
        qt = q_ref[pl.ds(r0, TM), :].astype(F32)
        lhs_sc[0:TM, :] = jnp.where(low, qt, 0.0).astype(BF16)
        lhs_sc[TM:rows, :] = jnp.where(low, 0.0, qt).astype(BF16)

    def chunk_keys(c):
        return slice(bounds[c], bounds[c + 1])

    def scores(keys):
        return jnp.dot(lhs_sc[...], kt_ref[0, 0, :, keys], preferred_element_type=F32)

    def softmax_pv(s, keys, first):
        width = s.shape[1]
        mx = s[:, 0:LANES]
        for t in range(1, width // LANES):
            mx = jnp.maximum(mx, s[:, t * LANES:(t + 1) * LANES])
        m_cur = jnp.max(mx, axis=1, keepdims=True)
        if first:
            m_next = jnp.broadcast_to(m_cur, (rows, LANES))
        else:
            m_prev = m_sc[...]
            m_next = jnp.maximum(m_prev, m_cur)
            alpha = jnp.exp2(m_prev - m_next)
        p = jnp.exp2(s - _tile_lanes(m_next, width))
        psum = p[:, 0:LANES]
        for t in range(1, width // LANES):
            psum = psum + p[:, t * LANES:(t + 1) * LANES]
        pv = jnp.dot(p.astype(BF16), v_ref[0, 0, keys, :], preferred_element_type=F32)
        if first:
            l_sc[...] = psum
            acc_sc[...] = pv
        else:
            l_sc[...] = alpha * l_sc[...] + psum
            acc_sc[...] = alpha * acc_sc[...] + pv
        m_sc[...] = m_next

    def finish(r0):
        o = acc_sc[...] / jnp.sum(l_sc[...], axis=1, keepdims=True)
        if diff:
            lv = lam_ref[...]
            lam = (jnp.exp(jnp.sum(lv[0:1] * lv[1:2], axis=1, keepdims=True))
                   - jnp.exp(jnp.sum(lv[2:3] * lv[3:4], axis=1, keepdims=True)) + lam_init)
            od = o[0:TM] - lam * o[TM:rows]
            od = od * lax.rsqrt(jnp.mean(od * od, axis=-1, keepdims=True) + EPS)
            out = (od * gout_ref[...]) * (1.0 - lam_init)
        else:
            out = jnp.where(low, o[0:TM], o[TM:rows])
        o_ref[pl.ds(r0, TM), :] = out.astype(o_ref.dtype)

    load_lhs(0)
    ctx_keys = slice(0, n_ctx)
    softmax_pv(scores(ctx_keys), ctx_keys, True)
    finish(0)

    def put(c):
        keys = chunk_keys(c)
        bufs[c % 2][:, 0:keys.stop - keys.start] = scores(keys)

    def take(c):
        keys = chunk_keys(c)
        softmax_pv(bufs[c % 2][:, 0:keys.stop - keys.start], keys, c == 0)

    def block(jb, carry):
        r0 = pl.multiple_of(jb * TM, TM)
        load_lhs(r0)

        @pl.when(jb > 0)
        def _():
            put(0)
            for c in range(n_chunks):
                if c + 1 < n_chunks:
                    put(c + 1)
                take(c)

        finish(r0)
        return carry

    lax.fori_loop(1, n_blocks, block, 0)


def _attention(q, kt, v, batch, nblk, *, diff, lam_vecs=None, g_out=None, lam_init=0.0):
    kv_heads = kt.shape[1]
    s_len = kt.shape[3]
    tiles = BRANCH_WIDTH // LANES
    rows = 2 * TM
    q_map = lambda b, h: (b, h)
    kv_map = lambda b, h: (b, h * kv_heads // tiles, 0, 0)
    const = lambda b, h: (0, 0)
    in_specs = [pl.BlockSpec((s_len, LANES), q_map),
                pl.BlockSpec((1, 1, LANES, s_len), kv_map),
                pl.BlockSpec((1, 1, s_len, LANES), kv_map)]
    args = [q, kt, v]
    if diff:
        in_specs += [pl.BlockSpec((4, HEAD_DIM), const), pl.BlockSpec((1, LANES), const)]
        args += [lam_vecs, g_out]
    n_ctx = TM
    tkl = min(SCORE_KEYS, s_len - n_ctx)
    assert (s_len - n_ctx) % tkl == 0
    bounds = [0] + list(range(n_ctx + tkl, s_len + 1, tkl))
    kern = functools.partial(_attn_kernel, diff=diff, lam_init=lam_init,
                             n_ctx=n_ctx, bounds=tuple(bounds), n_blocks=nblk)
    return pl.pallas_call(
        kern,
        grid=(batch, tiles),
        in_specs=in_specs,
        out_specs=pl.BlockSpec((s_len, LANES), q_map),
        out_shape=jax.ShapeDtypeStruct((batch * s_len, BRANCH_WIDTH), BF16),
        scratch_shapes=([pltpu.VMEM((rows, LANES), BF16)]
                        + [pltpu.VMEM((rows, LANES), F32)] * 3
                        + [pltpu.VMEM((rows, n_ctx + tkl), F32)] * 2),
        compiler_params=_cparams(2),
        name="diff_attn" if diff else "gqa_attn",
    )(*args)


def _s5_kernel(uf_ref, ub_ref, pin_ref, pout_ref, wd_ref, a_ref, cm_ref, yf_ref, yb_ref,
               bu_sc, st_sc):
    n_tiles = BRANCH_WIDTH // LANES
    half = SUBLANES * S5_STATE
    rows = TC * SUBLANES
    tok = (SUBLANES // 2) * TC

    @pl.when(pl.program_id(0) == 0)
    def _():
        st_sc[...] = jnp.zeros_like(st_sc)

    seq = lax.broadcasted_iota(jnp.int32, (rows, LANES), 0) & (SUBLANES - 1)
    fwd = seq < SUBLANES // 2
    lhs_dir = []
    for d, ref in enumerate((uf_ref, ub_ref)):
        ud = ref[...].reshape(tok, BRANCH_WIDTH).astype(BF16)
        lhs_dir.append(jnp.dot(pin_ref[d], ud, preferred_element_type=F32).astype(BF16))
    def drive(j):
        cs = slice(j * LANES, (j + 1) * LANES)
        lhs = jnp.concatenate([lhs_dir[0][:, cs], lhs_dir[1][:, cs]], axis=1)
        bu_sc[:, 2 * half * j:2 * half * (j + 1)] = jnp.dot(
            lhs, wd_ref[j], preferred_element_type=F32)

    def scan(j):
        re = slice(2 * half * j, 2 * half * j + half)
        im = slice(2 * half * j + half, 2 * half * (j + 1))
        ar = a_ref[0, :, half * j:half * (j + 1)]
        ai = a_ref[1, :, half * j:half * (j + 1)]

        def step(t, carry):
            xr, xi = carry
            r = pl.multiple_of(t * SUBLANES, SUBLANES)
            nxr = ar * xr - ai * xi + bu_sc[pl.ds(r, SUBLANES), re]
            nxi = ar * xi + ai * xr + bu_sc[pl.ds(r, SUBLANES), im]
            bu_sc[pl.ds(r, SUBLANES), re] = nxr
            bu_sc[pl.ds(r, SUBLANES), im] = nxi
            return nxr, nxi

        xr, xi = lax.fori_loop(0, TC, step, (st_sc[:, re], st_sc[:, im]), unroll=True)
        st_sc[:, re] = xr
        st_sc[:, im] = xi

    def readout(j):
        x = bu_sc[:, 2 * half * j:2 * half * (j + 1)].astype(BF16)
        yy = jnp.dot(x, cm_ref[j], preferred_element_type=F32)
        return jnp.where(fwd, yy[:, :LANES], yy[:, LANES:])

    ys = []
    drive(0)
    for j in range(n_tiles):
        if j + 1 < n_tiles:
            drive(j + 1)
        scan(j)
        ys.append(readout(j))
    y = jnp.concatenate(ys, axis=1)
    parts = []
    rest = y
    for _ in range(2):
        part = rest.astype(BF16)
        parts.append(part)
        rest = rest - part.astype(F32)
    stacked = jnp.concatenate(parts, axis=0)
    for d, ref in enumerate((yf_ref, yb_ref)):
        out = jnp.dot(pout_ref[d], stacked, preferred_element_type=F32)
        ref[...] = out.reshape(ref.shape)


def _scan_row_placement(n_seq):
    p = np.zeros((2, TC * 2 * n_seq, n_seq * TC), np.float32)
    for b in range(n_seq):
        for k in range(TC):
            p[0, 2 * n_seq * k + b, b * TC + k] = 1.0
            p[1, 2 * n_seq * (TC - 1 - k) + n_seq + b, b * TC + k] = 1.0
    pt = p.transpose(0, 2, 1)
    return jnp.asarray(p, BF16), jnp.asarray(np.concatenate([pt, pt], axis=2), BF16)


def _s5_scan(u3, wd, a8, cm, n_ctx):
    batch, s_len, _ = u3.shape
    rows = TC * SUBLANES
    n_tiles = BRANCH_WIDTH // LANES
    n_state = 2 * SUBLANES * S5_STATE * n_tiles
    n_steps = s_len // TC
    ctx_steps = n_ctx // TC
    p_in, p_out = _scan_row_placement(batch)
    fwd_map = lambda g: (0, g, 0)
    bwd_map = lambda g: (0, jnp.where(g < ctx_steps, ctx_steps - 1 - g,
                                      n_steps + ctx_steps - 1 - g), 0)
    c3 = lambda g: (0, 0, 0)
    blk = (batch, TC, BRANCH_WIDTH)
    out = jax.ShapeDtypeStruct(u3.shape, F32)
    return pl.pallas_call(
        _s5_kernel,
        grid=(n_steps,),
        in_specs=[pl.BlockSpec(blk, fwd_map),
                  pl.BlockSpec(blk, bwd_map),
                  pl.BlockSpec(p_in.shape, c3),
                  pl.BlockSpec(p_out.shape, c3),
                  pl.BlockSpec(wd.shape, c3),
                  pl.BlockSpec(a8.shape, c3),
                  pl.BlockSpec(cm.shape, c3)],
        out_specs=[pl.BlockSpec(blk, fwd_map), pl.BlockSpec(blk, bwd_map)],
        out_shape=[out, out],
        scratch_shapes=[pltpu.VMEM((rows, n_state), F32),
                        pltpu.VMEM((SUBLANES, n_state), F32)],
        compiler_params=_cparams(1),
        name="s5_scan",
    )(u3, u3, p_in, p_out, wd, a8, cm)


def _zoh(lam_re, lam_im, log_dt, b_re, b_im):
    dt = jnp.exp(log_dt)[..., None]
    mag = jnp.exp(lam_re * dt)
    a_re = mag * jnp.cos(lam_im * dt)
    a_im = mag * jnp.sin(lam_im * dt)
    den = lam_re * lam_re + lam_im * lam_im
    f_re = ((a_re - 1.0) * lam_re + a_im * lam_im) / den
    f_im = (a_im * lam_re - (a_re - 1.0) * lam_im) / den
    bb_re = f_re[..., None] * b_re - f_im[..., None] * b_im
    bb_im = f_re[..., None] * b_im + f_im[..., None] * b_re
    return a_re, a_im, bb_re, bb_im


def _s5_weights(lam_re, lam_im, log_dt, b_re, b_im, c_re, c_im, batch):
    n_tiles = BRANCH_WIDTH // LANES
    gpt = S5_GROUPS // n_tiles
    a_re, a_im, bb_re, bb_im = _zoh(lam_re, lam_im, log_dt, b_re, b_im)
    eye = jnp.eye(gpt, dtype=F32)

    def drive(bb):
        t = bb.reshape(2, n_tiles, gpt, S5_STATE, S5_CH)
        w = jnp.einsum('djgpc,gh->jdgchp', t, eye)
        return w.reshape(n_tiles, 2 * gpt * S5_CH, gpt * S5_STATE)

    def read(cc):
        t = cc.reshape(2, n_tiles, gpt, S5_CH, S5_STATE)
        w = jnp.einsum('djgcp,gh->jgpdhc', t, eye)
        return w.reshape(n_tiles, gpt * S5_STATE, 2 * gpt * S5_CH)

    wd = jnp.concatenate([drive(bb_re), drive(bb_im)], axis=2).astype(BF16)
    cm = jnp.concatenate([read(c_re), read(-c_im)], axis=1).astype(BF16)

    def per_seq(a):
        return jnp.repeat(a.reshape(2, 1, S5_GROUPS * S5_STATE), batch, axis=1).reshape(
            2 * batch, S5_GROUPS * S5_STATE)

    a8 = jnp.stack([per_seq(a_re), per_seq(a_im)])
    return wd, a8, cm


def _merge_kernel(x_ref, xc_ref, mod_ref, g1_ref, g2_ref, ya_ref, yd_ref, yf_ref, yr_ref, u_ref,
                  dsk_ref, wglu_ref, bglu_ref, wgate_ref, wbr_ref, wout_ref, x1_ref, h2_ref,
                  *, nblk):
    mods = mod_ref[0]
    proj = {k: jnp.dot(ref[...], wbr_ref[k], preferred_element_type=F32)
            for k, ref in ((0, ya_ref), (2, yd_ref))}
    x = _stream_tile(x_ref, xc_ref, nblk)
    h = _rms_mod(x, g1_ref[...], mods[0:1], mods[1:2]).astype(BF16)
    ys = yf_ref[...] + yr_ref[...] + dsk_ref[...] * u_ref[...]
    g = jax.nn.gelu(ys)
    yb = g * jax.nn.sigmoid(
        jnp.dot(g.astype(BF16), wglu_ref[...], preferred_element_type=F32) + bglu_ref[...])
    proj[1] = jnp.dot(yb.astype(BF16), wbr_ref[1], preferred_element_type=F32)
    m = None
    for k in range(3):
        gate = jax.nn.sigmoid(jnp.dot(h, wgate_ref[:, k * D_MODEL:(k + 1) * D_MODEL],
                                      preferred_element_type=F32))
        term = gate * proj[k]
        m = term if m is None else m + term
    y = jnp.dot(m.astype(BF16), wout_ref[...], preferred_element_type=F32)
    x1 = x + mods[2:3] * y
    x1_ref[...] = x1
    h2_ref[...] = _rms_mod(x1, g2_ref[...], mods[3:4], mods[4:5]).astype(BF16)


def _merge(x_lat, x_ctx, mods, g1, g2, ya, yd, yf, yr, u, d_skip, w_glu, b_glu, w_gate,
           w_branch, w_out, batch, nblk):
    t_rows = ya.shape[0]
    row = lambda i: (i, 0)
    mod_map = lambda i: (jnp.where(i % nblk == 0, batch, i // nblk), 0, 0)
    c2 = lambda i: (0, 0)
    c3 = lambda i: (0, 0, 0)
    x_specs, x_args = _stream_specs(x_lat, x_ctx, nblk)
    return pl.pallas_call(
        functools.partial(_merge_kernel, nblk=nblk),
        grid=(t_rows // TM,),
        in_specs=x_specs + [
                  pl.BlockSpec((1, 6, D_MODEL), mod_map),
                  pl.BlockSpec((1, D_MODEL), c2),
                  pl.BlockSpec((1, D_MODEL), c2),
                  pl.BlockSpec((TM, BRANCH_WIDTH), row),
                  pl.BlockSpec((TM, BRANCH_WIDTH), row),
                  pl.BlockSpec((TM, BRANCH_WIDTH), row),
                  pl.BlockSpec((TM, BRANCH_WIDTH), row),
                  pl.BlockSpec((TM, BRANCH_WIDTH), row),
                  pl.BlockSpec((1, BRANCH_WIDTH), c2),
                  pl.BlockSpec((BRANCH_WIDTH, BRANCH_WIDTH), c2),
                  pl.BlockSpec((1, BRANCH_WIDTH), c2),
                  pl.BlockSpec((D_MODEL, 3 * D_MODEL), c2),
                  pl.BlockSpec((3, BRANCH_WIDTH, D_MODEL), c3),
                  pl.BlockSpec((D_MODEL, D_MODEL), c2)],
        out_specs=[pl.BlockSpec((TM, D_MODEL), row), pl.BlockSpec((TM, D_MODEL), row)],
        out_shape=[jax.ShapeDtypeStruct((t_rows, D_MODEL), F32),
                   jax.ShapeDtypeStruct((t_rows, D_MODEL), BF16)],
        compiler_params=_cparams(1),
        name="merge",
    )(*x_args, mods, g1, g2, ya, yd, yf, yr, u, d_skip, w_glu, b_glu, w_gate, w_branch, w_out)


def _ffn_kernel(h_ref, hp_ref, hn_ref, x1_ref, mod_ref, perm_ref, wup_ref, cw_ref, wdn_ref, o_ref,
                lhs_sc, u0_sc, u1_sc, act_sc, *, nblk, n_chunks):
    pj = pl.program_id(0) % nblk
    left_ok = (pj >= 2).astype(F32)
    right_ok = jnp.logical_and(pj != 0, pj != nblk - 1).astype(F32)
    lhs_sc[0:TM, :] = jnp.dot(perm_ref[0], h_ref[...], preferred_element_type=F32).astype(BF16)
    hrow = lax.broadcasted_iota(jnp.int32, (HALO, D_MODEL), 0)
    halo = (jnp.where(hrow == HALO - 1, hp_ref[...].astype(F32) * left_ok, 0.0)
            + jnp.where(hrow == 0, hn_ref[...].astype(F32) * right_ok, 0.0))
    lhs_sc[TM:TM + HALO, :] = halo.astype(BF16)
    sub = lax.broadcasted_iota(jnp.int32, (SUBLANES, 2 * FF_CHUNK), 0)

    def up(k, buf):
        buf[...] = jnp.dot(lhs_sc[...], wup_ref[k], preferred_element_type=F32)

    def activate(k, buf):
        cw = cw_ref[k]
        before = buf[TM + HALO - 1:TM + HALO, :]
        after = buf[TM:TM + 1, :]
        first = jnp.where(sub == 0, before, pltpu.roll(buf[TM - SUBLANES:TM, :], 1, 0))
        last = jnp.where(sub == SUBLANES - 1, after,
                         pltpu.roll(buf[0:SUBLANES, :], SUBLANES - 1, 0))
        prev = jnp.concatenate([first, buf[0:TM - SUBLANES, :]], axis=0)
        nxt = jnp.concatenate([buf[SUBLANES:TM, :], last], axis=0)
        c = prev * cw[0:1] + buf[0:TM, :] * cw[1:2] + nxt * cw[2:3] + cw[3:4]
        act = jax.nn.silu(c[:, FF_CHUNK:]) * c[:, :FF_CHUNK]
        act_sc[:, k * FF_CHUNK:(k + 1) * FF_CHUNK] = act.astype(BF16)

    bufs = (u0_sc, u1_sc)
    up(0, bufs[0])
    for k in range(n_chunks):
        if k + 1 < n_chunks:
            up(k + 1, bufs[(k + 1) % 2])
        activate(k, bufs[k % 2])
    act = jnp.dot(perm_ref[1], act_sc[...], preferred_element_type=F32).astype(BF16)
    y = jnp.dot(act, wdn_ref[...], preferred_element_type=F32)
    o_ref[...] = x1_ref[...] + mod_ref[0][5:6] * y


def _strided_row_order():
    p = np.zeros((TM, TM), np.float32)
    groups = TM // SUBLANES
    for r in range(groups):
        for s in range(SUBLANES):
            p[SUBLANES * r + s, groups * s + r] = 1.0
    return jnp.asarray(np.stack([p, p.T]), BF16)


def _ffn(h2, x1, mods, wup, cw, wdn, batch, nblk, latent_only):
    t_rows = x1.shape[0]
    n_chunks = wup.shape[0]
    per = TM // HALO
    last = t_rows // HALO - 1
    row = lambda i: (i, 0)
    mod_map = lambda i: (jnp.where(i % nblk == 0, batch, i // nblk), 0, 0)
    kern = functools.partial(_ffn_kernel, nblk=nblk, n_chunks=n_chunks)
    if latent_only:
        out_map = lambda i: ((i // nblk) * (nblk - 1) + jnp.maximum(i % nblk - 1, 0), 0)
        out_rows = t_rows - batch * TM
    else:
        out_map, out_rows = row, t_rows
    return pl.pallas_call(
        kern,
        grid=(t_rows // TM,),
        in_specs=[pl.BlockSpec((TM, D_MODEL), row),
                  pl.BlockSpec((HALO, D_MODEL), lambda i: (jnp.maximum(i * per - 1, 0), 0)),
                  pl.BlockSpec((HALO, D_MODEL), lambda i: (jnp.minimum((i + 1) * per, last), 0)),
                  pl.BlockSpec((TM, D_MODEL), row),
                  pl.BlockSpec((1, 6, D_MODEL), mod_map),
                  pl.BlockSpec((2, TM, TM), lambda i: (0, 0, 0)),
                  pl.BlockSpec(wup.shape, lambda i: (0, 0, 0)),
                  pl.BlockSpec(cw.shape, lambda i: (0, 0, 0)),
                  pl.BlockSpec(wdn.shape, lambda i: (0, 0))],
        out_specs=pl.BlockSpec((TM, D_MODEL), out_map),
        out_shape=jax.ShapeDtypeStruct((out_rows, D_MODEL), F32),
        scratch_shapes=([pltpu.VMEM((TM + HALO, D_MODEL), BF16)]
                        + [pltpu.VMEM((TM + HALO, 2 * FF_CHUNK), F32)] * 2
                        + [pltpu.VMEM((TM, D_FF), BF16)]),
        compiler_params=_cparams(1),
        name="conv_ffn",
    )(h2, h2, h2, x1, mods, _strided_row_order(), wup, cw, wdn)


def _relayout_kernel(a_ref, g_ref, o_ref):
    o_ref[0] = jnp.concatenate([a_ref[0], g_ref[0]], axis=1).astype(o_ref.dtype)


def _chunk_major_bf16(w_up, layer, n_chunks):
    rows = w_up.shape[1]
    return pl.pallas_call(
        _relayout_kernel,
        grid=(n_chunks,),
        in_specs=[pl.BlockSpec((1, rows, FF_CHUNK), lambda k: (layer, 0, k)),
                  pl.BlockSpec((1, rows, FF_CHUNK), lambda k: (layer, 0, n_chunks + k))],
        out_specs=pl.BlockSpec((1, rows, 2 * FF_CHUNK), lambda k: (k, 0, 0)),
        out_shape=jax.ShapeDtypeStruct((n_chunks, rows, 2 * FF_CHUNK), BF16),
        compiler_params=_cparams(1),
        name="ffn_weight_layout",
    )(w_up, w_up)


def _cast_kernel(w_ref, o_ref):
    o_ref[...] = w_ref[0].astype(o_ref.dtype)


def _cast_cols_bf16(w, layer, first_col, n_cols):
    rows = w.shape[1]
    blk = 2 * LANES
    assert first_col % blk == 0 and n_cols % blk == 0
    return pl.pallas_call(
        _cast_kernel,
        grid=(n_cols // blk,),
        in_specs=[pl.BlockSpec((1, rows, blk), lambda k: (layer, 0, first_col // blk + k))],
        out_specs=pl.BlockSpec((rows, blk), lambda k: (0, k)),
        out_shape=jax.ShapeDtypeStruct((rows, n_cols), BF16),
        compiler_params=_cparams(1),
        name="weight_cast",
    )(w)


def _rope_tables(seq, ctx):
    rows = seq // GRID_W
    row = jnp.repeat(jnp.arange(rows, dtype=F32), GRID_W)
    col = jnp.tile(jnp.arange(GRID_W, dtype=F32), rows)
    n_freq = HEAD_DIM // 4
    inv_freq = ROPE_BASE ** (-jnp.arange(n_freq, dtype=F32) / n_freq)
    ang = jnp.concatenate([row[:, None] * inv_freq, col[:, None] * inv_freq], axis=-1)
    ang = jnp.concatenate([ang, ang], axis=-1)
    cos = jnp.concatenate([jnp.ones((ctx, HEAD_DIM), F32), jnp.cos(ang)], axis=0)
    sin = jnp.concatenate([jnp.zeros((ctx, HEAD_DIM), F32), jnp.sin(ang)], axis=0)
    sign = jnp.where(jnp.arange(HEAD_DIM) < HEAD_DIM // 2, -1.0, 1.0).astype(F32)
    two = lambda t: jnp.concatenate([t, t], axis=1)
    return two(cos), two(sin * sign)


def kernel(x, c, ctx, c_ctx, w_ada, b_ada, norm_g, w_in, qk_gain, ssm_lam_re, ssm_lam_im, ssm_log_dt, ssm_b_re, ssm_b_im, ssm_c_re, ssm_c_im, ssm_d, w_glu, b_glu, diff_lam, diff_norm_g, w_branch, w_out, w_up, conv_w, conv_b, w_down):
    batch, seq, _ = x.shape
    n_ctx = ctx.shape[1]
    depth = w_in.shape[0]
    assert n_ctx == TM and seq % TM == 0 and 2 * batch == SUBLANES
    s_len = n_ctx + seq
    nblk = s_len // TM
    assert s_len % TC == 0 and D_FF % FF_CHUNK == 0
    n_chunks = D_FF // FF_CHUNK

    cos_t, sin_t = _rope_tables(seq, n_ctx)
    cc = jnp.concatenate([c, c_ctx[None, :], jnp.zeros((SUBLANES - batch - 1, D_MODEL), F32)], axis=0)
    mods_all = _ada_mods(cc, w_ada, b_ada).reshape(depth, SUBLANES, 6, D_MODEL)

    xa = x.reshape(batch * seq, D_MODEL)
    xc = ctx.reshape(batch * n_ctx, D_MODEL)
    for i in range(depth):
        lam_init = 0.8 - 0.6 * math.exp(-0.3 * i)
        mods = mods_all[i]
        g1 = norm_g[i, 0:1]
        g2 = norm_g[i, 1:2]
        w_qkvu = _cast_cols_bf16(w_in, i, 0, N_QKVU)
        w_gate = _cast_cols_bf16(w_in, i, N_QKVU, 3 * D_MODEL)
        qa, kta, va, u, qc, ktc, vc = _inproj(xa, xc, mods, g1, w_qkvu,
                                              jnp.tile(qk_gain[i], (1, BRANCH_WIDTH // HEAD_DIM)),
                                              cos_t, sin_t, batch, nblk)
        ya = _attention(qa, kta, va, batch, nblk, diff=False)
        yd = _attention(qc, ktc, vc, batch, nblk, diff=True, lam_vecs=diff_lam[i],
                        g_out=diff_norm_g[i][None, :], lam_init=lam_init)

        wd, a8, cm = _s5_weights(ssm_lam_re[i], ssm_lam_im[i], ssm_log_dt[i], ssm_b_re[i],
                                 ssm_b_im[i], ssm_c_re[i], ssm_c_im[i], batch)
        yf, yr = _s5_scan(u.reshape(batch, s_len, BRANCH_WIDTH), wd, a8, cm, n_ctx)
        yf = yf.reshape(batch * s_len, BRANCH_WIDTH)
        yr = yr.reshape(batch * s_len, BRANCH_WIDTH)

        x1, h2 = _merge(xa, xc, mods, g1, g2, ya, yd, yf, yr, u, ssm_d[i][None, :],
                        w_glu[i].astype(BF16), b_glu[i][None, :], w_gate,
                        w_branch[i].astype(BF16), w_out[i].astype(BF16), batch, nblk)

        def chunked(t):
            a = t[:, :D_FF].reshape(t.shape[0], n_chunks, FF_CHUNK)
            g = t[:, D_FF:].reshape(t.shape[0], n_chunks, FF_CHUNK)
            return jnp.concatenate([a, g], axis=-1).transpose(1, 0, 2)

        wup = _chunk_major_bf16(w_up, i, n_chunks)
        cw = chunked(jnp.concatenate(
            [conv_w[i], conv_b[i][None, :], jnp.zeros((SUBLANES - 4, 2 * D_FF), F32)], axis=0))
        wdn = w_down[i].astype(BF16)
        xa = _ffn(h2, x1, mods, wup, cw, wdn, batch, nblk, latent_only=(i == depth - 1))
        xc = None

    return xa.reshape(batch, seq, D_MODEL)
```

```python
import functools
import math

import jax
import jax.numpy as jnp
import numpy as np
from jax import lax
from jax.experimental import pallas as pl
from jax.experimental.pallas import tpu as pltpu

F32 = jnp.float32
BF16 = jnp.bfloat16

D_MODEL = 1024
HEAD_DIM = 64
GRID_W = 64
ROPE_BASE = 10000.0
EPS = 1e-6
BRANCH_WIDTH = D_MODEL // 2
A_HEADS = BRANCH_WIDTH // HEAD_DIM
A_KV_HEADS = A_HEADS // 4
C_HEADS = BRANCH_WIDTH // (2 * HEAD_DIM)
S5_CH = 16
S5_STATE = 64
S5_GROUPS = BRANCH_WIDTH // S5_CH
D_FF = 2816
IN_SIZES = (512, 128, 128, 512, 512, 512, 512, 3 * D_MODEL)
IN_OFFS = tuple(int(v) for v in np.cumsum((0,) + IN_SIZES))
N_QKVU = IN_OFFS[7]

LANES = 128
SUBLANES = 8
TM = 256
SCORE_KEYS = 2048
TC = 64
FF_CHUNK = 128
HALO = 16
ADA_COLS = 1536
V7X_VMEM_BYTES = 64 * 1024 * 1024
VMEM_LIMIT = V7X_VMEM_BYTES * 7 // 8
SCORE_SCALE = HEAD_DIM ** -0.5 * math.log2(math.e)


def _cparams(n_axes):
    return pltpu.CompilerParams(dimension_semantics=("arbitrary",) * n_axes,
                                vmem_limit_bytes=VMEM_LIMIT)


def _rms_mod(x, gain, shift, scale):
    y = x * lax.rsqrt(jnp.mean(x * x, axis=-1, keepdims=True) + EPS)
    return (y * gain) * (1.0 + scale) + shift


def _head_avg_matrix(width):
    shift = HEAD_DIM.bit_length() - 1
    r = lax.broadcasted_iota(jnp.int32, (width, width), 0) >> shift
    c = lax.broadcasted_iota(jnp.int32, (width, width), 1) >> shift
    return jnp.where(r == c, 1.0 / HEAD_DIM, 0.0).astype(BF16)


def _head_rms(z, gain, avg):
    sq = z * z
    hi = sq.astype(BF16)
    lo = (sq - hi.astype(F32)).astype(BF16)
    ms = (jnp.dot(hi, avg, preferred_element_type=F32)
          + jnp.dot(lo, avg, preferred_element_type=F32))
    return z * lax.rsqrt(ms + EPS) * gain


def _tile_lanes(t, width):
    reps = width // t.shape[1]
    return t if reps == 1 else jnp.concatenate([t] * reps, axis=1)


def _rope(z, cos, sin_signed):
    width = z.shape[1]
    lane = lax.broadcasted_iota(jnp.int32, z.shape, 1)
    first_half = (lane & (HEAD_DIM - 1)) < HEAD_DIM // 2
    rot = jnp.where(first_half,
                    pltpu.roll(z, width - HEAD_DIM // 2, 1),
                    pltpu.roll(z, HEAD_DIM // 2, 1))
    return z * _tile_lanes(cos, width) + rot * _tile_lanes(sin_signed, width)


def _dup_halves(z):
    lane = lax.broadcasted_iota(jnp.int32, z.shape, 1)
    low = lane < HEAD_DIM
    sw = pltpu.roll(z, HEAD_DIM, 1)
    return jnp.where(low, z, sw), jnp.where(low, sw, z)


def _ada_kernel(c_ref, w_ref, b_ref, o_ref):
    a = jax.nn.silu(c_ref[...])
    o_ref[0] = jnp.dot(a, w_ref[0], preferred_element_type=F32,
                       precision=lax.Precision.HIGHEST) + b_ref[0]


def _ada_mods(cc, w_ada, b_ada):
    depth, _, n = w_ada.shape
    tn = ADA_COLS
    assert n % tn == 0
    return pl.pallas_call(
        _ada_kernel,
        grid=(depth, n // tn),
        in_specs=[pl.BlockSpec((SUBLANES, D_MODEL), lambda l, j: (0, 0)),
                  pl.BlockSpec((1, D_MODEL, tn), lambda l, j: (l, 0, j)),
                  pl.BlockSpec((1, 1, tn), lambda l, j: (l, 0, j))],
        out_specs=pl.BlockSpec((1, SUBLANES, tn), lambda l, j: (l, 0, j)),
        out_shape=jax.ShapeDtypeStruct((depth, SUBLANES, n), F32),
        compiler_params=_cparams(2),
        name="ada_mods",
    )(cc, w_ada, b_ada.reshape(depth, 1, n))


def _stream_specs(x_lat, x_ctx, nblk):
    blk = (TM, D_MODEL)
    if x_ctx is None:
        return ([pl.BlockSpec(blk, lambda i: (i, 0)),
                 pl.BlockSpec(blk, lambda i: ((i // nblk) * nblk, 0))], [x_lat, x_lat])
    lat_map = lambda i: ((i // nblk) * (nblk - 1) + jnp.maximum(i % nblk - 1, 0), 0)
    return ([pl.BlockSpec(blk, lat_map), pl.BlockSpec(blk, lambda i: (i // nblk, 0))],
            [x_lat, x_ctx])


def _stream_tile(x_ref, xc_ref, nblk):
    return jnp.where(pl.program_id(0) % nblk == 0, xc_ref[...], x_ref[...])


def _inproj_kernel(x_ref, xc_ref, mod_ref, g_ref, w_ref, qk_ref, cos_ref, sin_ref,
                   qa_ref, kta_ref, va_ref, u_ref, qc_ref, ktc_ref, vc_ref, *, nblk):
    mods = mod_ref[0]
    x = _stream_tile(x_ref, xc_ref, nblk)
    h = _rms_mod(x, g_ref[...], mods[0:1], mods[1:2]).astype(BF16)
    cos = cos_ref[...]
    sin = sin_ref[...]
    avg = _head_avg_matrix(BRANCH_WIDTH)

    def seg(k):
        return jnp.dot(h, w_ref[:, IN_OFFS[k]:IN_OFFS[k + 1]], preferred_element_type=F32)

    def gain(k, width=BRANCH_WIDTH):
        return qk_ref[k:k + 1, :width]

    z5 = seg(5)
    z1 = seg(1)
    z4 = seg(4)
    k = _rope(_head_rms(z5, gain(3), avg), cos, sin)
    for hh in range(C_HEADS):
        ktc_ref[0, hh] = k[:, hh * LANES:(hh + 1) * LANES].T.astype(BF16)
    z0 = seg(0)
    k = _rope(_head_rms(z1, gain(1, LANES), avg[:LANES, :LANES]), cos, sin)
    for hh, kd in enumerate(_dup_halves(k)):
        kta_ref[0, hh] = kd.T.astype(BF16)
    z2 = seg(2)
    q = _rope(_head_rms(z4, gain(2), avg), cos, sin)
    qc_ref[...] = (q * SCORE_SCALE).astype(BF16)
    q = _rope(_head_rms(z0, gain(0), avg), cos, sin)
    qa_ref[...] = (q * SCORE_SCALE).astype(BF16)
    for hh, vd in enumerate(_dup_halves(z2)):
        va_ref[0, hh] = vd.astype(BF16)
    v = seg(6)
    for hh in range(C_HEADS):
        vc_ref[0, hh] = v[:, hh * LANES:(hh + 1) * LANES].astype(BF16)
    u_ref[...] = seg(3)


def _inproj(x_lat, x_ctx, mods, norm_g, w_in, qk_gain, cos_t, sin_t, batch, nblk):
    s_len = nblk * TM
    t_rows = batch * s_len
    row = lambda i: (i, 0)
    mod_map = lambda i: (jnp.where(i % nblk == 0, batch, i // nblk), 0, 0)
    pos = lambda i: (i % nblk, 0)
    kt_map = lambda i: (i // nblk, 0, 0, i % nblk)
    v_map = lambda i: (i // nblk, 0, i % nblk, 0)
    x_specs, x_args = _stream_specs(x_lat, x_ctx, nblk)
    return pl.pallas_call(
        functools.partial(_inproj_kernel, nblk=nblk),
        grid=(t_rows // TM,),
        in_specs=x_specs + [
                  pl.BlockSpec((1, 6, D_MODEL), mod_map),
                  pl.BlockSpec((1, D_MODEL), lambda i: (0, 0)),
                  pl.BlockSpec((D_MODEL, N_QKVU), lambda i: (0, 0)),
                  pl.BlockSpec((4, BRANCH_WIDTH), lambda i: (0, 0)),
                  pl.BlockSpec((TM, LANES), pos),
                  pl.BlockSpec((TM, LANES), pos)],
        out_specs=[pl.BlockSpec((TM, BRANCH_WIDTH), row),
                   pl.BlockSpec((1, A_KV_HEADS, LANES, TM), kt_map),
                   pl.BlockSpec((1, A_KV_HEADS, TM, LANES), v_map),
                   pl.BlockSpec((TM, BRANCH_WIDTH), row),
                   pl.BlockSpec((TM, BRANCH_WIDTH), row),
                   pl.BlockSpec((1, C_HEADS, LANES, TM), kt_map),
                   pl.BlockSpec((1, C_HEADS, TM, LANES), v_map)],
        out_shape=[jax.ShapeDtypeStruct((t_rows, BRANCH_WIDTH), BF16),
                   jax.ShapeDtypeStruct((batch, A_KV_HEADS, LANES, s_len), BF16),
                   jax.ShapeDtypeStruct((batch, A_KV_HEADS, s_len, LANES), BF16),
                   jax.ShapeDtypeStruct((t_rows, BRANCH_WIDTH), F32),
                   jax.ShapeDtypeStruct((t_rows, BRANCH_WIDTH), BF16),
                   jax.ShapeDtypeStruct((batch, C_HEADS, LANES, s_len), BF16),
                   jax.ShapeDtypeStruct((batch, C_HEADS, s_len, LANES), BF16)],
        compiler_params=_cparams(1),
        name="in_proj",
    )(*x_args, mods, norm_g, w_in, qk_gain, cos_t, sin_t)


def _attn_kernel(*refs, diff, lam_init, n_ctx, bounds, n_blocks):
    if diff:
        q_ref, kt_ref, v_ref, lam_ref, gout_ref, o_ref = refs[:6]
    else:
        q_ref, kt_ref, v_ref, o_ref = refs[:4]
    lhs_sc, m_sc, l_sc, acc_sc, s0_sc, s1_sc = refs[-6:]
    bufs = (s0_sc, s1_sc)
    n_chunks = len(bounds) - 1
    rows = 2 * TM
    lane = lax.broadcasted_iota(jnp.int32, (TM, LANES), 1)
    low = lane < HEAD_DIM

    def load_lhs(r0):
        qt = q_ref[pl.ds(r0, TM), :].astype(F32)
        lhs_sc[0:TM, :] = jnp.where(low, qt, 0.0).astype(BF16)
        lhs_sc[TM:rows, :] = jnp.where(low, 0.0, qt).astype(BF16)

    def chunk_keys(c):
        return slice(bounds[c], bounds[c + 1])

    def scores(keys):
        return jnp.dot(lhs_sc[...], kt_ref[0, 0, :, keys], preferred_element_type=F32)

    def softmax_pv(s, keys, first):
        width = s.shape[1]
        mx = s[:, 0:LANES]
        for t in range(1, width // LANES):
            mx = jnp.maximum(mx, s[:, t * LANES:(t + 1) * LANES])
        m_cur = jnp.max(mx, axis=1, keepdims=True)
        if first:
            m_next = jnp.broadcast_to(m_cur, (rows, LANES))
        else:
            m_prev = m_sc[...]
            m_next = jnp.maximum(m_prev, m_cur)
            alpha = jnp.exp2(m_prev - m_next)
        p = jnp.exp2(s - _tile_lanes(m_next, width))
        psum = p[:, 0:LANES]
        for t in range(1, width // LANES):
            psum = psum + p[:, t * LANES:(t + 1) * LANES]
        pv = jnp.dot(p.astype(BF16), v_ref[0, 0, keys, :], preferred_element_type=F32)
        if first:
            l_sc[...] = psum
            acc_sc[...] = pv
        else:
            l_sc[...] = alpha * l_sc[...] + psum
            acc_sc[...] = alpha * acc_sc[...] + pv
        m_sc[...] = m_next

    def finish(r0):
        o = acc_sc[...] / jnp.sum(l_sc[...], axis=1, keepdims=True)
        if diff:
            lv = lam_ref[...]
            lam = (jnp.exp(jnp.sum(lv[0:1] * lv[1:2], axis=1, keepdims=True))
                   - jnp.exp(jnp.sum(lv[2:3] * lv[3:4], axis=1, keepdims=True)) + lam_init)
            od = o[0:TM] - lam * o[TM:rows]
            od = od * lax.rsqrt(jnp.mean(od * od, axis=-1, keepdims=True) + EPS)
            out = (od * gout_ref[...]) * (1.0 - lam_init)
        else:
            out = jnp.where(low, o[0:TM], o[TM:rows])
        o_ref[pl.ds(r0, TM), :] = out.astype(o_ref.dtype)

    load_lhs(0)
    ctx_keys = slice(0, n_ctx)
    softmax_pv(scores(ctx_keys), ctx_keys, True)
    finish(0)

    def put(c):
        keys = chunk_keys(c)
        bufs[c % 2][:, 0:keys.stop - keys.start] = scores(keys)

    def take(c):
        keys = chunk_keys(c)
        softmax_pv(bufs[c % 2][:, 0:keys.stop - keys.start], keys, c == 0)

    def block(jb, carry):
        r0 = pl.multiple_of(jb * TM, TM)

        @pl.when(jb > 0)
        def _():
            put(0)
            for c in range(n_chunks):
                if c + 1 < n_chunks:
                    put(c + 1)
                take(c)

        load_lhs(pl.multiple_of(jnp.minimum(jb + 1, n_blocks - 1) * TM, TM))
        finish(r0)
        return carry

    load_lhs(TM)
    lax.fori_loop(1, n_blocks, block, 0)


def _attention(q, kt, v, batch, nblk, *, diff, lam_vecs=None, g_out=None, lam_init=0.0):
    kv_heads = kt.shape[1]
    s_len = kt.shape[3]
    tiles = BRANCH_WIDTH // LANES
    rows = 2 * TM
    q_map = lambda b, h: (b, h)
    kv_map = lambda b, h: (b, h * kv_heads // tiles, 0, 0)
    const = lambda b, h: (0, 0)
    in_specs = [pl.BlockSpec((s_len, LANES), q_map),
                pl.BlockSpec((1, 1, LANES, s_len), kv_map),
                pl.BlockSpec((1, 1, s_len, LANES), kv_map)]
    args = [q, kt, v]
    if diff:
        in_specs += [pl.BlockSpec((4, HEAD_DIM), const), pl.BlockSpec((1, LANES), const)]
        args += [lam_vecs, g_out]
    n_ctx = TM
    tkl = min(SCORE_KEYS, s_len - n_ctx)
    assert (s_len - n_ctx) % tkl == 0
    bounds = [0] + list(range(n_ctx + tkl, s_len + 1, tkl))
    kern = functools.partial(_attn_kernel, diff=diff, lam_init=lam_init,
                             n_ctx=n_ctx, bounds=tuple(bounds), n_blocks=nblk)
    return pl.pallas_call(
        kern,
        grid=(batch, tiles),
        in_specs=in_specs,
        out_specs=pl.BlockSpec((s_len, LANES), q_map),
        out_shape=jax.ShapeDtypeStruct((batch * s_len, BRANCH_WIDTH), BF16),
        scratch_shapes=([pltpu.VMEM((rows, LANES), BF16)]
                        + [pltpu.VMEM((rows, LANES), F32)] * 3
                        + [pltpu.VMEM((rows, n_ctx + tkl), F32)] * 2),
        compiler_params=_cparams(2),
        name="diff_attn" if diff else "gqa_attn",
    )(*args)


def _s5_kernel(uf_ref, ub_ref, pin_ref, pout_ref, wd_ref, a_ref, cm_ref, yf_ref, yb_ref,
               bu_sc, st_sc):
    n_tiles = BRANCH_WIDTH // LANES
    half = SUBLANES * S5_STATE
    rows = TC * SUBLANES
    tok = (SUBLANES // 2) * TC

    @pl.when(pl.program_id(0) == 0)
    def _():
        st_sc[...] = jnp.zeros_like(st_sc)

    seq = lax.broadcasted_iota(jnp.int32, (rows, LANES), 0) & (SUBLANES - 1)
    fwd = seq < SUBLANES // 2
    lhs_dir = []
    for d, ref in enumerate((uf_ref, ub_ref)):
        ud = ref[...].reshape(tok, BRANCH_WIDTH).astype(BF16)
        lhs_dir.append(jnp.dot(pin_ref[d], ud, preferred_element_type=F32).astype(BF16))
    def drive(j):
        cs = slice(j * LANES, (j + 1) * LANES)
        lhs = jnp.concatenate([lhs_dir[0][:, cs], lhs_dir[1][:, cs]], axis=1)
        bu_sc[:, 2 * half * j:2 * half * (j + 1)] = jnp.dot(
            lhs, wd_ref[j], preferred_element_type=F32)

    def scan(j):
        re = slice(2 * half * j, 2 * half * j + half)
        im = slice(2 * half * j + half, 2 * half * (j + 1))
        ar = a_ref[0, :, half * j:half * (j + 1)]
        ai = a_ref[1, :, half * j:half * (j + 1)]

        def step(t, carry):
            xr, xi = carry
            r = pl.multiple_of(t * SUBLANES, SUBLANES)
            nxr = ar * xr - ai * xi + bu_sc[pl.ds(r, SUBLANES), re]
            nxi = ar * xi + ai * xr + bu_sc[pl.ds(r, SUBLANES), im]
            bu_sc[pl.ds(r, SUBLANES), re] = nxr
            bu_sc[pl.ds(r, SUBLANES), im] = nxi
            return nxr, nxi

        xr, xi = lax.fori_loop(0, TC, step, (st_sc[:, re], st_sc[:, im]), unroll=True)
        st_sc[:, re] = xr
        st_sc[:, im] = xi

    def readout(j):
        x = bu_sc[:, 2 * half * j:2 * half * (j + 1)].astype(BF16)
        yy = jnp.dot(x, cm_ref[j], preferred_element_type=F32)
        return jnp.where(fwd, yy[:, :LANES], yy[:, LANES:])

    ys = []
    drive(0)
    for j in range(n_tiles):
        if j + 1 < n_tiles:
            drive(j + 1)
        scan(j)
        ys.append(readout(j))
    y = jnp.concatenate(ys, axis=1)
    parts = []
    rest = y
    for _ in range(2):
        part = rest.astype(BF16)
        parts.append(part)
        rest = rest - part.astype(F32)
    stacked = jnp.concatenate(parts, axis=0)
    for d, ref in enumerate((yf_ref, yb_ref)):
        out = jnp.dot(pout_ref[d], stacked, preferred_element_type=F32)
        ref[...] = out.reshape(ref.shape)


def _scan_row_placement(n_seq):
    p = np.zeros((2, TC * 2 * n_seq, n_seq * TC), np.float32)
    for b in range(n_seq):
        for k in range(TC):
            p[0, 2 * n_seq * k + b, b * TC + k] = 1.0
            p[1, 2 * n_seq * (TC - 1 - k) + n_seq + b, b * TC + k] = 1.0
    pt = p.transpose(0, 2, 1)
    return jnp.asarray(p, BF16), jnp.asarray(np.concatenate([pt, pt], axis=2), BF16)


def _s5_scan(u3, wd, a8, cm, n_ctx):
    batch, s_len, _ = u3.shape
    rows = TC * SUBLANES
    n_tiles = BRANCH_WIDTH // LANES
    n_state = 2 * SUBLANES * S5_STATE * n_tiles
    n_steps = s_len // TC
    ctx_steps = n_ctx // TC
    p_in, p_out = _scan_row_placement(batch)
    fwd_map = lambda g: (0, g, 0)
    bwd_map = lambda g: (0, jnp.where(g < ctx_steps, ctx_steps - 1 - g,
                                      n_steps + ctx_steps - 1 - g), 0)
    c3 = lambda g: (0, 0, 0)
    blk = (batch, TC, BRANCH_WIDTH)
    out = jax.ShapeDtypeStruct(u3.shape, F32)
    return pl.pallas_call(
        _s5_kernel,
        grid=(n_steps,),
        in_specs=[pl.BlockSpec(blk, fwd_map),
                  pl.BlockSpec(blk, bwd_map),
                  pl.BlockSpec(p_in.shape, c3),
                  pl.BlockSpec(p_out.shape, c3),
                  pl.BlockSpec(wd.shape, c3),
                  pl.BlockSpec(a8.shape, c3),
                  pl.BlockSpec(cm.shape, c3)],
        out_specs=[pl.BlockSpec(blk, fwd_map), pl.BlockSpec(blk, bwd_map)],
        out_shape=[out, out],
        scratch_shapes=[pltpu.VMEM((rows, n_state), F32),
                        pltpu.VMEM((SUBLANES, n_state), F32)],
        compiler_params=_cparams(1),
        name="s5_scan",
    )(u3, u3, p_in, p_out, wd, a8, cm)


def _zoh(lam_re, lam_im, log_dt, b_re, b_im):
    dt = jnp.exp(log_dt)[..., None]
    mag = jnp.exp(lam_re * dt)
    a_re = mag * jnp.cos(lam_im * dt)
    a_im = mag * jnp.sin(lam_im * dt)
    den = lam_re * lam_re + lam_im * lam_im
    f_re = ((a_re - 1.0) * lam_re + a_im * lam_im) / den
    f_im = (a_im * lam_re - (a_re - 1.0) * lam_im) / den
    bb_re = f_re[..., None] * b_re - f_im[..., None] * b_im
    bb_im = f_re[..., None] * b_im + f_im[..., None] * b_re
    return a_re, a_im, bb_re, bb_im


def _s5_weights(lam_re, lam_im, log_dt, b_re, b_im, c_re, c_im, batch):
    n_tiles = BRANCH_WIDTH // LANES
    gpt = S5_GROUPS // n_tiles
    a_re, a_im, bb_re, bb_im = _zoh(lam_re, lam_im, log_dt, b_re, b_im)
    eye = jnp.eye(gpt, dtype=F32)

    def drive(bb):
        t = bb.reshape(2, n_tiles, gpt, S5_STATE, S5_CH)
        w = jnp.einsum('djgpc,gh->jdgchp', t, eye)
        return w.reshape(n_tiles, 2 * gpt * S5_CH, gpt * S5_STATE)

    def read(cc):
        t = cc.reshape(2, n_tiles, gpt, S5_CH, S5_STATE)
        w = jnp.einsum('djgcp,gh->jgpdhc', t, eye)
        return w.reshape(n_tiles, gpt * S5_STATE, 2 * gpt * S5_CH)

    wd = jnp.concatenate([drive(bb_re), drive(bb_im)], axis=2).astype(BF16)
    cm = jnp.concatenate([read(c_re), read(-c_im)], axis=1).astype(BF16)

    def per_seq(a):
        return jnp.repeat(a.reshape(2, 1, S5_GROUPS * S5_STATE), batch, axis=1).reshape(
            2 * batch, S5_GROUPS * S5_STATE)

    a8 = jnp.stack([per_seq(a_re), per_seq(a_im)])
    return wd, a8, cm


def _merge_kernel(x_ref, xc_ref, mod_ref, g1_ref, g2_ref, ya_ref, yd_ref, yf_ref, yr_ref, u_ref,
                  dsk_ref, wglu_ref, bglu_ref, wgate_ref, wbr_ref, wout_ref, x1_ref, h2_ref,
                  *, nblk):
    mods = mod_ref[0]
    proj = {k: jnp.dot(ref[...], wbr_ref[k], preferred_element_type=F32)
            for k, ref in ((0, ya_ref), (2, yd_ref))}
    x = _stream_tile(x_ref, xc_ref, nblk)
    h = _rms_mod(x, g1_ref[...], mods[0:1], mods[1:2]).astype(BF16)
    ys = yf_ref[...] + yr_ref[...] + dsk_ref[...] * u_ref[...]
    g = jax.nn.gelu(ys)
    yb = g * jax.nn.sigmoid(
        jnp.dot(g.astype(BF16), wglu_ref[...], preferred_element_type=F32) + bglu_ref[...])
    proj[1] = jnp.dot(yb.astype(BF16), wbr_ref[1], preferred_element_type=F32)
    m = None
    for k in range(3):
        gate = jax.nn.sigmoid(jnp.dot(h, wgate_ref[:, k * D_MODEL:(k + 1) * D_MODEL],
                                      preferred_element_type=F32))
        term = gate * proj[k]
        m = term if m is None else m + term
    y = jnp.dot(m.astype(BF16), wout_ref[...], preferred_element_type=F32)
    x1 = x + mods[2:3] * y
    x1_ref[...] = x1
    h2_ref[...] = _rms_mod(x1, g2_ref[...], mods[3:4], mods[4:5]).astype(BF16)


def _merge(x_lat, x_ctx, mods, g1, g2, ya, yd, yf, yr, u, d_skip, w_glu, b_glu, w_gate,
           w_branch, w_out, batch, nblk):
    t_rows = ya.shape[0]
    row = lambda i: (i, 0)
    mod_map = lambda i: (jnp.where(i % nblk == 0, batch, i // nblk), 0, 0)
    c2 = lambda i: (0, 0)
    c3 = lambda i: (0, 0, 0)
    x_specs, x_args = _stream_specs(x_lat, x_ctx, nblk)
    return pl.pallas_call(
        functools.partial(_merge_kernel, nblk=nblk),
        grid=(t_rows // TM,),
        in_specs=x_specs + [
                  pl.BlockSpec((1, 6, D_MODEL), mod_map),
                  pl.BlockSpec((1, D_MODEL), c2),
                  pl.BlockSpec((1, D_MODEL), c2),
                  pl.BlockSpec((TM, BRANCH_WIDTH), row),
                  pl.BlockSpec((TM, BRANCH_WIDTH), row),
                  pl.BlockSpec((TM, BRANCH_WIDTH), row),
                  pl.BlockSpec((TM, BRANCH_WIDTH), row),
                  pl.BlockSpec((TM, BRANCH_WIDTH), row),
                  pl.BlockSpec((1, BRANCH_WIDTH), c2),
                  pl.BlockSpec((BRANCH_WIDTH, BRANCH_WIDTH), c2),
                  pl.BlockSpec((1, BRANCH_WIDTH), c2),
                  pl.BlockSpec((D_MODEL, 3 * D_MODEL), c2),
                  pl.BlockSpec((3, BRANCH_WIDTH, D_MODEL), c3),
                  pl.BlockSpec((D_MODEL, D_MODEL), c2)],
        out_specs=[pl.BlockSpec((TM, D_MODEL), row), pl.BlockSpec((TM, D_MODEL), row)],
        out_shape=[jax.ShapeDtypeStruct((t_rows, D_MODEL), F32),
                   jax.ShapeDtypeStruct((t_rows, D_MODEL), BF16)],
        compiler_params=_cparams(1),
        name="merge",
    )(*x_args, mods, g1, g2, ya, yd, yf, yr, u, d_skip, w_glu, b_glu, w_gate, w_branch, w_out)


def _ffn_kernel(h_ref, hp_ref, hn_ref, x1_ref, mod_ref, perm_ref, wup_ref, cw_ref, wdn_ref, o_ref,
                lhs_sc, u0_sc, u1_sc, act_sc, *, nblk, n_chunks):
    pj = pl.program_id(0) % nblk
    left_ok = (pj >= 2).astype(F32)
    right_ok = jnp.logical_and(pj != 0, pj != nblk - 1).astype(F32)
    lhs_sc[0:TM, :] = jnp.dot(perm_ref[0], h_ref[...], preferred_element_type=F32).astype(BF16)
    hrow = lax.broadcasted_iota(jnp.int32, (HALO, D_MODEL), 0)
    halo = (jnp.where(hrow == HALO - 1, hp_ref[...].astype(F32) * left_ok, 0.0)
            + jnp.where(hrow == 0, hn_ref[...].astype(F32) * right_ok, 0.0))
    lhs_sc[TM:TM + HALO, :] = halo.astype(BF16)
    sub = lax.broadcasted_iota(jnp.int32, (SUBLANES, 2 * FF_CHUNK), 0)

    def up(k, buf):
        buf[...] = jnp.dot(lhs_sc[...], wup_ref[k], preferred_element_type=F32)

    def activate(k, buf):
        cw = cw_ref[k]
        before = buf[TM + HALO - 1:TM + HALO, :]
        after = buf[TM:TM + 1, :]
        first = jnp.where(sub == 0, before, pltpu.roll(buf[TM - SUBLANES:TM, :], 1, 0))
        last = jnp.where(sub == SUBLANES - 1, after,
                         pltpu.roll(buf[0:SUBLANES, :], SUBLANES - 1, 0))
        prev = jnp.concatenate([first, buf[0:TM - SUBLANES, :]], axis=0)
        nxt = jnp.concatenate([buf[SUBLANES:TM, :], last], axis=0)
        c = prev * cw[0:1] + buf[0:TM, :] * cw[1:2] + nxt * cw[2:3] + cw[3:4]
        act = jax.nn.silu(c[:, FF_CHUNK:]) * c[:, :FF_CHUNK]
        act_sc[:, k * FF_CHUNK:(k + 1) * FF_CHUNK] = act.astype(BF16)

    bufs = (u0_sc, u1_sc)
    up(0, bufs[0])
    for k in range(n_chunks):
        if k + 1 < n_chunks:
            up(k + 1, bufs[(k + 1) % 2])
        activate(k, bufs[k % 2])
    act = jnp.dot(perm_ref[1], act_sc[...], preferred_element_type=F32).astype(BF16)
    y = jnp.dot(act, wdn_ref[...], preferred_element_type=F32)
    o_ref[...] = x1_ref[...] + mod_ref[0][5:6] * y


def _strided_row_order():
    p = np.zeros((TM, TM), np.float32)
    groups = TM // SUBLANES
    for r in range(groups):
        for s in range(SUBLANES):
            p[SUBLANES * r + s, groups * s + r] = 1.0
    return jnp.asarray(np.stack([p, p.T]), BF16)


def _ffn(h2, x1, mods, wup, cw, wdn, batch, nblk, latent_only):
    t_rows = x1.shape[0]
    n_chunks = wup.shape[0]
    per = TM // HALO
    last = t_rows // HALO - 1
    row = lambda i: (i, 0)
    mod_map = lambda i: (jnp.where(i % nblk == 0, batch, i // nblk), 0, 0)
    kern = functools.partial(_ffn_kernel, nblk=nblk, n_chunks=n_chunks)
    if latent_only:
        out_map = lambda i: ((i // nblk) * (nblk - 1) + jnp.maximum(i % nblk - 1, 0), 0)
        out_rows = t_rows - batch * TM
    else:
        out_map, out_rows = row, t_rows
    return pl.pallas_call(
        kern,
        grid=(t_rows // TM,),
        in_specs=[pl.BlockSpec((TM, D_MODEL), row),
                  pl.BlockSpec((HALO, D_MODEL), lambda i: (jnp.maximum(i * per - 1, 0), 0)),
                  pl.BlockSpec((HALO, D_MODEL), lambda i: (jnp.minimum((i + 1) * per, last), 0)),
                  pl.BlockSpec((TM, D_MODEL), row),
                  pl.BlockSpec((1, 6, D_MODEL), mod_map),
                  pl.BlockSpec((2, TM, TM), lambda i: (0, 0, 0)),
                  pl.BlockSpec(wup.shape, lambda i: (0, 0, 0)),
                  pl.BlockSpec(cw.shape, lambda i: (0, 0, 0)),
                  pl.BlockSpec(wdn.shape, lambda i: (0, 0))],
        out_specs=pl.BlockSpec((TM, D_MODEL), out_map),
        out_shape=jax.ShapeDtypeStruct((out_rows, D_MODEL), F32),
        scratch_shapes=([pltpu.VMEM((TM + HALO, D_MODEL), BF16)]
                        + [pltpu.VMEM((TM + HALO, 2 * FF_CHUNK), F32)] * 2
                        + [pltpu.VMEM((TM, D_FF), BF16)]),
        compiler_params=_cparams(1),
        name="conv_ffn",
    )(h2, h2, h2, x1, mods, _strided_row_order(), wup, cw, wdn)


def _relayout_kernel(a_ref, g_ref, o_ref):
    o_ref[0] = jnp.concatenate([a_ref[0], g_ref[0]], axis=1).astype(o_ref.dtype)


def _chunk_major_bf16(w_up, layer, n_chunks):
    rows = w_up.shape[1]
    return pl.pallas_call(
        _relayout_kernel,
        grid=(n_chunks,),
        in_specs=[pl.BlockSpec((1, rows, FF_CHUNK), lambda k: (layer, 0, k)),
                  pl.BlockSpec((1, rows, FF_CHUNK), lambda k: (layer, 0, n_chunks + k))],
        out_specs=pl.BlockSpec((1, rows, 2 * FF_CHUNK), lambda k: (k, 0, 0)),
        out_shape=jax.ShapeDtypeStruct((n_chunks, rows, 2 * FF_CHUNK), BF16),
        compiler_params=_cparams(1),
        name="ffn_weight_layout",
    )(w_up, w_up)


def _cast_kernel(w_ref, o_ref):
    o_ref[...] = w_ref[0].astype(o_ref.dtype)


def _cast_cols_bf16(w, layer, first_col, n_cols):
    rows = w.shape[1]
    blk = 2 * LANES
    assert first_col % blk == 0 and n_cols % blk == 0
    return pl.pallas_call(
        _cast_kernel,
        grid=(n_cols // blk,),
        in_specs=[pl.BlockSpec((1, rows, blk), lambda k: (layer, 0, first_col // blk + k))],
        out_specs=pl.BlockSpec((rows, blk), lambda k: (0, k)),
        out_shape=jax.ShapeDtypeStruct((rows, n_cols), BF16),
        compiler_params=_cparams(1),
        name="weight_cast",
    )(w)


def _rope_tables(seq, ctx):
    rows = seq // GRID_W
    row = jnp.repeat(jnp.arange(rows, dtype=F32), GRID_W)
    col = jnp.tile(jnp.arange(GRID_W, dtype=F32), rows)
    n_freq = HEAD_DIM // 4
    inv_freq = ROPE_BASE ** (-jnp.arange(n_freq, dtype=F32) / n_freq)
    ang = jnp.concatenate([row[:, None] * inv_freq, col[:, None] * inv_freq], axis=-1)
    ang = jnp.concatenate([ang, ang], axis=-1)
    cos = jnp.concatenate([jnp.ones((ctx, HEAD_DIM), F32), jnp.cos(ang)], axis=0)
    sin = jnp.concatenate([jnp.zeros((ctx, HEAD_DIM), F32), jnp.sin(ang)], axis=0)
    sign = jnp.where(jnp.arange(HEAD_DIM) < HEAD_DIM // 2, -1.0, 1.0).astype(F32)
    two = lambda t: jnp.concatenate([t, t], axis=1)
    return two(cos), two(sin * sign)


def kernel(x, c, ctx, c_ctx, w_ada, b_ada, norm_g, w_in, qk_gain, ssm_lam_re, ssm_lam_im, ssm_log_dt, ssm_b_re, ssm_b_im, ssm_c_re, ssm_c_im, ssm_d, w_glu, b_glu, diff_lam, diff_norm_g, w_branch, w_out, w_up, conv_w, conv_b, w_down):
    batch, seq, _ = x.shape
    n_ctx = ctx.shape[1]
    depth = w_in.shape[0]
    assert n_ctx == TM and seq % TM == 0 and 2 * batch == SUBLANES
    s_len = n_ctx + seq
    nblk = s_len // TM
    assert s_len % TC == 0 and D_FF % FF_CHUNK == 0
    n_chunks = D_FF // FF_CHUNK

    cos_t, sin_t = _rope_tables(seq, n_ctx)
    cc = jnp.concatenate([c, c_ctx[None, :], jnp.zeros((SUBLANES - batch - 1, D_MODEL), F32)], axis=0)
    mods_all = _ada_mods(cc, w_ada, b_ada).reshape(depth, SUBLANES, 6, D_MODEL)

    xa = x.reshape(batch * seq, D_MODEL)
    xc = ctx.reshape(batch * n_ctx, D_MODEL)
    for i in range(depth):
        lam_init = 0.8 - 0.6 * math.exp(-0.3 * i)
        mods = mods_all[i]
        g1 = norm_g[i, 0:1]
        g2 = norm_g[i, 1:2]
        w_qkvu = _cast_cols_bf16(w_in, i, 0, N_QKVU)
        w_gate = _cast_cols_bf16(w_in, i, N_QKVU, 3 * D_MODEL)
        qa, kta, va, u, qc, ktc, vc = _inproj(xa, xc, mods, g1, w_qkvu,
                                              jnp.tile(qk_gain[i], (1, BRANCH_WIDTH // HEAD_DIM)),
                                              cos_t, sin_t, batch, nblk)
        ya = _attention(qa, kta, va, batch, nblk, diff=False)
        yd = _attention(qc, ktc, vc, batch, nblk, diff=True, lam_vecs=diff_lam[i],
                        g_out=diff_norm_g[i][None, :], lam_init=lam_init)

        wd, a8, cm = _s5_weights(ssm_lam_re[i], ssm_lam_im[i], ssm_log_dt[i], ssm_b_re[i],
                                 ssm_b_im[i], ssm_c_re[i], ssm_c_im[i], batch)
        yf, yr = _s5_scan(u.reshape(batch, s_len, BRANCH_WIDTH), wd, a8, cm, n_ctx)
        yf = yf.reshape(batch * s_len, BRANCH_WIDTH)
        yr = yr.reshape(batch * s_len, BRANCH_WIDTH)

        x1, h2 = _merge(xa, xc, mods, g1, g2, ya, yd, yf, yr, u, ssm_d[i][None, :],
                        w_glu[i].astype(BF16), b_glu[i][None, :], w_gate,
                        w_branch[i].astype(BF16), w_out[i].astype(BF16), batch, nblk)

        def chunked(t):
            a = t[:, :D_FF].reshape(t.shape[0], n_chunks, FF_CHUNK)
            g = t[:, D_FF:].reshape(t.shape[0], n_chunks, FF_CHUNK)
            return jnp.concatenate([a, g], axis=-1).transpose(1, 0, 2)

        wup = _chunk_major_bf16(w_up, i, n_chunks)
        cw = chunked(jnp.concatenate(
            [conv_w[i], conv_b[i][None, :], jnp.zeros((SUBLANES - 4, 2 * D_FF), F32)], axis=0))
        wdn = w_down[i].astype(BF16)
        xa = _ffn(h2, x1, mods, wup, cw, wdn, batch, nblk, latent_only=(i == depth - 1))
        xc = None

    return xa.reshape(batch, seq, D_MODEL)
```
